```python
import math
import jax, jax.numpy as jnp
from jax import lax
import numpy as np

D_MODEL = 1024
BATCH = 4
SEQ = 8192
DEPTH = 1

MIX_WIDTH = D_MODEL
ATTN_WIDTH = MIX_WIDTH // 2
REC_WIDTH = MIX_WIDTH - ATTN_WIDTH
HEAD_DIM = 64
N_HEADS = ATTN_WIDTH // HEAD_DIM
ROT_DIM = HEAD_DIM // 4
ROPE_THETA = 500000.0
ATTN_BRANCHES = ((128, 1), (512, 4), (2048, 16))
REC_BLOCKS = 8
REC_BLOCK_W = REC_WIDTH // REC_BLOCKS
CONV_WIDTH = 4
CONV_PAD = (2, 1)
RG_LRU_C = 8.0
IN_WIDTH = 3 * ATTN_WIDTH + 2 * REC_WIDTH
N_EXPERTS = 256
TOP_K = 8
N_GROUPS = 8
TOPK_GROUPS = 4
EXPERT_HIDDEN = 256
SHARED_HIDDEN = 256
ROUTED_SCALE = 2.5
EXPERT_BLOCK = 128
DEEPNORM_ALPHA = (2 * DEPTH) ** 0.25
DEEPNORM_BETA = (8 * DEPTH) ** -0.25
LN_EPS = 1e-5
NEG_INF = -1e30
MAX_POS_OFFSET = 4096

kernel_name = 'hybrid_dilated_attn_rglru_moe_encoder'


def _layer_norm(x, g, b):
    xf = x.astype(jnp.float32)
    mu = jnp.mean(xf, axis=-1, keepdims=True)
    var = jnp.mean(jnp.square(xf - mu), axis=-1, keepdims=True)
    return ((xf - mu) * lax.rsqrt(var + LN_EPS) * g + b).astype(x.dtype)


def _rms_norm(t, g):
    tf = t.astype(jnp.float32)
    return tf * lax.rsqrt(jnp.mean(jnp.square(tf), axis=-1, keepdims=True) + LN_EPS) * g


def _partial_rope(t, positions):
    half = ROT_DIM // 2
    inv_freq = ROPE_THETA ** (-jnp.arange(half, dtype=jnp.float32) * 2.0 / ROT_DIM)
    ang = positions.astype(jnp.float32)[:, :, None] * inv_freq
    cos = jnp.cos(ang)[:, :, None, :]
    sin = jnp.sin(ang)[:, :, None, :]
    tr = t[..., :ROT_DIM].astype(jnp.float32)
    t1, t2 = tr[..., :half], tr[..., half:]
    rot = jnp.concatenate([t1 * cos - t2 * sin, t2 * cos + t1 * sin], axis=-1).astype(t.dtype)
    return jnp.concatenate([rot, t[..., ROT_DIM:]], axis=-1)


def _band_attention(q, k, v, half):
    L, hd = q.shape[-2], q.shape[-1]
    blk = half
    nb = -(-L // blk)
    lead = q.shape[:-2]
    nl = len(lead)
    qb = jnp.pad(q, [(0, 0)] * nl + [(0, nb * blk - L), (0, 0)]).reshape(*lead, nb, blk, hd)

    def windows(t):
        tp = jnp.pad(t, [(0, 0)] * nl + [(blk, nb * blk - L + blk), (0, 0)])
        tp = tp.reshape(*lead, nb + 2, blk, hd)
        return jnp.concatenate([tp[..., :-2, :, :], tp[..., 1:-1, :, :], tp[..., 2:, :, :]], axis=-2)

    kw, vw = windows(k), windows(v)
    qpos = (jnp.arange(nb)[:, None] * blk + jnp.arange(blk)[None, :])[:, :, None]
    kpos = (jnp.arange(nb)[:, None] * blk - blk + jnp.arange(3 * blk)[None, :])[:, None, :]
    mask = (jnp.abs(qpos - kpos) <= half) & (kpos >= 0) & (kpos < L)
    s = jnp.einsum('...nqd,...nkd->...nqk', qb, kw, preferred_element_type=jnp.float32) * (hd ** -0.5)
    s = jnp.where(mask, s, NEG_INF)
    m = jnp.max(s, axis=-1, keepdims=True)
    p = jnp.exp(s - m)
    den = jnp.sum(p, axis=-1)
    o = jnp.einsum('...nqk,...nkd->...nqd', p, vw.astype(jnp.float32)) / den[..., None]
    lse = m[..., 0] + jnp.log(den)
    o = o.reshape(*lead, nb * blk, hd)[..., :L, :]
    lse = lse.reshape(*lead, nb * blk)[..., :L]
    return o, lse


def _dilated_attention(q, k, v):
    B, H, S, hd = q.shape
    outs, lses = [], []
    for window, dil in ATTN_BRANCHES:
        half = window // (2 * dil)

        def to_residue(t, d=dil):
            return t.reshape(B, H, S // d, d, hd).swapaxes(2, 3)

        o, lse = _band_attention(to_residue(q), to_residue(k), to_residue(v), half)
        outs.append(o.swapaxes(2, 3).reshape(B, H, S, hd))
        lses.append(lse.swapaxes(2, 3).reshape(B, H, S))
    wts = jax.nn.softmax(jnp.stack(lses), axis=0)
    return jnp.einsum('gbhs,gbhsd->bhsd', wts, jnp.stack(outs))


def _centred_depthwise_conv(u, w, b):
    y = lax.conv_general_dilated(u, w[:, None, :], window_strides=(1,), padding=[CONV_PAD],
                                 dimension_numbers=('NWC', 'WIO', 'NWC'),
                                 feature_group_count=u.shape[-1])
    return y + b


def _linear_combine(c1, c2):
    a1, b1 = c1
    a2, b2 = c2
    return a1 * a2, a2 * b1 + b2


def _rg_lru_scan(u, w_a, b_a, w_x, b_x, lam, reverse):
    B, S, W = u.shape
    ub = u.reshape(B, S, REC_BLOCKS, REC_BLOCK_W)
    r = jax.nn.sigmoid((jnp.einsum('bsni,nij->bsnj', ub, w_a).reshape(B, S, W) + b_a).astype(jnp.float32))
    i = jax.nn.sigmoid((jnp.einsum('bsni,nij->bsnj', ub, w_x).reshape(B, S, W) + b_x).astype(jnp.float32))
    log_a = -RG_LRU_C * r * jax.nn.softplus(-lam.astype(jnp.float32))
    a = jnp.exp(log_a)
    b = jnp.sqrt(-jnp.expm1(2.0 * log_a)) * i * u.astype(jnp.float32)
    _, h = lax.associative_scan(_linear_combine, (a, b), axis=1, reverse=reverse)
    return h


def _hybrid_mixer(x, positions, w_in, attn_gain, conv_w, conv_b, ga_w, ga_b, gx_w, gx_b, lam, rec_gain, w_out):
    B, S, _ = x.shape
    h = jnp.einsum('bsd,de->bse', x, w_in)
    q, k, v, rx, rg = jnp.split(h, [ATTN_WIDTH, 2 * ATTN_WIDTH, 3 * ATTN_WIDTH,
                                    3 * ATTN_WIDTH + REC_WIDTH], axis=-1)

    def heads(t):
        return t.reshape(B, S, N_HEADS, HEAD_DIM)

    q = _partial_rope(heads(q), positions).transpose(0, 2, 1, 3)
    k = _partial_rope(heads(k), positions).transpose(0, 2, 1, 3)
    v = heads(v).transpose(0, 2, 1, 3)
    attn = _dilated_attention(q, k, v).transpose(0, 2, 1, 3).reshape(B, S, ATTN_WIDTH)

    u = _centred_depthwise_conv(rx, conv_w, conv_b)
    h_fwd = _rg_lru_scan(u, ga_w[0], ga_b[0], gx_w[0], gx_b[0], lam[0], reverse=False)
    h_bwd = _rg_lru_scan(u, ga_w[1], ga_b[1], gx_w[1], gx_b[1], lam[1], reverse=True)
    rec = (h_fwd + h_bwd) * jax.nn.gelu(rg.astype(jnp.float32))

    mix = jnp.concatenate([_rms_norm(attn, attn_gain), _rms_norm(rec, rec_gain)], axis=-1).astype(x.dtype)
    return jnp.einsum('bse,ed->bsd', mix, w_out)


def _swiglu(x, wg, wu, wd):
    return (jax.nn.silu(x @ wg) * (x @ wu)) @ wd


def _routed_experts(xt, idx, gate, w_gate, w_up, w_down):
    T, D = xt.shape
    N = T * TOP_K
    n_blocks = -(-(N + N_EXPERTS * (EXPERT_BLOCK - 1)) // EXPERT_BLOCK)
    P = n_blocks * EXPERT_BLOCK
    flat_e = idx.reshape(N)
    order = jnp.argsort(flat_e)
    e_sorted = flat_e[order]
    tok_sorted = (order // TOP_K).astype(jnp.int32)
    w_sorted = gate.reshape(N)[order]
    counts = jnp.bincount(flat_e, length=N_EXPERTS)
    padded = (counts + EXPERT_BLOCK - 1) // EXPERT_BLOCK * EXPERT_BLOCK
    pad_end = jnp.cumsum(padded)
    pad_start = pad_end - padded
    start = jnp.cumsum(counts) - counts
    dest = pad_start[e_sorted] + jnp.arange(N) - start[e_sorted]
    slot_tok = jnp.full((P,), T, jnp.int32).at[dest].set(tok_sorted)
    slot_w = jnp.zeros((P,), xt.dtype).at[dest].set(w_sorted)
    block_e = jnp.minimum(jnp.searchsorted(pad_end, jnp.arange(n_blocks) * EXPERT_BLOCK, side='right'),
                          N_EXPERTS - 1)
    x_pad = jnp.concatenate([xt, jnp.zeros((1, D), xt.dtype)], axis=0)

    def step(acc, blk):
        tok, w, e = blk
        yb = _swiglu(x_pad[tok], w_gate[e], w_up[e], w_down[e]) * w[:, None]
        return acc.at[tok].add(yb), None

    acc, _ = lax.scan(step, jnp.zeros((T + 1, D), xt.dtype),
                      (slot_tok.reshape(n_blocks, EXPERT_BLOCK), slot_w.reshape(n_blocks, EXPERT_BLOCK), block_e))
    return acc[:T]


def _moe_ffn(x, router_w, router_bias, exp_w_gate, exp_w_up, exp_w_down, sh_gate, sh_up, sh_down):
    B, S, D = x.shape
    T = B * S
    xt = x.reshape(T, D)
    scores = jax.nn.sigmoid(jnp.einsum('td,de->te', xt, router_w, preferred_element_type=jnp.float32))
    biased = scores + router_bias.astype(jnp.float32)
    grp = biased.reshape(T, N_GROUPS, N_EXPERTS // N_GROUPS)
    grp_score = jnp.sum(lax.top_k(grp, 2)[0], axis=-1)
    _, top_grp = lax.top_k(grp_score, TOPK_GROUPS)
    grp_mask = jnp.sum(jax.nn.one_hot(top_grp, N_GROUPS, dtype=jnp.float32), axis=-2) > 0
    expert_mask = jnp.repeat(grp_mask, N_EXPERTS // N_GROUPS, axis=-1)
    _, idx = lax.top_k(jnp.where(expert_mask, biased, -jnp.inf), TOP_K)
    gate = jnp.take_along_axis(scores, idx, axis=-1)
    gate = gate / jnp.sum(gate, axis=-1, keepdims=True) * ROUTED_SCALE
    routed = _routed_experts(xt, idx, gate.astype(xt.dtype), exp_w_gate, exp_w_up, exp_w_down)
    shared = _swiglu(xt, sh_gate, sh_up, sh_down)
    return (routed + shared).reshape(B, S, D)


def setup_inputs(seed: int = 0) -> dict:
    key = jax.random.key(seed)
    ks = jax.random.split(key, 32)
    f32 = jnp.float32

    def nrm(k, shape, scale):
        return jax.random.normal(k, shape, f32) * scale

    x = nrm(ks[0], (BATCH, SEQ, D_MODEL), 1.0)
    positions = (jax.random.randint(ks[1], (BATCH, 1), 0, MAX_POS_OFFSET)
                 + jnp.arange(SEQ)[None, :]).astype(jnp.int32)
    col_scale = jnp.ones((IN_WIDTH,), f32).at[2 * ATTN_WIDTH:3 * ATTN_WIDTH].set(DEEPNORM_BETA)
    w_in = nrm(ks[2], (DEPTH, D_MODEL, IN_WIDTH), D_MODEL ** -0.5) * col_scale
    attn_gain = 1.0 + nrm(ks[3], (DEPTH, ATTN_WIDTH), 0.1)
    rec_conv_w = nrm(ks[4], (DEPTH, CONV_WIDTH, REC_WIDTH), CONV_WIDTH ** -0.5)
    rec_conv_b = nrm(ks[5], (DEPTH, REC_WIDTH), 0.02)
    rec_gate_a_w = nrm(ks[6], (DEPTH, 2, REC_BLOCKS, REC_BLOCK_W, REC_BLOCK_W), REC_BLOCK_W ** -0.5)
    rec_gate_a_b = nrm(ks[7], (DEPTH, 2, REC_WIDTH), 0.02)
    rec_gate_x_w = nrm(ks[8], (DEPTH, 2, REC_BLOCKS, REC_BLOCK_W, REC_BLOCK_W), REC_BLOCK_W ** -0.5)
    rec_gate_x_b = nrm(ks[9], (DEPTH, 2, REC_WIDTH), 0.02)
    a_c = jax.random.uniform(ks[10], (DEPTH, 2, REC_WIDTH), f32, minval=0.9, maxval=0.999)
    a0 = a_c ** (1.0 / RG_LRU_C)
    rec_lambda = jnp.log(a0) - jnp.log1p(-a0)
    rec_gain = 1.0 + nrm(ks[11], (DEPTH, REC_WIDTH), 0.1)
    w_out = nrm(ks[12], (DEPTH, MIX_WIDTH, D_MODEL), MIX_WIDTH ** -0.5 * DEEPNORM_BETA)
    ln1_g = 1.0 + nrm(ks[13], (DEPTH, D_MODEL), 0.1)
    ln1_b = nrm(ks[14], (DEPTH, D_MODEL), 0.02)
    router_w = nrm(ks[15], (DEPTH, D_MODEL, N_EXPERTS), D_MODEL ** -0.5)
    router_bias = nrm(ks[16], (DEPTH, N_EXPERTS), 0.01)
    exp_w_gate = nrm(ks[17], (DEPTH, N_EXPERTS, D_MODEL, EXPERT_HIDDEN), D_MODEL ** -0.5)
    exp_w_up = nrm(ks[18], (DEPTH, N_EXPERTS, D_MODEL, EXPERT_HIDDEN), D_MODEL ** -0.5)
    exp_w_down = nrm(ks[19], (DEPTH, N_EXPERTS, EXPERT_HIDDEN, D_MODEL), EXPERT_HIDDEN ** -0.5 * DEEPNORM_BETA)
    shared_w_gate = nrm(ks[20], (DEPTH, D_MODEL, SHARED_HIDDEN), D_MODEL ** -0.5)
    shared_w_up = nrm(ks[21], (DEPTH, D_MODEL, SHARED_HIDDEN), D_MODEL ** -0.5)
    shared_w_down = nrm(ks[22], (DEPTH, SHARED_HIDDEN, D_MODEL), SHARED_HIDDEN ** -0.5 * DEEPNORM_BETA)
    ln2_g = 1.0 + nrm(ks[23], (DEPTH, D_MODEL), 0.1)
    ln2_b = nrm(ks[24], (DEPTH, D_MODEL), 0.02)
    return {'x': x, 'positions': positions, 'w_in': w_in, 'attn_gain': attn_gain,
            'rec_conv_w': rec_conv_w, 'rec_conv_b': rec_conv_b,
            'rec_gate_a_w': rec_gate_a_w, 'rec_gate_a_b': rec_gate_a_b,
            'rec_gate_x_w': rec_gate_x_w, 'rec_gate_x_b': rec_gate_x_b,
            'rec_lambda': rec_lambda, 'rec_gain': rec_gain, 'w_out': w_out,
            'ln1_g': ln1_g, 'ln1_b': ln1_b, 'router_w': router_w, 'router_bias': router_bias,
            'exp_w_gate': exp_w_gate, 'exp_w_up': exp_w_up, 'exp_w_down': exp_w_down,
            'shared_w_gate': shared_w_gate, 'shared_w_up': shared_w_up, 'shared_w_down': shared_w_down,
            'ln2_g': ln2_g, 'ln2_b': ln2_b}


def reference(x, positions, w_in, attn_gain, rec_conv_w, rec_conv_b, rec_gate_a_w, rec_gate_a_b,
              rec_gate_x_w, rec_gate_x_b, rec_lambda, rec_gain, w_out, ln1_g, ln1_b, router_w,
              router_bias, exp_w_gate, exp_w_up, exp_w_down, shared_w_gate, shared_w_up,
              shared_w_down, ln2_g, ln2_b):
    for l in range(DEPTH):
        y = _hybrid_mixer(x, positions, w_in[l], attn_gain[l], rec_conv_w[l], rec_conv_b[l],
                          rec_gate_a_w[l], rec_gate_a_b[l], rec_gate_x_w[l], rec_gate_x_b[l],
                          rec_lambda[l], rec_gain[l], w_out[l])
        x = _layer_norm(DEEPNORM_ALPHA * x + y, ln1_g[l], ln1_b[l])
        f = _moe_ffn(x, router_w[l], router_bias[l], exp_w_gate[l], exp_w_up[l], exp_w_down[l],
                     shared_w_gate[l], shared_w_up[l], shared_w_down[l])
        x = _layer_norm(DEEPNORM_ALPHA * x + f, ln2_g[l], ln2_b[l])
    return x
```

```python
import functools
import math

import jax
import jax.numpy as jnp
import numpy as np
from jax import lax
from jax.experimental import pallas as pl
from jax.experimental.pallas import tpu as pltpu

f32 = jnp.float32
bf16 = jnp.bfloat16
i32 = jnp.int32

D_MODEL = 1024
ATTN_W = 512
REC_W = 512
HEAD_DIM = 64
ROT_DIM = 16
ROPE_THETA = 500000.0
BRANCHES = ((128, 1), (512, 4), (2048, 16))
HALF_BAND = 64
CONV_W = 4
RG_LRU_C = 8.0
N_EXPERTS = 256
TOP_K = 8
N_GROUPS = 8
GROUP_SZ = N_EXPERTS // N_GROUPS
TOPK_GROUPS = 4
EXPERT_H = 256
ROUTED_SCALE = 2.5
LN_EPS = 1e-5
NEG = -1e30

LANES = 128
SUBLANES = 8
CHUNKS = D_MODEL // LANES
VMEM_LIMIT = 56 * 1024 * 1024

TM_IN = 512
TQ = 128
KW = TQ + 2 * HALF_BAND
TS = 256
TM_MIX = 256
TM_CMB = 128
BM = 256
PITCH = 8


def _cparams(sem):
    return pltpu.CompilerParams(dimension_semantics=sem, vmem_limit_bytes=VMEM_LIMIT)


def _inproj_body(x_ref, pos_ref, w_ref, invf_ref, q_ref, k_ref, v_ref, rx_ref, rg_ref):
    xb = x_ref[...].astype(bf16)
    ang = pos_ref[...].astype(f32) * invf_ref[...]
    cos = jnp.cos(ang)
    sin = jnp.sin(ang)
    j = lax.broadcasted_iota(i32, (1, LANES), 1) % HEAD_DIM
    half = ROT_DIM // 2
    s_lo = jnp.where(j < half, -sin, 0.0)
    s_hi = jnp.where((j >= half) & (j < ROT_DIM), sin, 0.0)
    rep = ATTN_W // LANES
    cos_w = jnp.concatenate([cos] * rep, axis=1)
    s_lo_w = jnp.concatenate([s_lo] * rep, axis=1)
    s_hi_w = jnp.concatenate([s_hi] * rep, axis=1)

    def proj(c0, n):
        return jnp.dot(xb, w_ref[:, c0:c0 + n], preferred_element_type=f32)

    def rope(t):
        return (t * cos_w + pltpu.roll(t, ATTN_W - half, 1) * s_lo_w
                + pltpu.roll(t, half, 1) * s_hi_w)

    q_ref[...] = (rope(proj(0, ATTN_W)) * (HEAD_DIM ** -0.5)).astype(bf16)
    k_ref[...] = rope(proj(ATTN_W, ATTN_W)).astype(bf16)
    v_ref[...] = proj(2 * ATTN_W, ATTN_W).astype(bf16)
    rx_ref[...] = proj(3 * ATTN_W, REC_W)
    rg_ref[...] = proj(3 * ATTN_W + REC_W, REC_W)


def _inproj(x2, pos2, w_in_b, invf):
    T = x2.shape[0]
    tm = TM_IN
    in_w = w_in_b.shape[1]
    tok = lambda i: (i, 0)
    fixed = lambda i: (0, 0)
    return pl.pallas_call(
        _inproj_body,
        grid=(T // tm,),
        in_specs=[pl.BlockSpec((tm, D_MODEL), tok), pl.BlockSpec((tm, 1), tok),
                  pl.BlockSpec((D_MODEL, in_w), fixed), pl.BlockSpec((1, LANES), fixed)],
        out_specs=[pl.BlockSpec((tm, ATTN_W), tok)] * 3 + [pl.BlockSpec((tm, REC_W), tok)] * 2,
        out_shape=[jax.ShapeDtypeStruct((T, ATTN_W), bf16)] * 3
        + [jax.ShapeDtypeStruct((T, REC_W), f32)] * 2,
        compiler_params=_cparams(("parallel",)),
        name="inproj",
    )(x2, pos2, w_in_b, invf)


def _attn_body(q_ref, k_ref, v_ref, o_ref, lse_ref, *, L):
    lane = lax.broadcasted_iota(i32, (1, LANES), 1)
    head0 = lane < HEAD_DIM
    rel = (lax.broadcasted_iota(i32, (TQ, KW), 0) - lax.broadcasted_iota(i32, (TQ, KW), 1))

    def tile(i, carry):
        q0 = pl.multiple_of(i * TQ, TQ)
        start = pl.multiple_of(jnp.clip(q0 - HALF_BAND, 0, L - KW), HALF_BAND)
        q = q_ref[pl.ds(q0, TQ), :]
        k = k_ref[pl.ds(start, KW), :]
        v = v_ref[pl.ds(start, KW), :]
        valid = jnp.abs(rel + (q0 - start)) <= HALF_BAND
        outs = []
        for sel in (head0, jnp.logical_not(head0)):
            qh = jnp.where(sel, q, jnp.zeros_like(q))
            s = lax.dot_general(qh, k, (((1,), (1,)), ((), ())), preferred_element_type=f32)
            s = jnp.where(valid, s, NEG)
            m = jnp.max(s, axis=-1, keepdims=True)
            p = jnp.exp(s - m)
            den = jnp.sum(p, axis=-1, keepdims=True)
            o = jnp.dot(p.astype(bf16), v, preferred_element_type=f32) / den
            outs.append((o, m + jnp.log(den)))
        o_ref[pl.ds(q0, TQ), :] = jnp.where(head0, outs[0][0], outs[1][0]).astype(bf16)
        lse_ref[pl.ds(q0, TQ), :] = jnp.where(head0, outs[0][1], outs[1][1])
        return carry

    lax.fori_loop(0, L // TQ, tile, 0)


def _attn_branch(q, k, v, B, S, dil):
    L = S // dil
    assert L % TQ == 0 and L >= KW
    hp = ATTN_W // LANES
    view = lambda t: t.reshape(B, L, dil * ATTN_W)
    spec = pl.BlockSpec((None, L, LANES), lambda b, r, h: (b, 0, r * hp + h))
    o, lse = pl.pallas_call(
        functools.partial(_attn_body, L=L),
        grid=(B, dil, hp),
        in_specs=[spec, spec, spec],
        out_specs=[spec, spec],
        out_shape=[jax.ShapeDtypeStruct((B, L, dil * ATTN_W), bf16),
                   jax.ShapeDtypeStruct((B, L, dil * ATTN_W), f32)],
        compiler_params=_cparams(("parallel", "parallel", "parallel")),
        name=f"attn_d{dil}",
    )(view(q), view(k), view(v))
    return o.reshape(B * S, ATTN_W), lse.reshape(B * S, ATTN_W)


def _rec_body(rx_ref, rg_ref, cw_ref, cb_ref, wg_ref, gb_ref, lam_ref, out_ref, rxp_ref, hf_ref, *, S):
    nch = S // TS
    ntile = TS // SUBLANES
    zeros8 = jnp.zeros((SUBLANES, LANES), f32)
    rxp_ref[pl.ds(0, SUBLANES), :] = zeros8
    rxp_ref[pl.ds(S + SUBLANES, SUBLANES), :] = zeros8

    def pad_copy(c, carry):
        t0 = pl.multiple_of(c * TS, TS)
        rxp_ref[pl.ds(t0 + SUBLANES, TS), :] = rx_ref[pl.ds(t0, TS), :]
        return carry

    lax.fori_loop(0, nch, pad_copy, 0)

    lam = lam_ref[...]
    neg_sp = -RG_LRU_C * (jnp.maximum(-lam, 0.0) + jnp.log1p(jnp.exp(-jnp.abs(lam))))
    cw = cw_ref[...]
    cb = cb_ref[...]
    sub = lax.broadcasted_iota(i32, (ntile, SUBLANES, LANES), 1)
    nrow = TS + 2 * SUBLANES

    def gates(ci, d):
        t0 = pl.multiple_of(ci * TS, TS)
        xw = rxp_ref[pl.ds(t0, nrow), :]
        u = (cw[0:1] * pltpu.roll(xw, 2, 0) + cw[1:2] * pltpu.roll(xw, 1, 0) + cw[2:3] * xw
             + cw[3:4] * pltpu.roll(xw, nrow - 1, 0))[SUBLANES:SUBLANES + TS] + cb
        c0 = d * 2 * LANES
        g = jnp.dot(u.astype(bf16), wg_ref[:, c0:c0 + 2 * LANES], preferred_element_type=f32)
        g = g + gb_ref[:, c0:c0 + 2 * LANES]
        r = jax.nn.sigmoid(g[:, :LANES])
        gi = jax.nn.sigmoid(g[:, LANES:])
        a = jnp.exp(neg_sp[d:d + 1] * r)
        b = jnp.sqrt(1.0 - a * a) * gi * u
        return a.reshape(ntile, SUBLANES, LANES), b.reshape(ntile, SUBLANES, LANES)

    def chunk(i, carry):
        cf, cbk = carry
        a, b = gates(i, 0)
        for s in (1, 2, 4):
            ok = sub >= s
            a_s = pltpu.roll(a, s, 1)
            b_s = pltpu.roll(b, s, 1)
            b = jnp.where(ok, a * b_s + b, b)
            a = jnp.where(ok, a * a_s, a)
        t0 = pl.multiple_of(i * TS, TS)
        for j in range(ntile):
            h = a[j] * cf + b[j]
            hf_ref[pl.ds(t0 + j * SUBLANES, SUBLANES), :] = h
            cf = h[SUBLANES - 1:SUBLANES, :]
        ib = nch - 1 - i
        a, b = gates(ib, 1)
        for s in (1, 2, 4):
            ok = sub < SUBLANES - s
            a_s = pltpu.roll(a, SUBLANES - s, 1)
            b_s = pltpu.roll(b, SUBLANES - s, 1)
            b = jnp.where(ok, a * b_s + b, b)
            a = jnp.where(ok, a * a_s, a)
        t0 = pl.multiple_of(ib * TS, TS)
        for j in range(ntile - 1, -1, -1):
            h = a[j] * cbk + b[j]
            out_ref[pl.ds(t0 + j * SUBLANES, SUBLANES), :] = h
            cbk = h[0:1, :]
        return cf, cbk

    zrow = jnp.zeros((1, LANES), f32)
    lax.fori_loop(0, nch, chunk, (zrow, zrow))

    def finish(c, carry):
        t0 = pl.multiple_of(c * TS, TS)
        sl = pl.ds(t0, TS)
        out_ref[sl, :] = (hf_ref[sl, :] + out_ref[sl, :]) * jax.nn.gelu(rg_ref[sl, :])
        return carry

    lax.fori_loop(0, nch, finish, 0)


def _rec(rx, rg, conv_w, conv_b, wg, gb, lam, B, S):
    ng = REC_W // LANES
    assert S % TS == 0
    seq = pl.BlockSpec((None, S, LANES), lambda b, c: (b, 0, c))
    return pl.pallas_call(
        functools.partial(_rec_body, S=S),
        grid=(B, ng),
        in_specs=[seq, seq,
                  pl.BlockSpec((CONV_W, LANES), lambda b, c: (0, c)),
                  pl.BlockSpec((1, LANES), lambda b, c: (0, c)),
                  pl.BlockSpec((None, LANES, 4 * LANES), lambda b, c: (c, 0, 0)),
                  pl.BlockSpec((None, 1, 4 * LANES), lambda b, c: (c, 0, 0)),
                  pl.BlockSpec((2, LANES), lambda b, c: (0, c))],
        out_specs=seq,
        out_shape=jax.ShapeDtypeStruct((B, S, REC_W), f32),
        scratch_shapes=[pltpu.VMEM((S + 2 * SUBLANES, LANES), f32), pltpu.VMEM((S, LANES), f32)],
        compiler_params=_cparams(("parallel", "parallel")),
        name="rec",
    )(rx.reshape(B, S, REC_W), rg.reshape(B, S, REC_W), conv_w, conv_b, wg, gb, lam)


def _rows_to_buf(buf, val, rows):
    sp = rows + PITCH
    for j in range(CHUNKS):
        buf[pl.ds(j * sp, rows), :] = val[:, j * LANES:(j + 1) * LANES]


def _buf_to_rows(buf, rows):
    sp = rows + PITCH
    return jnp.concatenate([buf[pl.ds(j * sp, rows), :] for j in range(CHUNKS)], axis=1)


def _slab_ds(t, rows):
    return pl.ds(t, CHUNKS, stride=rows + PITCH)


def _rms(t, gain):
    return t * lax.rsqrt(jnp.mean(t * t, axis=-1, keepdims=True) + LN_EPS) * gain


def _layer_norm(z, g, b):
    mu = jnp.mean(z, axis=-1, keepdims=True)
    zc = z - mu
    var = jnp.mean(zc * zc, axis=-1, keepdims=True)
    return zc * lax.rsqrt(var + LN_EPS) * g + b


def _mixout_body(o1_ref, o2_ref, o3_ref, l1_ref, l2_ref, l3_ref, rec_ref, x_ref,
                 ag_ref, rgn_ref, wo_ref, g1_ref, b1_ref, rwh_ref, rwl_ref, rb_ref, tri_ref, ones_ref,
                 x1_ref, x1s_ref, idx_ref, gate_ref, rank_ref, cnt_ref, carry_ref, buf_ref, *, alpha):
    tm = TM_MIX

    @pl.when(pl.program_id(0) == 0)
    def _():
        carry_ref[...] = jnp.zeros_like(carry_ref)

    l1, l2, l3 = l1_ref[...], l2_ref[...], l3_ref[...]
    m = jnp.maximum(jnp.maximum(l1, l2), l3)
    e1, e2, e3 = jnp.exp(l1 - m), jnp.exp(l2 - m), jnp.exp(l3 - m)
    attn = (e1 * o1_ref[...].astype(f32) + e2 * o2_ref[...].astype(f32)
            + e3 * o3_ref[...].astype(f32)) / (e1 + e2 + e3)
    attn_n = _rms(attn, ag_ref[...]).astype(bf16)
    rec_n = _rms(rec_ref[...], rgn_ref[...]).astype(bf16)
    y = (jnp.dot(attn_n, wo_ref[0:ATTN_W, :], preferred_element_type=f32)
         + jnp.dot(rec_n, wo_ref[ATTN_W:, :], preferred_element_type=f32))
    x1 = _layer_norm(alpha * x_ref[...] + y, g1_ref[...], b1_ref[...])
    x1_ref[...] = x1
    _rows_to_buf(buf_ref, x1, tm)
    for t in range(tm):
        x1s_ref[t] = buf_ref[_slab_ds(t, tm), :]

    hi = x1.astype(bf16)
    lo = (x1 - hi.astype(f32)).astype(bf16)
    nt = (((1,), (1,)), ((), ()))
    logits = (lax.dot_general(rwh_ref[...], hi, nt, preferred_element_type=f32)
              + lax.dot_general(rwh_ref[...], lo, nt, preferred_element_type=f32)
              + lax.dot_general(rwl_ref[...], hi, nt, preferred_element_type=f32))
    scores = jax.nn.sigmoid(logits)
    biased = scores + jnp.concatenate([rb_ref[...]] * (tm // LANES), axis=1)

    rid = lax.broadcasted_iota(i32, (GROUP_SZ, tm), 0).astype(f32)
    grp = []
    for g in range(N_GROUPS):
        vg = biased[g * GROUP_SZ:(g + 1) * GROUP_SZ, :]
        m1 = jnp.max(vg, axis=0, keepdims=True)
        first = jnp.min(jnp.where(vg == m1, rid, float(GROUP_SZ)), axis=0, keepdims=True)
        m2 = jnp.max(jnp.where(rid == first, -jnp.inf, vg), axis=0, keepdims=True)
        grp.append(m1 + m2)
    eid = lax.broadcasted_iota(i32, (N_EXPERTS, tm), 0).astype(f32)
    egid = jnp.floor(eid * (1.0 / GROUP_SZ))
    masked = jnp.full((N_EXPERTS, tm), -jnp.inf, f32)
    for _ in range(TOPK_GROUPS):
        gm = functools.reduce(jnp.maximum, grp)
        gi = jnp.full((1, tm), float(N_GROUPS), f32)
        for g in range(N_GROUPS - 1, -1, -1):
            gi = jnp.where(grp[g] == gm, float(g), gi)
        grp = [jnp.where(gi == float(g), -jnp.inf, grp[g]) for g in range(N_GROUPS)]
        masked = jnp.where(egid == gi, biased, masked)

    onehot = jnp.zeros((N_EXPERTS, tm), f32)
    idxs, gts = [], []
    for _ in range(TOP_K):
        mx = jnp.max(masked, axis=0, keepdims=True)
        ix = jnp.min(jnp.where(masked == mx, eid, float(N_EXPERTS)), axis=0, keepdims=True)
        hit = eid == ix
        gts.append(jnp.sum(jnp.where(hit, scores, 0.0), axis=0, keepdims=True))
        idxs.append(ix)
        masked = jnp.where(hit, -jnp.inf, masked)
        onehot = onehot + jnp.where(hit, 1.0, 0.0)
    gsum = functools.reduce(lambda p, q: p + q, gts)
    for kk in range(TOP_K):
        idx_ref[kk:kk + 1, :] = idxs[kk].astype(i32)
        gate_ref[kk:kk + 1, :] = gts[kk] / gsum * ROUTED_SCALE

    oh = onehot.astype(bf16)
    before = carry_ref[...] + jnp.dot(oh, tri_ref[...], preferred_element_type=f32)
    for kk in range(TOP_K):
        rk = jnp.sum(jnp.where(eid == idxs[kk], before, 0.0), axis=0, keepdims=True)
        rank_ref[kk:kk + 1, :] = rk.astype(i32)
    total = carry_ref[...] + jnp.dot(oh, ones_ref[...], preferred_element_type=f32)
    carry_ref[...] = total
    cnt_ref[...] = total


def _mixout(o, lse, rec, x2, attn_gain, rec_gain, w_out_b, g1, b1, rw_hi, rw_lo, rbias, alpha):
    T = x2.shape[0]
    tm = TM_MIX
    tok = lambda i: (i, 0)
    fixed = lambda i: (0, 0)
    tri = jnp.asarray(np.triu(np.ones((tm, tm), np.float32), k=1), bf16)
    ones = jnp.ones((tm, tm), bf16)
    aw = pl.BlockSpec((tm, ATTN_W), tok)
    row = lambda n: pl.BlockSpec((1, n), fixed)
    kt = pl.BlockSpec((TOP_K, tm), lambda i: (0, i))
    return pl.pallas_call(
        functools.partial(_mixout_body, alpha=alpha),
        grid=(T // tm,),
        in_specs=[aw] * 7 + [pl.BlockSpec((tm, D_MODEL), tok), row(ATTN_W), row(REC_W),
                             pl.BlockSpec((D_MODEL, D_MODEL), fixed), row(D_MODEL), row(D_MODEL),
                             pl.BlockSpec((N_EXPERTS, D_MODEL), fixed),
                             pl.BlockSpec((N_EXPERTS, D_MODEL), fixed),
                             pl.BlockSpec((N_EXPERTS, LANES), fixed),
                             pl.BlockSpec((tm, tm), fixed), pl.BlockSpec((tm, tm), fixed)],
        out_specs=[pl.BlockSpec((tm, D_MODEL), tok),
                   pl.BlockSpec((tm, CHUNKS, LANES), lambda i: (i, 0, 0)),
                   kt, kt, kt, pl.BlockSpec((N_EXPERTS, tm), fixed)],
        out_shape=[jax.ShapeDtypeStruct((T, D_MODEL), f32),
                   jax.ShapeDtypeStruct((T, CHUNKS, LANES), f32),
                   jax.ShapeDtypeStruct((TOP_K, T), i32),
                   jax.ShapeDtypeStruct((TOP_K, T), f32),
                   jax.ShapeDtypeStruct((TOP_K, T), i32),
                   jax.ShapeDtypeStruct((N_EXPERTS, tm), f32)],
        scratch_shapes=[pltpu.VMEM((N_EXPERTS, tm), f32),
                        pltpu.VMEM((CHUNKS * (tm + PITCH), LANES), f32)],
        compiler_params=_cparams(("arbitrary",)),
        name="mixout",
    )(o[0], o[1], o[2], lse[0], lse[1], lse[2], rec, x2, attn_gain, rec_gain, w_out_b, g1, b1,
      rw_hi, rw_lo, rbias, tri, ones)


def _dispatch_body(dest_ref, x_ref, xs_zero_ref, xs_ref, sem):
    del xs_zero_ref
    tm = TM_MIX

    def copy(t, k):
        return pltpu.make_async_copy(x_ref.at[t], xs_ref.at[dest_ref[k, t]], sem)

    def issue(t, carry):
        for k in range(TOP_K):
            copy(t, k).start()
        return carry

    def drain(t, carry):
        for k in range(TOP_K):
            copy(t, k).wait()
        return carry

    lax.fori_loop(0, tm, issue, 0)
    lax.fori_loop(0, tm, drain, 0)


def _dispatch(dest, x1s, xs_init):
    T = x1s.shape[0]
    tm = TM_MIX
    return pl.pallas_call(
        _dispatch_body,
        grid=(T // tm,),
        in_specs=[pl.BlockSpec((TOP_K, tm), lambda i: (0, i), memory_space=pltpu.SMEM),
                  pl.BlockSpec((tm, CHUNKS, LANES), lambda i: (i, 0, 0)),
                  pl.BlockSpec(memory_space=pl.ANY)],
        out_specs=pl.BlockSpec(memory_space=pl.ANY),
        out_shape=jax.ShapeDtypeStruct(xs_init.shape, f32),
        scratch_shapes=[pltpu.SemaphoreType.DMA(())],
        input_output_aliases={2: 0},
        compiler_params=_cparams(("arbitrary",)),
        name="dispatch",
    )(dest, x1s, xs_init)


def _experts_body(be_ref, nb_ref, xs_ref, wg_ref, wu_ref, wd_ref, ys_ref, buf_ref):
    del be_ref

    @pl.when(pl.program_id(0) < nb_ref[0])
    def _():
        for t in range(BM):
            buf_ref[_slab_ds(t, BM), :] = xs_ref[t]
        x = _buf_to_rows(buf_ref, BM).astype(bf16)
        g = jnp.dot(x, wg_ref[...], preferred_element_type=f32)
        u = jnp.dot(x, wu_ref[...], preferred_element_type=f32)
        h = (g * jax.nn.sigmoid(g) * u).astype(bf16)
        y = jnp.dot(h, wd_ref[...], preferred_element_type=f32)
        _rows_to_buf(buf_ref, y, BM)
        for t in range(BM):
            ys_ref[t] = buf_ref[_slab_ds(t, BM), :]


def _experts(block_e, nb_used, xs, wg, wu, wd):
    P = xs.shape[0]
    nb = P // BM
    rows = lambda i, be, nbu: (jnp.minimum(i, nbu[0] - 1), 0, 0)
    wsel = lambda i, be, nbu: (be[i], 0, 0)
    gs = pltpu.PrefetchScalarGridSpec(
        num_scalar_prefetch=2,
        grid=(nb,),
        in_specs=[pl.BlockSpec((BM, CHUNKS, LANES), rows),
                  pl.BlockSpec((None, D_MODEL, EXPERT_H), wsel),
                  pl.BlockSpec((None, D_MODEL, EXPERT_H), wsel),
                  pl.BlockSpec((None, EXPERT_H, D_MODEL), wsel)],
        out_specs=pl.BlockSpec((BM, CHUNKS, LANES), rows),
        scratch_shapes=[pltpu.VMEM((CHUNKS * (BM + PITCH), LANES), f32)],
    )
    return pl.pallas_call(
        _experts_body,
        grid_spec=gs,
        out_shape=jax.ShapeDtypeStruct((P, CHUNKS, LANES), f32),
        compiler_params=_cparams(("arbitrary",)),
        name="experts",
    )(block_e, nb_used, xs, wg, wu, wd)


def _combine_body(dest_ref, gate_ref, x1_ref, ys_ref, sg_ref, su_ref, sd_ref, g2_ref, b2_ref,
                  out_ref, gbuf_ref, buf_ref, sem, *, alpha):
    tm = TM_CMB

    def copy(t, k):
        return pltpu.make_async_copy(ys_ref.at[dest_ref[k, t]], gbuf_ref.at[k * tm + t], sem)

    def issue(t, carry):
        for k in range(TOP_K):
            copy(t, k).start()
        return carry

    lax.fori_loop(0, tm, issue, 0)

    x1 = x1_ref[...]
    xb = x1.astype(bf16)
    sg = jnp.dot(xb, sg_ref[...], preferred_element_type=f32)
    su = jnp.dot(xb, su_ref[...], preferred_element_type=f32)
    shared = jnp.dot((sg * jax.nn.sigmoid(sg) * su).astype(bf16), sd_ref[...],
                     preferred_element_type=f32)

    def drain(t, carry):
        for k in range(TOP_K):
            copy(t, k).wait()
        return carry

    lax.fori_loop(0, tm, drain, 0)

    def mix(t, carry):
        acc = gate_ref[0, t] * gbuf_ref[t]
        for k in range(1, TOP_K):
            acc = acc + gate_ref[k, t] * gbuf_ref[k * tm + t]
        buf_ref[_slab_ds(t, tm), :] = acc
        return carry

    lax.fori_loop(0, tm, mix, 0)
    routed = _buf_to_rows(buf_ref, tm)
    out_ref[...] = _layer_norm(alpha * x1 + (routed + shared), g2_ref[...], b2_ref[...])


def _combine(dest, gate, x1, ys, sg, su, sd, g2, b2, alpha):
    T = x1.shape[0]
    tm = TM_CMB
    tok = lambda i: (i, 0)
    fixed = lambda i: (0, 0)
    kt = pl.BlockSpec((TOP_K, tm), lambda i: (0, i), memory_space=pltpu.SMEM)
    return pl.pallas_call(
        functools.partial(_combine_body, alpha=alpha),
        grid=(T // tm,),
        in_specs=[kt, kt, pl.BlockSpec((tm, D_MODEL), tok), pl.BlockSpec(memory_space=pl.ANY),
                  pl.BlockSpec(sg.shape, fixed), pl.BlockSpec(su.shape, fixed),
                  pl.BlockSpec(sd.shape, fixed), pl.BlockSpec((1, D_MODEL), fixed),
                  pl.BlockSpec((1, D_MODEL), fixed)],
        out_specs=pl.BlockSpec((tm, D_MODEL), tok),
        out_shape=jax.ShapeDtypeStruct((T, D_MODEL), f32),
        scratch_shapes=[pltpu.VMEM((TOP_K * tm, CHUNKS, LANES), f32),
                        pltpu.VMEM((CHUNKS * (tm + PITCH), LANES), f32),
                        pltpu.SemaphoreType.DMA(())],
        compiler_params=_cparams(("arbitrary",)),
        name="combine",
    )(dest, gate, x1, ys, sg, su, sd, g2, b2)


def _rope_inv_freq():
    half = ROT_DIM // 2
    inv = ROPE_THETA ** (-jnp.arange(half, dtype=f32) * 2.0 / ROT_DIM)
    j = np.arange(LANES) % HEAD_DIM
    table = jnp.where(j < ROT_DIM, inv[j % half], 0.0)
    return table.reshape(1, LANES).astype(f32)


def _gate_weights(ga_w, ga_b, gx_w, gx_b):
    ng = REC_W // LANES
    per = LANES // HEAD_DIM
    def bd(w):
        w = w.reshape(ng, per, HEAD_DIM, HEAD_DIM)
        z = jnp.zeros((ng, LANES, LANES), w.dtype)
        for p in range(per):
            z = z.at[:, p * HEAD_DIM:(p + 1) * HEAD_DIM, p * HEAD_DIM:(p + 1) * HEAD_DIM].set(w[:, p])
        return z
    wg = jnp.concatenate([bd(ga_w[0]), bd(gx_w[0]), bd(ga_w[1]), bd(gx_w[1])], axis=-1).astype(bf16)
    grp = lambda b: b.reshape(ng, 1, LANES)
    gb = jnp.concatenate([grp(ga_b[0]), grp(gx_b[0]), grp(ga_b[1]), grp(gx_b[1])], axis=-1)
    return wg, gb


def _layer(x, positions, w_in, attn_gain, conv_w, conv_b, ga_w, ga_b, gx_w, gx_b, lam, rec_gain,
           w_out, ln1_g, ln1_b, router_w, router_bias, e_wg, e_wu, e_wd, s_wg, s_wu, s_wd,
           ln2_g, ln2_b, alpha):
    B, S, _ = x.shape
    T = B * S
    x2 = x.reshape(T, D_MODEL)
    pos2 = positions.reshape(T, 1)

    q, k, v, rx, rg = _inproj(x2, pos2, w_in.astype(bf16), _rope_inv_freq())
    branch = [_attn_branch(q, k, v, B, S, dil) for _, dil in BRANCHES]
    wg, gb = _gate_weights(ga_w, ga_b, gx_w, gx_b)
    rec = _rec(rx, rg, conv_w, conv_b.reshape(1, REC_W), wg, gb, lam, B, S).reshape(T, REC_W)

    rw_t = router_w.T
    rw_hi = rw_t.astype(bf16)
    rw_lo = (rw_t - rw_hi.astype(f32)).astype(bf16)
    rbias = jnp.broadcast_to(router_bias.reshape(N_EXPERTS, 1), (N_EXPERTS, LANES))
    x1, x1s, idx, gate, rank, cnt = _mixout(
        [b[0] for b in branch], [b[1] for b in branch], rec, x2,
        attn_gain.reshape(1, ATTN_W), rec_gain.reshape(1, REC_W), w_out.astype(bf16),
        ln1_g.reshape(1, D_MODEL), ln1_b.reshape(1, D_MODEL), rw_hi, rw_lo, rbias, alpha)

    counts = cnt[:, 0].astype(i32)
    padded = (counts + BM - 1) // BM * BM
    pad_end = jnp.cumsum(padded)
    pad_start = pad_end - padded
    dest = pad_start[idx] + rank
    nb = (T * TOP_K + N_EXPERTS * (BM - 1)) // BM + 1
    block_e = jnp.minimum(
        jnp.searchsorted(pad_end, jnp.arange(nb, dtype=i32) * BM, side="right"), N_EXPERTS - 1
    ).astype(i32)
    nb_used = (pad_end[-1] // BM).astype(i32).reshape(1)

    xs = _dispatch(dest, x1s, jnp.zeros((nb * BM, CHUNKS, LANES), f32))
    ys = _experts(block_e, nb_used, xs, e_wg.astype(bf16), e_wu.astype(bf16), e_wd.astype(bf16))
    out = _combine(dest, gate, x1, ys, s_wg.astype(bf16), s_wu.astype(bf16), s_wd.astype(bf16),
                   ln2_g.reshape(1, D_MODEL), ln2_b.reshape(1, D_MODEL), alpha)
    return out.reshape(B, S, D_MODEL)


def kernel(x, positions, w_in, attn_gain, rec_conv_w, rec_conv_b, rec_gate_a_w, rec_gate_a_b,
           rec_gate_x_w, rec_gate_x_b, rec_lambda, rec_gain, w_out, ln1_g, ln1_b, router_w,
           router_bias, exp_w_gate, exp_w_up, exp_w_down, shared_w_gate, shared_w_up,
           shared_w_down, ln2_g, ln2_b):
    depth = w_in.shape[0]
    alpha = (2 * depth) ** 0.25
    for l in range(depth):
        x = _layer(x, positions, w_in[l], attn_gain[l], rec_conv_w[l], rec_conv_b[l],
                   rec_gate_a_w[l], rec_gate_a_b[l], rec_gate_x_w[l], rec_gate_x_b[l],
                   rec_lambda[l], rec_gain[l], w_out[l], ln1_g[l], ln1_b[l], router_w[l],
                   router_bias[l], exp_w_gate[l], exp_w_up[l], exp_w_down[l], shared_w_gate[l],
                   shared_w_up[l], shared_w_down[l], ln2_g[l], ln2_b[l], alpha)
    return x
```

```python
import functools
import math

import jax
import jax.numpy as jnp
import numpy as np
from jax import lax
from jax.experimental import pallas as pl
from jax.experimental.pallas import tpu as pltpu

f32 = jnp.float32
bf16 = jnp.bfloat16
i32 = jnp.int32

D_MODEL = 1024
ATTN_W = 512
REC_W = 512
HEAD_DIM = 64
ROT_DIM = 16
ROPE_THETA = 500000.0
BRANCHES = ((128, 1), (512, 4), (2048, 16))
HALF_BAND = 64
CONV_W = 4
RG_LRU_C = 8.0
N_EXPERTS = 256
TOP_K = 8
N_GROUPS = 8
GROUP_SZ = N_EXPERTS // N_GROUPS
TOPK_GROUPS = 4
EXPERT_H = 256
ROUTED_SCALE = 2.5
LN_EPS = 1e-5
NEG = -1e30

LANES = 128
SUBLANES = 8
CHUNKS = D_MODEL // LANES
VMEM_LIMIT = 56 * 1024 * 1024

TM_IN = 512
TQ = 128
KW = TQ + 2 * HALF_BAND
TS = 256
TM_MIX = 256
TM_SLOT = 512
TM_CMB = 128
BM = 256
PITCH = 8


def _cparams(sem):
    return pltpu.CompilerParams(dimension_semantics=sem, vmem_limit_bytes=VMEM_LIMIT)


def _inproj_body(x_ref, pos_ref, w_ref, invf_ref, q_ref, k_ref, v_ref, rx_ref, rg_ref):
    xb = x_ref[...].astype(bf16)
    ang = pos_ref[...].astype(f32) * invf_ref[...]
    cos = jnp.cos(ang)
    sin = jnp.sin(ang)
    j = lax.broadcasted_iota(i32, (1, LANES), 1) % HEAD_DIM
    half = ROT_DIM // 2
    s_lo = jnp.where(j < half, -sin, 0.0)
    s_hi = jnp.where((j >= half) & (j < ROT_DIM), sin, 0.0)
    rep = ATTN_W // LANES
    cos_w = jnp.concatenate([cos] * rep, axis=1)
    s_lo_w = jnp.concatenate([s_lo] * rep, axis=1)
    s_hi_w = jnp.concatenate([s_hi] * rep, axis=1)

    def proj(c0, n):
        return jnp.dot(xb, w_ref[:, c0:c0 + n], preferred_element_type=f32)

    def rope(t):
        return (t * cos_w + pltpu.roll(t, ATTN_W - half, 1) * s_lo_w
                + pltpu.roll(t, half, 1) * s_hi_w)

    q_ref[...] = (rope(proj(0, ATTN_W)) * (HEAD_DIM ** -0.5)).astype(bf16)
    k_ref[...] = rope(proj(ATTN_W, ATTN_W)).astype(bf16)
    v_ref[...] = proj(2 * ATTN_W, ATTN_W).astype(bf16)
    rx_ref[...] = proj(3 * ATTN_W, REC_W)
    rg_ref[...] = proj(3 * ATTN_W + REC_W, REC_W)


def _inproj(x2, pos2, w_in_b, invf):
    T = x2.shape[0]
    tm = TM_IN
    in_w = w_in_b.shape[1]
    tok = lambda i: (i, 0)
    fixed = lambda i: (0, 0)
    return pl.pallas_call(
        _inproj_body,
        grid=(T // tm,),
        in_specs=[pl.BlockSpec((tm, D_MODEL), tok), pl.BlockSpec((tm, 1), tok),
                  pl.BlockSpec((D_MODEL, in_w), fixed), pl.BlockSpec((1, LANES), fixed)],
        out_specs=[pl.BlockSpec((tm, ATTN_W), tok)] * 3 + [pl.BlockSpec((tm, REC_W), tok)] * 2,
        out_shape=[jax.ShapeDtypeStruct((T, ATTN_W), bf16)] * 3
        + [jax.ShapeDtypeStruct((T, REC_W), f32)] * 2,
        compiler_params=_cparams(("parallel",)),
        name="inproj",
    )(x2, pos2, w_in_b, invf)


def _attn_body(q_ref, k_ref, v_ref, o_ref, lse_ref, *, L):
    lane = lax.broadcasted_iota(i32, (1, LANES), 1)
    head0 = lane < HEAD_DIM
    rel = (lax.broadcasted_iota(i32, (TQ, KW), 0) - lax.broadcasted_iota(i32, (TQ, KW), 1))

    def tile(i, carry):
        q0 = pl.multiple_of(i * TQ, TQ)
        start = pl.multiple_of(jnp.clip(q0 - HALF_BAND, 0, L - KW), HALF_BAND)
        q = q_ref[pl.ds(q0, TQ), :]
        k = k_ref[pl.ds(start, KW), :]
        v = v_ref[pl.ds(start, KW), :]
        valid = jnp.abs(rel + (q0 - start)) <= HALF_BAND
        outs = []
        for sel in (head0, jnp.logical_not(head0)):
            qh = jnp.where(sel, q, jnp.zeros_like(q))
            s = lax.dot_general(qh, k, (((1,), (1,)), ((), ())), preferred_element_type=f32)
            s = jnp.where(valid, s, NEG)
            m = jnp.max(s, axis=-1, keepdims=True)
            p = jnp.exp(s - m)
            den = jnp.sum(p, axis=-1, keepdims=True)
            o = jnp.dot(p.astype(bf16), v, preferred_element_type=f32) / den
            outs.append((o, m + jnp.log(den)))
        o_ref[pl.ds(q0, TQ), :] = jnp.where(head0, outs[0][0], outs[1][0]).astype(bf16)
        lse_ref[pl.ds(q0, TQ), :] = jnp.where(head0, outs[0][1], outs[1][1])
        return carry

    lax.fori_loop(0, L // TQ, tile, 0)


def _attn_branch(q, k, v, B, S, dil):
    L = S // dil
    assert L % TQ == 0 and L >= KW
    hp = ATTN_W // LANES
    view = lambda t: t.reshape(B, L, dil * ATTN_W)
    spec = pl.BlockSpec((None, L, LANES), lambda b, r, h: (b, 0, r * hp + h))
    o, lse = pl.pallas_call(
        functools.partial(_attn_body, L=L),
        grid=(B, dil, hp),
        in_specs=[spec, spec, spec],
        out_specs=[spec, spec],
        out_shape=[jax.ShapeDtypeStruct((B, L, dil * ATTN_W), bf16),
                   jax.ShapeDtypeStruct((B, L, dil * ATTN_W), f32)],
        compiler_params=_cparams(("parallel", "parallel", "parallel")),
        name=f"attn_d{dil}",
    )(view(q), view(k), view(v))
    return o.reshape(B * S, ATTN_W), lse.reshape(B * S, ATTN_W)


def _rec_body(rx_ref, rg_ref, cw_ref, cb_ref, wg_ref, gb_ref, lam_ref, out_ref, rxp_ref, hf_ref, *, S):
    nch = S // TS
    ntile = TS // SUBLANES
    zeros8 = jnp.zeros((SUBLANES, LANES), f32)
    rxp_ref[pl.ds(0, SUBLANES), :] = zeros8
    rxp_ref[pl.ds(S + SUBLANES, SUBLANES), :] = zeros8

    def pad_copy(c, carry):
        t0 = pl.multiple_of(c * TS, TS)
        rxp_ref[pl.ds(t0 + SUBLANES, TS), :] = rx_ref[pl.ds(t0, TS), :]
        return carry

    lax.fori_loop(0, nch, pad_copy, 0)

    lam = lam_ref[...]
    neg_sp = -RG_LRU_C * (jnp.maximum(-lam, 0.0) + jnp.log1p(jnp.exp(-jnp.abs(lam))))
    cw = cw_ref[...]
    cb = cb_ref[...]
    sub = lax.broadcasted_iota(i32, (ntile, SUBLANES, LANES), 1)
    nrow = TS + 2 * SUBLANES

    def gates(ci, d):
        t0 = pl.multiple_of(ci * TS, TS)
        xw = rxp_ref[pl.ds(t0, nrow), :]
        u = (cw[0:1] * pltpu.roll(xw, 2, 0) + cw[1:2] * pltpu.roll(xw, 1, 0) + cw[2:3] * xw
             + cw[3:4] * pltpu.roll(xw, nrow - 1, 0))[SUBLANES:SUBLANES + TS] + cb
        c0 = d * 2 * LANES
        g = jnp.dot(u.astype(bf16), wg_ref[:, c0:c0 + 2 * LANES], preferred_element_type=f32)
        g = g + gb_ref[:, c0:c0 + 2 * LANES]
        r = jax.nn.sigmoid(g[:, :LANES])
        gi = jax.nn.sigmoid(g[:, LANES:])
        a = jnp.exp(neg_sp[d:d + 1] * r)
        b = jnp.sqrt(1.0 - a * a) * gi * u
        return a.reshape(ntile, SUBLANES, LANES), b.reshape(ntile, SUBLANES, LANES)

    def chunk(i, carry):
        cf, cbk = carry
        a, b = gates(i, 0)
        for s in (1, 2, 4):
            ok = sub >= s
            a_s = pltpu.roll(a, s, 1)
            b_s = pltpu.roll(b, s, 1)
            b = jnp.where(ok, a * b_s + b, b)
            a = jnp.where(ok, a * a_s, a)
        t0 = pl.multiple_of(i * TS, TS)
        for j in range(ntile):
            h = a[j] * cf + b[j]
            hf_ref[pl.ds(t0 + j * SUBLANES, SUBLANES), :] = h
            cf = h[SUBLANES - 1:SUBLANES, :]
        ib = nch - 1 - i
        a, b = gates(ib, 1)
        for s in (1, 2, 4):
            ok = sub < SUBLANES - s
            a_s = pltpu.roll(a, SUBLANES - s, 1)
            b_s = pltpu.roll(b, SUBLANES - s, 1)
            b = jnp.where(ok, a * b_s + b, b)
            a = jnp.where(ok, a * a_s, a)
        t0 = pl.multiple_of(ib * TS, TS)
        for j in range(ntile - 1, -1, -1):
            h = a[j] * cbk + b[j]
            out_ref[pl.ds(t0 + j * SUBLANES, SUBLANES), :] = h
            cbk = h[0:1, :]
        return cf, cbk

    zrow = jnp.zeros((1, LANES), f32)
    lax.fori_loop(0, nch, chunk, (zrow, zrow))

    def finish(c, carry):
        t0 = pl.multiple_of(c * TS, TS)
        sl = pl.ds(t0, TS)
        out_ref[sl, :] = (hf_ref[sl, :] + out_ref[sl, :]) * jax.nn.gelu(rg_ref[sl, :])
        return carry

    lax.fori_loop(0, nch, finish, 0)


def _rec(rx, rg, conv_w, conv_b, wg, gb, lam, B, S):
    ng = REC_W // LANES
    assert S % TS == 0
    seq = pl.BlockSpec((None, S, LANES), lambda b, c: (b, 0, c))
    return pl.pallas_call(
        functools.partial(_rec_body, S=S),
        grid=(B, ng),
        in_specs=[seq, seq,
                  pl.BlockSpec((CONV_W, LANES), lambda b, c: (0, c)),
                  pl.BlockSpec((1, LANES), lambda b, c: (0, c)),
                  pl.BlockSpec((None, LANES, 4 * LANES), lambda b, c: (c, 0, 0)),
                  pl.BlockSpec((None, 1, 4 * LANES), lambda b, c: (c, 0, 0)),
                  pl.BlockSpec((2, LANES), lambda b, c: (0, c))],
        out_specs=seq,
        out_shape=jax.ShapeDtypeStruct((B, S, REC_W), f32),
        scratch_shapes=[pltpu.VMEM((S + 2 * SUBLANES, LANES), f32), pltpu.VMEM((S, LANES), f32)],
        compiler_params=_cparams(("parallel", "parallel")),
        name="rec",
    )(rx.reshape(B, S, REC_W), rg.reshape(B, S, REC_W), conv_w, conv_b, wg, gb, lam)


def _rows_to_buf(buf, val, rows):
    sp = rows + PITCH
    for j in range(CHUNKS):
        buf[pl.ds(j * sp, rows), :] = val[:, j * LANES:(j + 1) * LANES]


def _buf_to_rows(buf, rows):
    sp = rows + PITCH
    return jnp.concatenate([buf[pl.ds(j * sp, rows), :] for j in range(CHUNKS)], axis=1)


def _slab_ds(t, rows):
    return pl.ds(t, CHUNKS, stride=rows + PITCH)


def _rms(t, gain):
    return t * lax.rsqrt(jnp.mean(t * t, axis=-1, keepdims=True) + LN_EPS) * gain


def _layer_norm(z, g, b):
    mu = jnp.mean(z, axis=-1, keepdims=True)
    zc = z - mu
    var = jnp.mean(zc * zc, axis=-1, keepdims=True)
    return zc * lax.rsqrt(var + LN_EPS) * g + b


def _mixout_body(o1_ref, o2_ref, o3_ref, l1_ref, l2_ref, l3_ref, rec_ref, x_ref,
                 ag_ref, rgn_ref, wo_ref, g1_ref, b1_ref, rwh_ref, rwl_ref, rb_ref, tri_ref, ones_ref,
                 x1_ref, x1s_ref, idx_ref, gate_ref, rank_ref, cnt_ref, carry_ref, buf_ref, *, alpha):
    tm = TM_MIX

    @pl.when(pl.program_id(0) == 0)
    def _():
        carry_ref[...] = jnp.zeros_like(carry_ref)

    l1, l2, l3 = l1_ref[...], l2_ref[...], l3_ref[...]
    m = jnp.maximum(jnp.maximum(l1, l2), l3)
    e1, e2, e3 = jnp.exp(l1 - m), jnp.exp(l2 - m), jnp.exp(l3 - m)
    attn = (e1 * o1_ref[...].astype(f32) + e2 * o2_ref[...].astype(f32)
            + e3 * o3_ref[...].astype(f32)) / (e1 + e2 + e3)
    attn_n = _rms(attn, ag_ref[...]).astype(bf16)
    rec_n = _rms(rec_ref[...], rgn_ref[...]).astype(bf16)
    y = (jnp.dot(attn_n, wo_ref[0:ATTN_W, :], preferred_element_type=f32)
         + jnp.dot(rec_n, wo_ref[ATTN_W:, :], preferred_element_type=f32))
    x1 = _layer_norm(alpha * x_ref[...] + y, g1_ref[...], b1_ref[...])
    x1_ref[...] = x1
    _rows_to_buf(buf_ref, x1, tm)
    for t in range(tm):
        x1s_ref[t] = buf_ref[_slab_ds(t, tm), :]

    hi = x1.astype(bf16)
    lo = (x1 - hi.astype(f32)).astype(bf16)
    nt = (((1,), (1,)), ((), ()))
    logits = (lax.dot_general(rwh_ref[...], hi, nt, preferred_element_type=f32)
              + lax.dot_general(rwh_ref[...], lo, nt, preferred_element_type=f32)
              + lax.dot_general(rwl_ref[...], hi, nt, preferred_element_type=f32))
    scores = jax.nn.sigmoid(logits)
    biased = scores + jnp.concatenate([rb_ref[...]] * (tm // LANES), axis=1)

    rid = lax.broadcasted_iota(i32, (GROUP_SZ, tm), 0).astype(f32)
    grp = []
    for g in range(N_GROUPS):
        vg = biased[g * GROUP_SZ:(g + 1) * GROUP_SZ, :]
        m1 = jnp.max(vg, axis=0, keepdims=True)
        first = jnp.min(jnp.where(vg == m1, rid, float(GROUP_SZ)), axis=0, keepdims=True)
        m2 = jnp.max(jnp.where(rid == first, -jnp.inf, vg), axis=0, keepdims=True)
        grp.append(m1 + m2)
    eid = lax.broadcasted_iota(i32, (N_EXPERTS, tm), 0).astype(f32)
    egid = jnp.floor(eid * (1.0 / GROUP_SZ))
    masked = jnp.full((N_EXPERTS, tm), -jnp.inf, f32)
    for _ in range(TOPK_GROUPS):
        gm = functools.reduce(jnp.maximum, grp)
        gi = jnp.full((1, tm), float(N_GROUPS), f32)
        for g in range(N_GROUPS - 1, -1, -1):
            gi = jnp.where(grp[g] == gm, float(g), gi)
        grp = [jnp.where(gi == float(g), -jnp.inf, grp[g]) for g in range(N_GROUPS)]
        masked = jnp.where(egid == gi, biased, masked)

    onehot = jnp.zeros((N_EXPERTS, tm), f32)
    idxs, gts = [], []
    for _ in range(TOP_K):
        mx = jnp.max(masked, axis=0, keepdims=True)
        ix = jnp.min(jnp.where(masked == mx, eid, float(N_EXPERTS)), axis=0, keepdims=True)
        hit = eid == ix
        gts.append(jnp.sum(jnp.where(hit, scores, 0.0), axis=0, keepdims=True))
        idxs.append(ix)
        masked = jnp.where(hit, -jnp.inf, masked)
        onehot = onehot + jnp.where(hit, 1.0, 0.0)
    gsum = functools.reduce(lambda p, q: p + q, gts)
    for kk in range(TOP_K):
        idx_ref[kk:kk + 1, :] = idxs[kk].astype(i32)
        gate_ref[kk:kk + 1, :] = gts[kk] / gsum * ROUTED_SCALE

    oh = onehot.astype(bf16)
    before = carry_ref[...] + jnp.dot(oh, tri_ref[...], preferred_element_type=f32)
    for kk in range(TOP_K):
        rk = jnp.sum(jnp.where(eid == idxs[kk], before, 0.0), axis=0, keepdims=True)
        rank_ref[kk:kk + 1, :] = rk.astype(i32)
    total = carry_ref[...] + jnp.dot(oh, ones_ref[...], preferred_element_type=f32)
    carry_ref[...] = total
    cnt_ref[...] = total


def _mixout(o, lse, rec, x2, attn_gain, rec_gain, w_out_b, g1, b1, rw_hi, rw_lo, rbias, alpha):
    T = x2.shape[0]
    tm = TM_MIX
    tok = lambda i: (i, 0)
    fixed = lambda i: (0, 0)
    tri = jnp.asarray(np.triu(np.ones((tm, tm), np.float32), k=1), bf16)
    ones = jnp.ones((tm, tm), bf16)
    aw = pl.BlockSpec((tm, ATTN_W), tok)
    row = lambda n: pl.BlockSpec((1, n), fixed)
    kt = pl.BlockSpec((TOP_K, tm), lambda i: (0, i))
    return pl.pallas_call(
        functools.partial(_mixout_body, alpha=alpha),
        grid=(T // tm,),
        in_specs=[aw] * 7 + [pl.BlockSpec((tm, D_MODEL), tok), row(ATTN_W), row(REC_W),
                             pl.BlockSpec((D_MODEL, D_MODEL), fixed), row(D_MODEL), row(D_MODEL),
                             pl.BlockSpec((N_EXPERTS, D_MODEL), fixed),
                             pl.BlockSpec((N_EXPERTS, D_MODEL), fixed),
                             pl.BlockSpec((N_EXPERTS, LANES), fixed),
                             pl.BlockSpec((tm, tm), fixed), pl.BlockSpec((tm, tm), fixed)],
        out_specs=[pl.BlockSpec((tm, D_MODEL), tok),
                   pl.BlockSpec((tm, CHUNKS, LANES), lambda i: (i, 0, 0)),
                   kt, kt, kt, pl.BlockSpec((N_EXPERTS, tm), fixed)],
        out_shape=[jax.ShapeDtypeStruct((T, D_MODEL), f32),
                   jax.ShapeDtypeStruct((T, CHUNKS, LANES), f32),
                   jax.ShapeDtypeStruct((TOP_K, T), i32),
                   jax.ShapeDtypeStruct((TOP_K, T), f32),
                   jax.ShapeDtypeStruct((TOP_K, T), i32),
                   jax.ShapeDtypeStruct((N_EXPERTS, tm), f32)],
        scratch_shapes=[pltpu.VMEM((N_EXPERTS, tm), f32),
                        pltpu.VMEM((CHUNKS * (tm + PITCH), LANES), f32)],
        compiler_params=_cparams(("arbitrary",)),
        name="mixout",
    )(o[0], o[1], o[2], lse[0], lse[1], lse[2], rec, x2, attn_gain, rec_gain, w_out_b, g1, b1,
      rw_hi, rw_lo, rbias, tri, ones)


def _slots_body(idx_ref, rank_ref, start_ref, dest_ref):
    tm = TM_SLOT
    eid = lax.broadcasted_iota(i32, (N_EXPERTS, tm), 0)
    start = jnp.concatenate([start_ref[...]] * (tm // LANES), axis=1)
    for kk in range(TOP_K):
        hit = eid == idx_ref[kk:kk + 1, :]
        base = jnp.sum(jnp.where(hit, start, 0.0), axis=0, keepdims=True)
        dest_ref[kk:kk + 1, :] = base.astype(i32) + rank_ref[kk:kk + 1, :]


def _slots(idx, rank, pad_start):
    T = idx.shape[1]
    tm = TM_SLOT
    kt = pl.BlockSpec((TOP_K, tm), lambda i: (0, i))
    start = jnp.broadcast_to(pad_start.astype(f32).reshape(N_EXPERTS, 1), (N_EXPERTS, LANES))
    return pl.pallas_call(
        _slots_body,
        grid=(T // tm,),
        in_specs=[kt, kt, pl.BlockSpec((N_EXPERTS, LANES), lambda i: (0, 0))],
        out_specs=kt,
        out_shape=jax.ShapeDtypeStruct((TOP_K, T), i32),
        compiler_params=_cparams(("parallel",)),
        name="slots",
    )(idx, rank, start)


def _dispatch_body(dest_ref, x_ref, xs_ref, sem):
    tm = TM_MIX

    def copy(t, k):
        return pltpu.make_async_copy(x_ref.at[t], xs_ref.at[dest_ref[k, t]], sem)

    def issue(t, carry):
        for k in range(TOP_K):
            copy(t, k).start()
        return carry

    def drain(t, carry):
        for k in range(TOP_K):
            copy(t, k).wait()
        return carry

    lax.fori_loop(0, tm, issue, 0)
    lax.fori_loop(0, tm, drain, 0)


def _dispatch(dest, x1s, n_slots):
    T = x1s.shape[0]
    tm = TM_MIX
    return pl.pallas_call(
        _dispatch_body,
        grid=(T // tm,),
        in_specs=[pl.BlockSpec((TOP_K, tm), lambda i: (0, i), memory_space=pltpu.SMEM),
                  pl.BlockSpec((tm, CHUNKS, LANES), lambda i: (i, 0, 0))],
        out_specs=pl.BlockSpec(memory_space=pl.ANY),
        out_shape=jax.ShapeDtypeStruct((n_slots, CHUNKS, LANES), f32),
        scratch_shapes=[pltpu.SemaphoreType.DMA(())],
        compiler_params=_cparams(("arbitrary",)),
        name="dispatch",
    )(dest, x1s)


def _experts_body(be_ref, nv_ref, nb_ref, xs_ref, wg_ref, wu_ref, wd_ref, ys_ref, buf_ref):
    del be_ref
    i = pl.program_id(0)

    @pl.when(i < nb_ref[0])
    def _():
        for t in range(BM):
            buf_ref[_slab_ds(t, BM), :] = xs_ref[t]
        live = lax.broadcasted_iota(i32, (BM, LANES), 0) < nv_ref[i]
        sp = BM + PITCH
        x = jnp.concatenate(
            [jnp.where(live, buf_ref[pl.ds(j * sp, BM), :], 0.0) for j in range(CHUNKS)],
            axis=1).astype(bf16)
        g = jnp.dot(x, wg_ref[...], preferred_element_type=f32)
        u = jnp.dot(x, wu_ref[...], preferred_element_type=f32)
        h = (g * jax.nn.sigmoid(g) * u).astype(bf16)
        y = jnp.dot(h, wd_ref[...], preferred_element_type=f32)
        _rows_to_buf(buf_ref, y, BM)
        for t in range(BM):
            ys_ref[t] = buf_ref[_slab_ds(t, BM), :]


def _experts(block_e, block_rows, nb_used, xs, wg, wu, wd):
    P = xs.shape[0]
    nb = P // BM
    rows = lambda i, be, nv, nbu: (jnp.minimum(i, nbu[0] - 1), 0, 0)
    wsel = lambda i, be, nv, nbu: (be[i], 0, 0)
    gs = pltpu.PrefetchScalarGridSpec(
        num_scalar_prefetch=3,
        grid=(nb,),
        in_specs=[pl.BlockSpec((BM, CHUNKS, LANES), rows),
                  pl.BlockSpec((None, D_MODEL, EXPERT_H), wsel),
                  pl.BlockSpec((None, D_MODEL, EXPERT_H), wsel),
                  pl.BlockSpec((None, EXPERT_H, D_MODEL), wsel)],
        out_specs=pl.BlockSpec((BM, CHUNKS, LANES), rows),
        scratch_shapes=[pltpu.VMEM((CHUNKS * (BM + PITCH), LANES), f32)],
    )
    return pl.pallas_call(
        _experts_body,
        grid_spec=gs,
        out_shape=jax.ShapeDtypeStruct((P, CHUNKS, LANES), f32),
        compiler_params=_cparams(("arbitrary",)),
        name="experts",
    )(block_e, block_rows, nb_used, xs, wg, wu, wd)


def _combine_body(dest_ref, gate_ref, x1_ref, ys_ref, sg_ref, su_ref, sd_ref, g2_ref, b2_ref,
                  out_ref, gbuf_ref, buf_ref, sem, *, alpha):
    tm = TM_CMB

    def copy(t, k):
        return pltpu.make_async_copy(ys_ref.at[dest_ref[k, t]], gbuf_ref.at[k * tm + t], sem)

    def issue(t, carry):
        for k in range(TOP_K):
            copy(t, k).start()
        return carry

    lax.fori_loop(0, tm, issue, 0)

    x1 = x1_ref[...]
    xb = x1.astype(bf16)
    sg = jnp.dot(xb, sg_ref[...], preferred_element_type=f32)
    su = jnp.dot(xb, su_ref[...], preferred_element_type=f32)
    shared = jnp.dot((sg * jax.nn.sigmoid(sg) * su).astype(bf16), sd_ref[...],
                     preferred_element_type=f32)

    def drain(t, carry):
        for k in range(TOP_K):
            copy(t, k).wait()
        return carry

    lax.fori_loop(0, tm, drain, 0)

    def mix(t, carry):
        acc = gate_ref[0, t] * gbuf_ref[t]
        for k in range(1, TOP_K):
            acc = acc + gate_ref[k, t] * gbuf_ref[k * tm + t]
        buf_ref[_slab_ds(t, tm), :] = acc
        return carry

    lax.fori_loop(0, tm, mix, 0)
    routed = _buf_to_rows(buf_ref, tm)
    out_ref[...] = _layer_norm(alpha * x1 + (routed + shared), g2_ref[...], b2_ref[...])


def _combine(dest, gate, x1, ys, sg, su, sd, g2, b2, alpha):
    T = x1.shape[0]
    tm = TM_CMB
    tok = lambda i: (i, 0)
    fixed = lambda i: (0, 0)
    kt = pl.BlockSpec((TOP_K, tm), lambda i: (0, i), memory_space=pltpu.SMEM)
    return pl.pallas_call(
        functools.partial(_combine_body, alpha=alpha),
        grid=(T // tm,),
        in_specs=[kt, kt, pl.BlockSpec((tm, D_MODEL), tok), pl.BlockSpec(memory_space=pl.ANY),
                  pl.BlockSpec(sg.shape, fixed), pl.BlockSpec(su.shape, fixed),
                  pl.BlockSpec(sd.shape, fixed), pl.BlockSpec((1, D_MODEL), fixed),
                  pl.BlockSpec((1, D_MODEL), fixed)],
        out_specs=pl.BlockSpec((tm, D_MODEL), tok),
        out_shape=jax.ShapeDtypeStruct((T, D_MODEL), f32),
        scratch_shapes=[pltpu.VMEM((TOP_K * tm, CHUNKS, LANES), f32),
                        pltpu.VMEM((CHUNKS * (tm + PITCH), LANES), f32),
                        pltpu.SemaphoreType.DMA(())],
        compiler_params=_cparams(("arbitrary",)),
        name="combine",
    )(dest, gate, x1, ys, sg, su, sd, g2, b2)


def _rope_inv_freq():
    half = ROT_DIM // 2
    inv = ROPE_THETA ** (-jnp.arange(half, dtype=f32) * 2.0 / ROT_DIM)
    j = np.arange(LANES) % HEAD_DIM
    table = jnp.where(j < ROT_DIM, inv[j % half], 0.0)
    return table.reshape(1, LANES).astype(f32)


def _gate_weights(ga_w, ga_b, gx_w, gx_b):
    ng = REC_W // LANES
    per = LANES // HEAD_DIM
    def bd(w):
        w = w.reshape(ng, per, HEAD_DIM, HEAD_DIM)
        z = jnp.zeros((ng, LANES, LANES), w.dtype)
        for p in range(per):
            z = z.at[:, p * HEAD_DIM:(p + 1) * HEAD_DIM, p * HEAD_DIM:(p + 1) * HEAD_DIM].set(w[:, p])
        return z
    wg = jnp.concatenate([bd(ga_w[0]), bd(gx_w[0]), bd(ga_w[1]), bd(gx_w[1])], axis=-1).astype(bf16)
    grp = lambda b: b.reshape(ng, 1, LANES)
    gb = jnp.concatenate([grp(ga_b[0]), grp(gx_b[0]), grp(ga_b[1]), grp(gx_b[1])], axis=-1)
    return wg, gb


def _layer(x, positions, w_in, attn_gain, conv_w, conv_b, ga_w, ga_b, gx_w, gx_b, lam, rec_gain,
           w_out, ln1_g, ln1_b, router_w, router_bias, e_wg, e_wu, e_wd, s_wg, s_wu, s_wd,
           ln2_g, ln2_b, alpha):
    B, S, _ = x.shape
    T = B * S
    x2 = x.reshape(T, D_MODEL)
    pos2 = positions.reshape(T, 1)

    q, k, v, rx, rg = _inproj(x2, pos2, w_in.astype(bf16), _rope_inv_freq())
    branch = [_attn_branch(q, k, v, B, S, dil) for _, dil in BRANCHES]
    wg, gb = _gate_weights(ga_w, ga_b, gx_w, gx_b)
    rec = _rec(rx, rg, conv_w, conv_b.reshape(1, REC_W), wg, gb, lam, B, S).reshape(T, REC_W)

    rw_t = router_w.T
    rw_hi = rw_t.astype(bf16)
    rw_lo = (rw_t - rw_hi.astype(f32)).astype(bf16)
    rbias = jnp.broadcast_to(router_bias.reshape(N_EXPERTS, 1), (N_EXPERTS, LANES))
    x1, x1s, idx, gate, rank, cnt = _mixout(
        [b[0] for b in branch], [b[1] for b in branch], rec, x2,
        attn_gain.reshape(1, ATTN_W), rec_gain.reshape(1, REC_W), w_out.astype(bf16),
        ln1_g.reshape(1, D_MODEL), ln1_b.reshape(1, D_MODEL), rw_hi, rw_lo, rbias, alpha)

    counts = cnt[:, 0].astype(i32)
    padded = (counts + BM - 1) // BM * BM
    pad_end = jnp.cumsum(padded)
    pad_start = pad_end - padded
    dest = _slots(idx, rank, pad_start)
    nb = (T * TOP_K + N_EXPERTS * (BM - 1)) // BM + 1
    first_row = jnp.arange(nb, dtype=i32) * BM
    block_e = jnp.minimum(jnp.sum((pad_end[None, :] <= first_row[:, None]).astype(i32), axis=1),
                          N_EXPERTS - 1)
    block_rows = jnp.clip((pad_start + counts)[block_e] - first_row, 0, BM).astype(i32)
    nb_used = (pad_end[-1] // BM).astype(i32).reshape(1)

    xs = _dispatch(dest, x1s, nb * BM)
    ys = _experts(block_e, block_rows, nb_used, xs,
                  e_wg.astype(bf16), e_wu.astype(bf16), e_wd.astype(bf16))
    out = _combine(dest, gate, x1, ys, s_wg.astype(bf16), s_wu.astype(bf16), s_wd.astype(bf16),
                   ln2_g.reshape(1, D_MODEL), ln2_b.reshape(1, D_MODEL), alpha)
    return out.reshape(B, S, D_MODEL)


def kernel(x, positions, w_in, attn_gain, rec_conv_w, rec_conv_b, rec_gate_a_w, rec_gate_a_b,
           rec_gate_x_w, rec_gate_x_b, rec_lambda, rec_gain, w_out, ln1_g, ln1_b, router_w,
           router_bias, exp_w_gate, exp_w_up, exp_w_down, shared_w_gate, shared_w_up,
           shared_w_down, ln2_g, ln2_b):
    depth = w_in.shape[0]
    alpha = (2 * depth) ** 0.25
    for l in range(depth):
        x = _layer(x, positions, w_in[l], attn_gain[l], rec_conv_w[l], rec_conv_b[l],
                   rec_gate_a_w[l], rec_gate_a_b[l], rec_gate_x_w[l], rec_gate_x_b[l],
                   rec_lambda[l], rec_gain[l], w_out[l], ln1_g[l], ln1_b[l], router_w[l],
                   router_bias[l], exp_w_gate[l], exp_w_up[l], exp_w_down[l], shared_w_gate[l],
                   shared_w_up[l], shared_w_down[l], ln2_g[l], ln2_b[l], alpha)
    return x
```

```python
import functools
import math

import jax
import jax.numpy as jnp
import numpy as np
from jax import lax
from jax.experimental import pallas as pl
from jax.experimental.pallas import tpu as pltpu

f32 = jnp.float32
bf16 = jnp.bfloat16
i32 = jnp.int32

D_MODEL = 1024
ATTN_W = 512
REC_W = 512
HEAD_DIM = 64
ROT_DIM = 16
ROPE_THETA = 500000.0
BRANCHES = ((128, 1), (512, 4), (2048, 16))
HALF_BAND = 64
CONV_W = 4
RG_LRU_C = 8.0
N_EXPERTS = 256
TOP_K = 8
N_GROUPS = 8
GROUP_SZ = N_EXPERTS // N_GROUPS
TOPK_GROUPS = 4
EXPERT_H = 256
ROUTED_SCALE = 2.5
LN_EPS = 1e-5
NEG = -1e30

LANES = 128
SUBLANES = 8
CHUNKS = D_MODEL // LANES
VMEM_LIMIT = 56 * 1024 * 1024

TM_IN = 512
TQ = 128
KW = TQ + 2 * HALF_BAND
SB = TQ * max(d for _, d in BRANCHES)
MERGE_ROWS = 256
TS = 256
TM_MIX = 256
TM_SLOT = 512
TM_CMB = 128
BM = 256
PITCH = 8


def _cparams(sem):
    return pltpu.CompilerParams(dimension_semantics=sem, vmem_limit_bytes=VMEM_LIMIT)


def _inproj_body(x_ref, pos_ref, w_ref, invf_ref, q_ref, k_ref, v_ref, rx_ref, rg_ref):
    xb = x_ref[...].astype(bf16)
    ang = pos_ref[...].astype(f32) * invf_ref[...]
    cos = jnp.cos(ang)
    sin = jnp.sin(ang)
    j = lax.broadcasted_iota(i32, (1, LANES), 1) % HEAD_DIM
    half = ROT_DIM // 2
    s_lo = jnp.where(j < half, -sin, 0.0)
    s_hi = jnp.where((j >= half) & (j < ROT_DIM), sin, 0.0)
    rep = ATTN_W // LANES
    cos_w = jnp.concatenate([cos] * rep, axis=1)
    s_lo_w = jnp.concatenate([s_lo] * rep, axis=1)
    s_hi_w = jnp.concatenate([s_hi] * rep, axis=1)

    def proj(c0, n):
        return jnp.dot(xb, w_ref[:, c0:c0 + n], preferred_element_type=f32)

    def rope(t):
        return (t * cos_w + pltpu.roll(t, ATTN_W - half, 1) * s_lo_w
                + pltpu.roll(t, half, 1) * s_hi_w)

    q_ref[...] = rope(proj(0, ATTN_W)) * (HEAD_DIM ** -0.5)
    k_ref[...] = rope(proj(ATTN_W, ATTN_W))
    v_ref[...] = proj(2 * ATTN_W, ATTN_W)
    rx_ref[...] = proj(3 * ATTN_W, REC_W)
    rg_ref[...] = proj(3 * ATTN_W + REC_W, REC_W)


def _inproj(x2, pos2, w_in_b, invf):
    T = x2.shape[0]
    tm = TM_IN
    in_w = w_in_b.shape[1]
    tok = lambda i: (i, 0)
    fixed = lambda i: (0, 0)
    return pl.pallas_call(
        _inproj_body,
        grid=(T // tm,),
        in_specs=[pl.BlockSpec((tm, D_MODEL), tok), pl.BlockSpec((tm, 1), tok),
                  pl.BlockSpec((D_MODEL, in_w), fixed), pl.BlockSpec((1, LANES), fixed)],
        out_specs=[pl.BlockSpec((tm, ATTN_W), tok)] * 3 + [pl.BlockSpec((tm, REC_W), tok)] * 2,
        out_shape=[jax.ShapeDtypeStruct((T, ATTN_W), f32)] * 3
        + [jax.ShapeDtypeStruct((T, REC_W), f32)] * 2,
        compiler_params=_cparams(("parallel",)),
        name="inproj",
    )(x2, pos2, w_in_b, invf)


def _attn_body(q_ref, k_ref, v_ref, out_ref, o1, o2, o3, l1, l2, l3, *, S):
    o_sc, l_sc = (o1, o2, o3), (l1, l2, l3)
    lane = lax.broadcasted_iota(i32, (1, LANES), 1)
    head0 = lane < HEAD_DIM
    rel = (lax.broadcasted_iota(i32, (TQ, KW), 0) - lax.broadcasted_iota(i32, (TQ, KW), 1))
    tiles = SB // TQ

    def rows(ref, start, n, d):
        return ref[pl.ds(start, n), :] if d == 1 else ref[pl.ds(start, n, stride=d), :]

    def tile(ti, n0):
        ctx = []
        for g, (_, d) in enumerate(BRANCHES):
            L = S // d
            sh = d.bit_length() - 1
            r = jnp.bitwise_and(ti, d - 1)
            m0 = jnp.right_shift(n0, sh) + jnp.right_shift(ti, sh) * TQ
            ks = jnp.clip(m0 - HALF_BAND, 0, L - KW)
            q = rows(q_ref, r + d * m0, TQ, d).astype(bf16)
            k = rows(k_ref, r + d * ks, KW, d).astype(bf16)
            v = rows(v_ref, r + d * ks, KW, d).astype(bf16)
            valid = jnp.abs(rel + (m0 - ks)) <= HALF_BAND
            local = r + d * (m0 - jnp.right_shift(n0, sh))
            ss = []
            for sel in (head0, jnp.logical_not(head0)):
                qh = jnp.where(sel, q, jnp.zeros_like(q))
                ss.append(lax.dot_general(qh, k, (((1,), (1,)), ((), ())),
                                          preferred_element_type=f32))
            ctx.append((d, v, valid, local, ss))
        soft = []
        for d, v, valid, local, ss in ctx:
            ps = []
            for s in ss:
                s = jnp.where(valid, s, NEG)
                m = jnp.max(s, axis=-1, keepdims=True)
                p = jnp.exp(s - m)
                den = jnp.sum(p, axis=-1, keepdims=True)
                ps.append((p.astype(bf16), den, m + jnp.log(den)))
            soft.append(ps)
        for g, ((d, v, valid, local, ss), ps) in enumerate(zip(ctx, soft)):
            outs = [(jnp.dot(p, v, preferred_element_type=f32) / den, lse) for p, den, lse in ps]
            o_val = jnp.where(head0, outs[0][0], outs[1][0])
            l_val = jnp.where(head0, outs[0][1], outs[1][1])
            if d == 1:
                o_sc[g][pl.ds(local, TQ), :] = o_val
                l_sc[g][pl.ds(local, TQ), :] = l_val
            else:
                o_sc[g][pl.ds(local, TQ, stride=d), :] = o_val
                l_sc[g][pl.ds(local, TQ, stride=d), :] = l_val
        return n0

    def merge(c, n0):
        sl = pl.ds(pl.multiple_of(c * MERGE_ROWS, MERGE_ROWS), MERGE_ROWS)
        ls = [l_sc[g][sl, :] for g in range(len(BRANCHES))]
        mx = functools.reduce(jnp.maximum, ls)
        es = [jnp.exp(l - mx) for l in ls]
        num = functools.reduce(lambda a, b: a + b, [e * o_sc[g][sl, :] for g, e in enumerate(es)])
        den = functools.reduce(lambda a, b: a + b, es)
        dst = pl.ds(pl.multiple_of(n0 + c * MERGE_ROWS, MERGE_ROWS), MERGE_ROWS)
        out_ref[dst, :] = (num / den).astype(bf16)
        return n0

    def superblock(sb, carry):
        n0 = pl.multiple_of(sb * SB, SB)
        lax.fori_loop(0, tiles, tile, n0)
        lax.fori_loop(0, SB // MERGE_ROWS, merge, n0)
        return carry

    lax.fori_loop(0, S // SB, superblock, 0)


def _attention(q, k, v, B, S):
    assert S % SB == 0 and all(S // d >= KW for _, d in BRANCHES)
    hp = ATTN_W // LANES
    view = lambda t: t.reshape(B, S, ATTN_W)
    spec = pl.BlockSpec((None, S, LANES), lambda b, h: (b, 0, h))
    out = pl.pallas_call(
        functools.partial(_attn_body, S=S),
        grid=(B, hp),
        in_specs=[spec, spec, spec],
        out_specs=spec,
        out_shape=jax.ShapeDtypeStruct((B, S, ATTN_W), bf16),
        scratch_shapes=[pltpu.VMEM((SB, LANES), f32)] * (2 * len(BRANCHES)),
        compiler_params=_cparams(("parallel", "parallel")),
        name="attention",
    )(view(q), view(k), view(v))
    return out.reshape(B * S, ATTN_W)


def _rec_body(rx_ref, rg_ref, cw_ref, cb_ref, wg_ref, gb_ref, lam_ref, out_ref, rxp_ref, hf_ref, *, S):
    nch = S // TS
    ntile = TS // SUBLANES
    zeros8 = jnp.zeros((SUBLANES, LANES), f32)
    rxp_ref[pl.ds(0, SUBLANES), :] = zeros8
    rxp_ref[pl.ds(S + SUBLANES, SUBLANES), :] = zeros8

    def pad_copy(c, carry):
        t0 = pl.multiple_of(c * TS, TS)
        rxp_ref[pl.ds(t0 + SUBLANES, TS), :] = rx_ref[pl.ds(t0, TS), :]
        return carry

    lax.fori_loop(0, nch, pad_copy, 0)

    lam = lam_ref[...]
    neg_sp = -RG_LRU_C * (jnp.maximum(-lam, 0.0) + jnp.log1p(jnp.exp(-jnp.abs(lam))))
    cw = cw_ref[...]
    cb = cb_ref[...]
    sub = lax.broadcasted_iota(i32, (ntile, SUBLANES, LANES), 1)
    nrow = TS + 2 * SUBLANES

    def gates(ci, d):
        t0 = pl.multiple_of(ci * TS, TS)
        xw = rxp_ref[pl.ds(t0, nrow), :]
        u = (cw[0:1] * pltpu.roll(xw, 2, 0) + cw[1:2] * pltpu.roll(xw, 1, 0) + cw[2:3] * xw
             + cw[3:4] * pltpu.roll(xw, nrow - 1, 0))[SUBLANES:SUBLANES + TS] + cb
        c0 = d * 2 * LANES
        g = jnp.dot(u.astype(bf16), wg_ref[:, c0:c0 + 2 * LANES], preferred_element_type=f32)
        g = g + gb_ref[:, c0:c0 + 2 * LANES]
        r = jax.nn.sigmoid(g[:, :LANES])
        gi = jax.nn.sigmoid(g[:, LANES:])
        a = jnp.exp(neg_sp[d:d + 1] * r)
        b = jnp.sqrt(1.0 - a * a) * gi * u
        return a.reshape(ntile, SUBLANES, LANES), b.reshape(ntile, SUBLANES, LANES)

    def chunk(i, carry):
        cf, cbk = carry
        a, b = gates(i, 0)
        for s in (1, 2, 4):
            ok = sub >= s
            a_s = pltpu.roll(a, s, 1)
            b_s = pltpu.roll(b, s, 1)
            b = jnp.where(ok, a * b_s + b, b)
            a = jnp.where(ok, a * a_s, a)
        t0 = pl.multiple_of(i * TS, TS)
        for j in range(ntile):
            h = a[j] * cf + b[j]
            hf_ref[pl.ds(t0 + j * SUBLANES, SUBLANES), :] = h
            cf = h[SUBLANES - 1:SUBLANES, :]
        ib = nch - 1 - i
        a, b = gates(ib, 1)
        for s in (1, 2, 4):
            ok = sub < SUBLANES - s
            a_s = pltpu.roll(a, SUBLANES - s, 1)
            b_s = pltpu.roll(b, SUBLANES - s, 1)
            b = jnp.where(ok, a * b_s + b, b)
            a = jnp.where(ok, a * a_s, a)
        t0 = pl.multiple_of(ib * TS, TS)
        for j in range(ntile - 1, -1, -1):
            h = a[j] * cbk + b[j]
            out_ref[pl.ds(t0 + j * SUBLANES, SUBLANES), :] = h
            cbk = h[0:1, :]
        return cf, cbk

    zrow = jnp.zeros((1, LANES), f32)
    lax.fori_loop(0, nch, chunk, (zrow, zrow))

    def finish(c, carry):
        t0 = pl.multiple_of(c * TS, TS)
        sl = pl.ds(t0, TS)
        out_ref[sl, :] = (hf_ref[sl, :] + out_ref[sl, :]) * jax.nn.gelu(rg_ref[sl, :])
        return carry

    lax.fori_loop(0, nch, finish, 0)


def _rec(rx, rg, conv_w, conv_b, wg, gb, lam, B, S):
    ng = REC_W // LANES
    assert S % TS == 0
    seq = pl.BlockSpec((None, S, LANES), lambda b, c: (b, 0, c))
    return pl.pallas_call(
        functools.partial(_rec_body, S=S),
        grid=(B, ng),
        in_specs=[seq, seq,
                  pl.BlockSpec((CONV_W, LANES), lambda b, c: (0, c)),
                  pl.BlockSpec((1, LANES), lambda b, c: (0, c)),
                  pl.BlockSpec((None, LANES, 4 * LANES), lambda b, c: (c, 0, 0)),
                  pl.BlockSpec((None, 1, 4 * LANES), lambda b, c: (c, 0, 0)),
                  pl.BlockSpec((2, LANES), lambda b, c: (0, c))],
        out_specs=seq,
        out_shape=jax.ShapeDtypeStruct((B, S, REC_W), f32),
        scratch_shapes=[pltpu.VMEM((S + 2 * SUBLANES, LANES), f32), pltpu.VMEM((S, LANES), f32)],
        compiler_params=_cparams(("parallel", "parallel")),
        name="rec",
    )(rx.reshape(B, S, REC_W), rg.reshape(B, S, REC_W), conv_w, conv_b, wg, gb, lam)


def _rows_to_buf(buf, val, rows):
    sp = rows + PITCH
    for j in range(CHUNKS):
        buf[pl.ds(j * sp, rows), :] = val[:, j * LANES:(j + 1) * LANES]


def _buf_to_rows(buf, rows):
    sp = rows + PITCH
    return jnp.concatenate([buf[pl.ds(j * sp, rows), :] for j in range(CHUNKS)], axis=1)


def _slab_ds(t, rows):
    return pl.ds(t, CHUNKS, stride=rows + PITCH)


def _rms(t, gain):
    return t * lax.rsqrt(jnp.mean(t * t, axis=-1, keepdims=True) + LN_EPS) * gain


def _layer_norm(z, g, b):
    mu = jnp.mean(z, axis=-1, keepdims=True)
    zc = z - mu
    var = jnp.mean(zc * zc, axis=-1, keepdims=True)
    return zc * lax.rsqrt(var + LN_EPS) * g + b


def _mixout_body(attn_ref, rec_ref, x_ref,
                 ag_ref, rgn_ref, wo_ref, g1_ref, b1_ref, rwh_ref, rwl_ref, rb_ref, tri_ref, ones_ref,
                 x1_ref, x1s_ref, idx_ref, gate_ref, rank_ref, cnt_ref, carry_ref, buf_ref, *, alpha):
    tm = TM_MIX

    @pl.when(pl.program_id(0) == 0)
    def _():
        carry_ref[...] = jnp.zeros_like(carry_ref)

    attn_n = _rms(attn_ref[...].astype(f32), ag_ref[...]).astype(bf16)
    rec_n = _rms(rec_ref[...], rgn_ref[...]).astype(bf16)
    y = (jnp.dot(attn_n, wo_ref[0:ATTN_W, :], preferred_element_type=f32)
         + jnp.dot(rec_n, wo_ref[ATTN_W:, :], preferred_element_type=f32))
    x1 = _layer_norm(alpha * x_ref[...] + y, g1_ref[...], b1_ref[...])
    x1_ref[...] = x1
    _rows_to_buf(buf_ref, x1, tm)
    for t in range(tm):
        x1s_ref[t] = buf_ref[_slab_ds(t, tm), :]

    hi = x1.astype(bf16)
    lo = (x1 - hi.astype(f32)).astype(bf16)
    nt = (((1,), (1,)), ((), ()))
    logits = (lax.dot_general(rwh_ref[...], hi, nt, preferred_element_type=f32)
              + lax.dot_general(rwh_ref[...], lo, nt, preferred_element_type=f32)
              + lax.dot_general(rwl_ref[...], hi, nt, preferred_element_type=f32))
    scores = jax.nn.sigmoid(logits)
    biased = scores + jnp.concatenate([rb_ref[...]] * (tm // LANES), axis=1)

    rid = lax.broadcasted_iota(i32, (GROUP_SZ, tm), 0).astype(f32)
    grp = []
    for g in range(N_GROUPS):
        vg = biased[g * GROUP_SZ:(g + 1) * GROUP_SZ, :]
        m1 = jnp.max(vg, axis=0, keepdims=True)
        first = jnp.min(jnp.where(vg == m1, rid, float(GROUP_SZ)), axis=0, keepdims=True)
        m2 = jnp.max(jnp.where(rid == first, -jnp.inf, vg), axis=0, keepdims=True)
        grp.append(m1 + m2)
    eid = lax.broadcasted_iota(i32, (N_EXPERTS, tm), 0).astype(f32)
    egid = jnp.floor(eid * (1.0 / GROUP_SZ))
    masked = jnp.full((N_EXPERTS, tm), -jnp.inf, f32)
    for _ in range(TOPK_GROUPS):
        gm = functools.reduce(jnp.maximum, grp)
        gi = jnp.full((1, tm), float(N_GROUPS), f32)
        for g in range(N_GROUPS - 1, -1, -1):
            gi = jnp.where(grp[g] == gm, float(g), gi)
        grp = [jnp.where(gi == float(g), -jnp.inf, grp[g]) for g in range(N_GROUPS)]
        masked = jnp.where(egid == gi, biased, masked)

    onehot = jnp.zeros((N_EXPERTS, tm), f32)
    idxs, gts = [], []
    for _ in range(TOP_K):
        mx = jnp.max(masked, axis=0, keepdims=True)
        ix = jnp.min(jnp.where(masked == mx, eid, float(N_EXPERTS)), axis=0, keepdims=True)
        hit = eid == ix
        gts.append(jnp.sum(jnp.where(hit, scores, 0.0), axis=0, keepdims=True))
        idxs.append(ix)
        masked = jnp.where(hit, -jnp.inf, masked)
        onehot = onehot + jnp.where(hit, 1.0, 0.0)
    gsum = functools.reduce(lambda p, q: p + q, gts)
    for kk in range(TOP_K):
        idx_ref[kk:kk + 1, :] = idxs[kk].astype(i32)
        gate_ref[kk:kk + 1, :] = gts[kk] / gsum * ROUTED_SCALE

    oh = onehot.astype(bf16)
    before = carry_ref[...] + jnp.dot(oh, tri_ref[...], preferred_element_type=f32)
    for kk in range(TOP_K):
        rk = jnp.sum(jnp.where(eid == idxs[kk], before, 0.0), axis=0, keepdims=True)
        rank_ref[kk:kk + 1, :] = rk.astype(i32)
    total = carry_ref[...] + jnp.dot(oh, ones_ref[...], preferred_element_type=f32)
    carry_ref[...] = total
    cnt_ref[...] = total


def _mixout(attn, rec, x2, attn_gain, rec_gain, w_out_b, g1, b1, rw_hi, rw_lo, rbias, alpha):
    T = x2.shape[0]
    tm = TM_MIX
    tok = lambda i: (i, 0)
    fixed = lambda i: (0, 0)
    tri = jnp.asarray(np.triu(np.ones((tm, tm), np.float32), k=1), bf16)
    ones = jnp.ones((tm, tm), bf16)
    aw = pl.BlockSpec((tm, ATTN_W), tok)
    row = lambda n: pl.BlockSpec((1, n), fixed)
    kt = pl.BlockSpec((TOP_K, tm), lambda i: (0, i))
    return pl.pallas_call(
        functools.partial(_mixout_body, alpha=alpha),
        grid=(T // tm,),
        in_specs=[aw] * 2 + [pl.BlockSpec((tm, D_MODEL), tok), row(ATTN_W), row(REC_W),
                             pl.BlockSpec((D_MODEL, D_MODEL), fixed), row(D_MODEL), row(D_MODEL),
                             pl.BlockSpec((N_EXPERTS, D_MODEL), fixed),
                             pl.BlockSpec((N_EXPERTS, D_MODEL), fixed),
                             pl.BlockSpec((N_EXPERTS, LANES), fixed),
                             pl.BlockSpec((tm, tm), fixed), pl.BlockSpec((tm, tm), fixed)],
        out_specs=[pl.BlockSpec((tm, D_MODEL), tok),
                   pl.BlockSpec((tm, CHUNKS, LANES), lambda i: (i, 0, 0)),
                   kt, kt, kt, pl.BlockSpec((N_EXPERTS, tm), fixed)],
        out_shape=[jax.ShapeDtypeStruct((T, D_MODEL), f32),
                   jax.ShapeDtypeStruct((T, CHUNKS, LANES), f32),
                   jax.ShapeDtypeStruct((TOP_K, T), i32),
                   jax.ShapeDtypeStruct((TOP_K, T), f32),
                   jax.ShapeDtypeStruct((TOP_K, T), i32),
                   jax.ShapeDtypeStruct((N_EXPERTS, tm), f32)],
        scratch_shapes=[pltpu.VMEM((N_EXPERTS, tm), f32),
                        pltpu.VMEM((CHUNKS * (tm + PITCH), LANES), f32)],
        compiler_params=_cparams(("arbitrary",)),
        name="mixout",
    )(attn, rec, x2, attn_gain, rec_gain, w_out_b, g1, b1,
      rw_hi, rw_lo, rbias, tri, ones)


def _slots_body(idx_ref, rank_ref, start_ref, dest_ref):
    tm = TM_SLOT
    eid = lax.broadcasted_iota(i32, (N_EXPERTS, tm), 0)
    start = jnp.concatenate([start_ref[...]] * (tm // LANES), axis=1)
    for kk in range(TOP_K):
        hit = eid == idx_ref[kk:kk + 1, :]
        base = jnp.sum(jnp.where(hit, start, 0.0), axis=0, keepdims=True)
        dest_ref[kk:kk + 1, :] = base.astype(i32) + rank_ref[kk:kk + 1, :]


def _slots(idx, rank, pad_start):
    T = idx.shape[1]
    tm = TM_SLOT
    kt = pl.BlockSpec((TOP_K, tm), lambda i: (0, i))
    start = jnp.broadcast_to(pad_start.astype(f32).reshape(N_EXPERTS, 1), (N_EXPERTS, LANES))
    return pl.pallas_call(
        _slots_body,
        grid=(T // tm,),
        in_specs=[kt, kt, pl.BlockSpec((N_EXPERTS, LANES), lambda i: (0, 0))],
        out_specs=kt,
        out_shape=jax.ShapeDtypeStruct((TOP_K, T), i32),
        compiler_params=_cparams(("parallel",)),
        name="slots",
    )(idx, rank, start)


def _dispatch_body(dest_ref, x_ref, xs_ref, sem):
    tm = TM_MIX

    def copy(t, k):
        return pltpu.make_async_copy(x_ref.at[t], xs_ref.at[dest_ref[k, t]], sem)

    def issue(t, carry):
        for k in range(TOP_K):
            copy(t, k).start()
        return carry

    def drain(t, carry):
        for k in range(TOP_K):
            copy(t, k).wait()
        return carry

    lax.fori_loop(0, tm, issue, 0)
    lax.fori_loop(0, tm, drain, 0)


def _dispatch(dest, x1s, n_slots):
    T = x1s.shape[0]
    tm = TM_MIX
    return pl.pallas_call(
        _dispatch_body,
        grid=(T // tm,),
        in_specs=[pl.BlockSpec((TOP_K, tm), lambda i: (0, i), memory_space=pltpu.SMEM),
                  pl.BlockSpec((tm, CHUNKS, LANES), lambda i: (i, 0, 0))],
        out_specs=pl.BlockSpec(memory_space=pl.ANY),
        out_shape=jax.ShapeDtypeStruct((n_slots, CHUNKS, LANES), f32),
        scratch_shapes=[pltpu.SemaphoreType.DMA(())],
        compiler_params=_cparams(("arbitrary",)),
        name="dispatch",
    )(dest, x1s)


def _experts_body(be_ref, nv_ref, nb_ref, xs_ref, wg_ref, wu_ref, wd_ref, ys_ref, buf_ref):
    del be_ref
    i = pl.program_id(0)

    @pl.when(i < nb_ref[0])
    def _():
        for t in range(BM):
            buf_ref[_slab_ds(t, BM), :] = xs_ref[t]
        live = lax.broadcasted_iota(i32, (BM, LANES), 0) < nv_ref[i]
        sp = BM + PITCH
        x = jnp.concatenate(
            [jnp.where(live, buf_ref[pl.ds(j * sp, BM), :], 0.0) for j in range(CHUNKS)],
            axis=1).astype(bf16)
        g = jnp.dot(x, wg_ref[...], preferred_element_type=f32)
        u = jnp.dot(x, wu_ref[...], preferred_element_type=f32)
        h = (g * jax.nn.sigmoid(g) * u).astype(bf16)
        y = jnp.dot(h, wd_ref[...], preferred_element_type=f32)
        _rows_to_buf(buf_ref, y, BM)
        for t in range(BM):
            ys_ref[t] = buf_ref[_slab_ds(t, BM), :]


def _experts(block_e, block_rows, nb_used, xs, wg, wu, wd):
    P = xs.shape[0]
    nb = P // BM
    rows = lambda i, be, nv, nbu: (jnp.minimum(i, nbu[0] - 1), 0, 0)
    wsel = lambda i, be, nv, nbu: (be[i], 0, 0)
    gs = pltpu.PrefetchScalarGridSpec(
        num_scalar_prefetch=3,
        grid=(nb,),
        in_specs=[pl.BlockSpec((BM, CHUNKS, LANES), rows),
                  pl.BlockSpec((None, D_MODEL, EXPERT_H), wsel),
                  pl.BlockSpec((None, D_MODEL, EXPERT_H), wsel),
                  pl.BlockSpec((None, EXPERT_H, D_MODEL), wsel)],
        out_specs=pl.BlockSpec((BM, CHUNKS, LANES), rows),
        scratch_shapes=[pltpu.VMEM((CHUNKS * (BM + PITCH), LANES), f32)],
    )
    return pl.pallas_call(
        _experts_body,
        grid_spec=gs,
        out_shape=jax.ShapeDtypeStruct((P, CHUNKS, LANES), f32),
        compiler_params=_cparams(("arbitrary",)),
        name="experts",
    )(block_e, block_rows, nb_used, xs, wg, wu, wd)


def _combine_body(dest_ref, gate_ref, x1_ref, ys_ref, sg_ref, su_ref, sd_ref, g2_ref, b2_ref,
                  out_ref, gbuf_ref, buf_ref, sem, *, alpha):
    tm = TM_CMB

    def copy(t, k):
        return pltpu.make_async_copy(ys_ref.at[dest_ref[k, t]], gbuf_ref.at[k * tm + t], sem)

    def issue(t, carry):
        for k in range(TOP_K):
            copy(t, k).start()
        return carry

    lax.fori_loop(0, tm, issue, 0)

    x1 = x1_ref[...]
    xb = x1.astype(bf16)
    sg = jnp.dot(xb, sg_ref[...], preferred_element_type=f32)
    su = jnp.dot(xb, su_ref[...], preferred_element_type=f32)
    shared = jnp.dot((sg * jax.nn.sigmoid(sg) * su).astype(bf16), sd_ref[...],
                     preferred_element_type=f32)

    def drain(t, carry):
        for k in range(TOP_K):
            copy(t, k).wait()
        return carry

    lax.fori_loop(0, tm, drain, 0)

    def mix(t, carry):
        acc = gate_ref[0, t] * gbuf_ref[t]
        for k in range(1, TOP_K):
            acc = acc + gate_ref[k, t] * gbuf_ref[k * tm + t]
        buf_ref[_slab_ds(t, tm), :] = acc
        return carry

    lax.fori_loop(0, tm, mix, 0)
    routed = _buf_to_rows(buf_ref, tm)
    out_ref[...] = _layer_norm(alpha * x1 + (routed + shared), g2_ref[...], b2_ref[...])


def _combine(dest, gate, x1, ys, sg, su, sd, g2, b2, alpha):
    T = x1.shape[0]
    tm = TM_CMB
    tok = lambda i: (i, 0)
    fixed = lambda i: (0, 0)
    kt = pl.BlockSpec((TOP_K, tm), lambda i: (0, i), memory_space=pltpu.SMEM)
    return pl.pallas_call(
        functools.partial(_combine_body, alpha=alpha),
        grid=(T // tm,),
        in_specs=[kt, kt, pl.BlockSpec((tm, D_MODEL), tok), pl.BlockSpec(memory_space=pl.ANY),
                  pl.BlockSpec(sg.shape, fixed), pl.BlockSpec(su.shape, fixed),
                  pl.BlockSpec(sd.shape, fixed), pl.BlockSpec((1, D_MODEL), fixed),
                  pl.BlockSpec((1, D_MODEL), fixed)],
        out_specs=pl.BlockSpec((tm, D_MODEL), tok),
        out_shape=jax.ShapeDtypeStruct((T, D_MODEL), f32),
        scratch_shapes=[pltpu.VMEM((TOP_K * tm, CHUNKS, LANES), f32),
                        pltpu.VMEM((CHUNKS * (tm + PITCH), LANES), f32),
                        pltpu.SemaphoreType.DMA(())],
        compiler_params=_cparams(("arbitrary",)),
        name="combine",
    )(dest, gate, x1, ys, sg, su, sd, g2, b2)


def _rope_inv_freq():
    half = ROT_DIM // 2
    inv = ROPE_THETA ** (-jnp.arange(half, dtype=f32) * 2.0 / ROT_DIM)
    j = np.arange(LANES) % HEAD_DIM
    table = jnp.where(j < ROT_DIM, inv[j % half], 0.0)
    return table.reshape(1, LANES).astype(f32)


def _gate_weights(ga_w, ga_b, gx_w, gx_b):
    ng = REC_W // LANES
    per = LANES // HEAD_DIM
    def bd(w):
        w = w.reshape(ng, per, HEAD_DIM, HEAD_DIM)
        z = jnp.zeros((ng, LANES, LANES), w.dtype)
        for p in range(per):
            z = z.at[:, p * HEAD_DIM:(p + 1) * HEAD_DIM, p * HEAD_DIM:(p + 1) * HEAD_DIM].set(w[:, p])
        return z
    wg = jnp.concatenate([bd(ga_w[0]), bd(gx_w[0]), bd(ga_w[1]), bd(gx_w[1])], axis=-1).astype(bf16)
    grp = lambda b: b.reshape(ng, 1, LANES)
    gb = jnp.concatenate([grp(ga_b[0]), grp(gx_b[0]), grp(ga_b[1]), grp(gx_b[1])], axis=-1)
    return wg, gb


def _layer(x, positions, w_in, attn_gain, conv_w, conv_b, ga_w, ga_b, gx_w, gx_b, lam, rec_gain,
           w_out, ln1_g, ln1_b, router_w, router_bias, e_wg, e_wu, e_wd, s_wg, s_wu, s_wd,
           ln2_g, ln2_b, alpha):
    B, S, _ = x.shape
    T = B * S
    x2 = x.reshape(T, D_MODEL)
    pos2 = positions.reshape(T, 1)

    q, k, v, rx, rg = _inproj(x2, pos2, w_in.astype(bf16), _rope_inv_freq())
    attn = _attention(q, k, v, B, S)
    wg, gb = _gate_weights(ga_w, ga_b, gx_w, gx_b)
    rec = _rec(rx, rg, conv_w, conv_b.reshape(1, REC_W), wg, gb, lam, B, S).reshape(T, REC_W)

    rw_t = router_w.T
    rw_hi = rw_t.astype(bf16)
    rw_lo = (rw_t - rw_hi.astype(f32)).astype(bf16)
    rbias = jnp.broadcast_to(router_bias.reshape(N_EXPERTS, 1), (N_EXPERTS, LANES))
    x1, x1s, idx, gate, rank, cnt = _mixout(
        attn, rec, x2,
        attn_gain.reshape(1, ATTN_W), rec_gain.reshape(1, REC_W), w_out.astype(bf16),
        ln1_g.reshape(1, D_MODEL), ln1_b.reshape(1, D_MODEL), rw_hi, rw_lo, rbias, alpha)

    counts = cnt[:, 0].astype(i32)
    padded = (counts + BM - 1) // BM * BM
    pad_end = jnp.cumsum(padded)
    pad_start = pad_end - padded
    dest = _slots(idx, rank, pad_start)
    nb = (T * TOP_K + N_EXPERTS * (BM - 1)) // BM + 1
    first_row = jnp.arange(nb, dtype=i32) * BM
    block_e = jnp.minimum(jnp.sum((pad_end[None, :] <= first_row[:, None]).astype(i32), axis=1),
                          N_EXPERTS - 1)
    block_rows = jnp.clip((pad_start + counts)[block_e] - first_row, 0, BM).astype(i32)
    nb_used = (pad_end[-1] // BM).astype(i32).reshape(1)

    xs = _dispatch(dest, x1s, nb * BM)
    ys = _experts(block_e, block_rows, nb_used, xs,
                  e_wg.astype(bf16), e_wu.astype(bf16), e_wd.astype(bf16))
    out = _combine(dest, gate, x1, ys, s_wg.astype(bf16), s_wu.astype(bf16), s_wd.astype(bf16),
                   ln2_g.reshape(1, D_MODEL), ln2_b.reshape(1, D_MODEL), alpha)
    return out.reshape(B, S, D_MODEL)


def kernel(x, positions, w_in, attn_gain, rec_conv_w, rec_conv_b, rec_gate_a_w, rec_gate_a_b,
           rec_gate_x_w, rec_gate_x_b, rec_lambda, rec_gain, w_out, ln1_g, ln1_b, router_w,
           router_bias, exp_w_gate, exp_w_up, exp_w_down, shared_w_gate, shared_w_up,
           shared_w_down, ln2_g, ln2_b):
    depth = w_in.shape[0]
    alpha = (2 * depth) ** 0.25
    for l in range(depth):
        x = _layer(x, positions, w_in[l], attn_gain[l], rec_conv_w[l], rec_conv_b[l],
                   rec_gate_a_w[l], rec_gate_a_b[l], rec_gate_x_w[l], rec_gate_x_b[l],
                   rec_lambda[l], rec_gain[l], w_out[l], ln1_g[l], ln1_b[l], router_w[l],
                   router_bias[l], exp_w_gate[l], exp_w_up[l], exp_w_down[l], shared_w_gate[l],
                   shared_w_up[l], shared_w_down[l], ln2_g[l], ln2_b[l], alpha)
    return x
```

```python
import functools
import math

import jax
import jax.numpy as jnp
import numpy as np
from jax import lax
from jax.experimental import pallas as pl
from jax.experimental.pallas import tpu as pltpu

f32 = jnp.float32
bf16 = jnp.bfloat16
i32 = jnp.int32

D_MODEL = 1024
ATTN_W = 512
REC_W = 512
HEAD_DIM = 64
ROT_DIM = 16
ROPE_THETA = 500000.0
BRANCHES = ((128, 1), (512, 4), (2048, 16))
HALF_BAND = 64
CONV_W = 4
RG_LRU_C = 8.0
N_EXPERTS = 256
TOP_K = 8
N_GROUPS = 8
GROUP_SZ = N_EXPERTS // N_GROUPS
TOPK_GROUPS = 4
EXPERT_H = 256
ROUTED_SCALE = 2.5
LN_EPS = 1e-5
NEG = -1e30

LANES = 128
SUBLANES = 8
WORDS = D_MODEL // 2
CHUNKS = WORDS // LANES
VMEM_LIMIT = 56 * 1024 * 1024
DMA_THREADS = 2

TM_IN = 512
TQ = 128
KW = TQ + 2 * HALF_BAND
SB = TQ * max(d for _, d in BRANCHES)
MERGE_ROWS = 256
TS = 256
TM_MIX = 256
TM_SLOT = 512
TM_CMB = 128
BM = 256
PITCH = 8


def _cparams(sem):
    return pltpu.CompilerParams(dimension_semantics=sem, vmem_limit_bytes=VMEM_LIMIT)


def _inproj_body(x_ref, pos_ref, w_ref, invf_ref, q_ref, k_ref, v_ref, rx_ref, rg_ref):
    xb = x_ref[...].astype(bf16)
    ang = pos_ref[...].astype(f32) * invf_ref[...]
    cos = jnp.cos(ang)
    sin = jnp.sin(ang)
    j = lax.broadcasted_iota(i32, (1, LANES), 1) % HEAD_DIM
    half = ROT_DIM // 2
    s_lo = jnp.where(j < half, -sin, 0.0)
    s_hi = jnp.where((j >= half) & (j < ROT_DIM), sin, 0.0)
    rep = ATTN_W // LANES
    cos_w = jnp.concatenate([cos] * rep, axis=1)
    s_lo_w = jnp.concatenate([s_lo] * rep, axis=1)
    s_hi_w = jnp.concatenate([s_hi] * rep, axis=1)

    def proj(c0, n):
        return jnp.dot(xb, w_ref[:, c0:c0 + n], preferred_element_type=f32)

    def rope(t):
        return (t * cos_w + pltpu.roll(t, ATTN_W - half, 1) * s_lo_w
                + pltpu.roll(t, half, 1) * s_hi_w)

    q_ref[...] = rope(proj(0, ATTN_W)) * (HEAD_DIM ** -0.5)
    k_ref[...] = rope(proj(ATTN_W, ATTN_W))
    v_ref[...] = proj(2 * ATTN_W, ATTN_W)
    rx_ref[...] = proj(3 * ATTN_W, REC_W)
    rg_ref[...] = proj(3 * ATTN_W + REC_W, REC_W)


def _inproj(x2, pos2, w_in_b, invf):
    T = x2.shape[0]
    tm = TM_IN
    in_w = w_in_b.shape[1]
    tok = lambda i: (i, 0)
    fixed = lambda i: (0, 0)
    return pl.pallas_call(
        _inproj_body,
        grid=(T // tm,),
        in_specs=[pl.BlockSpec((tm, D_MODEL), tok), pl.BlockSpec((tm, 1), tok),
                  pl.BlockSpec((D_MODEL, in_w), fixed), pl.BlockSpec((1, LANES), fixed)],
        out_specs=[pl.BlockSpec((tm, ATTN_W), tok)] * 3 + [pl.BlockSpec((tm, REC_W), tok)] * 2,
        out_shape=[jax.ShapeDtypeStruct((T, ATTN_W), f32)] * 3
        + [jax.ShapeDtypeStruct((T, REC_W), f32)] * 2,
        compiler_params=_cparams(("parallel",)),
        name="inproj",
    )(x2, pos2, w_in_b, invf)


def _attn_body(q_ref, k_ref, v_ref, out_ref, o1, o2, o3, l1, l2, l3, *, S):
    o_sc, l_sc = (o1, o2, o3), (l1, l2, l3)
    lane = lax.broadcasted_iota(i32, (1, LANES), 1)
    head0 = lane < HEAD_DIM
    rel = (lax.broadcasted_iota(i32, (TQ, KW), 0) - lax.broadcasted_iota(i32, (TQ, KW), 1))
    tiles = SB // TQ

    def rows(ref, start, n, d):
        return ref[pl.ds(start, n), :] if d == 1 else ref[pl.ds(start, n, stride=d), :]

    def tile(ti, n0):
        ctx = []
        for g, (_, d) in enumerate(BRANCHES):
            L = S // d
            sh = d.bit_length() - 1
            r = jnp.bitwise_and(ti, d - 1)
            m0 = jnp.right_shift(n0, sh) + jnp.right_shift(ti, sh) * TQ
            ks = jnp.clip(m0 - HALF_BAND, 0, L - KW)
            q = rows(q_ref, r + d * m0, TQ, d).astype(bf16)
            k = rows(k_ref, r + d * ks, KW, d).astype(bf16)
            v = rows(v_ref, r + d * ks, KW, d).astype(bf16)
            valid = jnp.abs(rel + (m0 - ks)) <= HALF_BAND
            local = r + d * (m0 - jnp.right_shift(n0, sh))
            ss = []
            for sel in (head0, jnp.logical_not(head0)):
                qh = jnp.where(sel, q, jnp.zeros_like(q))
                ss.append(lax.dot_general(qh, k, (((1,), (1,)), ((), ())),
                                          preferred_element_type=f32))
            ctx.append((d, v, valid, local, ss))
        soft = []
        for d, v, valid, local, ss in ctx:
            ps = []
            for s in ss:
                s = jnp.where(valid, s, NEG)
                m = jnp.max(s, axis=-1, keepdims=True)
                p = jnp.exp(s - m)
                den = jnp.sum(p, axis=-1, keepdims=True)
                ps.append((p.astype(bf16), den, m + jnp.log(den)))
            soft.append(ps)
        for g, ((d, v, valid, local, ss), ps) in enumerate(zip(ctx, soft)):
            outs = [(jnp.dot(p, v, preferred_element_type=f32) / den, lse) for p, den, lse in ps]
            o_val = jnp.where(head0, outs[0][0], outs[1][0])
            l_val = jnp.where(head0, outs[0][1], outs[1][1])
            if d == 1:
                o_sc[g][pl.ds(local, TQ), :] = o_val
                l_sc[g][pl.ds(local, TQ), :] = l_val
            else:
                o_sc[g][pl.ds(local, TQ, stride=d), :] = o_val
                l_sc[g][pl.ds(local, TQ, stride=d), :] = l_val
        return n0

    def merge(c, n0):
        sl = pl.ds(pl.multiple_of(c * MERGE_ROWS, MERGE_ROWS), MERGE_ROWS)
        ls = [l_sc[g][sl, :] for g in range(len(BRANCHES))]
        mx = functools.reduce(jnp.maximum, ls)
        es = [jnp.exp(l - mx) for l in ls]
        num = functools.reduce(lambda a, b: a + b, [e * o_sc[g][sl, :] for g, e in enumerate(es)])
        den = functools.reduce(lambda a, b: a + b, es)
        dst = pl.ds(pl.multiple_of(n0 + c * MERGE_ROWS, MERGE_ROWS), MERGE_ROWS)
        out_ref[dst, :] = (num / den).astype(bf16)
        return n0

    def superblock(sb, carry):
        n0 = pl.multiple_of(sb * SB, SB)
        lax.fori_loop(0, tiles, tile, n0)
        lax.fori_loop(0, SB // MERGE_ROWS, merge, n0)
        return carry

    lax.fori_loop(0, S // SB, superblock, 0)


def _attention(q, k, v, B, S):
    assert S % SB == 0 and all(S // d >= KW for _, d in BRANCHES)
    hp = ATTN_W // LANES
    view = lambda t: t.reshape(B, S, ATTN_W)
    spec = pl.BlockSpec((None, S, LANES), lambda b, h: (b, 0, h))
    out = pl.pallas_call(
        functools.partial(_attn_body, S=S),
        grid=(B, hp),
        in_specs=[spec, spec, spec],
        out_specs=spec,
        out_shape=jax.ShapeDtypeStruct((B, S, ATTN_W), bf16),
        scratch_shapes=[pltpu.VMEM((SB, LANES), f32)] * (2 * len(BRANCHES)),
        compiler_params=_cparams(("parallel", "parallel")),
        name="attention",
    )(view(q), view(k), view(v))
    return out.reshape(B * S, ATTN_W)


def _rec_body(rx_ref, rg_ref, cw_ref, cb_ref, wg_ref, gb_ref, lam_ref, out_ref, rxp_ref, hf_ref, *, S):
    nch = S // TS
    ntile = TS // SUBLANES
    zeros8 = jnp.zeros((SUBLANES, LANES), f32)
    rxp_ref[pl.ds(0, SUBLANES), :] = zeros8
    rxp_ref[pl.ds(S + SUBLANES, SUBLANES), :] = zeros8

    def pad_copy(c, carry):
        t0 = pl.multiple_of(c * TS, TS)
        rxp_ref[pl.ds(t0 + SUBLANES, TS), :] = rx_ref[pl.ds(t0, TS), :]
        return carry

    lax.fori_loop(0, nch, pad_copy, 0)

    lam = lam_ref[...]
    neg_sp = -RG_LRU_C * (jnp.maximum(-lam, 0.0) + jnp.log1p(jnp.exp(-jnp.abs(lam))))
    cw = cw_ref[...]
    cb = cb_ref[...]
    sub = lax.broadcasted_iota(i32, (ntile, SUBLANES, LANES), 1)
    nrow = TS + 2 * SUBLANES

    def gates(ci, d):
        t0 = pl.multiple_of(ci * TS, TS)
        xw = rxp_ref[pl.ds(t0, nrow), :]
        u = (cw[0:1] * pltpu.roll(xw, 2, 0) + cw[1:2] * pltpu.roll(xw, 1, 0) + cw[2:3] * xw
             + cw[3:4] * pltpu.roll(xw, nrow - 1, 0))[SUBLANES:SUBLANES + TS] + cb
        c0 = d * 2 * LANES
        g = jnp.dot(u.astype(bf16), wg_ref[:, c0:c0 + 2 * LANES], preferred_element_type=f32)
        g = g + gb_ref[:, c0:c0 + 2 * LANES]
        r = jax.nn.sigmoid(g[:, :LANES])
        gi = jax.nn.sigmoid(g[:, LANES:])
        a = jnp.exp(neg_sp[d:d + 1] * r)
        b = jnp.sqrt(1.0 - a * a) * gi * u
        return a.reshape(ntile, SUBLANES, LANES), b.reshape(ntile, SUBLANES, LANES)

    def chunk(i, carry):
        cf, cbk = carry
        a, b = gates(i, 0)
        for s in (1, 2, 4):
            ok = sub >= s
            a_s = pltpu.roll(a, s, 1)
            b_s = pltpu.roll(b, s, 1)
            b = jnp.where(ok, a * b_s + b, b)
            a = jnp.where(ok, a * a_s, a)
        t0 = pl.multiple_of(i * TS, TS)
        for j in range(ntile):
            h = a[j] * cf + b[j]
            hf_ref[pl.ds(t0 + j * SUBLANES, SUBLANES), :] = h
            cf = h[SUBLANES - 1:SUBLANES, :]
        ib = nch - 1 - i
        a, b = gates(ib, 1)
        for s in (1, 2, 4):
            ok = sub < SUBLANES - s
            a_s = pltpu.roll(a, SUBLANES - s, 1)
            b_s = pltpu.roll(b, SUBLANES - s, 1)
            b = jnp.where(ok, a * b_s + b, b)
            a = jnp.where(ok, a * a_s, a)
        t0 = pl.multiple_of(ib * TS, TS)
        for j in range(ntile - 1, -1, -1):
            h = a[j] * cbk + b[j]
            out_ref[pl.ds(t0 + j * SUBLANES, SUBLANES), :] = h
            cbk = h[0:1, :]
        return cf, cbk

    zrow = jnp.zeros((1, LANES), f32)
    lax.fori_loop(0, nch, chunk, (zrow, zrow))

    def finish(c, carry):
        t0 = pl.multiple_of(c * TS, TS)
        sl = pl.ds(t0, TS)
        out_ref[sl, :] = (hf_ref[sl, :] + out_ref[sl, :]) * jax.nn.gelu(rg_ref[sl, :])
        return carry

    lax.fori_loop(0, nch, finish, 0)


def _rec(rx, rg, conv_w, conv_b, wg, gb, lam, B, S):
    ng = REC_W // LANES
    assert S % TS == 0
    seq = pl.BlockSpec((None, S, LANES), lambda b, c: (b, 0, c))
    return pl.pallas_call(
        functools.partial(_rec_body, S=S),
        grid=(B, ng),
        in_specs=[seq, seq,
                  pl.BlockSpec((CONV_W, LANES), lambda b, c: (0, c)),
                  pl.BlockSpec((1, LANES), lambda b, c: (0, c)),
                  pl.BlockSpec((None, LANES, 4 * LANES), lambda b, c: (c, 0, 0)),
                  pl.BlockSpec((None, 1, 4 * LANES), lambda b, c: (c, 0, 0)),
                  pl.BlockSpec((2, LANES), lambda b, c: (0, c))],
        out_specs=seq,
        out_shape=jax.ShapeDtypeStruct((B, S, REC_W), f32),
        scratch_shapes=[pltpu.VMEM((S + 2 * SUBLANES, LANES), f32), pltpu.VMEM((S, LANES), f32)],
        compiler_params=_cparams(("parallel", "parallel")),
        name="rec",
    )(rx.reshape(B, S, REC_W), rg.reshape(B, S, REC_W), conv_w, conv_b, wg, gb, lam)


u32 = jnp.uint32


def _pack_rows(x):
    bits = lax.bitcast_convert_type(x, u32)
    r = (bits + u32(0x7FFF) + ((bits >> 16) & u32(1))) >> 16
    return (r[:, WORDS:] << 16) | r[:, :WORDS]


def _unpack_rows(w):
    lo = lax.bitcast_convert_type(w << 16, f32).astype(bf16)
    hi = lax.bitcast_convert_type(w & u32(0xFFFF0000), f32).astype(bf16)
    return lo, hi


def _rows_to_buf(buf, val, rows):
    sp = rows + PITCH
    for j in range(CHUNKS):
        buf[pl.ds(j * sp, rows), :] = val[:, j * LANES:(j + 1) * LANES]


def _buf_to_rows(buf, rows):
    sp = rows + PITCH
    return jnp.concatenate([buf[pl.ds(j * sp, rows), :] for j in range(CHUNKS)], axis=1)


def _slab_ds(t, rows):
    return pl.ds(t, CHUNKS, stride=rows + PITCH)


def _rms(t, gain):
    return t * lax.rsqrt(jnp.mean(t * t, axis=-1, keepdims=True) + LN_EPS) * gain


def _layer_norm(z, g, b):
    mu = jnp.mean(z, axis=-1, keepdims=True)
    zc = z - mu
    var = jnp.mean(zc * zc, axis=-1, keepdims=True)
    return zc * lax.rsqrt(var + LN_EPS) * g + b


def _mixout_body(attn_ref, rec_ref, x_ref,
                 ag_ref, rgn_ref, wo_ref, g1_ref, b1_ref, rwh_ref, rwl_ref, rb_ref, tri_ref, ones_ref,
                 x1_ref, x1p_ref, idx_ref, gate_ref, rank_ref, cnt_ref, carry_ref, buf_ref, *, alpha):
    tm = TM_MIX

    @pl.when(pl.program_id(0) == 0)
    def _():
        carry_ref[...] = jnp.zeros_like(carry_ref)

    attn_n = _rms(attn_ref[...].astype(f32), ag_ref[...]).astype(bf16)
    rec_n = _rms(rec_ref[...], rgn_ref[...]).astype(bf16)
    y = (jnp.dot(attn_n, wo_ref[0:ATTN_W, :], preferred_element_type=f32)
         + jnp.dot(rec_n, wo_ref[ATTN_W:, :], preferred_element_type=f32))
    x1 = _layer_norm(alpha * x_ref[...] + y, g1_ref[...], b1_ref[...])
    x1_ref[...] = x1
    _rows_to_buf(buf_ref, _pack_rows(x1), tm)
    for t in range(tm):
        x1p_ref[t] = buf_ref[_slab_ds(t, tm), :]

    hi = x1.astype(bf16)
    lo = (x1 - hi.astype(f32)).astype(bf16)
    nt = (((1,), (1,)), ((), ()))
    logits = (lax.dot_general(rwh_ref[...], hi, nt, preferred_element_type=f32)
              + lax.dot_general(rwh_ref[...], lo, nt, preferred_element_type=f32)
              + lax.dot_general(rwl_ref[...], hi, nt, preferred_element_type=f32))
    scores = jax.nn.sigmoid(logits)
    biased = scores + jnp.concatenate([rb_ref[...]] * (tm // LANES), axis=1)

    rid = lax.broadcasted_iota(i32, (GROUP_SZ, tm), 0).astype(f32)
    grp = []
    for g in range(N_GROUPS):
        vg = biased[g * GROUP_SZ:(g + 1) * GROUP_SZ, :]
        m1 = jnp.max(vg, axis=0, keepdims=True)
        first = jnp.min(jnp.where(vg == m1, rid, float(GROUP_SZ)), axis=0, keepdims=True)
        m2 = jnp.max(jnp.where(rid == first, -jnp.inf, vg), axis=0, keepdims=True)
        grp.append(m1 + m2)
    eid = lax.broadcasted_iota(i32, (N_EXPERTS, tm), 0).astype(f32)
    egid = jnp.floor(eid * (1.0 / GROUP_SZ))
    masked = jnp.full((N_EXPERTS, tm), -jnp.inf, f32)
    for _ in range(TOPK_GROUPS):
        gm = functools.reduce(jnp.maximum, grp)
        gi = jnp.full((1, tm), float(N_GROUPS), f32)
        for g in range(N_GROUPS - 1, -1, -1):
            gi = jnp.where(grp[g] == gm, float(g), gi)
        grp = [jnp.where(gi == float(g), -jnp.inf, grp[g]) for g in range(N_GROUPS)]
        masked = jnp.where(egid == gi, biased, masked)

    onehot = jnp.zeros((N_EXPERTS, tm), f32)
    idxs, gts = [], []
    for _ in range(TOP_K):
        mx = jnp.max(masked, axis=0, keepdims=True)
        ix = jnp.min(jnp.where(masked == mx, eid, float(N_EXPERTS)), axis=0, keepdims=True)
        hit = eid == ix
        gts.append(jnp.sum(jnp.where(hit, scores, 0.0), axis=0, keepdims=True))
        idxs.append(ix)
        masked = jnp.where(hit, -jnp.inf, masked)
        onehot = onehot + jnp.where(hit, 1.0, 0.0)
    gsum = functools.reduce(lambda p, q: p + q, gts)
    for kk in range(TOP_K):
        idx_ref[kk:kk + 1, :] = idxs[kk].astype(i32)
        gate_ref[kk:kk + 1, :] = gts[kk] / gsum * ROUTED_SCALE

    oh = onehot.astype(bf16)
    before = carry_ref[...] + jnp.dot(oh, tri_ref[...], preferred_element_type=f32)
    for kk in range(TOP_K):
        rk = jnp.sum(jnp.where(eid == idxs[kk], before, 0.0), axis=0, keepdims=True)
        rank_ref[kk:kk + 1, :] = rk.astype(i32)
    total = carry_ref[...] + jnp.dot(oh, ones_ref[...], preferred_element_type=f32)
    carry_ref[...] = total
    cnt_ref[...] = total


def _mixout(attn, rec, x2, attn_gain, rec_gain, w_out_b, g1, b1, rw_hi, rw_lo, rbias, alpha):
    T = x2.shape[0]
    tm = TM_MIX
    tok = lambda i: (i, 0)
    fixed = lambda i: (0, 0)
    tri = jnp.asarray(np.triu(np.ones((tm, tm), np.float32), k=1), bf16)
    ones = jnp.ones((tm, tm), bf16)
    aw = pl.BlockSpec((tm, ATTN_W), tok)
    row = lambda n: pl.BlockSpec((1, n), fixed)
    kt = pl.BlockSpec((TOP_K, tm), lambda i: (0, i))
    return pl.pallas_call(
        functools.partial(_mixout_body, alpha=alpha),
        grid=(T // tm,),
        in_specs=[aw] * 2 + [pl.BlockSpec((tm, D_MODEL), tok), row(ATTN_W), row(REC_W),
                             pl.BlockSpec((D_MODEL, D_MODEL), fixed), row(D_MODEL), row(D_MODEL),
                             pl.BlockSpec((N_EXPERTS, D_MODEL), fixed),
                             pl.BlockSpec((N_EXPERTS, D_MODEL), fixed),
                             pl.BlockSpec((N_EXPERTS, LANES), fixed),
                             pl.BlockSpec((tm, tm), fixed), pl.BlockSpec((tm, tm), fixed)],
        out_specs=[pl.BlockSpec((tm, D_MODEL), tok),
                   pl.BlockSpec((tm, CHUNKS, LANES), lambda i: (i, 0, 0)),
                   kt, kt, kt, pl.BlockSpec((N_EXPERTS, tm), fixed)],
        out_shape=[jax.ShapeDtypeStruct((T, D_MODEL), f32),
                   jax.ShapeDtypeStruct((T, CHUNKS, LANES), u32),
                   jax.ShapeDtypeStruct((TOP_K, T), i32),
                   jax.ShapeDtypeStruct((TOP_K, T), f32),
                   jax.ShapeDtypeStruct((TOP_K, T), i32),
                   jax.ShapeDtypeStruct((N_EXPERTS, tm), f32)],
        scratch_shapes=[pltpu.VMEM((N_EXPERTS, tm), f32),
                        pltpu.VMEM((CHUNKS * (tm + PITCH), LANES), u32)],
        compiler_params=_cparams(("arbitrary",)),
        name="mixout",
    )(attn, rec, x2, attn_gain, rec_gain, w_out_b, g1, b1,
      rw_hi, rw_lo, rbias, tri, ones)


def _slots_body(idx_ref, rank_ref, start_ref, dest_ref):
    tm = TM_SLOT
    eid = lax.broadcasted_iota(i32, (N_EXPERTS, tm), 0)
    start = jnp.concatenate([start_ref[...]] * (tm // LANES), axis=1)
    for kk in range(TOP_K):
        hit = eid == idx_ref[kk:kk + 1, :]
        base = jnp.sum(jnp.where(hit, start, 0.0), axis=0, keepdims=True)
        dest_ref[kk:kk + 1, :] = base.astype(i32) + rank_ref[kk:kk + 1, :]


def _slots(idx, rank, pad_start):
    T = idx.shape[1]
    tm = TM_SLOT
    kt = pl.BlockSpec((TOP_K, tm), lambda i: (0, i))
    start = jnp.broadcast_to(pad_start.astype(f32).reshape(N_EXPERTS, 1), (N_EXPERTS, LANES))
    return pl.pallas_call(
        _slots_body,
        grid=(T // tm,),
        in_specs=[kt, kt, pl.BlockSpec((N_EXPERTS, LANES), lambda i: (0, 0))],
        out_specs=kt,
        out_shape=jax.ShapeDtypeStruct((TOP_K, T), i32),
        compiler_params=_cparams(("parallel",)),
        name="slots",
    )(idx, rank, start)


def _dispatch_body(dest_ref, x_ref, xs_ref, sem):
    tm = TM_MIX

    def copy(t, k):
        return pltpu.make_async_copy(x_ref.at[t], xs_ref.at[dest_ref[k, t]], sem)

    def issue(t, carry):
        for k in range(TOP_K):
            copy(t, k).start(priority=k % DMA_THREADS)
        return carry

    def drain(t, carry):
        for k in range(TOP_K):
            copy(t, k).wait()
        return carry

    lax.fori_loop(0, tm, issue, 0)
    lax.fori_loop(0, tm, drain, 0)


def _dispatch(dest, x1p, n_slots):
    T = x1p.shape[0]
    tm = TM_MIX
    return pl.pallas_call(
        _dispatch_body,
        grid=(T // tm,),
        in_specs=[pl.BlockSpec((TOP_K, tm), lambda i: (0, i), memory_space=pltpu.SMEM),
                  pl.BlockSpec((tm, CHUNKS, LANES), lambda i: (i, 0, 0))],
        out_specs=pl.BlockSpec(memory_space=pl.ANY),
        out_shape=jax.ShapeDtypeStruct((n_slots, CHUNKS, LANES), u32),
        scratch_shapes=[pltpu.SemaphoreType.DMA(())],
        compiler_params=_cparams(("arbitrary",)),
        name="dispatch",
    )(dest, x1p)


def _experts_body(be_ref, nv_ref, nb_ref, xs_ref, wg_ref, wu_ref, wd_ref, ys_ref,
                  buf_ref, wgb_ref, wub_ref, wdb_ref):
    i = pl.program_id(0)
    live_block = i < nb_ref[0]

    @pl.when(jnp.logical_and(live_block,
                             jnp.logical_or(i == 0, be_ref[i] != be_ref[jnp.maximum(i - 1, 0)])))
    def _():
        wgb_ref[...] = wg_ref[...].astype(bf16)
        wub_ref[...] = wu_ref[...].astype(bf16)
        wdb_ref[...] = wd_ref[...].astype(bf16)

    @pl.when(live_block)
    def _():
        for t in range(BM):
            buf_ref[_slab_ds(t, BM), :] = xs_ref[t]
        live = lax.broadcasted_iota(i32, (BM, WORDS), 0) < nv_ref[i]
        lo, hi = _unpack_rows(jnp.where(live, _buf_to_rows(buf_ref, BM), u32(0)))

        def up(w_ref):
            return (jnp.dot(lo, w_ref[:WORDS, :], preferred_element_type=f32)
                    + jnp.dot(hi, w_ref[WORDS:, :], preferred_element_type=f32))

        g = up(wgb_ref)
        u = up(wub_ref)
        h = (g * jax.nn.sigmoid(g) * u).astype(bf16)
        y = jnp.dot(h, wdb_ref[...], preferred_element_type=f32)
        _rows_to_buf(buf_ref, _pack_rows(y), BM)
        for t in range(BM):
            ys_ref[t] = buf_ref[_slab_ds(t, BM), :]


def _experts(block_e, block_rows, nb_used, xs, wg, wu, wd):
    P = xs.shape[0]
    nb = P // BM
    rows = lambda i, be, nv, nbu: (jnp.minimum(i, nbu[0] - 1), 0, 0)
    wsel = lambda i, be, nv, nbu: (be[i], 0, 0)
    gs = pltpu.PrefetchScalarGridSpec(
        num_scalar_prefetch=3,
        grid=(nb,),
        in_specs=[pl.BlockSpec((BM, CHUNKS, LANES), rows),
                  pl.BlockSpec((None, D_MODEL, EXPERT_H), wsel),
                  pl.BlockSpec((None, D_MODEL, EXPERT_H), wsel),
                  pl.BlockSpec((None, EXPERT_H, D_MODEL), wsel)],
        out_specs=pl.BlockSpec((BM, CHUNKS, LANES), rows),
        scratch_shapes=[pltpu.VMEM((CHUNKS * (BM + PITCH), LANES), u32),
                        pltpu.VMEM((D_MODEL, EXPERT_H), bf16),
                        pltpu.VMEM((D_MODEL, EXPERT_H), bf16),
                        pltpu.VMEM((EXPERT_H, D_MODEL), bf16)],
    )
    return pl.pallas_call(
        _experts_body,
        grid_spec=gs,
        out_shape=jax.ShapeDtypeStruct((P, CHUNKS, LANES), u32),
        compiler_params=_cparams(("arbitrary",)),
        name="experts",
    )(block_e, block_rows, nb_used, xs, wg, wu, wd)


def _combine_body(dest_ref, dnext_ref, gate_ref, x1_ref, ys_ref, sg_ref, su_ref, sd_ref,
                  g2_ref, b2_ref, out_ref, gbuf_ref, lo_ref, hi_ref, sems, *, alpha):
    tm = TM_CMB
    i = pl.program_id(0)
    n = pl.num_programs(0)
    slot = lax.rem(i, 2)

    def copy(d_ref, sl, t, k):
        return pltpu.make_async_copy(ys_ref.at[d_ref[k, t]], gbuf_ref.at[sl, k * tm + t],
                                     sems.at[sl])

    def issue(d_ref, sl):
        def step(t, carry):
            for k in range(TOP_K):
                copy(d_ref, sl, t, k).start(priority=k % DMA_THREADS)
            return carry
        lax.fori_loop(0, tm, step, 0)

    @pl.when(i == 0)
    def _():
        issue(dest_ref, slot)

    @pl.when(i + 1 < n)
    def _():
        issue(dnext_ref, 1 - slot)

    x1 = x1_ref[...]
    xb = x1.astype(bf16)
    sg = jnp.dot(xb, sg_ref[...], preferred_element_type=f32)
    su = jnp.dot(xb, su_ref[...], preferred_element_type=f32)
    shared = jnp.dot((sg * jax.nn.sigmoid(sg) * su).astype(bf16), sd_ref[...],
                     preferred_element_type=f32)

    def drain(t, carry):
        for k in range(TOP_K):
            copy(dest_ref, slot, t, k).wait()
        return carry

    lax.fori_loop(0, tm, drain, 0)

    def mix(t, carry):
        acc_lo = jnp.zeros((CHUNKS, LANES), f32)
        acc_hi = jnp.zeros((CHUNKS, LANES), f32)
        for k in range(TOP_K):
            w = gbuf_ref[slot, k * tm + t]
            gk = gate_ref[k, t]
            acc_lo = acc_lo + gk * lax.bitcast_convert_type(w << 16, f32)
            acc_hi = acc_hi + gk * lax.bitcast_convert_type(w & u32(0xFFFF0000), f32)
        lo_ref[_slab_ds(t, tm), :] = acc_lo
        hi_ref[_slab_ds(t, tm), :] = acc_hi
        return carry

    lax.fori_loop(0, tm, mix, 0)
    routed = jnp.concatenate([_buf_to_rows(lo_ref, tm), _buf_to_rows(hi_ref, tm)], axis=1)
    out_ref[...] = _layer_norm(alpha * x1 + (routed + shared), g2_ref[...], b2_ref[...])


def _combine(dest, gate, x1, ys, sg, su, sd, g2, b2, alpha):
    T = x1.shape[0]
    tm = TM_CMB
    n = T // tm
    tok = lambda i: (i, 0)
    fixed = lambda i: (0, 0)
    kt = pl.BlockSpec((TOP_K, tm), lambda i: (0, i), memory_space=pltpu.SMEM)
    kt_next = pl.BlockSpec((TOP_K, tm), lambda i: (0, jnp.minimum(i + 1, n - 1)),
                           memory_space=pltpu.SMEM)
    return pl.pallas_call(
        functools.partial(_combine_body, alpha=alpha),
        grid=(n,),
        in_specs=[kt, kt_next, kt, pl.BlockSpec((tm, D_MODEL), tok),
                  pl.BlockSpec(memory_space=pl.ANY),
                  pl.BlockSpec(sg.shape, fixed), pl.BlockSpec(su.shape, fixed),
                  pl.BlockSpec(sd.shape, fixed), pl.BlockSpec((1, D_MODEL), fixed),
                  pl.BlockSpec((1, D_MODEL), fixed)],
        out_specs=pl.BlockSpec((tm, D_MODEL), tok),
        out_shape=jax.ShapeDtypeStruct((T, D_MODEL), f32),
        scratch_shapes=[pltpu.VMEM((2, TOP_K * tm, CHUNKS, LANES), u32),
                        pltpu.VMEM((CHUNKS * (tm + PITCH), LANES), f32),
                        pltpu.VMEM((CHUNKS * (tm + PITCH), LANES), f32),
                        pltpu.SemaphoreType.DMA((2,))],
        compiler_params=_cparams(("arbitrary",)),
        name="combine",
    )(dest, dest, gate, x1, ys, sg, su, sd, g2, b2)


def _rope_inv_freq():
    half = ROT_DIM // 2
    inv = ROPE_THETA ** (-jnp.arange(half, dtype=f32) * 2.0 / ROT_DIM)
    j = np.arange(LANES) % HEAD_DIM
    table = jnp.where(j < ROT_DIM, inv[j % half], 0.0)
    return table.reshape(1, LANES).astype(f32)


def _gate_weights(ga_w, ga_b, gx_w, gx_b):
    ng = REC_W // LANES
    per = LANES // HEAD_DIM
    def bd(w):
        w = w.reshape(ng, per, HEAD_DIM, HEAD_DIM)
        z = jnp.zeros((ng, LANES, LANES), w.dtype)
        for p in range(per):
            z = z.at[:, p * HEAD_DIM:(p + 1) * HEAD_DIM, p * HEAD_DIM:(p + 1) * HEAD_DIM].set(w[:, p])
        return z
    wg = jnp.concatenate([bd(ga_w[0]), bd(gx_w[0]), bd(ga_w[1]), bd(gx_w[1])], axis=-1).astype(bf16)
    grp = lambda b: b.reshape(ng, 1, LANES)
    gb = jnp.concatenate([grp(ga_b[0]), grp(gx_b[0]), grp(ga_b[1]), grp(gx_b[1])], axis=-1)
    return wg, gb


def _layer(x, positions, w_in, attn_gain, conv_w, conv_b, ga_w, ga_b, gx_w, gx_b, lam, rec_gain,
           w_out, ln1_g, ln1_b, router_w, router_bias, e_wg, e_wu, e_wd, s_wg, s_wu, s_wd,
           ln2_g, ln2_b, alpha):
    B, S, _ = x.shape
    T = B * S
    x2 = x.reshape(T, D_MODEL)
    pos2 = positions.reshape(T, 1)

    q, k, v, rx, rg = _inproj(x2, pos2, w_in.astype(bf16), _rope_inv_freq())
    attn = _attention(q, k, v, B, S)
    wg, gb = _gate_weights(ga_w, ga_b, gx_w, gx_b)
    rec = _rec(rx, rg, conv_w, conv_b.reshape(1, REC_W), wg, gb, lam, B, S).reshape(T, REC_W)

    rw_t = router_w.T
    rw_hi = rw_t.astype(bf16)
    rw_lo = (rw_t - rw_hi.astype(f32)).astype(bf16)
    rbias = jnp.broadcast_to(router_bias.reshape(N_EXPERTS, 1), (N_EXPERTS, LANES))
    x1, x1p, idx, gate, rank, cnt = _mixout(
        attn, rec, x2,
        attn_gain.reshape(1, ATTN_W), rec_gain.reshape(1, REC_W), w_out.astype(bf16),
        ln1_g.reshape(1, D_MODEL), ln1_b.reshape(1, D_MODEL), rw_hi, rw_lo, rbias, alpha)

    counts = cnt[:, 0].astype(i32)
    padded = (counts + BM - 1) // BM * BM
    pad_end = jnp.cumsum(padded)
    pad_start = pad_end - padded
    dest = _slots(idx, rank, pad_start)
    nb = (T * TOP_K + N_EXPERTS * (BM - 1)) // BM + 1
    first_row = jnp.arange(nb, dtype=i32) * BM
    block_e = jnp.minimum(jnp.sum((pad_end[None, :] <= first_row[:, None]).astype(i32), axis=1),
                          N_EXPERTS - 1)
    block_rows = jnp.clip((pad_start + counts)[block_e] - first_row, 0, BM).astype(i32)
    nb_used = (pad_end[-1] // BM).astype(i32).reshape(1)

    xs = _dispatch(dest, x1p, nb * BM)
    ys = _experts(block_e, block_rows, nb_used, xs, e_wg, e_wu, e_wd)
    out = _combine(dest, gate, x1, ys, s_wg.astype(bf16), s_wu.astype(bf16), s_wd.astype(bf16),
                   ln2_g.reshape(1, D_MODEL), ln2_b.reshape(1, D_MODEL), alpha)
    return out.reshape(B, S, D_MODEL)


def kernel(x, positions, w_in, attn_gain, rec_conv_w, rec_conv_b, rec_gate_a_w, rec_gate_a_b,
           rec_gate_x_w, rec_gate_x_b, rec_lambda, rec_gain, w_out, ln1_g, ln1_b, router_w,
           router_bias, exp_w_gate, exp_w_up, exp_w_down, shared_w_gate, shared_w_up,
           shared_w_down, ln2_g, ln2_b):
    depth = w_in.shape[0]
    alpha = (2 * depth) ** 0.25
    for l in range(depth):
        x = _layer(x, positions, w_in[l], attn_gain[l], rec_conv_w[l], rec_conv_b[l],
                   rec_gate_a_w[l], rec_gate_a_b[l], rec_gate_x_w[l], rec_gate_x_b[l],
                   rec_lambda[l], rec_gain[l], w_out[l], ln1_g[l], ln1_b[l], router_w[l],
                   router_bias[l], exp_w_gate[l], exp_w_up[l], exp_w_down[l], shared_w_gate[l],
                   shared_w_up[l], shared_w_down[l], ln2_g[l], ln2_b[l], alpha)
    return x
```

```python
import functools
import math

import jax
import jax.numpy as jnp
import numpy as np
from jax import lax
from jax.experimental import pallas as pl
from jax.experimental.pallas import tpu as pltpu

f32 = jnp.float32
bf16 = jnp.bfloat16
i32 = jnp.int32

D_MODEL = 1024
ATTN_W = 512
REC_W = 512
HEAD_DIM = 64
ROT_DIM = 16
ROPE_THETA = 500000.0
BRANCHES = ((128, 1), (512, 4), (2048, 16))
HALF_BAND = 64
CONV_W = 4
RG_LRU_C = 8.0
N_EXPERTS = 256
TOP_K = 8
N_GROUPS = 8
GROUP_SZ = N_EXPERTS // N_GROUPS
TOPK_GROUPS = 4
EXPERT_H = 256
ROUTED_SCALE = 2.5
LN_EPS = 1e-5
NEG = -1e30

LANES = 128
SUBLANES = 8
WORDS = D_MODEL // 2
CHUNKS = WORDS // LANES
VMEM_LIMIT = 56 * 1024 * 1024
DMA_THREADS = 2

TM_IN = 512
TQ = 128
KW = TQ + 2 * HALF_BAND
SB = TQ * max(d for _, d in BRANCHES)
MERGE_ROWS = 256
TS = 256
TM_MIX = 256
TM_SLOT = 512
TM_CMB = 128
BM = 256
PITCH = 8


def _cparams(sem):
    return pltpu.CompilerParams(dimension_semantics=sem, vmem_limit_bytes=VMEM_LIMIT)


def _inproj_body(x_ref, pos_ref, w_ref, invf_ref, q_ref, k_ref, v_ref, rx_ref, rg_ref):
    xb = x_ref[...].astype(bf16)
    ang = pos_ref[...].astype(f32) * invf_ref[...]
    cos = jnp.cos(ang)
    sin = jnp.sin(ang)
    j = lax.broadcasted_iota(i32, (1, LANES), 1) % HEAD_DIM
    half = ROT_DIM // 2
    s_lo = jnp.where(j < half, -sin, 0.0)
    s_hi = jnp.where((j >= half) & (j < ROT_DIM), sin, 0.0)
    rep = ATTN_W // LANES
    cos_w = jnp.concatenate([cos] * rep, axis=1)
    s_lo_w = jnp.concatenate([s_lo] * rep, axis=1)
    s_hi_w = jnp.concatenate([s_hi] * rep, axis=1)

    def proj(c0, n):
        return jnp.dot(xb, w_ref[:, c0:c0 + n], preferred_element_type=f32)

    def rope(t):
        return (t * cos_w + pltpu.roll(t, ATTN_W - half, 1) * s_lo_w
                + pltpu.roll(t, half, 1) * s_hi_w)

    q_ref[...] = rope(proj(0, ATTN_W)) * (HEAD_DIM ** -0.5)
    k_ref[...] = rope(proj(ATTN_W, ATTN_W))
    v_ref[...] = proj(2 * ATTN_W, ATTN_W)
    rx_ref[...] = proj(3 * ATTN_W, REC_W)
    rg_ref[...] = proj(3 * ATTN_W + REC_W, REC_W)


def _inproj(x2, pos2, w_in_b, invf):
    T = x2.shape[0]
    tm = TM_IN
    in_w = w_in_b.shape[1]
    tok = lambda i: (i, 0)
    fixed = lambda i: (0, 0)
    return pl.pallas_call(
        _inproj_body,
        grid=(T // tm,),
        in_specs=[pl.BlockSpec((tm, D_MODEL), tok), pl.BlockSpec((tm, 1), tok),
                  pl.BlockSpec((D_MODEL, in_w), fixed), pl.BlockSpec((1, LANES), fixed)],
        out_specs=[pl.BlockSpec((tm, ATTN_W), tok)] * 3 + [pl.BlockSpec((tm, REC_W), tok)] * 2,
        out_shape=[jax.ShapeDtypeStruct((T, ATTN_W), f32)] * 3
        + [jax.ShapeDtypeStruct((T, REC_W), f32)] * 2,
        compiler_params=_cparams(("parallel",)),
        name="inproj",
    )(x2, pos2, w_in_b, invf)


def _attn_body(q_ref, k_ref, v_ref, out_ref, o1, o2, o3, l1, l2, l3, *, S):
    o_sc, l_sc = (o1, o2, o3), (l1, l2, l3)
    lane = lax.broadcasted_iota(i32, (1, LANES), 1)
    head0 = lane < HEAD_DIM
    rel = (lax.broadcasted_iota(i32, (TQ, KW), 0) - lax.broadcasted_iota(i32, (TQ, KW), 1))
    tiles = SB // TQ

    def rows(ref, start, n, d):
        return ref[pl.ds(start, n), :] if d == 1 else ref[pl.ds(start, n, stride=d), :]

    def tile(ti, n0):
        ctx = []
        for g, (_, d) in enumerate(BRANCHES):
            L = S // d
            sh = d.bit_length() - 1
            r = jnp.bitwise_and(ti, d - 1)
            m0 = jnp.right_shift(n0, sh) + jnp.right_shift(ti, sh) * TQ
            ks = jnp.clip(m0 - HALF_BAND, 0, L - KW)
            q = rows(q_ref, r + d * m0, TQ, d).astype(bf16)
            k = rows(k_ref, r + d * ks, KW, d).astype(bf16)
            v = rows(v_ref, r + d * ks, KW, d).astype(bf16)
            valid = jnp.abs(rel + (m0 - ks)) <= HALF_BAND
            local = r + d * (m0 - jnp.right_shift(n0, sh))
            ss = []
            for sel in (head0, jnp.logical_not(head0)):
                qh = jnp.where(sel, q, jnp.zeros_like(q))
                ss.append(lax.dot_general(qh, k, (((1,), (1,)), ((), ())),
                                          preferred_element_type=f32))
            ctx.append((d, v, valid, local, ss))
        soft = []
        for d, v, valid, local, ss in ctx:
            ps = []
            for s in ss:
                s = jnp.where(valid, s, NEG)
                m = jnp.max(s, axis=-1, keepdims=True)
                p = jnp.exp(s - m)
                den = jnp.sum(p, axis=-1, keepdims=True)
                ps.append((p.astype(bf16), den, m + jnp.log(den)))
            soft.append(ps)
        for g, ((d, v, valid, local, ss), ps) in enumerate(zip(ctx, soft)):
            outs = [(jnp.dot(p, v, preferred_element_type=f32) / den, lse) for p, den, lse in ps]
            o_val = jnp.where(head0, outs[0][0], outs[1][0])
            l_val = jnp.where(head0, outs[0][1], outs[1][1])
            if d == 1:
                o_sc[g][pl.ds(local, TQ), :] = o_val
                l_sc[g][pl.ds(local, TQ), :] = l_val
            else:
                o_sc[g][pl.ds(local, TQ, stride=d), :] = o_val
                l_sc[g][pl.ds(local, TQ, stride=d), :] = l_val
        return n0

    def merge(c, n0):
        sl = pl.ds(pl.multiple_of(c * MERGE_ROWS, MERGE_ROWS), MERGE_ROWS)
        ls = [l_sc[g][sl, :] for g in range(len(BRANCHES))]
        mx = functools.reduce(jnp.maximum, ls)
        es = [jnp.exp(l - mx) for l in ls]
        num = functools.reduce(lambda a, b: a + b, [e * o_sc[g][sl, :] for g, e in enumerate(es)])
        den = functools.reduce(lambda a, b: a + b, es)
        dst = pl.ds(pl.multiple_of(n0 + c * MERGE_ROWS, MERGE_ROWS), MERGE_ROWS)
        out_ref[dst, :] = (num / den).astype(bf16)
        return n0

    def superblock(sb, carry):
        n0 = pl.multiple_of(sb * SB, SB)
        lax.fori_loop(0, tiles, tile, n0)
        lax.fori_loop(0, SB // MERGE_ROWS, merge, n0)
        return carry

    lax.fori_loop(0, S // SB, superblock, 0)


def _attention(q, k, v, B, S):
    assert S % SB == 0 and all(S // d >= KW for _, d in BRANCHES)
    hp = ATTN_W // LANES
    view = lambda t: t.reshape(B, S, ATTN_W)
    spec = pl.BlockSpec((None, S, LANES), lambda b, h: (b, 0, h))
    out = pl.pallas_call(
        functools.partial(_attn_body, S=S),
        grid=(B, hp),
        in_specs=[spec, spec, spec],
        out_specs=spec,
        out_shape=jax.ShapeDtypeStruct((B, S, ATTN_W), bf16),
        scratch_shapes=[pltpu.VMEM((SB, LANES), f32)] * (2 * len(BRANCHES)),
        compiler_params=_cparams(("parallel", "parallel")),
        name="attention",
    )(view(q), view(k), view(v))
    return out.reshape(B * S, ATTN_W)


def _rec_body(rx_ref, rg_ref, cw_ref, cb_ref, wg_ref, gb_ref, lam_ref, out_ref, rxp_ref, hf_ref, *, S):
    nch = S // TS
    ntile = TS // SUBLANES
    zeros8 = jnp.zeros((SUBLANES, LANES), f32)
    rxp_ref[pl.ds(0, SUBLANES), :] = zeros8
    rxp_ref[pl.ds(S + SUBLANES, SUBLANES), :] = zeros8

    def pad_copy(c, carry):
        t0 = pl.multiple_of(c * TS, TS)
        rxp_ref[pl.ds(t0 + SUBLANES, TS), :] = rx_ref[pl.ds(t0, TS), :]
        return carry

    lax.fori_loop(0, nch, pad_copy, 0)

    lam = lam_ref[...]
    neg_sp = -RG_LRU_C * (jnp.maximum(-lam, 0.0) + jnp.log1p(jnp.exp(-jnp.abs(lam))))
    cw = cw_ref[...]
    cb = cb_ref[...]
    sub = lax.broadcasted_iota(i32, (ntile, SUBLANES, LANES), 1)
    nrow = TS + 2 * SUBLANES

    def gates(ci, d):
        t0 = pl.multiple_of(ci * TS, TS)
        xw = rxp_ref[pl.ds(t0, nrow), :]
        u = (cw[0:1] * pltpu.roll(xw, 2, 0) + cw[1:2] * pltpu.roll(xw, 1, 0) + cw[2:3] * xw
             + cw[3:4] * pltpu.roll(xw, nrow - 1, 0))[SUBLANES:SUBLANES + TS] + cb
        c0 = d * 2 * LANES
        g = jnp.dot(u.astype(bf16), wg_ref[:, c0:c0 + 2 * LANES], preferred_element_type=f32)
        g = g + gb_ref[:, c0:c0 + 2 * LANES]
        r = jax.nn.sigmoid(g[:, :LANES])
        gi = jax.nn.sigmoid(g[:, LANES:])
        a = jnp.exp(neg_sp[d:d + 1] * r)
        b = jnp.sqrt(1.0 - a * a) * gi * u
        return a.reshape(ntile, SUBLANES, LANES), b.reshape(ntile, SUBLANES, LANES)

    def chunk(i, carry):
        cf, cbk = carry
        a, b = gates(i, 0)
        for s in (1, 2, 4):
            ok = sub >= s
            a_s = pltpu.roll(a, s, 1)
            b_s = pltpu.roll(b, s, 1)
            b = jnp.where(ok, a * b_s + b, b)
            a = jnp.where(ok, a * a_s, a)
        t0 = pl.multiple_of(i * TS, TS)
        for j in range(ntile):
            h = a[j] * cf + b[j]
            hf_ref[pl.ds(t0 + j * SUBLANES, SUBLANES), :] = h
            cf = h[SUBLANES - 1:SUBLANES, :]
        ib = nch - 1 - i
        a, b = gates(ib, 1)
        for s in (1, 2, 4):
            ok = sub < SUBLANES - s
            a_s = pltpu.roll(a, SUBLANES - s, 1)
            b_s = pltpu.roll(b, SUBLANES - s, 1)
            b = jnp.where(ok, a * b_s + b, b)
            a = jnp.where(ok, a * a_s, a)
        t0 = pl.multiple_of(ib * TS, TS)
        for j in range(ntile - 1, -1, -1):
            h = a[j] * cbk + b[j]
            out_ref[pl.ds(t0 + j * SUBLANES, SUBLANES), :] = h
            cbk = h[0:1, :]
        return cf, cbk

    zrow = jnp.zeros((1, LANES), f32)
    lax.fori_loop(0, nch, chunk, (zrow, zrow))

    def finish(c, carry):
        t0 = pl.multiple_of(c * TS, TS)
        sl = pl.ds(t0, TS)
        out_ref[sl, :] = (hf_ref[sl, :] + out_ref[sl, :]) * jax.nn.gelu(rg_ref[sl, :])
        return carry

    lax.fori_loop(0, nch, finish, 0)


def _rec(rx, rg, conv_w, conv_b, wg, gb, lam, B, S):
    ng = REC_W // LANES
    assert S % TS == 0
    seq = pl.BlockSpec((None, S, LANES), lambda b, c: (b, 0, c))
    return pl.pallas_call(
        functools.partial(_rec_body, S=S),
        grid=(B, ng),
        in_specs=[seq, seq,
                  pl.BlockSpec((CONV_W, LANES), lambda b, c: (0, c)),
                  pl.BlockSpec((1, LANES), lambda b, c: (0, c)),
                  pl.BlockSpec((None, LANES, 4 * LANES), lambda b, c: (c, 0, 0)),
                  pl.BlockSpec((None, 1, 4 * LANES), lambda b, c: (c, 0, 0)),
                  pl.BlockSpec((2, LANES), lambda b, c: (0, c))],
        out_specs=seq,
        out_shape=jax.ShapeDtypeStruct((B, S, REC_W), f32),
        scratch_shapes=[pltpu.VMEM((S + 2 * SUBLANES, LANES), f32), pltpu.VMEM((S, LANES), f32)],
        compiler_params=_cparams(("parallel", "parallel")),
        name="rec",
    )(rx.reshape(B, S, REC_W), rg.reshape(B, S, REC_W), conv_w, conv_b, wg, gb, lam)


u32 = jnp.uint32


def _pack_rows(x):
    bits = lax.bitcast_convert_type(x, u32)
    r = (bits + u32(0x7FFF) + ((bits >> 16) & u32(1))) >> 16
    return (r[:, WORDS:] << 16) | r[:, :WORDS]


def _unpack_rows(w):
    lo = lax.bitcast_convert_type(w << 16, f32).astype(bf16)
    hi = lax.bitcast_convert_type(w & u32(0xFFFF0000), f32).astype(bf16)
    return lo, hi


def _rows_to_buf(buf, val, rows):
    sp = rows + PITCH
    for j in range(CHUNKS):
        buf[pl.ds(j * sp, rows), :] = val[:, j * LANES:(j + 1) * LANES]


def _buf_to_rows(buf, rows):
    sp = rows + PITCH
    return jnp.concatenate([buf[pl.ds(j * sp, rows), :] for j in range(CHUNKS)], axis=1)


def _slab_ds(t, rows):
    return pl.ds(t, CHUNKS, stride=rows + PITCH)


def _rms(t, gain):
    return t * lax.rsqrt(jnp.mean(t * t, axis=-1, keepdims=True) + LN_EPS) * gain


def _layer_norm(z, g, b):
    mu = jnp.mean(z, axis=-1, keepdims=True)
    zc = z - mu
    var = jnp.mean(zc * zc, axis=-1, keepdims=True)
    return zc * lax.rsqrt(var + LN_EPS) * g + b


def _mixout_body(attn_ref, rec_ref, x_ref,
                 ag_ref, rgn_ref, wo_ref, g1_ref, b1_ref, rwh_ref, rwl_ref, rb_ref, tri_ref, ones_ref,
                 x1_ref, x1p_ref, idx_ref, gate_ref, rank_ref, cnt_ref, carry_ref, buf_ref, *, alpha):
    tm = TM_MIX

    @pl.when(pl.program_id(0) == 0)
    def _():
        carry_ref[...] = jnp.zeros_like(carry_ref)

    attn_n = _rms(attn_ref[...].astype(f32), ag_ref[...]).astype(bf16)
    rec_n = _rms(rec_ref[...], rgn_ref[...]).astype(bf16)
    y = (jnp.dot(attn_n, wo_ref[0:ATTN_W, :], preferred_element_type=f32)
         + jnp.dot(rec_n, wo_ref[ATTN_W:, :], preferred_element_type=f32))
    x1 = _layer_norm(alpha * x_ref[...] + y, g1_ref[...], b1_ref[...])
    x1_ref[...] = x1
    _rows_to_buf(buf_ref, _pack_rows(x1), tm)
    for t in range(tm):
        x1p_ref[t] = buf_ref[_slab_ds(t, tm), :]

    hi = x1.astype(bf16)
    lo = (x1 - hi.astype(f32)).astype(bf16)
    nt = (((1,), (1,)), ((), ()))
    logits = (lax.dot_general(rwh_ref[...], hi, nt, preferred_element_type=f32)
              + lax.dot_general(rwh_ref[...], lo, nt, preferred_element_type=f32)
              + lax.dot_general(rwl_ref[...], hi, nt, preferred_element_type=f32))
    scores = jax.nn.sigmoid(logits)
    biased = scores + jnp.concatenate([rb_ref[...]] * (tm // LANES), axis=1)

    rid = lax.broadcasted_iota(i32, (GROUP_SZ, tm), 0).astype(f32)
    grp = []
    for g in range(N_GROUPS):
        vg = biased[g * GROUP_SZ:(g + 1) * GROUP_SZ, :]
        m1 = jnp.max(vg, axis=0, keepdims=True)
        first = jnp.min(jnp.where(vg == m1, rid, float(GROUP_SZ)), axis=0, keepdims=True)
        m2 = jnp.max(jnp.where(rid == first, -jnp.inf, vg), axis=0, keepdims=True)
        grp.append(m1 + m2)
    eid = lax.broadcasted_iota(i32, (N_EXPERTS, tm), 0).astype(f32)
    egid = jnp.floor(eid * (1.0 / GROUP_SZ))
    masked = jnp.full((N_EXPERTS, tm), -jnp.inf, f32)
    for _ in range(TOPK_GROUPS):
        gm = functools.reduce(jnp.maximum, grp)
        gi = jnp.full((1, tm), float(N_GROUPS), f32)
        for g in range(N_GROUPS - 1, -1, -1):
            gi = jnp.where(grp[g] == gm, float(g), gi)
        grp = [jnp.where(gi == float(g), -jnp.inf, grp[g]) for g in range(N_GROUPS)]
        masked = jnp.where(egid == gi, biased, masked)

    onehot = jnp.zeros((N_EXPERTS, tm), f32)
    idxs, gts = [], []
    for _ in range(TOP_K):
        mx = jnp.max(masked, axis=0, keepdims=True)
        ix = jnp.min(jnp.where(masked == mx, eid, float(N_EXPERTS)), axis=0, keepdims=True)
        hit = eid == ix
        gts.append(jnp.sum(jnp.where(hit, scores, 0.0), axis=0, keepdims=True))
        idxs.append(ix)
        masked = jnp.where(hit, -jnp.inf, masked)
        onehot = onehot + jnp.where(hit, 1.0, 0.0)
    gsum = functools.reduce(lambda p, q: p + q, gts)
    for kk in range(TOP_K):
        idx_ref[kk:kk + 1, :] = idxs[kk].astype(i32)
        gate_ref[kk:kk + 1, :] = gts[kk] / gsum * ROUTED_SCALE

    oh = onehot.astype(bf16)
    before = carry_ref[...] + jnp.dot(oh, tri_ref[...], preferred_element_type=f32)
    for kk in range(TOP_K):
        rk = jnp.sum(jnp.where(eid == idxs[kk], before, 0.0), axis=0, keepdims=True)
        rank_ref[kk:kk + 1, :] = rk.astype(i32)
    total = carry_ref[...] + jnp.dot(oh, ones_ref[...], preferred_element_type=f32)
    carry_ref[...] = total
    cnt_ref[...] = total


def _mixout(attn, rec, x2, attn_gain, rec_gain, w_out_b, g1, b1, rw_hi, rw_lo, rbias, alpha):
    T = x2.shape[0]
    tm = TM_MIX
    tok = lambda i: (i, 0)
    fixed = lambda i: (0, 0)
    tri = jnp.asarray(np.triu(np.ones((tm, tm), np.float32), k=1), bf16)
    ones = jnp.ones((tm, tm), bf16)
    aw = pl.BlockSpec((tm, ATTN_W), tok)
    row = lambda n: pl.BlockSpec((1, n), fixed)
    kt = pl.BlockSpec((TOP_K, tm), lambda i: (0, i))
    return pl.pallas_call(
        functools.partial(_mixout_body, alpha=alpha),
        grid=(T // tm,),
        in_specs=[aw] * 2 + [pl.BlockSpec((tm, D_MODEL), tok), row(ATTN_W), row(REC_W),
                             pl.BlockSpec((D_MODEL, D_MODEL), fixed), row(D_MODEL), row(D_MODEL),
                             pl.BlockSpec((N_EXPERTS, D_MODEL), fixed),
                             pl.BlockSpec((N_EXPERTS, D_MODEL), fixed),
                             pl.BlockSpec((N_EXPERTS, LANES), fixed),
                             pl.BlockSpec((tm, tm), fixed), pl.BlockSpec((tm, tm), fixed)],
        out_specs=[pl.BlockSpec((tm, D_MODEL), tok),
                   pl.BlockSpec((tm, CHUNKS, LANES), lambda i: (i, 0, 0)),
                   kt, kt, kt, pl.BlockSpec((N_EXPERTS, tm), fixed)],
        out_shape=[jax.ShapeDtypeStruct((T, D_MODEL), f32),
                   jax.ShapeDtypeStruct((T, CHUNKS, LANES), u32),
                   jax.ShapeDtypeStruct((TOP_K, T), i32),
                   jax.ShapeDtypeStruct((TOP_K, T), f32),
                   jax.ShapeDtypeStruct((TOP_K, T), i32),
                   jax.ShapeDtypeStruct((N_EXPERTS, tm), f32)],
        scratch_shapes=[pltpu.VMEM((N_EXPERTS, tm), f32),
                        pltpu.VMEM((CHUNKS * (tm + PITCH), LANES), u32)],
        compiler_params=_cparams(("arbitrary",)),
        name="mixout",
    )(attn, rec, x2, attn_gain, rec_gain, w_out_b, g1, b1,
      rw_hi, rw_lo, rbias, tri, ones)


def _slots_body(idx_ref, rank_ref, start_ref, dest_ref):
    tm = TM_SLOT
    eid = lax.broadcasted_iota(i32, (N_EXPERTS, tm), 0)
    start = jnp.concatenate([start_ref[...]] * (tm // LANES), axis=1)
    for kk in range(TOP_K):
        hit = eid == idx_ref[kk:kk + 1, :]
        base = jnp.sum(jnp.where(hit, start, 0.0), axis=0, keepdims=True)
        dest_ref[kk:kk + 1, :] = base.astype(i32) + rank_ref[kk:kk + 1, :]


def _slots(idx, rank, pad_start):
    T = idx.shape[1]
    tm = TM_SLOT
    kt = pl.BlockSpec((TOP_K, tm), lambda i: (0, i))
    start = jnp.broadcast_to(pad_start.astype(f32).reshape(N_EXPERTS, 1), (N_EXPERTS, LANES))
    return pl.pallas_call(
        _slots_body,
        grid=(T // tm,),
        in_specs=[kt, kt, pl.BlockSpec((N_EXPERTS, LANES), lambda i: (0, 0))],
        out_specs=kt,
        out_shape=jax.ShapeDtypeStruct((TOP_K, T), i32),
        compiler_params=_cparams(("parallel",)),
        name="slots",
    )(idx, rank, start)


def _dispatch_body(dest_ref, x_ref, xs_ref, sem):
    tm = TM_MIX

    def copy(t, k):
        return pltpu.make_async_copy(x_ref.at[t], xs_ref.at[dest_ref[k, t]], sem)

    def issue(t, carry):
        for k in range(TOP_K):
            copy(t, k).start(priority=k % DMA_THREADS)
        return carry

    def drain(t, carry):
        for k in range(TOP_K):
            copy(t, k).wait()
        return carry

    lax.fori_loop(0, tm, issue, 0)
    lax.fori_loop(0, tm, drain, 0)


def _dispatch(dest, x1p, n_slots):
    T = x1p.shape[0]
    tm = TM_MIX
    return pl.pallas_call(
        _dispatch_body,
        grid=(T // tm,),
        in_specs=[pl.BlockSpec((TOP_K, tm), lambda i: (0, i), memory_space=pltpu.SMEM),
                  pl.BlockSpec((tm, CHUNKS, LANES), lambda i: (i, 0, 0))],
        out_specs=pl.BlockSpec(memory_space=pl.ANY),
        out_shape=jax.ShapeDtypeStruct((n_slots, CHUNKS, LANES), u32),
        scratch_shapes=[pltpu.SemaphoreType.DMA(())],
        compiler_params=_cparams(("arbitrary",)),
        name="dispatch",
    )(dest, x1p)


def _experts_body(be_ref, nv_ref, nb_ref, xs_ref, wg_ref, wu_ref, wd_ref, ys_ref,
                  buf_ref, wgb_ref, wub_ref, wdb_ref):
    i = pl.program_id(0)
    live_block = i < nb_ref[0]

    @pl.when(jnp.logical_and(live_block,
                             jnp.logical_or(i == 0, be_ref[i] != be_ref[jnp.maximum(i - 1, 0)])))
    def _():
        wgb_ref[...] = wg_ref[...].astype(bf16)
        wub_ref[...] = wu_ref[...].astype(bf16)
        wdb_ref[...] = wd_ref[...].astype(bf16)

    @pl.when(live_block)
    def _():
        for t in range(BM):
            buf_ref[_slab_ds(t, BM), :] = xs_ref[t]
        live = lax.broadcasted_iota(i32, (BM, WORDS), 0) < nv_ref[i]
        lo, hi = _unpack_rows(jnp.where(live, _buf_to_rows(buf_ref, BM), u32(0)))

        def up(w_ref):
            return (jnp.dot(lo, w_ref[:WORDS, :], preferred_element_type=f32)
                    + jnp.dot(hi, w_ref[WORDS:, :], preferred_element_type=f32))

        g = up(wgb_ref)
        u = up(wub_ref)
        h = (g * jax.nn.sigmoid(g) * u).astype(bf16)
        y = jnp.dot(h, wdb_ref[...], preferred_element_type=f32)
        _rows_to_buf(buf_ref, _pack_rows(y), BM)
        for t in range(BM):
            ys_ref[t] = buf_ref[_slab_ds(t, BM), :]


def _experts(block_e, block_rows, nb_used, xs, wg, wu, wd):
    P = xs.shape[0]
    nb = P // BM
    rows = lambda i, be, nv, nbu: (jnp.minimum(i, nbu[0] - 1), 0, 0)
    wsel = lambda i, be, nv, nbu: (be[i], 0, 0)
    gs = pltpu.PrefetchScalarGridSpec(
        num_scalar_prefetch=3,
        grid=(nb,),
        in_specs=[pl.BlockSpec((BM, CHUNKS, LANES), rows),
                  pl.BlockSpec((None, D_MODEL, EXPERT_H), wsel),
                  pl.BlockSpec((None, D_MODEL, EXPERT_H), wsel),
                  pl.BlockSpec((None, EXPERT_H, D_MODEL), wsel)],
        out_specs=pl.BlockSpec((BM, CHUNKS, LANES), rows),
        scratch_shapes=[pltpu.VMEM((CHUNKS * (BM + PITCH), LANES), u32),
                        pltpu.VMEM((D_MODEL, EXPERT_H), bf16),
                        pltpu.VMEM((D_MODEL, EXPERT_H), bf16),
                        pltpu.VMEM((EXPERT_H, D_MODEL), bf16)],
    )
    return pl.pallas_call(
        _experts_body,
        grid_spec=gs,
        out_shape=jax.ShapeDtypeStruct((P, CHUNKS, LANES), u32),
        compiler_params=_cparams(("arbitrary",)),
        name="experts",
    )(block_e, block_rows, nb_used, xs, wg, wu, wd)


def _combine_body(dest_ref, dnext_ref, gate_ref, x1_ref, ys_ref, sg_ref, su_ref, sd_ref,
                  g2_ref, b2_ref, out_ref, gbuf_ref, lo_ref, hi_ref, sems, *, alpha):
    tm = TM_CMB
    j = pl.program_id(0)
    n = pl.num_programs(0)

    def copy(d_ref, base, sl, t, k):
        src = ys_ref.at[d_ref[(base + t) * TOP_K + k]]
        return pltpu.make_async_copy(src, gbuf_ref.at[sl, k * tm + t], sems.at[sl])

    def issue(d_ref, base, sl):
        def step(t, carry):
            for k in range(TOP_K):
                copy(d_ref, base, sl, t, k).start(priority=k % DMA_THREADS)
            return carry
        lax.fori_loop(0, tm, step, 0)

    def finish(half, sl):
        base = half * tm
        rows = pl.ds(base, tm)
        x1 = x1_ref[rows, :]
        xb = x1.astype(bf16)
        sg = jnp.dot(xb, sg_ref[...], preferred_element_type=f32)
        su = jnp.dot(xb, su_ref[...], preferred_element_type=f32)
        shared = jnp.dot((sg * jax.nn.sigmoid(sg) * su).astype(bf16), sd_ref[...],
                         preferred_element_type=f32)

        def drain(t, carry):
            for k in range(TOP_K):
                copy(dest_ref, base, sl, t, k).wait()
            return carry

        lax.fori_loop(0, tm, drain, 0)

        def mix(t, carry):
            acc_lo = jnp.zeros((CHUNKS, LANES), f32)
            acc_hi = jnp.zeros((CHUNKS, LANES), f32)
            for k in range(TOP_K):
                w = gbuf_ref[sl, k * tm + t]
                gk = gate_ref[(base + t) * TOP_K + k]
                acc_lo = acc_lo + gk * lax.bitcast_convert_type(w << 16, f32)
                acc_hi = acc_hi + gk * lax.bitcast_convert_type(w & u32(0xFFFF0000), f32)
            lo_ref[_slab_ds(t, tm), :] = acc_lo
            hi_ref[_slab_ds(t, tm), :] = acc_hi
            return carry

        lax.fori_loop(0, tm, mix, 0)
        routed = jnp.concatenate([_buf_to_rows(lo_ref, tm), _buf_to_rows(hi_ref, tm)], axis=1)
        out_ref[rows, :] = _layer_norm(alpha * x1 + (routed + shared), g2_ref[...], b2_ref[...])

    @pl.when(j == 0)
    def _():
        issue(dest_ref, 0, 0)

    issue(dest_ref, tm, 1)
    finish(0, 0)

    @pl.when(j + 1 < n)
    def _():
        issue(dnext_ref, 0, 0)

    finish(1, 1)


def _combine(dest, gate, x1, ys, sg, su, sd, g2, b2, alpha):
    T = x1.shape[0]
    tm = TM_CMB
    n = T // (2 * tm)
    tok = lambda j: (j, 0)
    fixed = lambda j: (0, 0)
    tbl = pl.BlockSpec((2 * tm * TOP_K,), lambda j: (j,), memory_space=pltpu.SMEM)
    tbl_next = pl.BlockSpec((2 * tm * TOP_K,), lambda j: (jnp.minimum(j + 1, n - 1),),
                            memory_space=pltpu.SMEM)
    return pl.pallas_call(
        functools.partial(_combine_body, alpha=alpha),
        grid=(n,),
        in_specs=[tbl, tbl_next, tbl, pl.BlockSpec((2 * tm, D_MODEL), tok),
                  pl.BlockSpec(memory_space=pl.ANY),
                  pl.BlockSpec(sg.shape, fixed), pl.BlockSpec(su.shape, fixed),
                  pl.BlockSpec(sd.shape, fixed), pl.BlockSpec((1, D_MODEL), fixed),
                  pl.BlockSpec((1, D_MODEL), fixed)],
        out_specs=pl.BlockSpec((2 * tm, D_MODEL), tok),
        out_shape=jax.ShapeDtypeStruct((T, D_MODEL), f32),
        scratch_shapes=[pltpu.VMEM((2, TOP_K * tm, CHUNKS, LANES), u32),
                        pltpu.VMEM((CHUNKS * (tm + PITCH), LANES), f32),
                        pltpu.VMEM((CHUNKS * (tm + PITCH), LANES), f32),
                        pltpu.SemaphoreType.DMA((2,))],
        compiler_params=_cparams(("arbitrary",)),
        name="combine",
    )(dest, dest, gate, x1, ys, sg, su, sd, g2, b2)


def _rope_inv_freq():
    half = ROT_DIM // 2
    inv = ROPE_THETA ** (-jnp.arange(half, dtype=f32) * 2.0 / ROT_DIM)
    j = np.arange(LANES) % HEAD_DIM
    table = jnp.where(j < ROT_DIM, inv[j % half], 0.0)
    return table.reshape(1, LANES).astype(f32)


def _gate_weights(ga_w, ga_b, gx_w, gx_b):
    ng = REC_W // LANES
    per = LANES // HEAD_DIM
    def bd(w):
        w = w.reshape(ng, per, HEAD_DIM, HEAD_DIM)
        z = jnp.zeros((ng, LANES, LANES), w.dtype)
        for p in range(per):
            z = z.at[:, p * HEAD_DIM:(p + 1) * HEAD_DIM, p * HEAD_DIM:(p + 1) * HEAD_DIM].set(w[:, p])
        return z
    wg = jnp.concatenate([bd(ga_w[0]), bd(gx_w[0]), bd(ga_w[1]), bd(gx_w[1])], axis=-1).astype(bf16)
    grp = lambda b: b.reshape(ng, 1, LANES)
    gb = jnp.concatenate([grp(ga_b[0]), grp(gx_b[0]), grp(ga_b[1]), grp(gx_b[1])], axis=-1)
    return wg, gb


def _layer(x, positions, w_in, attn_gain, conv_w, conv_b, ga_w, ga_b, gx_w, gx_b, lam, rec_gain,
           w_out, ln1_g, ln1_b, router_w, router_bias, e_wg, e_wu, e_wd, s_wg, s_wu, s_wd,
           ln2_g, ln2_b, alpha):
    B, S, _ = x.shape
    T = B * S
    x2 = x.reshape(T, D_MODEL)
    pos2 = positions.reshape(T, 1)

    q, k, v, rx, rg = _inproj(x2, pos2, w_in.astype(bf16), _rope_inv_freq())
    attn = _attention(q, k, v, B, S)
    wg, gb = _gate_weights(ga_w, ga_b, gx_w, gx_b)
    rec = _rec(rx, rg, conv_w, conv_b.reshape(1, REC_W), wg, gb, lam, B, S).reshape(T, REC_W)

    rw_t = router_w.T
    rw_hi = rw_t.astype(bf16)
    rw_lo = (rw_t - rw_hi.astype(f32)).astype(bf16)
    rbias = jnp.broadcast_to(router_bias.reshape(N_EXPERTS, 1), (N_EXPERTS, LANES))
    x1, x1p, idx, gate, rank, cnt = _mixout(
        attn, rec, x2,
        attn_gain.reshape(1, ATTN_W), rec_gain.reshape(1, REC_W), w_out.astype(bf16),
        ln1_g.reshape(1, D_MODEL), ln1_b.reshape(1, D_MODEL), rw_hi, rw_lo, rbias, alpha)

    counts = cnt[:, 0].astype(i32)
    padded = (counts + BM - 1) // BM * BM
    pad_end = jnp.cumsum(padded)
    pad_start = pad_end - padded
    dest = _slots(idx, rank, pad_start)
    nb = (T * TOP_K + N_EXPERTS * (BM - 1)) // BM + 1
    first_row = jnp.arange(nb, dtype=i32) * BM
    block_e = jnp.minimum(jnp.sum((pad_end[None, :] <= first_row[:, None]).astype(i32), axis=1),
                          N_EXPERTS - 1)
    block_rows = jnp.clip((pad_start + counts)[block_e] - first_row, 0, BM).astype(i32)
    nb_used = (pad_end[-1] // BM).astype(i32).reshape(1)

    xs = _dispatch(dest, x1p, nb * BM)
    ys = _experts(block_e, block_rows, nb_used, xs, e_wg, e_wu, e_wd)
    out = _combine(dest.T.reshape(-1), gate.T.reshape(-1), x1, ys, s_wg.astype(bf16), s_wu.astype(bf16), s_wd.astype(bf16),
                   ln2_g.reshape(1, D_MODEL), ln2_b.reshape(1, D_MODEL), alpha)
    return out.reshape(B, S, D_MODEL)


def kernel(x, positions, w_in, attn_gain, rec_conv_w, rec_conv_b, rec_gate_a_w, rec_gate_a_b,
           rec_gate_x_w, rec_gate_x_b, rec_lambda, rec_gain, w_out, ln1_g, ln1_b, router_w,
           router_bias, exp_w_gate, exp_w_up, exp_w_down, shared_w_gate, shared_w_up,
           shared_w_down, ln2_g, ln2_b):
    depth = w_in.shape[0]
    alpha = (2 * depth) ** 0.25
    for l in range(depth):
        x = _layer(x, positions, w_in[l], attn_gain[l], rec_conv_w[l], rec_conv_b[l],
                   rec_gate_a_w[l], rec_gate_a_b[l], rec_gate_x_w[l], rec_gate_x_b[l],
                   rec_lambda[l], rec_gain[l], w_out[l], ln1_g[l], ln1_b[l], router_w[l],
                   router_bias[l], exp_w_gate[l], exp_w_up[l], exp_w_down[l], shared_w_gate[l],
                   shared_w_up[l], shared_w_down[l], ln2_g[l], ln2_b[l], alpha)
    return x
```

```python
import functools
import math

import jax
import jax.numpy as jnp
import numpy as np
from jax import lax
from jax.experimental import pallas as pl
from jax.experimental.pallas import tpu as pltpu

f32 = jnp.float32
bf16 = jnp.bfloat16
i32 = jnp.int32

D_MODEL = 1024
ATTN_W = 512
REC_W = 512
HEAD_DIM = 64
ROT_DIM = 16
ROPE_THETA = 500000.0
BRANCHES = ((128, 1), (512, 4), (2048, 16))
HALF_BAND = 64
CONV_W = 4
RG_LRU_C = 8.0
N_EXPERTS = 256
TOP_K = 8
N_GROUPS = 8
GROUP_SZ = N_EXPERTS // N_GROUPS
TOPK_GROUPS = 4
EXPERT_H = 256
ROUTED_SCALE = 2.5
LN_EPS = 1e-5
NEG = -1e30

LANES = 128
SUBLANES = 8
WORDS = D_MODEL // 2
CHUNKS = WORDS // LANES
VMEM_LIMIT = 56 * 1024 * 1024
DMA_THREADS = 2

TM_IN = 512
TQ = 128
KW = TQ + 2 * HALF_BAND
SB = TQ * max(d for _, d in BRANCHES)
MERGE_ROWS = 256
TS = 256
TM_MIX = 256
TM_SLOT = 512
TM_CMB = 128
BM = 256


def _cparams(sem):
    return pltpu.CompilerParams(dimension_semantics=sem, vmem_limit_bytes=VMEM_LIMIT)


def _inproj_body(x_ref, pos_ref, w_ref, invf_ref, q_ref, k_ref, v_ref, rx_ref, rg_ref):
    xb = x_ref[...].astype(bf16)
    ang = pos_ref[...].astype(f32) * invf_ref[...]
    cos = jnp.cos(ang)
    sin = jnp.sin(ang)
    j = lax.broadcasted_iota(i32, (1, LANES), 1) % HEAD_DIM
    half = ROT_DIM // 2
    s_lo = jnp.where(j < half, -sin, 0.0)
    s_hi = jnp.where((j >= half) & (j < ROT_DIM), sin, 0.0)
    rep = ATTN_W // LANES
    cos_w = jnp.concatenate([cos] * rep, axis=1)
    s_lo_w = jnp.concatenate([s_lo] * rep, axis=1)
    s_hi_w = jnp.concatenate([s_hi] * rep, axis=1)

    def proj(c0, n):
        return jnp.dot(xb, w_ref[:, c0:c0 + n], preferred_element_type=f32)

    def rope(t):
        return (t * cos_w + pltpu.roll(t, ATTN_W - half, 1) * s_lo_w
                + pltpu.roll(t, half, 1) * s_hi_w)

    q_ref[...] = rope(proj(0, ATTN_W)) * (HEAD_DIM ** -0.5)
    k_ref[...] = rope(proj(ATTN_W, ATTN_W))
    v_ref[...] = proj(2 * ATTN_W, ATTN_W)
    rx_ref[...] = proj(3 * ATTN_W, REC_W)
    rg_ref[...] = proj(3 * ATTN_W + REC_W, REC_W)


def _inproj(x2, pos2, w_in_b, invf):
    T = x2.shape[0]
    tm = TM_IN
    in_w = w_in_b.shape[1]
    tok = lambda i: (i, 0)
    fixed = lambda i: (0, 0)
    return pl.pallas_call(
        _inproj_body,
        grid=(T // tm,),
        in_specs=[pl.BlockSpec((tm, D_MODEL), tok), pl.BlockSpec((tm, 1), tok),
                  pl.BlockSpec((D_MODEL, in_w), fixed), pl.BlockSpec((1, LANES), fixed)],
        out_specs=[pl.BlockSpec((tm, ATTN_W), tok)] * 3 + [pl.BlockSpec((tm, REC_W), tok)] * 2,
        out_shape=[jax.ShapeDtypeStruct((T, ATTN_W), f32)] * 3
        + [jax.ShapeDtypeStruct((T, REC_W), f32)] * 2,
        compiler_params=_cparams(("parallel",)),
        name="inproj",
    )(x2, pos2, w_in_b, invf)


def _attn_body(q_ref, k_ref, v_ref, out_ref, o1, o2, o3, l1, l2, l3, *, S):
    o_sc, l_sc = (o1, o2, o3), (l1, l2, l3)
    lane = lax.broadcasted_iota(i32, (1, LANES), 1)
    head0 = lane < HEAD_DIM
    rel = (lax.broadcasted_iota(i32, (TQ, KW), 0) - lax.broadcasted_iota(i32, (TQ, KW), 1))
    tiles = SB // TQ

    def rows(ref, start, n, d):
        return ref[pl.ds(start, n), :] if d == 1 else ref[pl.ds(start, n, stride=d), :]

    def tile(ti, n0):
        ctx = []
        for g, (_, d) in enumerate(BRANCHES):
            L = S // d
            sh = d.bit_length() - 1
            r = jnp.bitwise_and(ti, d - 1)
            m0 = jnp.right_shift(n0, sh) + jnp.right_shift(ti, sh) * TQ
            ks = jnp.clip(m0 - HALF_BAND, 0, L - KW)
            q = rows(q_ref, r + d * m0, TQ, d).astype(bf16)
            k = rows(k_ref, r + d * ks, KW, d).astype(bf16)
            v = rows(v_ref, r + d * ks, KW, d).astype(bf16)
            valid = jnp.abs(rel + (m0 - ks)) <= HALF_BAND
            local = r + d * (m0 - jnp.right_shift(n0, sh))
            ss = []
            for sel in (head0, jnp.logical_not(head0)):
                qh = jnp.where(sel, q, jnp.zeros_like(q))
                ss.append(lax.dot_general(qh, k, (((1,), (1,)), ((), ())),
                                          preferred_element_type=f32))
            ctx.append((d, v, valid, local, ss))
        soft = []
        for d, v, valid, local, ss in ctx:
            ps = []
            for s in ss:
                s = jnp.where(valid, s, NEG)
                m = jnp.max(s, axis=-1, keepdims=True)
                p = jnp.exp(s - m)
                den = jnp.sum(p, axis=-1, keepdims=True)
                ps.append((p.astype(bf16), den, m + jnp.log(den)))
            soft.append(ps)
        for g, ((d, v, valid, local, ss), ps) in enumerate(zip(ctx, soft)):
            outs = [(jnp.dot(p, v, preferred_element_type=f32) / den, lse) for p, den, lse in ps]
            o_val = jnp.where(head0, outs[0][0], outs[1][0])
            l_val = jnp.where(head0, outs[0][1], outs[1][1])
            if d == 1:
                o_sc[g][pl.ds(local, TQ), :] = o_val
                l_sc[g][pl.ds(local, TQ), :] = l_val
            else:
                o_sc[g][pl.ds(local, TQ, stride=d), :] = o_val
                l_sc[g][pl.ds(local, TQ, stride=d), :] = l_val
        return n0

    def merge(c, n0):
        sl = pl.ds(pl.multiple_of(c * MERGE_ROWS, MERGE_ROWS), MERGE_ROWS)
        ls = [l_sc[g][sl, :] for g in range(len(BRANCHES))]
        mx = functools.reduce(jnp.maximum, ls)
        es = [jnp.exp(l - mx) for l in ls]
        num = functools.reduce(lambda a, b: a + b, [e * o_sc[g][sl, :] for g, e in enumerate(es)])
        den = functools.reduce(lambda a, b: a + b, es)
        dst = pl.ds(pl.multiple_of(n0 + c * MERGE_ROWS, MERGE_ROWS), MERGE_ROWS)
        out_ref[dst, :] = (num / den).astype(bf16)
        return n0

    def superblock(sb, carry):
        n0 = pl.multiple_of(sb * SB, SB)
        lax.fori_loop(0, tiles, tile, n0)
        lax.fori_loop(0, SB // MERGE_ROWS, merge, n0)
        return carry

    lax.fori_loop(0, S // SB, superblock, 0)


def _attention(q, k, v, B, S):
    assert S % SB == 0 and all(S // d >= KW for _, d in BRANCHES)
    hp = ATTN_W // LANES
    view = lambda t: t.reshape(B, S, ATTN_W)
    spec = pl.BlockSpec((None, S, LANES), lambda b, h: (b, 0, h))
    out = pl.pallas_call(
        functools.partial(_attn_body, S=S),
        grid=(B, hp),
        in_specs=[spec, spec, spec],
        out_specs=spec,
        out_shape=jax.ShapeDtypeStruct((B, S, ATTN_W), bf16),
        scratch_shapes=[pltpu.VMEM((SB, LANES), f32)] * (2 * len(BRANCHES)),
        compiler_params=_cparams(("parallel", "parallel")),
        name="attention",
    )(view(q), view(k), view(v))
    return out.reshape(B * S, ATTN_W)


def _rec_body(rx_ref, rg_ref, cw_ref, cb_ref, wg_ref, gb_ref, lam_ref, out_ref, rxp_ref, hf_ref, *, S):
    nch = S // TS
    ntile = TS // SUBLANES
    zeros8 = jnp.zeros((SUBLANES, LANES), f32)
    rxp_ref[pl.ds(0, SUBLANES), :] = zeros8
    rxp_ref[pl.ds(S + SUBLANES, SUBLANES), :] = zeros8

    def pad_copy(c, carry):
        t0 = pl.multiple_of(c * TS, TS)
        rxp_ref[pl.ds(t0 + SUBLANES, TS), :] = rx_ref[pl.ds(t0, TS), :]
        return carry

    lax.fori_loop(0, nch, pad_copy, 0)

    lam = lam_ref[...]
    neg_sp = -RG_LRU_C * (jnp.maximum(-lam, 0.0) + jnp.log1p(jnp.exp(-jnp.abs(lam))))
    cw = cw_ref[...]
    cb = cb_ref[...]
    sub = lax.broadcasted_iota(i32, (ntile, SUBLANES, LANES), 1)
    nrow = TS + 2 * SUBLANES

    def gates(ci, d):
        t0 = pl.multiple_of(ci * TS, TS)
        xw = rxp_ref[pl.ds(t0, nrow), :]
        u = (cw[0:1] * pltpu.roll(xw, 2, 0) + cw[1:2] * pltpu.roll(xw, 1, 0) + cw[2:3] * xw
             + cw[3:4] * pltpu.roll(xw, nrow - 1, 0))[SUBLANES:SUBLANES + TS] + cb
        c0 = d * 2 * LANES
        g = jnp.dot(u.astype(bf16), wg_ref[:, c0:c0 + 2 * LANES], preferred_element_type=f32)
        g = g + gb_ref[:, c0:c0 + 2 * LANES]
        r = jax.nn.sigmoid(g[:, :LANES])
        gi = jax.nn.sigmoid(g[:, LANES:])
        a = jnp.exp(neg_sp[d:d + 1] * r)
        b = jnp.sqrt(1.0 - a * a) * gi * u
        return a.reshape(ntile, SUBLANES, LANES), b.reshape(ntile, SUBLANES, LANES)

    def chunk(i, carry):
        cf, cbk = carry
        a, b = gates(i, 0)
        for s in (1, 2, 4):
            ok = sub >= s
            a_s = pltpu.roll(a, s, 1)
            b_s = pltpu.roll(b, s, 1)
            b = jnp.where(ok, a * b_s + b, b)
            a = jnp.where(ok, a * a_s, a)
        t0 = pl.multiple_of(i * TS, TS)
        for j in range(ntile):
            h = a[j] * cf + b[j]
            hf_ref[pl.ds(t0 + j * SUBLANES, SUBLANES), :] = h
            cf = h[SUBLANES - 1:SUBLANES, :]
        ib = nch - 1 - i
        a, b = gates(ib, 1)
        for s in (1, 2, 4):
            ok = sub < SUBLANES - s
            a_s = pltpu.roll(a, SUBLANES - s, 1)
            b_s = pltpu.roll(b, SUBLANES - s, 1)
            b = jnp.where(ok, a * b_s + b, b)
            a = jnp.where(ok, a * a_s, a)
        t0 = pl.multiple_of(ib * TS, TS)
        for j in range(ntile - 1, -1, -1):
            h = a[j] * cbk + b[j]
            out_ref[pl.ds(t0 + j * SUBLANES, SUBLANES), :] = h
            cbk = h[0:1, :]
        return cf, cbk

    zrow = jnp.zeros((1, LANES), f32)
    lax.fori_loop(0, nch, chunk, (zrow, zrow))

    def finish(c, carry):
        t0 = pl.multiple_of(c * TS, TS)
        sl = pl.ds(t0, TS)
        out_ref[sl, :] = (hf_ref[sl, :] + out_ref[sl, :]) * jax.nn.gelu(rg_ref[sl, :])
        return carry

    lax.fori_loop(0, nch, finish, 0)


def _rec(rx, rg, conv_w, conv_b, wg, gb, lam, B, S):
    ng = REC_W // LANES
    assert S % TS == 0
    seq = pl.BlockSpec((None, S, LANES), lambda b, c: (b, 0, c))
    return pl.pallas_call(
        functools.partial(_rec_body, S=S),
        grid=(B, ng),
        in_specs=[seq, seq,
                  pl.BlockSpec((CONV_W, LANES), lambda b, c: (0, c)),
                  pl.BlockSpec((1, LANES), lambda b, c: (0, c)),
                  pl.BlockSpec((None, LANES, 4 * LANES), lambda b, c: (c, 0, 0)),
                  pl.BlockSpec((None, 1, 4 * LANES), lambda b, c: (c, 0, 0)),
                  pl.BlockSpec((2, LANES), lambda b, c: (0, c))],
        out_specs=seq,
        out_shape=jax.ShapeDtypeStruct((B, S, REC_W), f32),
        scratch_shapes=[pltpu.VMEM((S + 2 * SUBLANES, LANES), f32), pltpu.VMEM((S, LANES), f32)],
        compiler_params=_cparams(("parallel", "parallel")),
        name="rec",
    )(rx.reshape(B, S, REC_W), rg.reshape(B, S, REC_W), conv_w, conv_b, wg, gb, lam)


u32 = jnp.uint32


def _pack_rows(x):
    bits = lax.bitcast_convert_type(x, u32)
    r = (bits + u32(0x7FFF) + ((bits >> 16) & u32(1))) >> 16
    return (r[:, WORDS:] << 16) | r[:, :WORDS]


def _unpack_rows(w):
    lo = lax.bitcast_convert_type(w << 16, f32).astype(bf16)
    hi = lax.bitcast_convert_type(w & u32(0xFFFF0000), f32).astype(bf16)
    return lo, hi


def _slab_chunks(flat, rows, base=0):
    return jnp.concatenate(
        [flat[pl.ds(base * CHUNKS + c, rows, stride=CHUNKS), :] for c in range(CHUNKS)], axis=1)


def _store_slab_chunks(flat, val, rows, base=0):
    for c in range(CHUNKS):
        flat[pl.ds(base * CHUNKS + c, rows, stride=CHUNKS), :] = val[:, c * LANES:(c + 1) * LANES]


def _rms(t, gain):
    return t * lax.rsqrt(jnp.mean(t * t, axis=-1, keepdims=True) + LN_EPS) * gain


def _layer_norm(z, g, b):
    mu = jnp.mean(z, axis=-1, keepdims=True)
    zc = z - mu
    var = jnp.mean(zc * zc, axis=-1, keepdims=True)
    return zc * lax.rsqrt(var + LN_EPS) * g + b


def _mixout_body(attn_ref, rec_ref, x_ref,
                 ag_ref, rgn_ref, wo_ref, g1_ref, b1_ref, rwh_ref, rwl_ref, rb_ref, tri_ref, ones_ref,
                 x1_ref, x1p_ref, idx_ref, gate_ref, rank_ref, cnt_ref, carry_ref, *, alpha):
    tm = TM_MIX

    @pl.when(pl.program_id(0) == 0)
    def _():
        carry_ref[...] = jnp.zeros_like(carry_ref)

    attn_n = _rms(attn_ref[...].astype(f32), ag_ref[...]).astype(bf16)
    rec_n = _rms(rec_ref[...], rgn_ref[...]).astype(bf16)
    y = (jnp.dot(attn_n, wo_ref[0:ATTN_W, :], preferred_element_type=f32)
         + jnp.dot(rec_n, wo_ref[ATTN_W:, :], preferred_element_type=f32))
    x1 = _layer_norm(alpha * x_ref[...] + y, g1_ref[...], b1_ref[...])
    x1_ref[...] = x1
    _store_slab_chunks(x1p_ref, _pack_rows(x1), tm)

    hi = x1.astype(bf16)
    lo = (x1 - hi.astype(f32)).astype(bf16)
    nt = (((1,), (1,)), ((), ()))
    logits = (lax.dot_general(rwh_ref[...], hi, nt, preferred_element_type=f32)
              + lax.dot_general(rwh_ref[...], lo, nt, preferred_element_type=f32)
              + lax.dot_general(rwl_ref[...], hi, nt, preferred_element_type=f32))
    scores = jax.nn.sigmoid(logits)
    biased = scores + jnp.concatenate([rb_ref[...]] * (tm // LANES), axis=1)

    rid = lax.broadcasted_iota(i32, (GROUP_SZ, tm), 0).astype(f32)
    grp = []
    for g in range(N_GROUPS):
        vg = biased[g * GROUP_SZ:(g + 1) * GROUP_SZ, :]
        m1 = jnp.max(vg, axis=0, keepdims=True)
        first = jnp.min(jnp.where(vg == m1, rid, float(GROUP_SZ)), axis=0, keepdims=True)
        m2 = jnp.max(jnp.where(rid == first, -jnp.inf, vg), axis=0, keepdims=True)
        grp.append(m1 + m2)
    eid = lax.broadcasted_iota(i32, (N_EXPERTS, tm), 0).astype(f32)
    egid = jnp.floor(eid * (1.0 / GROUP_SZ))
    masked = jnp.full((N_EXPERTS, tm), -jnp.inf, f32)
    for _ in range(TOPK_GROUPS):
        gm = functools.reduce(jnp.maximum, grp)
        gi = jnp.full((1, tm), float(N_GROUPS), f32)
        for g in range(N_GROUPS - 1, -1, -1):
            gi = jnp.where(grp[g] == gm, float(g), gi)
        grp = [jnp.where(gi == float(g), -jnp.inf, grp[g]) for g in range(N_GROUPS)]
        masked = jnp.where(egid == gi, biased, masked)

    onehot = jnp.zeros((N_EXPERTS, tm), f32)
    idxs, gts = [], []
    for _ in range(TOP_K):
        mx = jnp.max(masked, axis=0, keepdims=True)
        ix = jnp.min(jnp.where(masked == mx, eid, float(N_EXPERTS)), axis=0, keepdims=True)
        hit = eid == ix
        gts.append(jnp.sum(jnp.where(hit, scores, 0.0), axis=0, keepdims=True))
        idxs.append(ix)
        masked = jnp.where(hit, -jnp.inf, masked)
        onehot = onehot + jnp.where(hit, 1.0, 0.0)
    gsum = functools.reduce(lambda p, q: p + q, gts)
    for kk in range(TOP_K):
        idx_ref[kk:kk + 1, :] = idxs[kk].astype(i32)
        gate_ref[kk:kk + 1, :] = gts[kk] / gsum * ROUTED_SCALE

    oh = onehot.astype(bf16)
    before = carry_ref[...] + jnp.dot(oh, tri_ref[...], preferred_element_type=f32)
    for kk in range(TOP_K):
        rk = jnp.sum(jnp.where(eid == idxs[kk], before, 0.0), axis=0, keepdims=True)
        rank_ref[kk:kk + 1, :] = rk.astype(i32)
    total = carry_ref[...] + jnp.dot(oh, ones_ref[...], preferred_element_type=f32)
    carry_ref[...] = total
    cnt_ref[...] = total


def _mixout(attn, rec, x2, attn_gain, rec_gain, w_out_b, g1, b1, rw_hi, rw_lo, rbias, alpha):
    T = x2.shape[0]
    tm = TM_MIX
    tok = lambda i: (i, 0)
    fixed = lambda i: (0, 0)
    tri = jnp.asarray(np.triu(np.ones((tm, tm), np.float32), k=1), bf16)
    ones = jnp.ones((tm, tm), bf16)
    aw = pl.BlockSpec((tm, ATTN_W), tok)
    row = lambda n: pl.BlockSpec((1, n), fixed)
    kt = pl.BlockSpec((TOP_K, tm), lambda i: (0, i))
    return pl.pallas_call(
        functools.partial(_mixout_body, alpha=alpha),
        grid=(T // tm,),
        in_specs=[aw] * 2 + [pl.BlockSpec((tm, D_MODEL), tok), row(ATTN_W), row(REC_W),
                             pl.BlockSpec((D_MODEL, D_MODEL), fixed), row(D_MODEL), row(D_MODEL),
                             pl.BlockSpec((N_EXPERTS, D_MODEL), fixed),
                             pl.BlockSpec((N_EXPERTS, D_MODEL), fixed),
                             pl.BlockSpec((N_EXPERTS, LANES), fixed),
                             pl.BlockSpec((tm, tm), fixed), pl.BlockSpec((tm, tm), fixed)],
        out_specs=[pl.BlockSpec((tm, D_MODEL), tok),
                   pl.BlockSpec((tm * CHUNKS, LANES), tok),
                   kt, kt, kt, pl.BlockSpec((N_EXPERTS, tm), fixed)],
        out_shape=[jax.ShapeDtypeStruct((T, D_MODEL), f32),
                   jax.ShapeDtypeStruct((T * CHUNKS, LANES), u32),
                   jax.ShapeDtypeStruct((TOP_K, T), i32),
                   jax.ShapeDtypeStruct((TOP_K, T), f32),
                   jax.ShapeDtypeStruct((TOP_K, T), i32),
                   jax.ShapeDtypeStruct((N_EXPERTS, tm), f32)],
        scratch_shapes=[pltpu.VMEM((N_EXPERTS, tm), f32)],
        compiler_params=_cparams(("arbitrary",)),
        name="mixout",
    )(attn, rec, x2, attn_gain, rec_gain, w_out_b, g1, b1,
      rw_hi, rw_lo, rbias, tri, ones)


def _slots_body(idx_ref, rank_ref, start_ref, dest_ref):
    tm = TM_SLOT
    eid = lax.broadcasted_iota(i32, (N_EXPERTS, tm), 0)
    start = jnp.concatenate([start_ref[...]] * (tm // LANES), axis=1)
    for kk in range(TOP_K):
        hit = eid == idx_ref[kk:kk + 1, :]
        base = jnp.sum(jnp.where(hit, start, 0.0), axis=0, keepdims=True)
        dest_ref[kk:kk + 1, :] = base.astype(i32) + rank_ref[kk:kk + 1, :]


def _slots(idx, rank, pad_start):
    T = idx.shape[1]
    tm = TM_SLOT
    kt = pl.BlockSpec((TOP_K, tm), lambda i: (0, i))
    start = jnp.broadcast_to(pad_start.astype(f32).reshape(N_EXPERTS, 1), (N_EXPERTS, LANES))
    return pl.pallas_call(
        _slots_body,
        grid=(T // tm,),
        in_specs=[kt, kt, pl.BlockSpec((N_EXPERTS, LANES), lambda i: (0, 0))],
        out_specs=kt,
        out_shape=jax.ShapeDtypeStruct((TOP_K, T), i32),
        compiler_params=_cparams(("parallel",)),
        name="slots",
    )(idx, rank, start)


def _dispatch_body(dest_ref, x_ref, xs_ref, sem):
    tm = TM_MIX

    def copy(t, k):
        return pltpu.make_async_copy(x_ref.at[t], xs_ref.at[dest_ref[k, t]], sem)

    def issue(t, carry):
        for k in range(TOP_K):
            copy(t, k).start(priority=k % DMA_THREADS)
        return carry

    def drain(t, carry):
        for k in range(TOP_K):
            copy(t, k).wait()
        return carry

    lax.fori_loop(0, tm, issue, 0)
    lax.fori_loop(0, tm, drain, 0)


def _dispatch(dest, x1p, n_slots):
    T = x1p.shape[0]
    tm = TM_MIX
    return pl.pallas_call(
        _dispatch_body,
        grid=(T // tm,),
        in_specs=[pl.BlockSpec((TOP_K, tm), lambda i: (0, i), memory_space=pltpu.SMEM),
                  pl.BlockSpec((tm, CHUNKS, LANES), lambda i: (i, 0, 0))],
        out_specs=pl.BlockSpec(memory_space=pl.ANY),
        out_shape=jax.ShapeDtypeStruct((n_slots, CHUNKS, LANES), u32),
        scratch_shapes=[pltpu.SemaphoreType.DMA(())],
        compiler_params=_cparams(("arbitrary",)),
        name="dispatch",
    )(dest, x1p)


def _experts_body(be_ref, nv_ref, nb_ref, xs_ref, wga_ref, wua_ref, wda_ref, wgb_ref, wub_ref,
                  wdb_ref, ys_ref, sga, sua, sda, sgb, sub, sdb):
    p = pl.program_id(0)
    blocks = (2 * p, 2 * p + 1)
    f32_w = ((wga_ref, wua_ref, wda_ref), (wgb_ref, wub_ref, wdb_ref))
    bf_w = ((sga, sua, sda), (sgb, sub, sdb))

    for half, blk in enumerate(blocks):
        prev = jnp.maximum(blk - 2, 0)

        @pl.when(jnp.logical_and(blocks[0] < nb_ref[0],
                                 jnp.logical_or(p == 0, be_ref[blk] != be_ref[prev])))
        def _(half=half):
            for src, dst in zip(f32_w[half], bf_w[half]):
                dst[...] = src[...].astype(bf16)

    @pl.when(blocks[0] < nb_ref[0])
    def _():
        xs = []
        for half, blk in enumerate(blocks):
            rows = jnp.where(blk < nb_ref[0], nv_ref[blk], 0)
            live = lax.broadcasted_iota(i32, (BM, WORDS), 0) < rows
            words = _slab_chunks(xs_ref, BM, half * BM)
            xs.append(_unpack_rows(jnp.where(live, words, u32(0))))

        def up(half, w_ref):
            lo, hi = xs[half]
            return (jnp.dot(lo, w_ref[:WORDS, :], preferred_element_type=f32)
                    + jnp.dot(hi, w_ref[WORDS:, :], preferred_element_type=f32))

        gs = [up(half, bf_w[half][0]) for half in range(2)]
        us = [up(half, bf_w[half][1]) for half in range(2)]
        hs = [(g * jax.nn.sigmoid(g) * u).astype(bf16) for g, u in zip(gs, us)]
        ys = [jnp.dot(h, bf_w[half][2][...], preferred_element_type=f32)
              for half, h in enumerate(hs)]
        for half in range(2):
            _store_slab_chunks(ys_ref, _pack_rows(ys[half]), BM, half * BM)


def _experts(block_e, block_rows, nb_used, xs, wg, wu, wd):
    P = xs.shape[0] // CHUNKS
    nb = P // BM
    assert nb % 2 == 0
    last = lambda nbu: jnp.maximum((nbu[0] - 1) // 2, 0)
    rows = lambda p, be, nv, nbu: (jnp.minimum(p, last(nbu)), 0)
    wsel = lambda half: (lambda p, be, nv, nbu: (be[2 * p + half], 0, 0))
    wspecs = [pl.BlockSpec((None, D_MODEL, EXPERT_H), wsel(h)) for h in (0, 0)] \
        + [pl.BlockSpec((None, EXPERT_H, D_MODEL), wsel(0))] \
        + [pl.BlockSpec((None, D_MODEL, EXPERT_H), wsel(h)) for h in (1, 1)] \
        + [pl.BlockSpec((None, EXPERT_H, D_MODEL), wsel(1))]
    wscratch = [pltpu.VMEM((D_MODEL, EXPERT_H), bf16), pltpu.VMEM((D_MODEL, EXPERT_H), bf16),
                pltpu.VMEM((EXPERT_H, D_MODEL), bf16)]
    gs = pltpu.PrefetchScalarGridSpec(
        num_scalar_prefetch=3,
        grid=(nb // 2,),
        in_specs=[pl.BlockSpec((2 * BM * CHUNKS, LANES), rows)] + wspecs,
        out_specs=pl.BlockSpec((2 * BM * CHUNKS, LANES), rows),
        scratch_shapes=wscratch * 2,
    )
    return pl.pallas_call(
        _experts_body,
        grid_spec=gs,
        out_shape=jax.ShapeDtypeStruct((P * CHUNKS, LANES), u32),
        compiler_params=_cparams(("arbitrary",)),
        name="experts",
    )(block_e, block_rows, nb_used, xs, wg, wu, wd, wg, wu, wd)


def _combine_body(dest_ref, dnext_ref, gate_ref, x1_ref, ys_ref, sg_ref, su_ref, sd_ref,
                  g2_ref, b2_ref, out_ref, gbuf0, gbuf1, sems, *, alpha):
    tm = TM_CMB
    j = pl.program_id(0)
    n = pl.num_programs(0)
    gbufs = (gbuf0, gbuf1)

    def copy(d_ref, base, sl, t, k):
        src = ys_ref.at[d_ref[(base + t) * TOP_K + k]]
        dst = gbufs[sl].at[pl.ds((k * tm + t) * CHUNKS, CHUNKS)]
        return pltpu.make_async_copy(src, dst, sems.at[sl])

    def issue(d_ref, base, sl):
        def step(t, carry):
            for k in range(TOP_K):
                copy(d_ref, base, sl, t, k).start(priority=k % DMA_THREADS)
            return carry
        lax.fori_loop(0, tm, step, 0)

    def finish(half, sl):
        base = half * tm
        rows = pl.ds(base, tm)
        x1 = x1_ref[rows, :]
        xb = x1.astype(bf16)
        sg = jnp.dot(xb, sg_ref[...], preferred_element_type=f32)
        su = jnp.dot(xb, su_ref[...], preferred_element_type=f32)
        shared = jnp.dot((sg * jax.nn.sigmoid(sg) * su).astype(bf16), sd_ref[...],
                         preferred_element_type=f32)

        def drain(t, carry):
            for k in range(TOP_K):
                copy(dest_ref, base, sl, t, k).wait()
            return carry

        lax.fori_loop(0, tm, drain, 0)

        gates = gate_ref[rows, :]
        lo = [jnp.zeros((tm, LANES), f32) for _ in range(CHUNKS)]
        hi = [jnp.zeros((tm, LANES), f32) for _ in range(CHUNKS)]
        flat = gbufs[sl]
        for k in range(TOP_K):
            gk = jnp.broadcast_to(gates[:, k:k + 1], (tm, LANES))
            for c in range(CHUNKS):
                w = flat[pl.ds(k * tm * CHUNKS + c, tm, stride=CHUNKS), :]
                lo[c] = lo[c] + gk * lax.bitcast_convert_type(w << 16, f32)
                hi[c] = hi[c] + gk * lax.bitcast_convert_type(w & u32(0xFFFF0000), f32)
        routed = jnp.concatenate(lo + hi, axis=1)
        out_ref[rows, :] = _layer_norm(alpha * x1 + (routed + shared), g2_ref[...], b2_ref[...])

    @pl.when(j == 0)
    def _():
        issue(dest_ref, 0, 0)

    issue(dest_ref, tm, 1)
    finish(0, 0)

    @pl.when(j + 1 < n)
    def _():
        issue(dnext_ref, 0, 0)

    finish(1, 1)


def _combine(dest, gate, x1, ys, sg, su, sd, g2, b2, alpha):
    T = x1.shape[0]
    tm = TM_CMB
    n = T // (2 * tm)
    tok = lambda j: (j, 0)
    fixed = lambda j: (0, 0)
    tbl = pl.BlockSpec((2 * tm * TOP_K,), lambda j: (j,), memory_space=pltpu.SMEM)
    tbl_next = pl.BlockSpec((2 * tm * TOP_K,), lambda j: (jnp.minimum(j + 1, n - 1),),
                            memory_space=pltpu.SMEM)
    return pl.pallas_call(
        functools.partial(_combine_body, alpha=alpha),
        grid=(n,),
        in_specs=[tbl, tbl_next, pl.BlockSpec((2 * tm, TOP_K), tok),
                  pl.BlockSpec((2 * tm, D_MODEL), tok),
                  pl.BlockSpec(memory_space=pl.ANY),
                  pl.BlockSpec(sg.shape, fixed), pl.BlockSpec(su.shape, fixed),
                  pl.BlockSpec(sd.shape, fixed), pl.BlockSpec((1, D_MODEL), fixed),
                  pl.BlockSpec((1, D_MODEL), fixed)],
        out_specs=pl.BlockSpec((2 * tm, D_MODEL), tok),
        out_shape=jax.ShapeDtypeStruct((T, D_MODEL), f32),
        scratch_shapes=[pltpu.VMEM((TOP_K * tm * CHUNKS, LANES), u32),
                        pltpu.VMEM((TOP_K * tm * CHUNKS, LANES), u32),
                        pltpu.SemaphoreType.DMA((2,))],
        compiler_params=_cparams(("arbitrary",)),
        name="combine",
    )(dest, dest, gate, x1, ys, sg, su, sd, g2, b2)


def _rope_inv_freq():
    half = ROT_DIM // 2
    inv = ROPE_THETA ** (-jnp.arange(half, dtype=f32) * 2.0 / ROT_DIM)
    j = np.arange(LANES) % HEAD_DIM
    table = jnp.where(j < ROT_DIM, inv[j % half], 0.0)
    return table.reshape(1, LANES).astype(f32)


def _gate_weights(ga_w, ga_b, gx_w, gx_b):
    ng = REC_W // LANES
    per = LANES // HEAD_DIM
    def bd(w):
        w = w.reshape(ng, per, HEAD_DIM, HEAD_DIM)
        z = jnp.zeros((ng, LANES, LANES), w.dtype)
        for p in range(per):
            z = z.at[:, p * HEAD_DIM:(p + 1) * HEAD_DIM, p * HEAD_DIM:(p + 1) * HEAD_DIM].set(w[:, p])
        return z
    wg = jnp.concatenate([bd(ga_w[0]), bd(gx_w[0]), bd(ga_w[1]), bd(gx_w[1])], axis=-1).astype(bf16)
    grp = lambda b: b.reshape(ng, 1, LANES)
    gb = jnp.concatenate([grp(ga_b[0]), grp(gx_b[0]), grp(ga_b[1]), grp(gx_b[1])], axis=-1)
    return wg, gb


def _layer(x, positions, w_in, attn_gain, conv_w, conv_b, ga_w, ga_b, gx_w, gx_b, lam, rec_gain,
           w_out, ln1_g, ln1_b, router_w, router_bias, e_wg, e_wu, e_wd, s_wg, s_wu, s_wd,
           ln2_g, ln2_b, alpha):
    B, S, _ = x.shape
    T = B * S
    x2 = x.reshape(T, D_MODEL)
    pos2 = positions.reshape(T, 1)

    q, k, v, rx, rg = _inproj(x2, pos2, w_in.astype(bf16), _rope_inv_freq())
    attn = _attention(q, k, v, B, S)
    wg, gb = _gate_weights(ga_w, ga_b, gx_w, gx_b)
    rec = _rec(rx, rg, conv_w, conv_b.reshape(1, REC_W), wg, gb, lam, B, S).reshape(T, REC_W)

    rw_t = router_w.T
    rw_hi = rw_t.astype(bf16)
    rw_lo = (rw_t - rw_hi.astype(f32)).astype(bf16)
    rbias = jnp.broadcast_to(router_bias.reshape(N_EXPERTS, 1), (N_EXPERTS, LANES))
    x1, x1p, idx, gate, rank, cnt = _mixout(
        attn, rec, x2,
        attn_gain.reshape(1, ATTN_W), rec_gain.reshape(1, REC_W), w_out.astype(bf16),
        ln1_g.reshape(1, D_MODEL), ln1_b.reshape(1, D_MODEL), rw_hi, rw_lo, rbias, alpha)

    counts = cnt[:, 0].astype(i32)
    padded = (counts + BM - 1) // BM * BM
    pad_end = jnp.cumsum(padded)
    pad_start = pad_end - padded
    slot = _slots(idx, rank, pad_start)
    nb = (T * TOP_K + N_EXPERTS * (BM - 1)) // BM + 1
    nb += nb % 2
    nb_used = pad_end[-1] // BM
    h = (nb_used + 1) // 2
    pos = jnp.arange(nb, dtype=i32)
    blk = jnp.where(pos % 2 == 0, pos // 2, h + pos // 2)
    live = jnp.logical_and(pos < 2 * h, blk < nb_used)
    first_row = blk * BM
    block_e = jnp.minimum(jnp.sum((pad_end[None, :] <= first_row[:, None]).astype(i32), axis=1),
                          N_EXPERTS - 1)
    block_e = jnp.where(live, block_e, N_EXPERTS - 1)
    block_rows = jnp.where(live, jnp.clip((pad_start + counts)[block_e] - first_row, 0, BM), 0)
    sblk = slot // BM
    dest = jnp.where(sblk < h, 2 * sblk, 2 * (sblk - h) + 1) * BM + slot % BM
    nb_used = (2 * h).astype(i32).reshape(1)

    xs = _dispatch(dest, x1p.reshape(T, CHUNKS, LANES), nb * BM)
    ys = _experts(block_e.astype(i32), block_rows.astype(i32), nb_used,
                  xs.reshape(nb * BM * CHUNKS, LANES), e_wg, e_wu, e_wd)
    ys = ys.reshape(nb * BM, CHUNKS, LANES)
    out = _combine(dest.T.reshape(-1), gate.T, x1, ys, s_wg.astype(bf16), s_wu.astype(bf16), s_wd.astype(bf16),
                   ln2_g.reshape(1, D_MODEL), ln2_b.reshape(1, D_MODEL), alpha)
    return out.reshape(B, S, D_MODEL)


def kernel(x, positions, w_in, attn_gain, rec_conv_w, rec_conv_b, rec_gate_a_w, rec_gate_a_b,
           rec_gate_x_w, rec_gate_x_b, rec_lambda, rec_gain, w_out, ln1_g, ln1_b, router_w,
           router_bias, exp_w_gate, exp_w_up, exp_w_down, shared_w_gate, shared_w_up,
           shared_w_down, ln2_g, ln2_b):
    depth = w_in.shape[0]
    alpha = (2 * depth) ** 0.25
    for l in range(depth):
        x = _layer(x, positions, w_in[l], attn_gain[l], rec_conv_w[l], rec_conv_b[l],
                   rec_gate_a_w[l], rec_gate_a_b[l], rec_gate_x_w[l], rec_gate_x_b[l],
                   rec_lambda[l], rec_gain[l], w_out[l], ln1_g[l], ln1_b[l], router_w[l],
                   router_bias[l], exp_w_gate[l], exp_w_up[l], exp_w_down[l], shared_w_gate[l],
                   shared_w_up[l], shared_w_down[l], ln2_g[l], ln2_b[l], alpha)
    return x
```

```python
import functools
import math

import jax
import jax.numpy as jnp
import numpy as np
from jax import lax
from jax.experimental import pallas as pl
from jax.experimental.pallas import tpu as pltpu

f32 = jnp.float32
bf16 = jnp.bfloat16
i32 = jnp.int32

D_MODEL = 1024
ATTN_W = 512
REC_W = 512
HEAD_DIM = 64
ROT_DIM = 16
ROPE_THETA = 500000.0
BRANCHES = ((128, 1), (512, 4), (2048, 16))
HALF_BAND = 64
CONV_W = 4
RG_LRU_C = 8.0
N_EXPERTS = 256
TOP_K = 8
N_GROUPS = 8
GROUP_SZ = N_EXPERTS // N_GROUPS
TOPK_GROUPS = 4
EXPERT_H = 256
ROUTED_SCALE = 2.5
LN_EPS = 1e-5
NEG = -1e30

LANES = 128
SUBLANES = 8
WORDS = D_MODEL // 2
CHUNKS = WORDS // LANES
VMEM_LIMIT = 56 * 1024 * 1024
DMA_THREADS = 2

TM_IN = 512
TQ = 128
KW = TQ + 2 * HALF_BAND
SB = TQ * max(d for _, d in BRANCHES)
MERGE_ROWS = 256
TS = 256
TM_MIX = 256
TM_SLOT = 512
TM_CMB = 128
BM = 256


def _cparams(sem):
    return pltpu.CompilerParams(dimension_semantics=sem, vmem_limit_bytes=VMEM_LIMIT)


def _inproj_body(x_ref, pos_ref, w_ref, invf_ref, q_ref, k_ref, v_ref, rx_ref, rg_ref):
    xb = x_ref[...].astype(bf16)
    ang = pos_ref[...].astype(f32) * invf_ref[...]
    cos = jnp.cos(ang)
    sin = jnp.sin(ang)
    j = lax.broadcasted_iota(i32, (1, LANES), 1) % HEAD_DIM
    half = ROT_DIM // 2
    s_lo = jnp.where(j < half, -sin, 0.0)
    s_hi = jnp.where((j >= half) & (j < ROT_DIM), sin, 0.0)
    rep = ATTN_W // LANES
    cos_w = jnp.concatenate([cos] * rep, axis=1)
    s_lo_w = jnp.concatenate([s_lo] * rep, axis=1)
    s_hi_w = jnp.concatenate([s_hi] * rep, axis=1)

    def proj(c0, n):
        return jnp.dot(xb, w_ref[:, c0:c0 + n], preferred_element_type=f32)

    def rope(t):
        return (t * cos_w + pltpu.roll(t, ATTN_W - half, 1) * s_lo_w
                + pltpu.roll(t, half, 1) * s_hi_w)

    q_ref[...] = rope(proj(0, ATTN_W)) * (HEAD_DIM ** -0.5)
    k_ref[...] = rope(proj(ATTN_W, ATTN_W))
    v_ref[...] = proj(2 * ATTN_W, ATTN_W)
    rx_ref[...] = proj(3 * ATTN_W, REC_W)
    rg_ref[...] = proj(3 * ATTN_W + REC_W, REC_W)


def _inproj(x2, pos2, w_in_b, invf):
    T = x2.shape[0]
    tm = TM_IN
    in_w = w_in_b.shape[1]
    tok = lambda i: (i, 0)
    fixed = lambda i: (0, 0)
    return pl.pallas_call(
        _inproj_body,
        grid=(T // tm,),
        in_specs=[pl.BlockSpec((tm, D_MODEL), tok), pl.BlockSpec((tm, 1), tok),
                  pl.BlockSpec((D_MODEL, in_w), fixed), pl.BlockSpec((1, LANES), fixed)],
        out_specs=[pl.BlockSpec((tm, ATTN_W), tok)] * 3 + [pl.BlockSpec((tm, REC_W), tok)] * 2,
        out_shape=[jax.ShapeDtypeStruct((T, ATTN_W), f32)] * 3
        + [jax.ShapeDtypeStruct((T, REC_W), f32)] * 2,
        compiler_params=_cparams(("parallel",)),
        name="inproj",
    )(x2, pos2, w_in_b, invf)


def _attn_body(q_ref, k_ref, v_ref, out_ref, o1, o2, o3, l1, l2, l3, *, S):
    o_sc, l_sc = (o1, o2, o3), (l1, l2, l3)
    lane = lax.broadcasted_iota(i32, (1, LANES), 1)
    head0 = lane < HEAD_DIM
    rel = (lax.broadcasted_iota(i32, (TQ, KW), 0) - lax.broadcasted_iota(i32, (TQ, KW), 1))
    tiles = SB // TQ

    def rows(ref, start, n, d):
        return ref[pl.ds(start, n), :] if d == 1 else ref[pl.ds(start, n, stride=d), :]

    def tile(ti, n0):
        ctx = []
        for g, (_, d) in enumerate(BRANCHES):
            L = S // d
            sh = d.bit_length() - 1
            r = jnp.bitwise_and(ti, d - 1)
            m0 = jnp.right_shift(n0, sh) + jnp.right_shift(ti, sh) * TQ
            ks = jnp.clip(m0 - HALF_BAND, 0, L - KW)
            q = rows(q_ref, r + d * m0, TQ, d).astype(bf16)
            k = rows(k_ref, r + d * ks, KW, d).astype(bf16)
            v = rows(v_ref, r + d * ks, KW, d).astype(bf16)
            valid = jnp.abs(rel + (m0 - ks)) <= HALF_BAND
            local = r + d * (m0 - jnp.right_shift(n0, sh))
            ss = []
            for sel in (head0, jnp.logical_not(head0)):
                qh = jnp.where(sel, q, jnp.zeros_like(q))
                ss.append(lax.dot_general(qh, k, (((1,), (1,)), ((), ())),
                                          preferred_element_type=f32))
            ctx.append((d, v, valid, local, ss))
        soft = []
        for d, v, valid, local, ss in ctx:
            ps = []
            for s in ss:
                s = jnp.where(valid, s, NEG)
                m = jnp.max(s, axis=-1, keepdims=True)
                p = jnp.exp(s - m)
                den = jnp.sum(p, axis=-1, keepdims=True)
                ps.append((p.astype(bf16), den, m + jnp.log(den)))
            soft.append(ps)
        for g, ((d, v, valid, local, ss), ps) in enumerate(zip(ctx, soft)):
            outs = [(jnp.dot(p, v, preferred_element_type=f32) / den, lse) for p, den, lse in ps]
            o_val = jnp.where(head0, outs[0][0], outs[1][0])
            l_val = jnp.where(head0, outs[0][1], outs[1][1])
            if d == 1:
                o_sc[g][pl.ds(local, TQ), :] = o_val
                l_sc[g][pl.ds(local, TQ), :] = l_val
            else:
                o_sc[g][pl.ds(local, TQ, stride=d), :] = o_val
                l_sc[g][pl.ds(local, TQ, stride=d), :] = l_val
        return n0

    def merge(c, n0):
        sl = pl.ds(pl.multiple_of(c * MERGE_ROWS, MERGE_ROWS), MERGE_ROWS)
        ls = [l_sc[g][sl, :] for g in range(len(BRANCHES))]
        mx = functools.reduce(jnp.maximum, ls)
        es = [jnp.exp(l - mx) for l in ls]
        num = functools.reduce(lambda a, b: a + b, [e * o_sc[g][sl, :] for g, e in enumerate(es)])
        den = functools.reduce(lambda a, b: a + b, es)
        dst = pl.ds(pl.multiple_of(n0 + c * MERGE_ROWS, MERGE_ROWS), MERGE_ROWS)
        out_ref[dst, :] = (num / den).astype(bf16)
        return n0

    def superblock(sb, carry):
        n0 = pl.multiple_of(sb * SB, SB)
        lax.fori_loop(0, tiles, tile, n0)
        lax.fori_loop(0, SB // MERGE_ROWS, merge, n0)
        return carry

    lax.fori_loop(0, S // SB, superblock, 0)


def _attention(q, k, v, B, S):
    assert S % SB == 0 and all(S // d >= KW for _, d in BRANCHES)
    hp = ATTN_W // LANES
    view = lambda t: t.reshape(B, S, ATTN_W)
    spec = pl.BlockSpec((None, S, LANES), lambda b, h: (b, 0, h))
    out = pl.pallas_call(
        functools.partial(_attn_body, S=S),
        grid=(B, hp),
        in_specs=[spec, spec, spec],
        out_specs=spec,
        out_shape=jax.ShapeDtypeStruct((B, S, ATTN_W), bf16),
        scratch_shapes=[pltpu.VMEM((SB, LANES), f32)] * (2 * len(BRANCHES)),
        compiler_params=_cparams(("parallel", "parallel")),
        name="attention",
    )(view(q), view(k), view(v))
    return out.reshape(B * S, ATTN_W)


def _rec_body(rx_ref, rg_ref, cw_ref, cb_ref, wg_ref, gb_ref, lam_ref, out_ref, rxp_ref, hf_ref, *, S):
    nch = S // TS
    ntile = TS // SUBLANES
    zeros8 = jnp.zeros((SUBLANES, LANES), f32)
    rxp_ref[pl.ds(0, SUBLANES), :] = zeros8
    rxp_ref[pl.ds(S + SUBLANES, SUBLANES), :] = zeros8

    def pad_copy(c, carry):
        t0 = pl.multiple_of(c * TS, TS)
        rxp_ref[pl.ds(t0 + SUBLANES, TS), :] = rx_ref[pl.ds(t0, TS), :]
        return carry

    lax.fori_loop(0, nch, pad_copy, 0)

    lam = lam_ref[...]
    neg_sp = -RG_LRU_C * (jnp.maximum(-lam, 0.0) + jnp.log1p(jnp.exp(-jnp.abs(lam))))
    cw = cw_ref[...]
    cb = cb_ref[...]
    sub = lax.broadcasted_iota(i32, (ntile, SUBLANES, LANES), 1)
    nrow = TS + 2 * SUBLANES

    def gates(ci, d):
        t0 = pl.multiple_of(ci * TS, TS)
        xw = rxp_ref[pl.ds(t0, nrow), :]
        u = (cw[0:1] * pltpu.roll(xw, 2, 0) + cw[1:2] * pltpu.roll(xw, 1, 0) + cw[2:3] * xw
             + cw[3:4] * pltpu.roll(xw, nrow - 1, 0))[SUBLANES:SUBLANES + TS] + cb
        c0 = d * 2 * LANES
        g = jnp.dot(u.astype(bf16), wg_ref[:, c0:c0 + 2 * LANES], preferred_element_type=f32)
        g = g + gb_ref[:, c0:c0 + 2 * LANES]
        r = jax.nn.sigmoid(g[:, :LANES])
        gi = jax.nn.sigmoid(g[:, LANES:])
        a = jnp.exp(neg_sp[d:d + 1] * r)
        b = jnp.sqrt(1.0 - a * a) * gi * u
        return a.reshape(ntile, SUBLANES, LANES), b.reshape(ntile, SUBLANES, LANES)

    def chunk(i, carry):
        cf, cbk = carry
        a, b = gates(i, 0)
        for s in (1, 2, 4):
            ok = sub >= s
            a_s = pltpu.roll(a, s, 1)
            b_s = pltpu.roll(b, s, 1)
            b = jnp.where(ok, a * b_s + b, b)
            a = jnp.where(ok, a * a_s, a)
        t0 = pl.multiple_of(i * TS, TS)
        for j in range(ntile):
            h = a[j] * cf + b[j]
            hf_ref[pl.ds(t0 + j * SUBLANES, SUBLANES), :] = h
            cf = h[SUBLANES - 1:SUBLANES, :]
        ib = nch - 1 - i
        a, b = gates(ib, 1)
        for s in (1, 2, 4):
            ok = sub < SUBLANES - s
            a_s = pltpu.roll(a, SUBLANES - s, 1)
            b_s = pltpu.roll(b, SUBLANES - s, 1)
            b = jnp.where(ok, a * b_s + b, b)
            a = jnp.where(ok, a * a_s, a)
        t0 = pl.multiple_of(ib * TS, TS)
        for j in range(ntile - 1, -1, -1):
            h = a[j] * cbk + b[j]
            out_ref[pl.ds(t0 + j * SUBLANES, SUBLANES), :] = h
            cbk = h[0:1, :]
        return cf, cbk

    zrow = jnp.zeros((1, LANES), f32)
    lax.fori_loop(0, nch, chunk, (zrow, zrow))

    def finish(c, carry):
        t0 = pl.multiple_of(c * TS, TS)
        sl = pl.ds(t0, TS)
        out_ref[sl, :] = (hf_ref[sl, :] + out_ref[sl, :]) * jax.nn.gelu(rg_ref[sl, :])
        return carry

    lax.fori_loop(0, nch, finish, 0)


def _rec(rx, rg, conv_w, conv_b, wg, gb, lam, B, S):
    ng = REC_W // LANES
    assert S % TS == 0
    seq = pl.BlockSpec((None, S, LANES), lambda b, c: (b, 0, c))
    return pl.pallas_call(
        functools.partial(_rec_body, S=S),
        grid=(B, ng),
        in_specs=[seq, seq,
                  pl.BlockSpec((CONV_W, LANES), lambda b, c: (0, c)),
                  pl.BlockSpec((1, LANES), lambda b, c: (0, c)),
                  pl.BlockSpec((None, LANES, 4 * LANES), lambda b, c: (c, 0, 0)),
                  pl.BlockSpec((None, 1, 4 * LANES), lambda b, c: (c, 0, 0)),
                  pl.BlockSpec((2, LANES), lambda b, c: (0, c))],
        out_specs=seq,
        out_shape=jax.ShapeDtypeStruct((B, S, REC_W), f32),
        scratch_shapes=[pltpu.VMEM((S + 2 * SUBLANES, LANES), f32), pltpu.VMEM((S, LANES), f32)],
        compiler_params=_cparams(("parallel", "parallel")),
        name="rec",
    )(rx.reshape(B, S, REC_W), rg.reshape(B, S, REC_W), conv_w, conv_b, wg, gb, lam)


u32 = jnp.uint32


def _pack_rows(x):
    bits = lax.bitcast_convert_type(x, u32)
    r = (bits + u32(0x7FFF) + ((bits >> 16) & u32(1))) >> 16
    return (r[:, WORDS:] << 16) | r[:, :WORDS]


def _unpack_rows(w):
    lo = lax.bitcast_convert_type(w << 16, f32).astype(bf16)
    hi = lax.bitcast_convert_type(w & u32(0xFFFF0000), f32).astype(bf16)
    return lo, hi


def _slab_chunks(flat, rows, base=0):
    return jnp.concatenate(
        [flat[pl.ds(base * CHUNKS + c, rows, stride=CHUNKS), :] for c in range(CHUNKS)], axis=1)


def _store_slab_chunks(flat, val, rows, base=0):
    for c in range(CHUNKS):
        flat[pl.ds(base * CHUNKS + c, rows, stride=CHUNKS), :] = val[:, c * LANES:(c + 1) * LANES]


def _rms(t, gain):
    return t * lax.rsqrt(jnp.mean(t * t, axis=-1, keepdims=True) + LN_EPS) * gain


def _layer_norm(z, g, b):
    mu = jnp.mean(z, axis=-1, keepdims=True)
    zc = z - mu
    var = jnp.mean(zc * zc, axis=-1, keepdims=True)
    return zc * lax.rsqrt(var + LN_EPS) * g + b


def _mixout_body(attn_ref, rec_ref, x_ref,
                 ag_ref, rgn_ref, wo_ref, g1_ref, b1_ref, rwh_ref, rwl_ref, rb_ref, tri_ref, ones_ref,
                 x1_ref, x1p_ref, idx_ref, gate_ref, rank_ref, cnt_ref, carry_ref, *, alpha):
    tm = TM_MIX

    @pl.when(pl.program_id(0) == 0)
    def _():
        carry_ref[...] = jnp.zeros_like(carry_ref)

    attn_n = _rms(attn_ref[...].astype(f32), ag_ref[...]).astype(bf16)
    rec_n = _rms(rec_ref[...], rgn_ref[...]).astype(bf16)
    y = (jnp.dot(attn_n, wo_ref[0:ATTN_W, :], preferred_element_type=f32)
         + jnp.dot(rec_n, wo_ref[ATTN_W:, :], preferred_element_type=f32))
    x1 = _layer_norm(alpha * x_ref[...] + y, g1_ref[...], b1_ref[...])
    x1_ref[...] = x1
    _store_slab_chunks(x1p_ref, _pack_rows(x1), tm)

    hi = x1.astype(bf16)
    lo = (x1 - hi.astype(f32)).astype(bf16)
    nt = (((1,), (1,)), ((), ()))
    logits = (lax.dot_general(rwh_ref[...], hi, nt, preferred_element_type=f32)
              + lax.dot_general(rwh_ref[...], lo, nt, preferred_element_type=f32)
              + lax.dot_general(rwl_ref[...], hi, nt, preferred_element_type=f32))
    scores = jax.nn.sigmoid(logits)
    biased = scores + jnp.concatenate([rb_ref[...]] * (tm // LANES), axis=1)

    rid = lax.broadcasted_iota(i32, (GROUP_SZ, tm), 0).astype(f32)
    grp = []
    for g in range(N_GROUPS):
        vg = biased[g * GROUP_SZ:(g + 1) * GROUP_SZ, :]
        m1 = jnp.max(vg, axis=0, keepdims=True)
        first = jnp.min(jnp.where(vg == m1, rid, float(GROUP_SZ)), axis=0, keepdims=True)
        m2 = jnp.max(jnp.where(rid == first, -jnp.inf, vg), axis=0, keepdims=True)
        grp.append(m1 + m2)
    eid = lax.broadcasted_iota(i32, (N_EXPERTS, tm), 0).astype(f32)
    egid = jnp.floor(eid * (1.0 / GROUP_SZ))
    masked = jnp.full((N_EXPERTS, tm), -jnp.inf, f32)
    for _ in range(TOPK_GROUPS):
        gm = functools.reduce(jnp.maximum, grp)
        gi = jnp.full((1, tm), float(N_GROUPS), f32)
        for g in range(N_GROUPS - 1, -1, -1):
            gi = jnp.where(grp[g] == gm, float(g), gi)
        grp = [jnp.where(gi == float(g), -jnp.inf, grp[g]) for g in range(N_GROUPS)]
        masked = jnp.where(egid == gi, biased, masked)

    onehot = jnp.zeros((N_EXPERTS, tm), f32)
    idxs, gts = [], []
    for _ in range(TOP_K):
        mx = jnp.max(masked, axis=0, keepdims=True)
        ix = jnp.min(jnp.where(masked == mx, eid, float(N_EXPERTS)), axis=0, keepdims=True)
        hit = eid == ix
        gts.append(jnp.sum(jnp.where(hit, scores, 0.0), axis=0, keepdims=True))
        idxs.append(ix)
        masked = jnp.where(hit, -jnp.inf, masked)
        onehot = onehot + jnp.where(hit, 1.0, 0.0)
    gsum = functools.reduce(lambda p, q: p + q, gts)
    for kk in range(TOP_K):
        idx_ref[kk:kk + 1, :] = idxs[kk].astype(i32)
        gate_ref[kk:kk + 1, :] = gts[kk] / gsum * ROUTED_SCALE

    oh = onehot.astype(bf16)
    before = carry_ref[...] + jnp.dot(oh, tri_ref[...], preferred_element_type=f32)
    for kk in range(TOP_K):
        rk = jnp.sum(jnp.where(eid == idxs[kk], before, 0.0), axis=0, keepdims=True)
        rank_ref[kk:kk + 1, :] = rk.astype(i32)
    total = carry_ref[...] + jnp.dot(oh, ones_ref[...], preferred_element_type=f32)
    carry_ref[...] = total
    cnt_ref[...] = total


def _mixout(attn, rec, x2, attn_gain, rec_gain, w_out_b, g1, b1, rw_hi, rw_lo, rbias, alpha):
    T = x2.shape[0]
    tm = TM_MIX
    tok = lambda i: (i, 0)
    fixed = lambda i: (0, 0)
    tri = jnp.asarray(np.triu(np.ones((tm, tm), np.float32), k=1), bf16)
    ones = jnp.ones((tm, tm), bf16)
    aw = pl.BlockSpec((tm, ATTN_W), tok)
    row = lambda n: pl.BlockSpec((1, n), fixed)
    kt = pl.BlockSpec((TOP_K, tm), lambda i: (0, i))
    return pl.pallas_call(
        functools.partial(_mixout_body, alpha=alpha),
        grid=(T // tm,),
        in_specs=[aw] * 2 + [pl.BlockSpec((tm, D_MODEL), tok), row(ATTN_W), row(REC_W),
                             pl.BlockSpec((D_MODEL, D_MODEL), fixed), row(D_MODEL), row(D_MODEL),
                             pl.BlockSpec((N_EXPERTS, D_MODEL), fixed),
                             pl.BlockSpec((N_EXPERTS, D_MODEL), fixed),
                             pl.BlockSpec((N_EXPERTS, LANES), fixed),
                             pl.BlockSpec((tm, tm), fixed), pl.BlockSpec((tm, tm), fixed)],
        out_specs=[pl.BlockSpec((tm, D_MODEL), tok),
                   pl.BlockSpec((tm * CHUNKS, LANES), tok),
                   kt, kt, kt, pl.BlockSpec((N_EXPERTS, tm), fixed)],
        out_shape=[jax.ShapeDtypeStruct((T, D_MODEL), f32),
                   jax.ShapeDtypeStruct((T * CHUNKS, LANES), u32),
                   jax.ShapeDtypeStruct((TOP_K, T), i32),
                   jax.ShapeDtypeStruct((TOP_K, T), f32),
                   jax.ShapeDtypeStruct((TOP_K, T), i32),
                   jax.ShapeDtypeStruct((N_EXPERTS, tm), f32)],
        scratch_shapes=[pltpu.VMEM((N_EXPERTS, tm), f32)],
        compiler_params=_cparams(("arbitrary",)),
        name="mixout",
    )(attn, rec, x2, attn_gain, rec_gain, w_out_b, g1, b1,
      rw_hi, rw_lo, rbias, tri, ones)


def _slots_body(idx_ref, rank_ref, start_ref, dest_ref):
    tm = TM_SLOT
    eid = lax.broadcasted_iota(i32, (N_EXPERTS, tm), 0)
    start = jnp.concatenate([start_ref[...]] * (tm // LANES), axis=1)
    for kk in range(TOP_K):
        hit = eid == idx_ref[kk:kk + 1, :]
        base = jnp.sum(jnp.where(hit, start, 0.0), axis=0, keepdims=True)
        dest_ref[kk:kk + 1, :] = base.astype(i32) + rank_ref[kk:kk + 1, :]


def _slots(idx, rank, pad_start):
    T = idx.shape[1]
    tm = TM_SLOT
    kt = pl.BlockSpec((TOP_K, tm), lambda i: (0, i))
    start = jnp.broadcast_to(pad_start.astype(f32).reshape(N_EXPERTS, 1), (N_EXPERTS, LANES))
    return pl.pallas_call(
        _slots_body,
        grid=(T // tm,),
        in_specs=[kt, kt, pl.BlockSpec((N_EXPERTS, LANES), lambda i: (0, 0))],
        out_specs=kt,
        out_shape=jax.ShapeDtypeStruct((TOP_K, T), i32),
        compiler_params=_cparams(("parallel",)),
        name="slots",
    )(idx, rank, start)


def _dispatch_body(dest_ref, x_ref, xs_ref, sem):
    tm = TM_MIX

    def copy(t, k):
        return pltpu.make_async_copy(x_ref.at[t], xs_ref.at[dest_ref[k, t]], sem)

    def issue(t, carry):
        for k in range(TOP_K):
            copy(t, k).start(priority=k % DMA_THREADS)
        return carry

    def drain(t, carry):
        for k in range(TOP_K):
            copy(t, k).wait()
        return carry

    lax.fori_loop(0, tm, issue, 0)
    lax.fori_loop(0, tm, drain, 0)


def _dispatch(dest, x1p, n_slots):
    T = x1p.shape[0]
    tm = TM_MIX
    return pl.pallas_call(
        _dispatch_body,
        grid=(T // tm,),
        in_specs=[pl.BlockSpec((TOP_K, tm), lambda i: (0, i), memory_space=pltpu.SMEM),
                  pl.BlockSpec((tm, CHUNKS, LANES), lambda i: (i, 0, 0))],
        out_specs=pl.BlockSpec(memory_space=pl.ANY),
        out_shape=jax.ShapeDtypeStruct((n_slots, CHUNKS, LANES), u32),
        scratch_shapes=[pltpu.SemaphoreType.DMA(())],
        compiler_params=_cparams(("arbitrary",)),
        name="dispatch",
    )(dest, x1p)


def _experts_body(be_ref, nx_ref, nv_ref, nb_ref, xs_ref, wg_hbm, wu_hbm, wd_hbm, ys_ref,
                  fga, fua, fda, fgb, fub, fdb, sga, sua, sda, sgb, sub, sdb, sems):
    p = pl.program_id(0)
    blocks = (2 * p, 2 * p + 1)
    hbm_w = (wg_hbm, wu_hbm, wd_hbm)
    f32_w = ((fga, fua, fda), (fgb, fub, fdb))
    bf_w = ((sga, sua, sda), (sgb, sub, sdb))
    live_step = blocks[0] < nb_ref[0]

    def fetch(half, e):
        return [pltpu.make_async_copy(src.at[e], dst, sems.at[half])
                for src, dst in zip(hbm_w, f32_w[half])]

    for half, blk in enumerate(blocks):
        @pl.when(jnp.logical_and(live_step, p == 0))
        def _(half=half, blk=blk):
            for c in fetch(half, be_ref[blk]):
                c.start()

        prev = jnp.maximum(blk - 2, 0)

        @pl.when(jnp.logical_and(live_step,
                                 jnp.logical_or(p == 0, be_ref[blk] != be_ref[prev])))
        def _(half=half, blk=blk):
            for c in fetch(half, be_ref[blk]):
                c.wait()
            for src, dst in zip(f32_w[half], bf_w[half]):
                dst[...] = src[...].astype(bf16)

            @pl.when(nx_ref[blk] >= 0)
            def _():
                for c in fetch(half, nx_ref[blk]):
                    c.start()

    @pl.when(live_step)
    def _():
        xs = []
        for half, blk in enumerate(blocks):
            rows = jnp.where(blk < nb_ref[0], nv_ref[blk], 0)
            live = lax.broadcasted_iota(i32, (BM, WORDS), 0) < rows
            words = _slab_chunks(xs_ref, BM, half * BM)
            xs.append(_unpack_rows(jnp.where(live, words, u32(0))))

        def up(half, w_ref):
            lo, hi = xs[half]
            return (jnp.dot(lo, w_ref[:WORDS, :], preferred_element_type=f32)
                    + jnp.dot(hi, w_ref[WORDS:, :], preferred_element_type=f32))

        gs = [up(half, bf_w[half][0]) for half in range(2)]
        us = [up(half, bf_w[half][1]) for half in range(2)]
        hs = [(g * jax.nn.sigmoid(g) * u).astype(bf16) for g, u in zip(gs, us)]
        ys = [jnp.dot(h, bf_w[half][2][...], preferred_element_type=f32)
              for half, h in enumerate(hs)]
        for half in range(2):
            _store_slab_chunks(ys_ref, _pack_rows(ys[half]), BM, half * BM)


def _experts(block_e, next_e, block_rows, nb_used, xs, wg, wu, wd):
    P = xs.shape[0] // CHUNKS
    nb = P // BM
    assert nb % 2 == 0
    last = lambda nbu: jnp.maximum((nbu[0] - 1) // 2, 0)
    rows = lambda p, be, nx, nv, nbu: (jnp.minimum(p, last(nbu)), 0)
    any_spec = pl.BlockSpec(memory_space=pl.ANY)
    up_shape, down_shape = (D_MODEL, EXPERT_H), (EXPERT_H, D_MODEL)
    per_stream = lambda dt: [pltpu.VMEM(up_shape, dt), pltpu.VMEM(up_shape, dt),
                             pltpu.VMEM(down_shape, dt)]
    gs = pltpu.PrefetchScalarGridSpec(
        num_scalar_prefetch=4,
        grid=(nb // 2,),
        in_specs=[pl.BlockSpec((2 * BM * CHUNKS, LANES), rows), any_spec, any_spec, any_spec],
        out_specs=pl.BlockSpec((2 * BM * CHUNKS, LANES), rows),
        scratch_shapes=per_stream(f32) * 2 + per_stream(bf16) * 2
        + [pltpu.SemaphoreType.DMA((2,))],
    )
    return pl.pallas_call(
        _experts_body,
        grid_spec=gs,
        out_shape=jax.ShapeDtypeStruct((P * CHUNKS, LANES), u32),
        compiler_params=_cparams(("arbitrary",)),
        name="experts",
    )(block_e, next_e, block_rows, nb_used, xs, wg, wu, wd)


def _combine_body(dest_ref, dnext_ref, gate_ref, x1_ref, ys_ref, sg_ref, su_ref, sd_ref,
                  g2_ref, b2_ref, out_ref, gbuf0, gbuf1, sems, *, alpha):
    tm = TM_CMB
    j = pl.program_id(0)
    n = pl.num_programs(0)
    gbufs = (gbuf0, gbuf1)

    def copy(d_ref, base, sl, t, k):
        src = ys_ref.at[d_ref[(base + t) * TOP_K + k]]
        dst = gbufs[sl].at[pl.ds((k * tm + t) * CHUNKS, CHUNKS)]
        return pltpu.make_async_copy(src, dst, sems.at[sl])

    def issue(d_ref, base, sl):
        def step(t, carry):
            for k in range(TOP_K):
                copy(d_ref, base, sl, t, k).start(priority=k % DMA_THREADS)
            return carry
        lax.fori_loop(0, tm, step, 0)

    def finish(half, sl):
        base = half * tm
        rows = pl.ds(base, tm)
        x1 = x1_ref[rows, :]
        xb = x1.astype(bf16)
        sg = jnp.dot(xb, sg_ref[...], preferred_element_type=f32)
        su = jnp.dot(xb, su_ref[...], preferred_element_type=f32)
        shared = jnp.dot((sg * jax.nn.sigmoid(sg) * su).astype(bf16), sd_ref[...],
                         preferred_element_type=f32)

        def drain(t, carry):
            for k in range(TOP_K):
                copy(dest_ref, base, sl, t, k).wait()
            return carry

        lax.fori_loop(0, tm, drain, 0)

        gates = gate_ref[rows, :]
        lo = [jnp.zeros((tm, LANES), f32) for _ in range(CHUNKS)]
        hi = [jnp.zeros((tm, LANES), f32) for _ in range(CHUNKS)]
        flat = gbufs[sl]
        for k in range(TOP_K):
            gk = jnp.broadcast_to(gates[:, k:k + 1], (tm, LANES))
            for c in range(CHUNKS):
                w = flat[pl.ds(k * tm * CHUNKS + c, tm, stride=CHUNKS), :]
                lo[c] = lo[c] + gk * lax.bitcast_convert_type(w << 16, f32)
                hi[c] = hi[c] + gk * lax.bitcast_convert_type(w & u32(0xFFFF0000), f32)
        routed = jnp.concatenate(lo + hi, axis=1)
        out_ref[rows, :] = _layer_norm(alpha * x1 + (routed + shared), g2_ref[...], b2_ref[...])

    @pl.when(j == 0)
    def _():
        issue(dest_ref, 0, 0)

    issue(dest_ref, tm, 1)
    finish(0, 0)

    @pl.when(j + 1 < n)
    def _():
        issue(dnext_ref, 0, 0)

    finish(1, 1)


def _combine(dest, gate, x1, ys, sg, su, sd, g2, b2, alpha):
    T = x1.shape[0]
    tm = TM_CMB
    n = T // (2 * tm)
    tok = lambda j: (j, 0)
    fixed = lambda j: (0, 0)
    tbl = pl.BlockSpec((2 * tm * TOP_K,), lambda j: (j,), memory_space=pltpu.SMEM)
    tbl_next = pl.BlockSpec((2 * tm * TOP_K,), lambda j: (jnp.minimum(j + 1, n - 1),),
                            memory_space=pltpu.SMEM)
    return pl.pallas_call(
        functools.partial(_combine_body, alpha=alpha),
        grid=(n,),
        in_specs=[tbl, tbl_next, pl.BlockSpec((2 * tm, TOP_K), tok),
                  pl.BlockSpec((2 * tm, D_MODEL), tok),
                  pl.BlockSpec(memory_space=pl.ANY),
                  pl.BlockSpec(sg.shape, fixed), pl.BlockSpec(su.shape, fixed),
                  pl.BlockSpec(sd.shape, fixed), pl.BlockSpec((1, D_MODEL), fixed),
                  pl.BlockSpec((1, D_MODEL), fixed)],
        out_specs=pl.BlockSpec((2 * tm, D_MODEL), tok),
        out_shape=jax.ShapeDtypeStruct((T, D_MODEL), f32),
        scratch_shapes=[pltpu.VMEM((TOP_K * tm * CHUNKS, LANES), u32),
                        pltpu.VMEM((TOP_K * tm * CHUNKS, LANES), u32),
                        pltpu.SemaphoreType.DMA((2,))],
        compiler_params=_cparams(("arbitrary",)),
        name="combine",
    )(dest, dest, gate, x1, ys, sg, su, sd, g2, b2)


def _rope_inv_freq():
    half = ROT_DIM // 2
    inv = ROPE_THETA ** (-jnp.arange(half, dtype=f32) * 2.0 / ROT_DIM)
    j = np.arange(LANES) % HEAD_DIM
    table = jnp.where(j < ROT_DIM, inv[j % half], 0.0)
    return table.reshape(1, LANES).astype(f32)


def _gate_weights(ga_w, ga_b, gx_w, gx_b):
    ng = REC_W // LANES
    per = LANES // HEAD_DIM
    def bd(w):
        w = w.reshape(ng, per, HEAD_DIM, HEAD_DIM)
        z = jnp.zeros((ng, LANES, LANES), w.dtype)
        for p in range(per):
            z = z.at[:, p * HEAD_DIM:(p + 1) * HEAD_DIM, p * HEAD_DIM:(p + 1) * HEAD_DIM].set(w[:, p])
        return z
    wg = jnp.concatenate([bd(ga_w[0]), bd(gx_w[0]), bd(ga_w[1]), bd(gx_w[1])], axis=-1).astype(bf16)
    grp = lambda b: b.reshape(ng, 1, LANES)
    gb = jnp.concatenate([grp(ga_b[0]), grp(gx_b[0]), grp(ga_b[1]), grp(gx_b[1])], axis=-1)
    return wg, gb


def _layer(x, positions, w_in, attn_gain, conv_w, conv_b, ga_w, ga_b, gx_w, gx_b, lam, rec_gain,
           w_out, ln1_g, ln1_b, router_w, router_bias, e_wg, e_wu, e_wd, s_wg, s_wu, s_wd,
           ln2_g, ln2_b, alpha):
    B, S, _ = x.shape
    T = B * S
    x2 = x.reshape(T, D_MODEL)
    pos2 = positions.reshape(T, 1)

    q, k, v, rx, rg = _inproj(x2, pos2, w_in.astype(bf16), _rope_inv_freq())
    attn = _attention(q, k, v, B, S)
    wg, gb = _gate_weights(ga_w, ga_b, gx_w, gx_b)
    rec = _rec(rx, rg, conv_w, conv_b.reshape(1, REC_W), wg, gb, lam, B, S).reshape(T, REC_W)

    rw_t = router_w.T
    rw_hi = rw_t.astype(bf16)
    rw_lo = (rw_t - rw_hi.astype(f32)).astype(bf16)
    rbias = jnp.broadcast_to(router_bias.reshape(N_EXPERTS, 1), (N_EXPERTS, LANES))
    x1, x1p, idx, gate, rank, cnt = _mixout(
        attn, rec, x2,
        attn_gain.reshape(1, ATTN_W), rec_gain.reshape(1, REC_W), w_out.astype(bf16),
        ln1_g.reshape(1, D_MODEL), ln1_b.reshape(1, D_MODEL), rw_hi, rw_lo, rbias, alpha)

    counts = cnt[:, 0].astype(i32)
    padded = (counts + BM - 1) // BM * BM
    pad_end = jnp.cumsum(padded)
    pad_start = pad_end - padded
    slot = _slots(idx, rank, pad_start)
    nb = (T * TOP_K + N_EXPERTS * (BM - 1)) // BM + 1
    nb += nb % 2
    nb_used = pad_end[-1] // BM
    h = (nb_used + 1) // 2
    pos = jnp.arange(nb, dtype=i32)
    blk = jnp.where(pos % 2 == 0, pos // 2, h + pos // 2)
    live = jnp.logical_and(pos < 2 * h, blk < nb_used)
    first_row = blk * BM
    block_e = jnp.minimum(jnp.sum((pad_end[None, :] <= first_row[:, None]).astype(i32), axis=1),
                          N_EXPERTS - 1)
    block_e = jnp.where(live, block_e, N_EXPERTS - 1)
    block_rows = jnp.where(live, jnp.clip((pad_start + counts)[block_e] - first_row, 0, BM), 0)
    pairs = block_e.reshape(nb // 2, 2)
    step_id = jnp.arange(nb // 2, dtype=i32)[:, None]
    change = jnp.concatenate([pairs[1:] != pairs[:-1], jnp.zeros((1, 2), bool)], axis=0)
    change = jnp.logical_and(change, step_id + 1 < h)
    nxt_step = lax.cummin(jnp.where(change, step_id + 1, nb), axis=0, reverse=True)
    next_e = jnp.where(nxt_step < nb // 2,
                       jnp.take_along_axis(pairs, jnp.minimum(nxt_step, nb // 2 - 1), axis=0), -1)
    next_e = next_e.reshape(nb).astype(i32)
    sblk = slot // BM
    dest = jnp.where(sblk < h, 2 * sblk, 2 * (sblk - h) + 1) * BM + slot % BM
    nb_used = (2 * h).astype(i32).reshape(1)

    xs = _dispatch(dest, x1p.reshape(T, CHUNKS, LANES), nb * BM)
    ys = _experts(block_e.astype(i32), next_e, block_rows.astype(i32), nb_used,
                  xs.reshape(nb * BM * CHUNKS, LANES), e_wg, e_wu, e_wd)
    ys = ys.reshape(nb * BM, CHUNKS, LANES)
    out = _combine(dest.T.reshape(-1), gate.T, x1, ys, s_wg.astype(bf16), s_wu.astype(bf16), s_wd.astype(bf16),
                   ln2_g.reshape(1, D_MODEL), ln2_b.reshape(1, D_MODEL), alpha)
    return out.reshape(B, S, D_MODEL)


def kernel(x, positions, w_in, attn_gain, rec_conv_w, rec_conv_b, rec_gate_a_w, rec_gate_a_b,
           rec_gate_x_w, rec_gate_x_b, rec_lambda, rec_gain, w_out, ln1_g, ln1_b, router_w,
           router_bias, exp_w_gate, exp_w_up, exp_w_down, shared_w_gate, shared_w_up,
           shared_w_down, ln2_g, ln2_b):
    depth = w_in.shape[0]
    alpha = (2 * depth) ** 0.25
    for l in range(depth):
        x = _layer(x, positions, w_in[l], attn_gain[l], rec_conv_w[l], rec_conv_b[l],
                   rec_gate_a_w[l], rec_gate_a_b[l], rec_gate_x_w[l], rec_gate_x_b[l],
                   rec_lambda[l], rec_gain[l], w_out[l], ln1_g[l], ln1_b[l], router_w[l],
                   router_bias[l], exp_w_gate[l], exp_w_up[l], exp_w_down[l], shared_w_gate[l],
                   shared_w_up[l], shared_w_down[l], ln2_g[l], ln2_b[l], alpha)
    return x
```

```python
import functools
import math

import jax
import jax.numpy as jnp
import numpy as np
from jax import lax
from jax.experimental import pallas as pl
from jax.experimental.pallas import tpu as pltpu

f32 = jnp.float32
bf16 = jnp.bfloat16
i32 = jnp.int32

D_MODEL = 1024
ATTN_W = 512
REC_W = 512
HEAD_DIM = 64
ROT_DIM = 16
ROPE_THETA = 500000.0
BRANCHES = ((128, 1), (512, 4), (2048, 16))
HALF_BAND = 64
CONV_W = 4
RG_LRU_C = 8.0
N_EXPERTS = 256
TOP_K = 8
N_GROUPS = 8
GROUP_SZ = N_EXPERTS // N_GROUPS
TOPK_GROUPS = 4
EXPERT_H = 256
ROUTED_SCALE = 2.5
LN_EPS = 1e-5
NEG = -1e30

LANES = 128
SUBLANES = 8
WORDS = D_MODEL // 2
CHUNKS = WORDS // LANES
VMEM_LIMIT = 56 * 1024 * 1024
DMA_THREADS = 2

TM_IN = 512
TQ = 128
KW = TQ + 2 * HALF_BAND
SB = TQ * max(d for _, d in BRANCHES)
MERGE_ROWS = 256
TS = 256
TM_MIX = 256
TM_SLOT = 512
TM_CMB = 128
BM = 256


def _cparams(sem):
    return pltpu.CompilerParams(dimension_semantics=sem, vmem_limit_bytes=VMEM_LIMIT)


def _inproj_body(x_ref, pos_ref, w_ref, invf_ref, q_ref, k_ref, v_ref, rx_ref, rg_ref):
    xb = x_ref[...].astype(bf16)
    ang = pos_ref[...].astype(f32) * invf_ref[...]
    cos = jnp.cos(ang)
    sin = jnp.sin(ang)
    j = lax.broadcasted_iota(i32, (1, LANES), 1) % HEAD_DIM
    half = ROT_DIM // 2
    s_lo = jnp.where(j < half, -sin, 0.0)
    s_hi = jnp.where((j >= half) & (j < ROT_DIM), sin, 0.0)
    rep = ATTN_W // LANES
    cos_w = jnp.concatenate([cos] * rep, axis=1)
    s_lo_w = jnp.concatenate([s_lo] * rep, axis=1)
    s_hi_w = jnp.concatenate([s_hi] * rep, axis=1)

    def proj(c0, n):
        return jnp.dot(xb, w_ref[:, c0:c0 + n], preferred_element_type=f32)

    def rope(t):
        return (t * cos_w + pltpu.roll(t, ATTN_W - half, 1) * s_lo_w
                + pltpu.roll(t, half, 1) * s_hi_w)

    q_ref[...] = rope(proj(0, ATTN_W)) * (HEAD_DIM ** -0.5)
    k_ref[...] = rope(proj(ATTN_W, ATTN_W))
    v_ref[...] = proj(2 * ATTN_W, ATTN_W)
    rx_ref[...] = proj(3 * ATTN_W, REC_W)
    rg_ref[...] = proj(3 * ATTN_W + REC_W, REC_W)


def _inproj(x2, pos2, w_in_b, invf):
    T = x2.shape[0]
    tm = TM_IN
    in_w = w_in_b.shape[1]
    tok = lambda i: (i, 0)
    fixed = lambda i: (0, 0)
    return pl.pallas_call(
        _inproj_body,
        grid=(T // tm,),
        in_specs=[pl.BlockSpec((tm, D_MODEL), tok), pl.BlockSpec((tm, 1), tok),
                  pl.BlockSpec((D_MODEL, in_w), fixed), pl.BlockSpec((1, LANES), fixed)],
        out_specs=[pl.BlockSpec((tm, ATTN_W), tok)] * 3 + [pl.BlockSpec((tm, REC_W), tok)] * 2,
        out_shape=[jax.ShapeDtypeStruct((T, ATTN_W), f32)] * 3
        + [jax.ShapeDtypeStruct((T, REC_W), f32)] * 2,
        compiler_params=_cparams(("parallel",)),
        name="inproj",
    )(x2, pos2, w_in_b, invf)


def _attn_body(q_ref, k_ref, v_ref, out_ref, o1, o2, o3, l1, l2, l3, bias_ref, *, S):
    o_sc, l_sc = (o1, o2, o3), (l1, l2, l3)
    lane = lax.broadcasted_iota(i32, (1, LANES), 1)
    head0 = lane < HEAD_DIM
    rel = (lax.broadcasted_iota(i32, (TQ, KW), 0) - lax.broadcasted_iota(i32, (TQ, KW), 1))
    tiles = SB // TQ
    for case in range(3):
        bias_ref[case] = jnp.where(jnp.abs(rel + case * HALF_BAND) <= HALF_BAND, 0.0, NEG)

    def rows(ref, start, n, d):
        return ref[pl.ds(start, n), :] if d == 1 else ref[pl.ds(start, n, stride=d), :]

    def tile(ti, n0):
        ctx = []
        for g, (_, d) in enumerate(BRANCHES):
            L = S // d
            sh = d.bit_length() - 1
            r = jnp.bitwise_and(ti, d - 1)
            m0 = jnp.right_shift(n0, sh) + jnp.right_shift(ti, sh) * TQ
            ks = jnp.clip(m0 - HALF_BAND, 0, L - KW)
            q = rows(q_ref, r + d * m0, TQ, d).astype(bf16)
            k = rows(k_ref, r + d * ks, KW, d).astype(bf16)
            v = rows(v_ref, r + d * ks, KW, d).astype(bf16)
            bias = bias_ref[(m0 - ks) // HALF_BAND]
            local = r + d * (m0 - jnp.right_shift(n0, sh))
            ss = []
            for sel in (head0, jnp.logical_not(head0)):
                qh = jnp.where(sel, q, jnp.zeros_like(q))
                ss.append(lax.dot_general(qh, k, (((1,), (1,)), ((), ())),
                                          preferred_element_type=f32))
            ctx.append((d, v, bias, local, ss))
        soft = []
        for d, v, bias, local, ss in ctx:
            ps = []
            for s in ss:
                s = s + bias
                m = jnp.max(s, axis=-1, keepdims=True)
                p = jnp.exp(s - m)
                den = jnp.sum(p, axis=-1, keepdims=True)
                ps.append((p.astype(bf16), den, m + jnp.log(den)))
            soft.append(ps)
        for g, ((d, v, bias, local, ss), ps) in enumerate(zip(ctx, soft)):
            outs = [(jnp.dot(p, v, preferred_element_type=f32) / den, lse) for p, den, lse in ps]
            o_val = jnp.where(head0, outs[0][0], outs[1][0])
            l_val = jnp.where(head0, outs[0][1], outs[1][1])
            if d == 1:
                o_sc[g][pl.ds(local, TQ), :] = o_val
                l_sc[g][pl.ds(local, TQ), :] = l_val
            else:
                o_sc[g][pl.ds(local, TQ, stride=d), :] = o_val
                l_sc[g][pl.ds(local, TQ, stride=d), :] = l_val
        return n0

    def merge(c, n0):
        sl = pl.ds(pl.multiple_of(c * MERGE_ROWS, MERGE_ROWS), MERGE_ROWS)
        ls = [l_sc[g][sl, :] for g in range(len(BRANCHES))]
        mx = functools.reduce(jnp.maximum, ls)
        es = [jnp.exp(l - mx) for l in ls]
        num = functools.reduce(lambda a, b: a + b, [e * o_sc[g][sl, :] for g, e in enumerate(es)])
        den = functools.reduce(lambda a, b: a + b, es)
        dst = pl.ds(pl.multiple_of(n0 + c * MERGE_ROWS, MERGE_ROWS), MERGE_ROWS)
        out_ref[dst, :] = (num / den).astype(bf16)
        return n0

    def superblock(sb, carry):
        n0 = pl.multiple_of(sb * SB, SB)
        lax.fori_loop(0, tiles, tile, n0)
        lax.fori_loop(0, SB // MERGE_ROWS, merge, n0)
        return carry

    lax.fori_loop(0, S // SB, superblock, 0)


def _attention(q, k, v, B, S):
    assert S % SB == 0 and all(S // d >= KW for _, d in BRANCHES)
    hp = ATTN_W // LANES
    view = lambda t: t.reshape(B, S, ATTN_W)
    spec = pl.BlockSpec((None, S, LANES), lambda b, h: (b, 0, h))
    out = pl.pallas_call(
        functools.partial(_attn_body, S=S),
        grid=(B, hp),
        in_specs=[spec, spec, spec],
        out_specs=spec,
        out_shape=jax.ShapeDtypeStruct((B, S, ATTN_W), bf16),
        scratch_shapes=[pltpu.VMEM((SB, LANES), f32)] * (2 * len(BRANCHES))
        + [pltpu.VMEM((3, TQ, KW), f32)],
        compiler_params=_cparams(("parallel", "parallel")),
        name="attention",
    )(view(q), view(k), view(v))
    return out.reshape(B * S, ATTN_W)


def _rec_body(rx_ref, rg_ref, cw_ref, cb_ref, wg_ref, gb_ref, lam_ref, out_ref, rxp_ref, hf_ref, *, S):
    nch = S // TS
    ntile = TS // SUBLANES
    zeros8 = jnp.zeros((SUBLANES, LANES), f32)
    rxp_ref[pl.ds(0, SUBLANES), :] = zeros8
    rxp_ref[pl.ds(S + SUBLANES, SUBLANES), :] = zeros8

    def pad_copy(c, carry):
        t0 = pl.multiple_of(c * TS, TS)
        rxp_ref[pl.ds(t0 + SUBLANES, TS), :] = rx_ref[pl.ds(t0, TS), :]
        return carry

    lax.fori_loop(0, nch, pad_copy, 0)

    lam = lam_ref[...]
    neg_sp = -RG_LRU_C * (jnp.maximum(-lam, 0.0) + jnp.log1p(jnp.exp(-jnp.abs(lam))))
    cw = cw_ref[...]
    cb = cb_ref[...]
    sub = lax.broadcasted_iota(i32, (ntile, SUBLANES, LANES), 1)
    nrow = TS + 2 * SUBLANES

    def gates(ci, d):
        t0 = pl.multiple_of(ci * TS, TS)
        xw = rxp_ref[pl.ds(t0, nrow), :]
        u = (cw[0:1] * pltpu.roll(xw, 2, 0) + cw[1:2] * pltpu.roll(xw, 1, 0) + cw[2:3] * xw
             + cw[3:4] * pltpu.roll(xw, nrow - 1, 0))[SUBLANES:SUBLANES + TS] + cb
        c0 = d * 2 * LANES
        g = jnp.dot(u.astype(bf16), wg_ref[:, c0:c0 + 2 * LANES], preferred_element_type=f32)
        g = g + gb_ref[:, c0:c0 + 2 * LANES]
        r = jax.nn.sigmoid(g[:, :LANES])
        gi = jax.nn.sigmoid(g[:, LANES:])
        a = jnp.exp(neg_sp[d:d + 1] * r)
        b = jnp.sqrt(1.0 - a * a) * gi * u
        return a.reshape(ntile, SUBLANES, LANES), b.reshape(ntile, SUBLANES, LANES)

    def chunk(i, carry):
        cf, cbk = carry
        a, b = gates(i, 0)
        for s in (1, 2, 4):
            ok = sub >= s
            a_s = pltpu.roll(a, s, 1)
            b_s = pltpu.roll(b, s, 1)
            b = jnp.where(ok, a * b_s + b, b)
            a = jnp.where(ok, a * a_s, a)
        t0 = pl.multiple_of(i * TS, TS)
        for j in range(ntile):
            h = a[j] * cf + b[j]
            hf_ref[pl.ds(t0 + j * SUBLANES, SUBLANES), :] = h
            cf = h[SUBLANES - 1:SUBLANES, :]
        ib = nch - 1 - i
        a, b = gates(ib, 1)
        for s in (1, 2, 4):
            ok = sub < SUBLANES - s
            a_s = pltpu.roll(a, SUBLANES - s, 1)
            b_s = pltpu.roll(b, SUBLANES - s, 1)
            b = jnp.where(ok, a * b_s + b, b)
            a = jnp.where(ok, a * a_s, a)
        t0 = pl.multiple_of(ib * TS, TS)
        for j in range(ntile - 1, -1, -1):
            h = a[j] * cbk + b[j]
            out_ref[pl.ds(t0 + j * SUBLANES, SUBLANES), :] = h
            cbk = h[0:1, :]
        return cf, cbk

    zrow = jnp.zeros((1, LANES), f32)
    lax.fori_loop(0, nch, chunk, (zrow, zrow))

    def finish(c, carry):
        t0 = pl.multiple_of(c * TS, TS)
        sl = pl.ds(t0, TS)
        out_ref[sl, :] = (hf_ref[sl, :] + out_ref[sl, :]) * jax.nn.gelu(rg_ref[sl, :])
        return carry

    lax.fori_loop(0, nch, finish, 0)


def _rec(rx, rg, conv_w, conv_b, wg, gb, lam, B, S):
    ng = REC_W // LANES
    assert S % TS == 0
    seq = pl.BlockSpec((None, S, LANES), lambda b, c: (b, 0, c))
    return pl.pallas_call(
        functools.partial(_rec_body, S=S),
        grid=(B, ng),
        in_specs=[seq, seq,
                  pl.BlockSpec((CONV_W, LANES), lambda b, c: (0, c)),
                  pl.BlockSpec((1, LANES), lambda b, c: (0, c)),
                  pl.BlockSpec((None, LANES, 4 * LANES), lambda b, c: (c, 0, 0)),
                  pl.BlockSpec((None, 1, 4 * LANES), lambda b, c: (c, 0, 0)),
                  pl.BlockSpec((2, LANES), lambda b, c: (0, c))],
        out_specs=seq,
        out_shape=jax.ShapeDtypeStruct((B, S, REC_W), f32),
        scratch_shapes=[pltpu.VMEM((S + 2 * SUBLANES, LANES), f32), pltpu.VMEM((S, LANES), f32)],
        compiler_params=_cparams(("parallel", "parallel")),
        name="rec",
    )(rx.reshape(B, S, REC_W), rg.reshape(B, S, REC_W), conv_w, conv_b, wg, gb, lam)


u32 = jnp.uint32


def _pack_rows(x):
    bits = lax.bitcast_convert_type(x, u32)
    r = (bits + u32(0x7FFF) + ((bits >> 16) & u32(1))) >> 16
    return (r[:, WORDS:] << 16) | r[:, :WORDS]


def _unpack_rows(w):
    lo = lax.bitcast_convert_type(w << 16, f32).astype(bf16)
    hi = lax.bitcast_convert_type(w & u32(0xFFFF0000), f32).astype(bf16)
    return lo, hi


def _slab_chunks(flat, rows, base=0):
    return jnp.concatenate(
        [flat[pl.ds(base * CHUNKS + c, rows, stride=CHUNKS), :] for c in range(CHUNKS)], axis=1)


def _store_slab_chunks(flat, val, rows, base=0):
    for c in range(CHUNKS):
        flat[pl.ds(base * CHUNKS + c, rows, stride=CHUNKS), :] = val[:, c * LANES:(c + 1) * LANES]


def _rms(t, gain):
    return t * lax.rsqrt(jnp.mean(t * t, axis=-1, keepdims=True) + LN_EPS) * gain


def _layer_norm(z, g, b):
    mu = jnp.mean(z, axis=-1, keepdims=True)
    zc = z - mu
    var = jnp.mean(zc * zc, axis=-1, keepdims=True)
    return zc * lax.rsqrt(var + LN_EPS) * g + b


def _mixout_body(attn_ref, rec_ref, x_ref,
                 ag_ref, rgn_ref, wo_ref, g1_ref, b1_ref, rwh_ref, rwl_ref, rb_ref, tri_ref, ones_ref,
                 x1_ref, x1p_ref, idx_ref, gate_ref, rank_ref, cnt_ref, carry_ref, *, alpha):
    tm = TM_MIX

    @pl.when(pl.program_id(0) == 0)
    def _():
        carry_ref[...] = jnp.zeros_like(carry_ref)

    attn_n = _rms(attn_ref[...].astype(f32), ag_ref[...]).astype(bf16)
    rec_n = _rms(rec_ref[...], rgn_ref[...]).astype(bf16)
    y = (jnp.dot(attn_n, wo_ref[0:ATTN_W, :], preferred_element_type=f32)
         + jnp.dot(rec_n, wo_ref[ATTN_W:, :], preferred_element_type=f32))
    x1 = _layer_norm(alpha * x_ref[...] + y, g1_ref[...], b1_ref[...])
    x1_ref[...] = x1
    _store_slab_chunks(x1p_ref, _pack_rows(x1), tm)

    hi = x1.astype(bf16)
    lo = (x1 - hi.astype(f32)).astype(bf16)
    nt = (((1,), (1,)), ((), ()))
    logits = (lax.dot_general(rwh_ref[...], hi, nt, preferred_element_type=f32)
              + lax.dot_general(rwh_ref[...], lo, nt, preferred_element_type=f32)
              + lax.dot_general(rwl_ref[...], hi, nt, preferred_element_type=f32))
    scores = jax.nn.sigmoid(logits)
    biased = scores + jnp.concatenate([rb_ref[...]] * (tm // LANES), axis=1)

    rid = lax.broadcasted_iota(i32, (GROUP_SZ, tm), 0).astype(f32)
    grp = []
    for g in range(N_GROUPS):
        vg = biased[g * GROUP_SZ:(g + 1) * GROUP_SZ, :]
        m1 = jnp.max(vg, axis=0, keepdims=True)
        first = jnp.min(jnp.where(vg == m1, rid, float(GROUP_SZ)), axis=0, keepdims=True)
        m2 = jnp.max(jnp.where(rid == first, -jnp.inf, vg), axis=0, keepdims=True)
        grp.append(m1 + m2)
    eid = lax.broadcasted_iota(i32, (N_EXPERTS, tm), 0).astype(f32)
    keep = [jnp.zeros((1, tm), f32) for _ in range(N_GROUPS)]
    for _ in range(TOPK_GROUPS):
        gm = functools.reduce(jnp.maximum, grp)
        gi = jnp.full((1, tm), float(N_GROUPS), f32)
        for g in range(N_GROUPS - 1, -1, -1):
            gi = jnp.where(grp[g] == gm, float(g), gi)
        hits = [gi == float(g) for g in range(N_GROUPS)]
        grp = [jnp.where(hit, -jnp.inf, sc) for hit, sc in zip(hits, grp)]
        keep = [jnp.where(hit, 1.0, kp) for hit, kp in zip(hits, keep)]
    masked = jnp.concatenate(
        [jnp.where(jnp.broadcast_to(keep[g], (GROUP_SZ, tm)) > 0.5,
                   biased[g * GROUP_SZ:(g + 1) * GROUP_SZ, :], -jnp.inf)
         for g in range(N_GROUPS)], axis=0)

    onehot = jnp.zeros((N_EXPERTS, tm), f32)
    idxs, gts = [], []
    for _ in range(TOP_K):
        mx = jnp.max(masked, axis=0, keepdims=True)
        ix = jnp.min(jnp.where(masked == mx, eid, float(N_EXPERTS)), axis=0, keepdims=True)
        hit = eid == ix
        gts.append(jnp.sum(jnp.where(hit, scores, 0.0), axis=0, keepdims=True))
        idxs.append(ix)
        masked = jnp.where(hit, -jnp.inf, masked)
        onehot = onehot + jnp.where(hit, 1.0, 0.0)
    gsum = functools.reduce(lambda p, q: p + q, gts)
    for kk in range(TOP_K):
        idx_ref[kk:kk + 1, :] = idxs[kk].astype(i32)
        gate_ref[kk:kk + 1, :] = gts[kk] / gsum * ROUTED_SCALE

    oh = onehot.astype(bf16)
    before = carry_ref[...] + jnp.dot(oh, tri_ref[...], preferred_element_type=f32)
    for kk in range(TOP_K):
        rk = jnp.sum(jnp.where(eid == idxs[kk], before, 0.0), axis=0, keepdims=True)
        rank_ref[kk:kk + 1, :] = rk.astype(i32)
    total = carry_ref[...] + jnp.dot(oh, ones_ref[...], preferred_element_type=f32)
    carry_ref[...] = total
    cnt_ref[...] = total


def _mixout(attn, rec, x2, attn_gain, rec_gain, w_out_b, g1, b1, rw_hi, rw_lo, rbias, alpha):
    T = x2.shape[0]
    tm = TM_MIX
    tok = lambda i: (i, 0)
    fixed = lambda i: (0, 0)
    tri = jnp.asarray(np.triu(np.ones((tm, tm), np.float32), k=1), bf16)
    ones = jnp.ones((tm, tm), bf16)
    aw = pl.BlockSpec((tm, ATTN_W), tok)
    row = lambda n: pl.BlockSpec((1, n), fixed)
    kt = pl.BlockSpec((TOP_K, tm), lambda i: (0, i))
    return pl.pallas_call(
        functools.partial(_mixout_body, alpha=alpha),
        grid=(T // tm,),
        in_specs=[aw] * 2 + [pl.BlockSpec((tm, D_MODEL), tok), row(ATTN_W), row(REC_W),
                             pl.BlockSpec((D_MODEL, D_MODEL), fixed), row(D_MODEL), row(D_MODEL),
                             pl.BlockSpec((N_EXPERTS, D_MODEL), fixed),
                             pl.BlockSpec((N_EXPERTS, D_MODEL), fixed),
                             pl.BlockSpec((N_EXPERTS, LANES), fixed),
                             pl.BlockSpec((tm, tm), fixed), pl.BlockSpec((tm, tm), fixed)],
        out_specs=[pl.BlockSpec((tm, D_MODEL), tok),
                   pl.BlockSpec((tm * CHUNKS, LANES), tok),
                   kt, kt, kt, pl.BlockSpec((N_EXPERTS, tm), fixed)],
        out_shape=[jax.ShapeDtypeStruct((T, D_MODEL), f32),
                   jax.ShapeDtypeStruct((T * CHUNKS, LANES), u32),
                   jax.ShapeDtypeStruct((TOP_K, T), i32),
                   jax.ShapeDtypeStruct((TOP_K, T), f32),
                   jax.ShapeDtypeStruct((TOP_K, T), i32),
                   jax.ShapeDtypeStruct((N_EXPERTS, tm), f32)],
        scratch_shapes=[pltpu.VMEM((N_EXPERTS, tm), f32)],
        compiler_params=_cparams(("arbitrary",)),
        name="mixout",
    )(attn, rec, x2, attn_gain, rec_gain, w_out_b, g1, b1,
      rw_hi, rw_lo, rbias, tri, ones)


def _slots_body(idx_ref, rank_ref, start_ref, dest_ref):
    tm = TM_SLOT
    eid = lax.broadcasted_iota(i32, (N_EXPERTS, tm), 0)
    start = jnp.concatenate([start_ref[...]] * (tm // LANES), axis=1)
    for kk in range(TOP_K):
        hit = eid == idx_ref[kk:kk + 1, :]
        base = jnp.sum(jnp.where(hit, start, 0.0), axis=0, keepdims=True)
        dest_ref[kk:kk + 1, :] = base.astype(i32) + rank_ref[kk:kk + 1, :]


def _slots(idx, rank, pad_start):
    T = idx.shape[1]
    tm = TM_SLOT
    kt = pl.BlockSpec((TOP_K, tm), lambda i: (0, i))
    start = jnp.broadcast_to(pad_start.astype(f32).reshape(N_EXPERTS, 1), (N_EXPERTS, LANES))
    return pl.pallas_call(
        _slots_body,
        grid=(T // tm,),
        in_specs=[kt, kt, pl.BlockSpec((N_EXPERTS, LANES), lambda i: (0, 0))],
        out_specs=kt,
        out_shape=jax.ShapeDtypeStruct((TOP_K, T), i32),
        compiler_params=_cparams(("parallel",)),
        name="slots",
    )(idx, rank, start)


def _dispatch_body(dest_ref, x_ref, xs_ref, sem):
    tm = TM_MIX

    def copy(t, k):
        return pltpu.make_async_copy(x_ref.at[t], xs_ref.at[dest_ref[k, t]], sem)

    def issue(t, carry):
        for k in range(TOP_K):
            copy(t, k).start(priority=k % DMA_THREADS)
        return carry

    def drain(t, carry):
        for k in range(TOP_K):
            copy(t, k).wait()
        return carry

    lax.fori_loop(0, tm, issue, 0)
    lax.fori_loop(0, tm, drain, 0)


def _dispatch(dest, x1p, n_slots):
    T = x1p.shape[0]
    tm = TM_MIX
    return pl.pallas_call(
        _dispatch_body,
        grid=(T // tm,),
        in_specs=[pl.BlockSpec((TOP_K, tm), lambda i: (0, i), memory_space=pltpu.SMEM),
                  pl.BlockSpec((tm, CHUNKS, LANES), lambda i: (i, 0, 0))],
        out_specs=pl.BlockSpec(memory_space=pl.ANY),
        out_shape=jax.ShapeDtypeStruct((n_slots, CHUNKS, LANES), u32),
        scratch_shapes=[pltpu.SemaphoreType.DMA(())],
        compiler_params=_cparams(("arbitrary",)),
        name="dispatch",
    )(dest, x1p)


def _experts_body(be_ref, nx_ref, nv_ref, nb_ref, xs_ref, wg_hbm, wu_hbm, wd_hbm, ys_ref,
                  fga, fua, fda, fgb, fub, fdb, sga, sua, sda, sgb, sub, sdb, sems):
    p = pl.program_id(0)
    blocks = (2 * p, 2 * p + 1)
    hbm_w = (wg_hbm, wu_hbm, wd_hbm)
    f32_w = ((fga, fua, fda), (fgb, fub, fdb))
    bf_w = ((sga, sua, sda), (sgb, sub, sdb))
    live_step = blocks[0] < nb_ref[0]

    def fetch(half, e):
        return [pltpu.make_async_copy(src.at[e], dst, sems.at[half])
                for src, dst in zip(hbm_w, f32_w[half])]

    for half, blk in enumerate(blocks):
        @pl.when(jnp.logical_and(live_step, p == 0))
        def _(half=half, blk=blk):
            for c in fetch(half, be_ref[blk]):
                c.start()

        prev = jnp.maximum(blk - 2, 0)

        @pl.when(jnp.logical_and(live_step,
                                 jnp.logical_or(p == 0, be_ref[blk] != be_ref[prev])))
        def _(half=half, blk=blk):
            for c in fetch(half, be_ref[blk]):
                c.wait()
            for src, dst in zip(f32_w[half], bf_w[half]):
                dst[...] = src[...].astype(bf16)

            @pl.when(nx_ref[blk] >= 0)
            def _():
                for c in fetch(half, nx_ref[blk]):
                    c.start()

    @pl.when(live_step)
    def _():
        xs = []
        for half, blk in enumerate(blocks):
            rows = jnp.where(blk < nb_ref[0], nv_ref[blk], 0)
            live = lax.broadcasted_iota(i32, (BM, WORDS), 0) < rows
            words = _slab_chunks(xs_ref, BM, half * BM)
            xs.append(_unpack_rows(jnp.where(live, words, u32(0))))

        def up(half, w_ref):
            lo, hi = xs[half]
            return (jnp.dot(lo, w_ref[:WORDS, :], preferred_element_type=f32)
                    + jnp.dot(hi, w_ref[WORDS:, :], preferred_element_type=f32))

        gs = [up(half, bf_w[half][0]) for half in range(2)]
        us = [up(half, bf_w[half][1]) for half in range(2)]
        hs = [(g * jax.nn.sigmoid(g) * u).astype(bf16) for g, u in zip(gs, us)]
        ys = [jnp.dot(h, bf_w[half][2][...], preferred_element_type=f32)
              for half, h in enumerate(hs)]
        for half in range(2):
            _store_slab_chunks(ys_ref, _pack_rows(ys[half]), BM, half * BM)


def _experts(block_e, next_e, block_rows, nb_used, xs, wg, wu, wd):
    P = xs.shape[0] // CHUNKS
    nb = P // BM
    assert nb % 2 == 0
    last = lambda nbu: jnp.maximum((nbu[0] - 1) // 2, 0)
    rows = lambda p, be, nx, nv, nbu: (jnp.minimum(p, last(nbu)), 0)
    any_spec = pl.BlockSpec(memory_space=pl.ANY)
    up_shape, down_shape = (D_MODEL, EXPERT_H), (EXPERT_H, D_MODEL)
    per_stream = lambda dt: [pltpu.VMEM(up_shape, dt), pltpu.VMEM(up_shape, dt),
                             pltpu.VMEM(down_shape, dt)]
    gs = pltpu.PrefetchScalarGridSpec(
        num_scalar_prefetch=4,
        grid=(nb // 2,),
        in_specs=[pl.BlockSpec((2 * BM * CHUNKS, LANES), rows), any_spec, any_spec, any_spec],
        out_specs=pl.BlockSpec((2 * BM * CHUNKS, LANES), rows),
        scratch_shapes=per_stream(f32) * 2 + per_stream(bf16) * 2
        + [pltpu.SemaphoreType.DMA((2,))],
    )
    return pl.pallas_call(
        _experts_body,
        grid_spec=gs,
        out_shape=jax.ShapeDtypeStruct((P * CHUNKS, LANES), u32),
        compiler_params=_cparams(("arbitrary",)),
        name="experts",
    )(block_e, next_e, block_rows, nb_used, xs, wg, wu, wd)


def _combine_body(dest_ref, dnext_ref, gate_ref, x1_ref, ys_ref, ysflat_ref, sg_ref, su_ref, sd_ref,
                  g2_ref, b2_ref, out_ref, gb00, gb01, gb10, gb11, sems, *, alpha):
    tm = TM_CMB
    j = pl.program_id(0)
    n = pl.num_programs(0)
    gbufs = ((gb00, gb01), (gb10, gb11))

    def issue(d_ref, st, half):
        base = half * tm

        def step(t, carry):
            for k in range(TOP_K):
                src = ys_ref.at[d_ref[(base + t) * TOP_K + k]]
                dst = gbufs[st][half].at[pl.ds((k * tm + t) * CHUNKS, CHUNKS)]
                pltpu.make_async_copy(src, dst, sems.at[st, half]).start(priority=k % DMA_THREADS)
            return carry
        lax.fori_loop(0, tm, step, 0)

    def finish(st, half):
        rows = pl.ds(half * tm, tm)
        x1 = x1_ref[rows, :]
        xb = x1.astype(bf16)
        sg = jnp.dot(xb, sg_ref[...], preferred_element_type=f32)
        su = jnp.dot(xb, su_ref[...], preferred_element_type=f32)
        shared = jnp.dot((sg * jax.nn.sigmoid(sg) * su).astype(bf16), sd_ref[...],
                         preferred_element_type=f32)
        flat = gbufs[st][half]
        pltpu.make_async_copy(ysflat_ref.at[pl.ds(0, TOP_K * tm * CHUNKS)], flat,
                              sems.at[st, half]).wait()

        gates = gate_ref[rows, :]
        lo = [jnp.zeros((tm, LANES), f32) for _ in range(CHUNKS)]
        hi = [jnp.zeros((tm, LANES), f32) for _ in range(CHUNKS)]
        for k in range(TOP_K):
            gk = jnp.broadcast_to(gates[:, k:k + 1], (tm, LANES))
            for c in range(CHUNKS):
                w = flat[pl.ds(k * tm * CHUNKS + c, tm, stride=CHUNKS), :]
                lo[c] = lo[c] + gk * lax.bitcast_convert_type(w << 16, f32)
                hi[c] = hi[c] + gk * lax.bitcast_convert_type(w & u32(0xFFFF0000), f32)
        routed = jnp.concatenate(lo + hi, axis=1)
        out_ref[rows, :] = _layer_norm(alpha * x1 + (routed + shared), g2_ref[...], b2_ref[...])

    @pl.when(j == 0)
    def _():
        issue(dest_ref, 0, 0)
        issue(dest_ref, 0, 1)

    for st in range(2):
        @pl.when(lax.rem(j, 2) == st)
        def _(st=st):
            @pl.when(j + 1 < n)
            def _():
                issue(dnext_ref, 1 - st, 0)
                issue(dnext_ref, 1 - st, 1)

            finish(st, 0)
            finish(st, 1)


def _combine(dest, gate, x1, ys, sg, su, sd, g2, b2, alpha):
    T = x1.shape[0]
    tm = TM_CMB
    n = T // (2 * tm)
    tok = lambda j: (j, 0)
    fixed = lambda j: (0, 0)
    tbl = pl.BlockSpec((2 * tm * TOP_K,), lambda j: (j,), memory_space=pltpu.SMEM)
    tbl_next = pl.BlockSpec((2 * tm * TOP_K,), lambda j: (jnp.minimum(j + 1, n - 1),),
                            memory_space=pltpu.SMEM)
    any_spec = pl.BlockSpec(memory_space=pl.ANY)
    return pl.pallas_call(
        functools.partial(_combine_body, alpha=alpha),
        grid=(n,),
        in_specs=[tbl, tbl_next, pl.BlockSpec((2 * tm, TOP_K), tok),
                  pl.BlockSpec((2 * tm, D_MODEL), tok), any_spec, any_spec,
                  pl.BlockSpec(sg.shape, fixed), pl.BlockSpec(su.shape, fixed),
                  pl.BlockSpec(sd.shape, fixed), pl.BlockSpec((1, D_MODEL), fixed),
                  pl.BlockSpec((1, D_MODEL), fixed)],
        out_specs=pl.BlockSpec((2 * tm, D_MODEL), tok),
        out_shape=jax.ShapeDtypeStruct((T, D_MODEL), f32),
        scratch_shapes=[pltpu.VMEM((TOP_K * tm * CHUNKS, LANES), u32)] * 4
        + [pltpu.SemaphoreType.DMA((2, 2))],
        compiler_params=_cparams(("arbitrary",)),
        name="combine",
    )(dest, dest, gate, x1, ys, ys.reshape(ys.shape[0] * CHUNKS, LANES), sg, su, sd, g2, b2)


def _rope_inv_freq():
    half = ROT_DIM // 2
    inv = ROPE_THETA ** (-jnp.arange(half, dtype=f32) * 2.0 / ROT_DIM)
    j = np.arange(LANES) % HEAD_DIM
    table = jnp.where(j < ROT_DIM, inv[j % half], 0.0)
    return table.reshape(1, LANES).astype(f32)


def _gate_weights(ga_w, ga_b, gx_w, gx_b):
    ng = REC_W // LANES
    per = LANES // HEAD_DIM
    def bd(w):
        w = w.reshape(ng, per, HEAD_DIM, HEAD_DIM)
        z = jnp.zeros((ng, LANES, LANES), w.dtype)
        for p in range(per):
            z = z.at[:, p * HEAD_DIM:(p + 1) * HEAD_DIM, p * HEAD_DIM:(p + 1) * HEAD_DIM].set(w[:, p])
        return z
    wg = jnp.concatenate([bd(ga_w[0]), bd(gx_w[0]), bd(ga_w[1]), bd(gx_w[1])], axis=-1).astype(bf16)
    grp = lambda b: b.reshape(ng, 1, LANES)
    gb = jnp.concatenate([grp(ga_b[0]), grp(gx_b[0]), grp(ga_b[1]), grp(gx_b[1])], axis=-1)
    return wg, gb


def _layer(x, positions, w_in, attn_gain, conv_w, conv_b, ga_w, ga_b, gx_w, gx_b, lam, rec_gain,
           w_out, ln1_g, ln1_b, router_w, router_bias, e_wg, e_wu, e_wd, s_wg, s_wu, s_wd,
           ln2_g, ln2_b, alpha):
    B, S, _ = x.shape
    T = B * S
    x2 = x.reshape(T, D_MODEL)
    pos2 = positions.reshape(T, 1)

    q, k, v, rx, rg = _inproj(x2, pos2, w_in.astype(bf16), _rope_inv_freq())
    attn = _attention(q, k, v, B, S)
    wg, gb = _gate_weights(ga_w, ga_b, gx_w, gx_b)
    rec = _rec(rx, rg, conv_w, conv_b.reshape(1, REC_W), wg, gb, lam, B, S).reshape(T, REC_W)

    rw_t = router_w.T
    rw_hi = rw_t.astype(bf16)
    rw_lo = (rw_t - rw_hi.astype(f32)).astype(bf16)
    rbias = jnp.broadcast_to(router_bias.reshape(N_EXPERTS, 1), (N_EXPERTS, LANES))
    x1, x1p, idx, gate, rank, cnt = _mixout(
        attn, rec, x2,
        attn_gain.reshape(1, ATTN_W), rec_gain.reshape(1, REC_W), w_out.astype(bf16),
        ln1_g.reshape(1, D_MODEL), ln1_b.reshape(1, D_MODEL), rw_hi, rw_lo, rbias, alpha)

    counts = cnt[:, 0].astype(i32)
    padded = (counts + BM - 1) // BM * BM
    pad_end = jnp.cumsum(padded)
    pad_start = pad_end - padded
    slot = _slots(idx, rank, pad_start)
    nb = (T * TOP_K + N_EXPERTS * (BM - 1)) // BM + 1
    nb += nb % 2
    nb_used = pad_end[-1] // BM
    h = (nb_used + 1) // 2
    pos = jnp.arange(nb, dtype=i32)
    blk = jnp.where(pos % 2 == 0, pos // 2, h + pos // 2)
    live = jnp.logical_and(pos < 2 * h, blk < nb_used)
    first_row = blk * BM
    block_e = jnp.minimum(jnp.sum((pad_end[None, :] <= first_row[:, None]).astype(i32), axis=1),
                          N_EXPERTS - 1)
    block_e = jnp.where(live, block_e, N_EXPERTS - 1)
    block_rows = jnp.where(live, jnp.clip((pad_start + counts)[block_e] - first_row, 0, BM), 0)
    pairs = block_e.reshape(nb // 2, 2)
    step_id = jnp.arange(nb // 2, dtype=i32)[:, None]
    change = jnp.concatenate([pairs[1:] != pairs[:-1], jnp.zeros((1, 2), bool)], axis=0)
    change = jnp.logical_and(change, step_id + 1 < h)
    nxt_step = lax.cummin(jnp.where(change, step_id + 1, nb), axis=0, reverse=True)
    next_e = jnp.where(nxt_step < nb // 2,
                       jnp.take_along_axis(pairs, jnp.minimum(nxt_step, nb // 2 - 1), axis=0), -1)
    next_e = next_e.reshape(nb).astype(i32)
    sblk = slot // BM
    dest = jnp.where(sblk < h, 2 * sblk, 2 * (sblk - h) + 1) * BM + slot % BM
    nb_used = (2 * h).astype(i32).reshape(1)

    xs = _dispatch(dest, x1p.reshape(T, CHUNKS, LANES), nb * BM)
    ys = _experts(block_e.astype(i32), next_e, block_rows.astype(i32), nb_used,
                  xs.reshape(nb * BM * CHUNKS, LANES), e_wg, e_wu, e_wd)
    ys = ys.reshape(nb * BM, CHUNKS, LANES)
    out = _combine(dest.T.reshape(-1), gate.T, x1, ys, s_wg.astype(bf16), s_wu.astype(bf16), s_wd.astype(bf16),
                   ln2_g.reshape(1, D_MODEL), ln2_b.reshape(1, D_MODEL), alpha)
    return out.reshape(B, S, D_MODEL)


def kernel(x, positions, w_in, attn_gain, rec_conv_w, rec_conv_b, rec_gate_a_w, rec_gate_a_b,
           rec_gate_x_w, rec_gate_x_b, rec_lambda, rec_gain, w_out, ln1_g, ln1_b, router_w,
           router_bias, exp_w_gate, exp_w_up, exp_w_down, shared_w_gate, shared_w_up,
           shared_w_down, ln2_g, ln2_b):
    depth = w_in.shape[0]
    alpha = (2 * depth) ** 0.25
    for l in range(depth):
        x = _layer(x, positions, w_in[l], attn_gain[l], rec_conv_w[l], rec_conv_b[l],
                   rec_gate_a_w[l], rec_gate_a_b[l], rec_gate_x_w[l], rec_gate_x_b[l],
                   rec_lambda[l], rec_gain[l], w_out[l], ln1_g[l], ln1_b[l], router_w[l],
                   router_bias[l], exp_w_gate[l], exp_w_up[l], exp_w_down[l], shared_w_gate[l],
                   shared_w_up[l], shared_w_down[l], ln2_g[l], ln2_b[l], alpha)
    return x
```

```python
import functools
import math

import jax
import jax.numpy as jnp
import numpy as np
from jax import lax
from jax.experimental import pallas as pl
from jax.experimental.pallas import tpu as pltpu

f32 = jnp.float32
bf16 = jnp.bfloat16
i32 = jnp.int32

D_MODEL = 1024
ATTN_W = 512
REC_W = 512
HEAD_DIM = 64
ROT_DIM = 16
ROPE_THETA = 500000.0
BRANCHES = ((128, 1), (512, 4), (2048, 16))
HALF_BAND = 64
CONV_W = 4
RG_LRU_C = 8.0
N_EXPERTS = 256
TOP_K = 8
N_GROUPS = 8
GROUP_SZ = N_EXPERTS // N_GROUPS
TOPK_GROUPS = 4
EXPERT_H = 256
ROUTED_SCALE = 2.5
LN_EPS = 1e-5
NEG = -1e30

LANES = 128
SUBLANES = 8
WORDS = D_MODEL // 2
CHUNKS = WORDS // LANES
VMEM_LIMIT = 56 * 1024 * 1024
DMA_THREADS = 2

TM_IN = 512
TQ = 128
KW = TQ + 2 * HALF_BAND
SB = TQ * max(d for _, d in BRANCHES)
MERGE_ROWS = 256
TS = 256
TM_MIX = 256
TM_DISP = 512
TM_SLOT = 512
TM_CMB = 128
BM = 256


def _cparams(sem):
    return pltpu.CompilerParams(dimension_semantics=sem, vmem_limit_bytes=VMEM_LIMIT)


def _inproj_body(x_ref, pos_ref, w_ref, invf_ref, q_ref, k_ref, v_ref, rx_ref, rg_ref):
    xb = x_ref[...].astype(bf16)
    ang = pos_ref[...].astype(f32) * invf_ref[...]
    cos = jnp.cos(ang)
    sin = jnp.sin(ang)
    j = lax.broadcasted_iota(i32, (1, LANES), 1) % HEAD_DIM
    half = ROT_DIM // 2
    s_lo = jnp.where(j < half, -sin, 0.0)
    s_hi = jnp.where((j >= half) & (j < ROT_DIM), sin, 0.0)
    rep = ATTN_W // LANES
    cos_w = jnp.concatenate([cos] * rep, axis=1)
    s_lo_w = jnp.concatenate([s_lo] * rep, axis=1)
    s_hi_w = jnp.concatenate([s_hi] * rep, axis=1)

    def proj(c0, n):
        return jnp.dot(xb, w_ref[:, c0:c0 + n], preferred_element_type=f32)

    def rope(t):
        return (t * cos_w + pltpu.roll(t, ATTN_W - half, 1) * s_lo_w
                + pltpu.roll(t, half, 1) * s_hi_w)

    q_ref[...] = rope(proj(0, ATTN_W)) * (HEAD_DIM ** -0.5)
    k_ref[...] = rope(proj(ATTN_W, ATTN_W))
    v_ref[...] = proj(2 * ATTN_W, ATTN_W)
    rx_ref[...] = proj(3 * ATTN_W, REC_W)
    rg_ref[...] = proj(3 * ATTN_W + REC_W, REC_W)


def _inproj(x2, pos2, w_in_b, invf):
    T = x2.shape[0]
    tm = TM_IN
    in_w = w_in_b.shape[1]
    tok = lambda i: (i, 0)
    fixed = lambda i: (0, 0)
    return pl.pallas_call(
        _inproj_body,
        grid=(T // tm,),
        in_specs=[pl.BlockSpec((tm, D_MODEL), tok), pl.BlockSpec((tm, 1), tok),
                  pl.BlockSpec((D_MODEL, in_w), fixed), pl.BlockSpec((1, LANES), fixed)],
        out_specs=[pl.BlockSpec((tm, ATTN_W), tok)] * 3 + [pl.BlockSpec((tm, REC_W), tok)] * 2,
        out_shape=[jax.ShapeDtypeStruct((T, ATTN_W), f32)] * 3
        + [jax.ShapeDtypeStruct((T, REC_W), f32)] * 2,
        compiler_params=_cparams(("parallel",)),
        name="inproj",
    )(x2, pos2, w_in_b, invf)


def _attn_body(q_ref, k_ref, v_ref, out_ref, o1, o2, o3, l1, l2, l3, bias_ref, *, S):
    o_sc, l_sc = (o1, o2, o3), (l1, l2, l3)
    lane = lax.broadcasted_iota(i32, (1, LANES), 1)
    head0 = lane < HEAD_DIM
    rel = (lax.broadcasted_iota(i32, (TQ, KW), 0) - lax.broadcasted_iota(i32, (TQ, KW), 1))
    tiles = SB // TQ
    for case in range(3):
        bias_ref[case] = jnp.where(jnp.abs(rel + case * HALF_BAND) <= HALF_BAND, 0.0, NEG)

    def rows(ref, start, n, d):
        return ref[pl.ds(start, n), :] if d == 1 else ref[pl.ds(start, n, stride=d), :]

    def tile(ti, n0):
        ctx = []
        for g, (_, d) in enumerate(BRANCHES):
            L = S // d
            sh = d.bit_length() - 1
            r = jnp.bitwise_and(ti, d - 1)
            m0 = jnp.right_shift(n0, sh) + jnp.right_shift(ti, sh) * TQ
            ks = jnp.clip(m0 - HALF_BAND, 0, L - KW)
            q = rows(q_ref, r + d * m0, TQ, d).astype(bf16)
            k = rows(k_ref, r + d * ks, KW, d).astype(bf16)
            v = rows(v_ref, r + d * ks, KW, d).astype(bf16)
            bias = bias_ref[(m0 - ks) // HALF_BAND]
            local = r + d * (m0 - jnp.right_shift(n0, sh))
            ss = []
            for sel in (head0, jnp.logical_not(head0)):
                qh = jnp.where(sel, q, jnp.zeros_like(q))
                ss.append(lax.dot_general(qh, k, (((1,), (1,)), ((), ())),
                                          preferred_element_type=f32))
            ctx.append((d, v, bias, local, ss))
        soft = []
        for d, v, bias, local, ss in ctx:
            ps = []
            for s in ss:
                s = s + bias
                m = jnp.max(s, axis=-1, keepdims=True)
                p = jnp.exp(s - m)
                den = jnp.sum(p, axis=-1, keepdims=True)
                ps.append((p.astype(bf16), den, m + jnp.log(den)))
            soft.append(ps)
        for g, ((d, v, bias, local, ss), ps) in enumerate(zip(ctx, soft)):
            outs = [(jnp.dot(p, v, preferred_element_type=f32) / den, lse) for p, den, lse in ps]
            o_val = jnp.where(head0, outs[0][0], outs[1][0])
            l_val = jnp.where(head0, outs[0][1], outs[1][1])
            if d == 1:
                o_sc[g][pl.ds(local, TQ), :] = o_val
                l_sc[g][pl.ds(local, TQ), :] = l_val
            else:
                o_sc[g][pl.ds(local, TQ, stride=d), :] = o_val
                l_sc[g][pl.ds(local, TQ, stride=d), :] = l_val
        return n0

    def merge(c, n0):
        sl = pl.ds(pl.multiple_of(c * MERGE_ROWS, MERGE_ROWS), MERGE_ROWS)
        ls = [l_sc[g][sl, :] for g in range(len(BRANCHES))]
        mx = functools.reduce(jnp.maximum, ls)
        es = [jnp.exp(l - mx) for l in ls]
        num = functools.reduce(lambda a, b: a + b, [e * o_sc[g][sl, :] for g, e in enumerate(es)])
        den = functools.reduce(lambda a, b: a + b, es)
        dst = pl.ds(pl.multiple_of(n0 + c * MERGE_ROWS, MERGE_ROWS), MERGE_ROWS)
        out_ref[dst, :] = (num / den).astype(bf16)
        return n0

    def superblock(sb, carry):
        n0 = pl.multiple_of(sb * SB, SB)
        lax.fori_loop(0, tiles, tile, n0)
        lax.fori_loop(0, SB // MERGE_ROWS, merge, n0)
        return carry

    lax.fori_loop(0, S // SB, superblock, 0)


def _attention(q, k, v, B, S):
    assert S % SB == 0 and all(S // d >= KW for _, d in BRANCHES)
    hp = ATTN_W // LANES
    view = lambda t: t.reshape(B, S, ATTN_W)
    spec = pl.BlockSpec((None, S, LANES), lambda b, h: (b, 0, h))
    out = pl.pallas_call(
        functools.partial(_attn_body, S=S),
        grid=(B, hp),
        in_specs=[spec, spec, spec],
        out_specs=spec,
        out_shape=jax.ShapeDtypeStruct((B, S, ATTN_W), bf16),
        scratch_shapes=[pltpu.VMEM((SB, LANES), f32)] * (2 * len(BRANCHES))
        + [pltpu.VMEM((3, TQ, KW), f32)],
        compiler_params=_cparams(("parallel", "parallel")),
        name="attention",
    )(view(q), view(k), view(v))
    return out.reshape(B * S, ATTN_W)


def _rec_body(rx_ref, rg_ref, cw_ref, cb_ref, wg_ref, gb_ref, lam_ref, out_ref, rxp_ref, u_ref,
              hf_ref, *, S):
    nch = S // TS
    ntile = TS // SUBLANES
    zeros8 = jnp.zeros((SUBLANES, LANES), f32)
    rxp_ref[pl.ds(0, SUBLANES), :] = zeros8
    rxp_ref[pl.ds(S + SUBLANES, SUBLANES), :] = zeros8

    def pad_copy(c, carry):
        t0 = pl.multiple_of(c * TS, TS)
        rxp_ref[pl.ds(t0 + SUBLANES, TS), :] = rx_ref[pl.ds(t0, TS), :]
        return carry

    lax.fori_loop(0, nch, pad_copy, 0)

    cw = cw_ref[...]
    cb = cb_ref[...]
    nrow = TS + 2 * SUBLANES

    def conv(c, carry):
        t0 = pl.multiple_of(c * TS, TS)
        xw = rxp_ref[pl.ds(t0, nrow), :]
        u_ref[pl.ds(t0, TS), :] = (
            cw[0:1] * pltpu.roll(xw, 2, 0) + cw[1:2] * pltpu.roll(xw, 1, 0) + cw[2:3] * xw
            + cw[3:4] * pltpu.roll(xw, nrow - 1, 0))[SUBLANES:SUBLANES + TS] + cb
        return carry

    lax.fori_loop(0, nch, conv, 0)

    lam = lam_ref[...]
    neg_sp = -RG_LRU_C * (jnp.maximum(-lam, 0.0) + jnp.log1p(jnp.exp(-jnp.abs(lam))))
    sub = lax.broadcasted_iota(i32, (ntile, SUBLANES, LANES), 1)

    def gates(ci, d):
        t0 = pl.multiple_of(ci * TS, TS)
        u = u_ref[pl.ds(t0, TS), :]
        c0 = d * 2 * LANES
        g = jnp.dot(u.astype(bf16), wg_ref[:, c0:c0 + 2 * LANES], preferred_element_type=f32)
        g = g + gb_ref[:, c0:c0 + 2 * LANES]
        r = jax.nn.sigmoid(g[:, :LANES])
        gi = jax.nn.sigmoid(g[:, LANES:])
        a = jnp.exp(neg_sp[d:d + 1] * r)
        b = jnp.sqrt(1.0 - a * a) * gi * u
        return a.reshape(ntile, SUBLANES, LANES), b.reshape(ntile, SUBLANES, LANES)

    def chunk(i, carry):
        cf, cbk = carry
        a, b = gates(i, 0)
        for s in (1, 2, 4):
            ok = sub >= s
            a_s = pltpu.roll(a, s, 1)
            b_s = pltpu.roll(b, s, 1)
            b = jnp.where(ok, a * b_s + b, b)
            a = jnp.where(ok, a * a_s, a)
        t0 = pl.multiple_of(i * TS, TS)
        for j in range(ntile):
            h = a[j] * cf + b[j]
            hf_ref[pl.ds(t0 + j * SUBLANES, SUBLANES), :] = h
            cf = h[SUBLANES - 1:SUBLANES, :]
        ib = nch - 1 - i
        a, b = gates(ib, 1)
        for s in (1, 2, 4):
            ok = sub < SUBLANES - s
            a_s = pltpu.roll(a, SUBLANES - s, 1)
            b_s = pltpu.roll(b, SUBLANES - s, 1)
            b = jnp.where(ok, a * b_s + b, b)
            a = jnp.where(ok, a * a_s, a)
        t0 = pl.multiple_of(ib * TS, TS)
        for j in range(ntile - 1, -1, -1):
            h = a[j] * cbk + b[j]
            out_ref[pl.ds(t0 + j * SUBLANES, SUBLANES), :] = h
            cbk = h[0:1, :]
        return cf, cbk

    zrow = jnp.zeros((1, LANES), f32)
    lax.fori_loop(0, nch, chunk, (zrow, zrow))

    def finish(c, carry):
        t0 = pl.multiple_of(c * TS, TS)
        sl = pl.ds(t0, TS)
        out_ref[sl, :] = (hf_ref[sl, :] + out_ref[sl, :]) * jax.nn.gelu(rg_ref[sl, :])
        return carry

    lax.fori_loop(0, nch, finish, 0)


def _rec(rx, rg, conv_w, conv_b, wg, gb, lam, B, S):
    ng = REC_W // LANES
    assert S % TS == 0
    seq = pl.BlockSpec((None, S, LANES), lambda b, c: (b, 0, c))
    return pl.pallas_call(
        functools.partial(_rec_body, S=S),
        grid=(B, ng),
        in_specs=[seq, seq,
                  pl.BlockSpec((CONV_W, LANES), lambda b, c: (0, c)),
                  pl.BlockSpec((1, LANES), lambda b, c: (0, c)),
                  pl.BlockSpec((None, LANES, 4 * LANES), lambda b, c: (c, 0, 0)),
                  pl.BlockSpec((None, 1, 4 * LANES), lambda b, c: (c, 0, 0)),
                  pl.BlockSpec((2, LANES), lambda b, c: (0, c))],
        out_specs=seq,
        out_shape=jax.ShapeDtypeStruct((B, S, REC_W), f32),
        scratch_shapes=[pltpu.VMEM((S + 2 * SUBLANES, LANES), f32), pltpu.VMEM((S, LANES), f32),
                        pltpu.VMEM((S, LANES), f32)],
        compiler_params=_cparams(("parallel", "parallel")),
        name="rec",
    )(rx.reshape(B, S, REC_W), rg.reshape(B, S, REC_W), conv_w, conv_b, wg, gb, lam)


u32 = jnp.uint32


def _pack_rows(x):
    bits = lax.bitcast_convert_type(x, u32)
    r = (bits + u32(0x7FFF) + ((bits >> 16) & u32(1))) >> 16
    return (r[:, WORDS:] << 16) | r[:, :WORDS]


def _unpack_rows(w):
    lo = lax.bitcast_convert_type(w << 16, f32).astype(bf16)
    hi = lax.bitcast_convert_type(w & u32(0xFFFF0000), f32).astype(bf16)
    return lo, hi


def _slab_chunks(flat, rows, base=0):
    return jnp.concatenate(
        [flat[pl.ds(base * CHUNKS + c, rows, stride=CHUNKS), :] for c in range(CHUNKS)], axis=1)


def _store_slab_chunks(flat, val, rows, base=0):
    for c in range(CHUNKS):
        flat[pl.ds(base * CHUNKS + c, rows, stride=CHUNKS), :] = val[:, c * LANES:(c + 1) * LANES]


def _rms(t, gain):
    return t * lax.rsqrt(jnp.mean(t * t, axis=-1, keepdims=True) + LN_EPS) * gain


def _layer_norm(z, g, b):
    mu = jnp.mean(z, axis=-1, keepdims=True)
    zc = z - mu
    var = jnp.mean(zc * zc, axis=-1, keepdims=True)
    return zc * lax.rsqrt(var + LN_EPS) * g + b


def _mixout_body(attn_ref, rec_ref, x_ref,
                 ag_ref, rgn_ref, wo_ref, g1_ref, b1_ref, rwh_ref, rwl_ref, rb_ref, tri_ref, ones_ref,
                 x1_ref, x1p_ref, idx_ref, gate_ref, rank_ref, cnt_ref, carry_ref, *, alpha):
    tm = TM_MIX

    @pl.when(pl.program_id(0) == 0)
    def _():
        carry_ref[...] = jnp.zeros_like(carry_ref)

    attn_n = _rms(attn_ref[...].astype(f32), ag_ref[...]).astype(bf16)
    rec_n = _rms(rec_ref[...], rgn_ref[...]).astype(bf16)
    y = (jnp.dot(attn_n, wo_ref[0:ATTN_W, :], preferred_element_type=f32)
         + jnp.dot(rec_n, wo_ref[ATTN_W:, :], preferred_element_type=f32))
    x1 = _layer_norm(alpha * x_ref[...] + y, g1_ref[...], b1_ref[...])
    x1_ref[...] = x1
    _store_slab_chunks(x1p_ref, _pack_rows(x1), tm)

    hi = x1.astype(bf16)
    lo = (x1 - hi.astype(f32)).astype(bf16)
    nt = (((1,), (1,)), ((), ()))
    logits = (lax.dot_general(rwh_ref[...], hi, nt, preferred_element_type=f32)
              + lax.dot_general(rwh_ref[...], lo, nt, preferred_element_type=f32)
              + lax.dot_general(rwl_ref[...], hi, nt, preferred_element_type=f32))
    scores = jax.nn.sigmoid(logits)
    biased = scores + jnp.concatenate([rb_ref[...]] * (tm // LANES), axis=1)

    rid = lax.broadcasted_iota(i32, (GROUP_SZ, tm), 0).astype(f32)
    grp = []
    for g in range(N_GROUPS):
        vg = biased[g * GROUP_SZ:(g + 1) * GROUP_SZ, :]
        m1 = jnp.max(vg, axis=0, keepdims=True)
        first = jnp.min(jnp.where(vg == m1, rid, float(GROUP_SZ)), axis=0, keepdims=True)
        m2 = jnp.max(jnp.where(rid == first, -jnp.inf, vg), axis=0, keepdims=True)
        grp.append(m1 + m2)
    eid = lax.broadcasted_iota(i32, (N_EXPERTS, tm), 0).astype(f32)
    keep = [jnp.zeros((1, tm), f32) for _ in range(N_GROUPS)]
    for _ in range(TOPK_GROUPS):
        gm = functools.reduce(jnp.maximum, grp)
        gi = jnp.full((1, tm), float(N_GROUPS), f32)
        for g in range(N_GROUPS - 1, -1, -1):
            gi = jnp.where(grp[g] == gm, float(g), gi)
        hits = [gi == float(g) for g in range(N_GROUPS)]
        grp = [jnp.where(hit, -jnp.inf, sc) for hit, sc in zip(hits, grp)]
        keep = [jnp.where(hit, 1.0, kp) for hit, kp in zip(hits, keep)]
    masked = jnp.concatenate(
        [jnp.where(jnp.broadcast_to(keep[g], (GROUP_SZ, tm)) > 0.5,
                   biased[g * GROUP_SZ:(g + 1) * GROUP_SZ, :], -jnp.inf)
         for g in range(N_GROUPS)], axis=0)

    onehot = jnp.zeros((N_EXPERTS, tm), f32)
    idxs, gts = [], []
    for _ in range(TOP_K):
        mx = jnp.max(masked, axis=0, keepdims=True)
        ix = jnp.min(jnp.where(masked == mx, eid, float(N_EXPERTS)), axis=0, keepdims=True)
        hit = eid == ix
        gts.append(jnp.sum(jnp.where(hit, scores, 0.0), axis=0, keepdims=True))
        idxs.append(ix)
        masked = jnp.where(hit, -jnp.inf, masked)
        onehot = onehot + jnp.where(hit, 1.0, 0.0)
    gsum = functools.reduce(lambda p, q: p + q, gts)
    for kk in range(TOP_K):
        idx_ref[kk:kk + 1, :] = idxs[kk].astype(i32)
        gate_ref[kk:kk + 1, :] = gts[kk] / gsum * ROUTED_SCALE

    oh = onehot.astype(bf16)
    before = carry_ref[...] + jnp.dot(oh, tri_ref[...], preferred_element_type=f32)
    for kk in range(TOP_K):
        rk = jnp.sum(jnp.where(eid == idxs[kk], before, 0.0), axis=0, keepdims=True)
        rank_ref[kk:kk + 1, :] = rk.astype(i32)
    total = carry_ref[...] + jnp.dot(oh, ones_ref[...], preferred_element_type=f32)
    carry_ref[...] = total
    cnt_ref[...] = total


def _mixout(attn, rec, x2, attn_gain, rec_gain, w_out_b, g1, b1, rw_hi, rw_lo, rbias, alpha):
    T = x2.shape[0]
    tm = TM_MIX
    tok = lambda i: (i, 0)
    fixed = lambda i: (0, 0)
    tri = jnp.asarray(np.triu(np.ones((tm, tm), np.float32), k=1), bf16)
    ones = jnp.ones((tm, tm), bf16)
    aw = pl.BlockSpec((tm, ATTN_W), tok)
    row = lambda n: pl.BlockSpec((1, n), fixed)
    kt = pl.BlockSpec((TOP_K, tm), lambda i: (0, i))
    return pl.pallas_call(
        functools.partial(_mixout_body, alpha=alpha),
        grid=(T // tm,),
        in_specs=[aw] * 2 + [pl.BlockSpec((tm, D_MODEL), tok), row(ATTN_W), row(REC_W),
                             pl.BlockSpec((D_MODEL, D_MODEL), fixed), row(D_MODEL), row(D_MODEL),
                             pl.BlockSpec((N_EXPERTS, D_MODEL), fixed),
                             pl.BlockSpec((N_EXPERTS, D_MODEL), fixed),
                             pl.BlockSpec((N_EXPERTS, LANES), fixed),
                             pl.BlockSpec((tm, tm), fixed), pl.BlockSpec((tm, tm), fixed)],
        out_specs=[pl.BlockSpec((tm, D_MODEL), tok),
                   pl.BlockSpec((tm * CHUNKS, LANES), tok),
                   kt, kt, kt, pl.BlockSpec((N_EXPERTS, tm), fixed)],
        out_shape=[jax.ShapeDtypeStruct((T, D_MODEL), f32),
                   jax.ShapeDtypeStruct((T * CHUNKS, LANES), u32),
                   jax.ShapeDtypeStruct((TOP_K, T), i32),
                   jax.ShapeDtypeStruct((TOP_K, T), f32),
                   jax.ShapeDtypeStruct((TOP_K, T), i32),
                   jax.ShapeDtypeStruct((N_EXPERTS, tm), f32)],
        scratch_shapes=[pltpu.VMEM((N_EXPERTS, tm), f32)],
        compiler_params=_cparams(("arbitrary",)),
        name="mixout",
    )(attn, rec, x2, attn_gain, rec_gain, w_out_b, g1, b1,
      rw_hi, rw_lo, rbias, tri, ones)


def _slots_body(idx_ref, rank_ref, start_ref, dest_ref):
    tm = TM_SLOT
    eid = lax.broadcasted_iota(i32, (N_EXPERTS, tm), 0)
    start = jnp.concatenate([start_ref[...]] * (tm // LANES), axis=1)
    for kk in range(TOP_K):
        hit = eid == idx_ref[kk:kk + 1, :]
        base = jnp.sum(jnp.where(hit, start, 0.0), axis=0, keepdims=True)
        dest_ref[kk:kk + 1, :] = base.astype(i32) + rank_ref[kk:kk + 1, :]


def _slots(idx, rank, pad_start):
    T = idx.shape[1]
    tm = TM_SLOT
    kt = pl.BlockSpec((TOP_K, tm), lambda i: (0, i))
    start = jnp.broadcast_to(pad_start.astype(f32).reshape(N_EXPERTS, 1), (N_EXPERTS, LANES))
    return pl.pallas_call(
        _slots_body,
        grid=(T // tm,),
        in_specs=[kt, kt, pl.BlockSpec((N_EXPERTS, LANES), lambda i: (0, 0))],
        out_specs=kt,
        out_shape=jax.ShapeDtypeStruct((TOP_K, T), i32),
        compiler_params=_cparams(("parallel",)),
        name="slots",
    )(idx, rank, start)


def _dispatch_body(dest_ref, x_ref, xs_ref, sem):
    tm = TM_DISP

    def copy(t, k):
        return pltpu.make_async_copy(x_ref.at[t], xs_ref.at[dest_ref[k, t]], sem)

    def issue(t, carry):
        for k in range(TOP_K):
            copy(t, k).start(priority=k % DMA_THREADS)
        return carry

    def drain(t, carry):
        for k in range(TOP_K):
            copy(t, k).wait()
        return carry

    lax.fori_loop(0, tm, issue, 0)
    lax.fori_loop(0, tm, drain, 0)


def _dispatch(dest, x1p, n_slots):
    T = x1p.shape[0]
    tm = TM_DISP
    return pl.pallas_call(
        _dispatch_body,
        grid=(T // tm,),
        in_specs=[pl.BlockSpec((TOP_K, tm), lambda i: (0, i), memory_space=pltpu.SMEM),
                  pl.BlockSpec((tm, CHUNKS, LANES), lambda i: (i, 0, 0))],
        out_specs=pl.BlockSpec(memory_space=pl.ANY),
        out_shape=jax.ShapeDtypeStruct((n_slots, CHUNKS, LANES), u32),
        scratch_shapes=[pltpu.SemaphoreType.DMA(())],
        compiler_params=_cparams(("arbitrary",)),
        name="dispatch",
    )(dest, x1p)


def _experts_body(be_ref, nx_ref, nv_ref, nb_ref, xs_ref, wg_hbm, wu_hbm, wd_hbm, ys_ref,
                  fga, fua, fda, fgb, fub, fdb, sga, sua, sda, sgb, sub, sdb, sems):
    p = pl.program_id(0)
    blocks = (2 * p, 2 * p + 1)
    hbm_w = (wg_hbm, wu_hbm, wd_hbm)
    f32_w = ((fga, fua, fda), (fgb, fub, fdb))
    bf_w = ((sga, sua, sda), (sgb, sub, sdb))
    live_step = blocks[0] < nb_ref[0]

    def fetch(half, e):
        return [pltpu.make_async_copy(src.at[e], dst, sems.at[half])
                for src, dst in zip(hbm_w, f32_w[half])]

    for half, blk in enumerate(blocks):
        @pl.when(jnp.logical_and(live_step, p == 0))
        def _(half=half, blk=blk):
            for c in fetch(half, be_ref[blk]):
                c.start()

        prev = jnp.maximum(blk - 2, 0)

        @pl.when(jnp.logical_and(live_step,
                                 jnp.logical_or(p == 0, be_ref[blk] != be_ref[prev])))
        def _(half=half, blk=blk):
            for c in fetch(half, be_ref[blk]):
                c.wait()
            for src, dst in zip(f32_w[half], bf_w[half]):
                dst[...] = src[...].astype(bf16)

            @pl.when(nx_ref[blk] >= 0)
            def _():
                for c in fetch(half, nx_ref[blk]):
                    c.start()

    @pl.when(live_step)
    def _():
        xs = []
        for half, blk in enumerate(blocks):
            rows = jnp.where(blk < nb_ref[0], nv_ref[blk], 0)
            live = lax.broadcasted_iota(i32, (BM, WORDS), 0) < rows
            words = _slab_chunks(xs_ref, BM, half * BM)
            xs.append(_unpack_rows(jnp.where(live, words, u32(0))))

        def up(half, w_ref):
            lo, hi = xs[half]
            return (jnp.dot(lo, w_ref[:WORDS, :], preferred_element_type=f32)
                    + jnp.dot(hi, w_ref[WORDS:, :], preferred_element_type=f32))

        gs = [up(half, bf_w[half][0]) for half in range(2)]
        us = [up(half, bf_w[half][1]) for half in range(2)]
        hs = [(g * jax.nn.sigmoid(g) * u).astype(bf16) for g, u in zip(gs, us)]
        ys = [jnp.dot(h, bf_w[half][2][...], preferred_element_type=f32)
              for half, h in enumerate(hs)]
        for half in range(2):
            _store_slab_chunks(ys_ref, _pack_rows(ys[half]), BM, half * BM)


def _experts(block_e, next_e, block_rows, nb_used, xs, wg, wu, wd):
    P = xs.shape[0] // CHUNKS
    nb = P // BM
    assert nb % 2 == 0
    last = lambda nbu: jnp.maximum((nbu[0] - 1) // 2, 0)
    rows = lambda p, be, nx, nv, nbu: (jnp.minimum(p, last(nbu)), 0)
    any_spec = pl.BlockSpec(memory_space=pl.ANY)
    up_shape, down_shape = (D_MODEL, EXPERT_H), (EXPERT_H, D_MODEL)
    per_stream = lambda dt: [pltpu.VMEM(up_shape, dt), pltpu.VMEM(up_shape, dt),
                             pltpu.VMEM(down_shape, dt)]
    gs = pltpu.PrefetchScalarGridSpec(
        num_scalar_prefetch=4,
        grid=(nb // 2,),
        in_specs=[pl.BlockSpec((2 * BM * CHUNKS, LANES), rows), any_spec, any_spec, any_spec],
        out_specs=pl.BlockSpec((2 * BM * CHUNKS, LANES), rows),
        scratch_shapes=per_stream(f32) * 2 + per_stream(bf16) * 2
        + [pltpu.SemaphoreType.DMA((2,))],
    )
    return pl.pallas_call(
        _experts_body,
        grid_spec=gs,
        out_shape=jax.ShapeDtypeStruct((P * CHUNKS, LANES), u32),
        compiler_params=_cparams(("arbitrary",)),
        name="experts",
    )(block_e, next_e, block_rows, nb_used, xs, wg, wu, wd)


def _combine_body(dest_ref, dnext_ref, gate_ref, x1_ref, ys_ref, ysflat_ref, sg_ref, su_ref, sd_ref,
                  g2_ref, b2_ref, out_ref, gb00, gb01, gb10, gb11, sems, *, alpha):
    tm = TM_CMB
    j = pl.program_id(0)
    n = pl.num_programs(0)
    gbufs = ((gb00, gb01), (gb10, gb11))

    def issue(d_ref, st, half):
        base = half * tm

        def step(t, carry):
            for k in range(TOP_K):
                src = ys_ref.at[d_ref[(base + t) * TOP_K + k]]
                dst = gbufs[st][half].at[pl.ds((k * tm + t) * CHUNKS, CHUNKS)]
                pltpu.make_async_copy(src, dst, sems.at[st, half]).start(priority=k % DMA_THREADS)
            return carry
        lax.fori_loop(0, tm, step, 0)

    def finish(st, half):
        rows = pl.ds(half * tm, tm)
        x1 = x1_ref[rows, :]
        xb = x1.astype(bf16)
        sg = jnp.dot(xb, sg_ref[...], preferred_element_type=f32)
        su = jnp.dot(xb, su_ref[...], preferred_element_type=f32)
        shared = jnp.dot((sg * jax.nn.sigmoid(sg) * su).astype(bf16), sd_ref[...],
                         preferred_element_type=f32)
        flat = gbufs[st][half]
        pltpu.make_async_copy(ysflat_ref.at[pl.ds(0, TOP_K * tm * CHUNKS)], flat,
                              sems.at[st, half]).wait()

        gates = gate_ref[rows, :]
        lo = [jnp.zeros((tm, LANES), f32) for _ in range(CHUNKS)]
        hi = [jnp.zeros((tm, LANES), f32) for _ in range(CHUNKS)]
        for k in range(TOP_K):
            gk = jnp.broadcast_to(gates[:, k:k + 1], (tm, LANES))
            for c in range(CHUNKS):
                w = flat[pl.ds(k * tm * CHUNKS + c, tm, stride=CHUNKS), :]
                lo[c] = lo[c] + gk * lax.bitcast_convert_type(w << 16, f32)
                hi[c] = hi[c] + gk * lax.bitcast_convert_type(w & u32(0xFFFF0000), f32)
        routed = jnp.concatenate(lo + hi, axis=1)
        out_ref[rows, :] = _layer_norm(alpha * x1 + (routed + shared), g2_ref[...], b2_ref[...])

    @pl.when(j == 0)
    def _():
        issue(dest_ref, 0, 0)
        issue(dest_ref, 0, 1)

    for st in range(2):
        @pl.when(lax.rem(j, 2) == st)
        def _(st=st):
            @pl.when(j + 1 < n)
            def _():
                issue(dnext_ref, 1 - st, 0)
                issue(dnext_ref, 1 - st, 1)

            finish(st, 0)
            finish(st, 1)


def _combine(dest, gate, x1, ys, sg, su, sd, g2, b2, alpha):
    T = x1.shape[0]
    tm = TM_CMB
    n = T // (2 * tm)
    tok = lambda j: (j, 0)
    fixed = lambda j: (0, 0)
    tbl = pl.BlockSpec((2 * tm * TOP_K,), lambda j: (j,), memory_space=pltpu.SMEM)
    tbl_next = pl.BlockSpec((2 * tm * TOP_K,), lambda j: (jnp.minimum(j + 1, n - 1),),
                            memory_space=pltpu.SMEM)
    any_spec = pl.BlockSpec(memory_space=pl.ANY)
    return pl.pallas_call(
        functools.partial(_combine_body, alpha=alpha),
        grid=(n,),
        in_specs=[tbl, tbl_next, pl.BlockSpec((2 * tm, TOP_K), tok),
                  pl.BlockSpec((2 * tm, D_MODEL), tok), any_spec, any_spec,
                  pl.BlockSpec(sg.shape, fixed), pl.BlockSpec(su.shape, fixed),
                  pl.BlockSpec(sd.shape, fixed), pl.BlockSpec((1, D_MODEL), fixed),
                  pl.BlockSpec((1, D_MODEL), fixed)],
        out_specs=pl.BlockSpec((2 * tm, D_MODEL), tok),
        out_shape=jax.ShapeDtypeStruct((T, D_MODEL), f32),
        scratch_shapes=[pltpu.VMEM((TOP_K * tm * CHUNKS, LANES), u32)] * 4
        + [pltpu.SemaphoreType.DMA((2, 2))],
        compiler_params=_cparams(("arbitrary",)),
        name="combine",
    )(dest, dest, gate, x1, ys, ys.reshape(ys.shape[0] * CHUNKS, LANES), sg, su, sd, g2, b2)


def _rope_inv_freq():
    half = ROT_DIM // 2
    inv = ROPE_THETA ** (-jnp.arange(half, dtype=f32) * 2.0 / ROT_DIM)
    j = np.arange(LANES) % HEAD_DIM
    table = jnp.where(j < ROT_DIM, inv[j % half], 0.0)
    return table.reshape(1, LANES).astype(f32)


def _gate_weights(ga_w, ga_b, gx_w, gx_b):
    ng = REC_W // LANES
    per = LANES // HEAD_DIM
    def bd(w):
        w = w.reshape(ng, per, HEAD_DIM, HEAD_DIM)
        z = jnp.zeros((ng, LANES, LANES), w.dtype)
        for p in range(per):
            z = z.at[:, p * HEAD_DIM:(p + 1) * HEAD_DIM, p * HEAD_DIM:(p + 1) * HEAD_DIM].set(w[:, p])
        return z
    wg = jnp.concatenate([bd(ga_w[0]), bd(gx_w[0]), bd(ga_w[1]), bd(gx_w[1])], axis=-1).astype(bf16)
    grp = lambda b: b.reshape(ng, 1, LANES)
    gb = jnp.concatenate([grp(ga_b[0]), grp(gx_b[0]), grp(ga_b[1]), grp(gx_b[1])], axis=-1)
    return wg, gb


def _layer(x, positions, w_in, attn_gain, conv_w, conv_b, ga_w, ga_b, gx_w, gx_b, lam, rec_gain,
           w_out, ln1_g, ln1_b, router_w, router_bias, e_wg, e_wu, e_wd, s_wg, s_wu, s_wd,
           ln2_g, ln2_b, alpha):
    B, S, _ = x.shape
    T = B * S
    x2 = x.reshape(T, D_MODEL)
    pos2 = positions.reshape(T, 1)

    q, k, v, rx, rg = _inproj(x2, pos2, w_in.astype(bf16), _rope_inv_freq())
    attn = _attention(q, k, v, B, S)
    wg, gb = _gate_weights(ga_w, ga_b, gx_w, gx_b)
    rec = _rec(rx, rg, conv_w, conv_b.reshape(1, REC_W), wg, gb, lam, B, S).reshape(T, REC_W)

    rw_t = router_w.T
    rw_hi = rw_t.astype(bf16)
    rw_lo = (rw_t - rw_hi.astype(f32)).astype(bf16)
    rbias = jnp.broadcast_to(router_bias.reshape(N_EXPERTS, 1), (N_EXPERTS, LANES))
    x1, x1p, idx, gate, rank, cnt = _mixout(
        attn, rec, x2,
        attn_gain.reshape(1, ATTN_W), rec_gain.reshape(1, REC_W), w_out.astype(bf16),
        ln1_g.reshape(1, D_MODEL), ln1_b.reshape(1, D_MODEL), rw_hi, rw_lo, rbias, alpha)

    counts = cnt[:, 0].astype(i32)
    padded = (counts + BM - 1) // BM * BM
    pad_end = jnp.cumsum(padded)
    pad_start = pad_end - padded
    slot = _slots(idx, rank, pad_start)
    nb = (T * TOP_K + N_EXPERTS * (BM - 1)) // BM + 1
    nb += nb % 2
    nb_used = pad_end[-1] // BM
    h = (nb_used + 1) // 2
    pos = jnp.arange(nb, dtype=i32)
    blk = jnp.where(pos % 2 == 0, pos // 2, h + pos // 2)
    live = jnp.logical_and(pos < 2 * h, blk < nb_used)
    first_row = blk * BM
    block_e = jnp.minimum(jnp.sum((pad_end[None, :] <= first_row[:, None]).astype(i32), axis=1),
                          N_EXPERTS - 1)
    block_e = jnp.where(live, block_e, N_EXPERTS - 1)
    block_rows = jnp.where(live, jnp.clip((pad_start + counts)[block_e] - first_row, 0, BM), 0)
    pairs = block_e.reshape(nb // 2, 2)
    step_id = jnp.arange(nb // 2, dtype=i32)[:, None]
    change = jnp.concatenate([pairs[1:] != pairs[:-1], jnp.zeros((1, 2), bool)], axis=0)
    change = jnp.logical_and(change, step_id + 1 < h)
    nxt_step = lax.cummin(jnp.where(change, step_id + 1, nb), axis=0, reverse=True)
    next_e = jnp.where(nxt_step < nb // 2,
                       jnp.take_along_axis(pairs, jnp.minimum(nxt_step, nb // 2 - 1), axis=0), -1)
    next_e = next_e.reshape(nb).astype(i32)
    sblk = slot // BM
    dest = jnp.where(sblk < h, 2 * sblk, 2 * (sblk - h) + 1) * BM + slot % BM
    nb_used = (2 * h).astype(i32).reshape(1)

    xs = _dispatch(dest, x1p.reshape(T, CHUNKS, LANES), nb * BM)
    ys = _experts(block_e.astype(i32), next_e, block_rows.astype(i32), nb_used,
                  xs.reshape(nb * BM * CHUNKS, LANES), e_wg, e_wu, e_wd)
    ys = ys.reshape(nb * BM, CHUNKS, LANES)
    out = _combine(dest.T.reshape(-1), gate.T, x1, ys, s_wg.astype(bf16), s_wu.astype(bf16), s_wd.astype(bf16),
                   ln2_g.reshape(1, D_MODEL), ln2_b.reshape(1, D_MODEL), alpha)
    return out.reshape(B, S, D_MODEL)


def kernel(x, positions, w_in, attn_gain, rec_conv_w, rec_conv_b, rec_gate_a_w, rec_gate_a_b,
           rec_gate_x_w, rec_gate_x_b, rec_lambda, rec_gain, w_out, ln1_g, ln1_b, router_w,
           router_bias, exp_w_gate, exp_w_up, exp_w_down, shared_w_gate, shared_w_up,
           shared_w_down, ln2_g, ln2_b):
    depth = w_in.shape[0]
    alpha = (2 * depth) ** 0.25
    for l in range(depth):
        x = _layer(x, positions, w_in[l], attn_gain[l], rec_conv_w[l], rec_conv_b[l],
                   rec_gate_a_w[l], rec_gate_a_b[l], rec_gate_x_w[l], rec_gate_x_b[l],
                   rec_lambda[l], rec_gain[l], w_out[l], ln1_g[l], ln1_b[l], router_w[l],
                   router_bias[l], exp_w_gate[l], exp_w_up[l], exp_w_down[l], shared_w_gate[l],
                   shared_w_up[l], shared_w_down[l], ln2_g[l], ln2_b[l], alpha)
    return x
```

```python
import functools
import math

import jax
import jax.numpy as jnp
import numpy as np
from jax import lax
from jax.experimental import pallas as pl
from jax.experimental.pallas import tpu as pltpu

f32 = jnp.float32
bf16 = jnp.bfloat16
i32 = jnp.int32

D_MODEL = 1024
ATTN_W = 512
REC_W = 512
HEAD_DIM = 64
ROT_DIM = 16
ROPE_THETA = 500000.0
BRANCHES = ((128, 1), (512, 4), (2048, 16))
HALF_BAND = 64
CONV_W = 4
RG_LRU_C = 8.0
N_EXPERTS = 256
TOP_K = 8
N_GROUPS = 8
GROUP_SZ = N_EXPERTS // N_GROUPS
TOPK_GROUPS = 4
EXPERT_H = 256
ROUTED_SCALE = 2.5
LN_EPS = 1e-5
NEG = -1e30

LANES = 128
SUBLANES = 8
WORDS = D_MODEL // 2
CHUNKS = WORDS // LANES
VMEM_LIMIT = 56 * 1024 * 1024
DMA_THREADS = 2

TM_IN = 512
TQ = 128
KW = TQ + 2 * HALF_BAND
SB = TQ * max(d for _, d in BRANCHES)
MERGE_ROWS = 256
TS = 256
TM_MIX = 512
TM_SLOT = 512
TM_CMB = 128
BM = 256


def _cparams(sem):
    return pltpu.CompilerParams(dimension_semantics=sem, vmem_limit_bytes=VMEM_LIMIT)


def _inproj_body(x_ref, pos_ref, w_ref, invf_ref, q_ref, k_ref, v_ref, rx_ref, rg_ref):
    xb = x_ref[...].astype(bf16)
    ang = pos_ref[...].astype(f32) * invf_ref[...]
    cos = jnp.cos(ang)
    sin = jnp.sin(ang)
    j = lax.broadcasted_iota(i32, (1, LANES), 1) % HEAD_DIM
    half = ROT_DIM // 2
    s_lo = jnp.where(j < half, -sin, 0.0)
    s_hi = jnp.where((j >= half) & (j < ROT_DIM), sin, 0.0)
    rep = ATTN_W // LANES
    cos_w = jnp.concatenate([cos] * rep, axis=1)
    s_lo_w = jnp.concatenate([s_lo] * rep, axis=1)
    s_hi_w = jnp.concatenate([s_hi] * rep, axis=1)

    def proj(c0, n):
        return jnp.dot(xb, w_ref[:, c0:c0 + n], preferred_element_type=f32)

    def rope(t):
        return (t * cos_w + pltpu.roll(t, ATTN_W - half, 1) * s_lo_w
                + pltpu.roll(t, half, 1) * s_hi_w)

    q_ref[...] = rope(proj(0, ATTN_W)) * (HEAD_DIM ** -0.5)
    k_ref[...] = rope(proj(ATTN_W, ATTN_W))
    v_ref[...] = proj(2 * ATTN_W, ATTN_W)
    rx_ref[...] = proj(3 * ATTN_W, REC_W)
    rg_ref[...] = proj(3 * ATTN_W + REC_W, REC_W)


def _inproj(x2, pos2, w_in_b, invf):
    T = x2.shape[0]
    tm = TM_IN
    in_w = w_in_b.shape[1]
    tok = lambda i: (i, 0)
    fixed = lambda i: (0, 0)
    return pl.pallas_call(
        _inproj_body,
        grid=(T // tm,),
        in_specs=[pl.BlockSpec((tm, D_MODEL), tok), pl.BlockSpec((tm, 1), tok),
                  pl.BlockSpec((D_MODEL, in_w), fixed), pl.BlockSpec((1, LANES), fixed)],
        out_specs=[pl.BlockSpec((tm, ATTN_W), tok)] * 3 + [pl.BlockSpec((tm, REC_W), tok)] * 2,
        out_shape=[jax.ShapeDtypeStruct((T, ATTN_W), f32)] * 3
        + [jax.ShapeDtypeStruct((T, REC_W), f32)] * 2,
        compiler_params=_cparams(("parallel",)),
        name="inproj",
    )(x2, pos2, w_in_b, invf)


def _attn_body(q_ref, k_ref, v_ref, out_ref, o1, o2, o3, l1, l2, l3, bias_ref, *, S):
    o_sc, l_sc = (o1, o2, o3), (l1, l2, l3)
    lane = lax.broadcasted_iota(i32, (1, LANES), 1)
    head0 = lane < HEAD_DIM
    rel = (lax.broadcasted_iota(i32, (TQ, KW), 0) - lax.broadcasted_iota(i32, (TQ, KW), 1))
    tiles = SB // TQ
    for case in range(3):
        bias_ref[case] = jnp.where(jnp.abs(rel + case * HALF_BAND) <= HALF_BAND, 0.0, NEG)

    def rows(ref, start, n, d):
        return ref[pl.ds(start, n), :] if d == 1 else ref[pl.ds(start, n, stride=d), :]

    def tile(ti, n0):
        ctx = []
        for g, (_, d) in enumerate(BRANCHES):
            L = S // d
            sh = d.bit_length() - 1
            r = jnp.bitwise_and(ti, d - 1)
            m0 = jnp.right_shift(n0, sh) + jnp.right_shift(ti, sh) * TQ
            ks = jnp.clip(m0 - HALF_BAND, 0, L - KW)
            q = rows(q_ref, r + d * m0, TQ, d).astype(bf16)
            k = rows(k_ref, r + d * ks, KW, d).astype(bf16)
            v = rows(v_ref, r + d * ks, KW, d).astype(bf16)
            bias = bias_ref[(m0 - ks) // HALF_BAND]
            local = r + d * (m0 - jnp.right_shift(n0, sh))
            ss = []
            for sel in (head0, jnp.logical_not(head0)):
                qh = jnp.where(sel, q, jnp.zeros_like(q))
                ss.append(lax.dot_general(qh, k, (((1,), (1,)), ((), ())),
                                          preferred_element_type=f32))
            ctx.append((d, v, bias, local, ss))
        soft = []
        for d, v, bias, local, ss in ctx:
            ps = []
            for s in ss:
                s = s + bias
                m = jnp.max(s, axis=-1, keepdims=True)
                p = jnp.exp(s - m)
                den = jnp.sum(p, axis=-1, keepdims=True)
                ps.append((p.astype(bf16), den, m + jnp.log(den)))
            soft.append(ps)
        for g, ((d, v, bias, local, ss), ps) in enumerate(zip(ctx, soft)):
            outs = [(jnp.dot(p, v, preferred_element_type=f32) / den, lse) for p, den, lse in ps]
            o_val = jnp.where(head0, outs[0][0], outs[1][0])
            l_val = jnp.where(head0, outs[0][1], outs[1][1])
            if d == 1:
                o_sc[g][pl.ds(local, TQ), :] = o_val
                l_sc[g][pl.ds(local, TQ), :] = l_val
            else:
                o_sc[g][pl.ds(local, TQ, stride=d), :] = o_val
                l_sc[g][pl.ds(local, TQ, stride=d), :] = l_val
        return n0

    def merge(c, n0):
        sl = pl.ds(pl.multiple_of(c * MERGE_ROWS, MERGE_ROWS), MERGE_ROWS)
        ls = [l_sc[g][sl, :] for g in range(len(BRANCHES))]
        mx = functools.reduce(jnp.maximum, ls)
        es = [jnp.exp(l - mx) for l in ls]
        num = functools.reduce(lambda a, b: a + b, [e * o_sc[g][sl, :] for g, e in enumerate(es)])
        den = functools.reduce(lambda a, b: a + b, es)
        dst = pl.ds(pl.multiple_of(n0 + c * MERGE_ROWS, MERGE_ROWS), MERGE_ROWS)
        out_ref[dst, :] = (num / den).astype(bf16)
        return n0

    def superblock(sb, carry):
        n0 = pl.multiple_of(sb * SB, SB)
        lax.fori_loop(0, tiles, tile, n0)
        lax.fori_loop(0, SB // MERGE_ROWS, merge, n0)
        return carry

    lax.fori_loop(0, S // SB, superblock, 0)


def _attention(q, k, v, B, S):
    assert S % SB == 0 and all(S // d >= KW for _, d in BRANCHES)
    hp = ATTN_W // LANES
    view = lambda t: t.reshape(B, S, ATTN_W)
    spec = pl.BlockSpec((None, S, LANES), lambda b, h: (b, 0, h))
    out = pl.pallas_call(
        functools.partial(_attn_body, S=S),
        grid=(B, hp),
        in_specs=[spec, spec, spec],
        out_specs=spec,
        out_shape=jax.ShapeDtypeStruct((B, S, ATTN_W), bf16),
        scratch_shapes=[pltpu.VMEM((SB, LANES), f32)] * (2 * len(BRANCHES))
        + [pltpu.VMEM((3, TQ, KW), f32)],
        compiler_params=_cparams(("parallel", "parallel")),
        name="attention",
    )(view(q), view(k), view(v))
    return out.reshape(B * S, ATTN_W)


def _rec_body(rx_ref, rg_ref, cw_ref, cb_ref, wg_ref, gb_ref, lam_ref, out_ref, rxp_ref, hf_ref, *, S):
    nch = S // TS
    ntile = TS // SUBLANES
    zeros8 = jnp.zeros((SUBLANES, LANES), f32)
    rxp_ref[pl.ds(0, SUBLANES), :] = zeros8
    rxp_ref[pl.ds(S + SUBLANES, SUBLANES), :] = zeros8

    def pad_copy(c, carry):
        t0 = pl.multiple_of(c * TS, TS)
        rxp_ref[pl.ds(t0 + SUBLANES, TS), :] = rx_ref[pl.ds(t0, TS), :]
        return carry

    lax.fori_loop(0, nch, pad_copy, 0)

    lam = lam_ref[...]
    neg_sp = -RG_LRU_C * (jnp.maximum(-lam, 0.0) + jnp.log1p(jnp.exp(-jnp.abs(lam))))
    cw = cw_ref[...]
    cb = cb_ref[...]
    sub = lax.broadcasted_iota(i32, (ntile, SUBLANES, LANES), 1)
    nrow = TS + 2 * SUBLANES

    def gates(ci, d):
        t0 = pl.multiple_of(ci * TS, TS)
        xw = rxp_ref[pl.ds(t0, nrow), :]
        u = (cw[0:1] * pltpu.roll(xw, 2, 0) + cw[1:2] * pltpu.roll(xw, 1, 0) + cw[2:3] * xw
             + cw[3:4] * pltpu.roll(xw, nrow - 1, 0))[SUBLANES:SUBLANES + TS] + cb
        c0 = d * 2 * LANES
        g = jnp.dot(u.astype(bf16), wg_ref[:, c0:c0 + 2 * LANES], preferred_element_type=f32)
        g = g + gb_ref[:, c0:c0 + 2 * LANES]
        r = jax.nn.sigmoid(g[:, :LANES])
        gi = jax.nn.sigmoid(g[:, LANES:])
        a = jnp.exp(neg_sp[d:d + 1] * r)
        b = jnp.sqrt(1.0 - a * a) * gi * u
        return a.reshape(ntile, SUBLANES, LANES), b.reshape(ntile, SUBLANES, LANES)

    def chunk(i, carry):
        cf, cbk = carry
        a, b = gates(i, 0)
        for s in (1, 2, 4):
            ok = sub >= s
            a_s = pltpu.roll(a, s, 1)
            b_s = pltpu.roll(b, s, 1)
            b = jnp.where(ok, a * b_s + b, b)
            a = jnp.where(ok, a * a_s, a)
        t0 = pl.multiple_of(i * TS, TS)
        for j in range(ntile):
            h = a[j] * cf + b[j]
            hf_ref[pl.ds(t0 + j * SUBLANES, SUBLANES), :] = h
            cf = h[SUBLANES - 1:SUBLANES, :]
        ib = nch - 1 - i
        a, b = gates(ib, 1)
        for s in (1, 2, 4):
            ok = sub < SUBLANES - s
            a_s = pltpu.roll(a, SUBLANES - s, 1)
            b_s = pltpu.roll(b, SUBLANES - s, 1)
            b = jnp.where(ok, a * b_s + b, b)
            a = jnp.where(ok, a * a_s, a)
        t0 = pl.multiple_of(ib * TS, TS)
        for j in range(ntile - 1, -1, -1):
            h = a[j] * cbk + b[j]
            out_ref[pl.ds(t0 + j * SUBLANES, SUBLANES), :] = h
            cbk = h[0:1, :]
        return cf, cbk

    zrow = jnp.zeros((1, LANES), f32)
    lax.fori_loop(0, nch, chunk, (zrow, zrow))

    def finish(c, carry):
        t0 = pl.multiple_of(c * TS, TS)
        sl = pl.ds(t0, TS)
        out_ref[sl, :] = (hf_ref[sl, :] + out_ref[sl, :]) * jax.nn.gelu(rg_ref[sl, :])
        return carry

    lax.fori_loop(0, nch, finish, 0)


def _rec(rx, rg, conv_w, conv_b, wg, gb, lam, B, S):
    ng = REC_W // LANES
    assert S % TS == 0
    seq = pl.BlockSpec((None, S, LANES), lambda b, c: (b, 0, c))
    return pl.pallas_call(
        functools.partial(_rec_body, S=S),
        grid=(B, ng),
        in_specs=[seq, seq,
                  pl.BlockSpec((CONV_W, LANES), lambda b, c: (0, c)),
                  pl.BlockSpec((1, LANES), lambda b, c: (0, c)),
                  pl.BlockSpec((None, LANES, 4 * LANES), lambda b, c: (c, 0, 0)),
                  pl.BlockSpec((None, 1, 4 * LANES), lambda b, c: (c, 0, 0)),
                  pl.BlockSpec((2, LANES), lambda b, c: (0, c))],
        out_specs=seq,
        out_shape=jax.ShapeDtypeStruct((B, S, REC_W), f32),
        scratch_shapes=[pltpu.VMEM((S + 2 * SUBLANES, LANES), f32), pltpu.VMEM((S, LANES), f32)],
        compiler_params=_cparams(("parallel", "parallel")),
        name="rec",
    )(rx.reshape(B, S, REC_W), rg.reshape(B, S, REC_W), conv_w, conv_b, wg, gb, lam)


u32 = jnp.uint32


def _pack_rows(x):
    bits = lax.bitcast_convert_type(x, u32)
    r = (bits + u32(0x7FFF) + ((bits >> 16) & u32(1))) >> 16
    return (r[:, WORDS:] << 16) | r[:, :WORDS]


def _unpack_rows(w):
    lo = lax.bitcast_convert_type(w << 16, f32).astype(bf16)
    hi = lax.bitcast_convert_type(w & u32(0xFFFF0000), f32).astype(bf16)
    return lo, hi


def _slab_chunks(flat, rows, base=0):
    return jnp.concatenate(
        [flat[pl.ds(base * CHUNKS + c, rows, stride=CHUNKS), :] for c in range(CHUNKS)], axis=1)


def _store_slab_chunks(flat, val, rows, base=0):
    for c in range(CHUNKS):
        flat[pl.ds(base * CHUNKS + c, rows, stride=CHUNKS), :] = val[:, c * LANES:(c + 1) * LANES]


def _rms(t, gain):
    return t * lax.rsqrt(jnp.mean(t * t, axis=-1, keepdims=True) + LN_EPS) * gain


def _layer_norm(z, g, b):
    mu = jnp.mean(z, axis=-1, keepdims=True)
    zc = z - mu
    var = jnp.mean(zc * zc, axis=-1, keepdims=True)
    return zc * lax.rsqrt(var + LN_EPS) * g + b


def _mixout_body(attn_ref, rec_ref, x_ref,
                 ag_ref, rgn_ref, wo_ref, g1_ref, b1_ref, rwh_ref, rwl_ref, rb_ref, tri_ref, ones_ref,
                 x1_ref, x1p_ref, idx_ref, gate_ref, rank_ref, cnt_ref, carry_ref, *, alpha):
    tm = TM_MIX

    @pl.when(pl.program_id(0) == 0)
    def _():
        carry_ref[...] = jnp.zeros_like(carry_ref)

    attn_n = _rms(attn_ref[...].astype(f32), ag_ref[...]).astype(bf16)
    rec_n = _rms(rec_ref[...], rgn_ref[...]).astype(bf16)
    y = (jnp.dot(attn_n, wo_ref[0:ATTN_W, :], preferred_element_type=f32)
         + jnp.dot(rec_n, wo_ref[ATTN_W:, :], preferred_element_type=f32))
    x1 = _layer_norm(alpha * x_ref[...] + y, g1_ref[...], b1_ref[...])
    x1_ref[...] = x1
    _store_slab_chunks(x1p_ref, _pack_rows(x1), tm)

    hi = x1.astype(bf16)
    lo = (x1 - hi.astype(f32)).astype(bf16)
    nt = (((1,), (1,)), ((), ()))
    logits = (lax.dot_general(rwh_ref[...], hi, nt, preferred_element_type=f32)
              + lax.dot_general(rwh_ref[...], lo, nt, preferred_element_type=f32)
              + lax.dot_general(rwl_ref[...], hi, nt, preferred_element_type=f32))
    scores = jax.nn.sigmoid(logits)
    biased = scores + jnp.concatenate([rb_ref[...]] * (tm // LANES), axis=1)

    rid = lax.broadcasted_iota(i32, (GROUP_SZ, tm), 0).astype(f32)
    grp = []
    for g in range(N_GROUPS):
        vg = biased[g * GROUP_SZ:(g + 1) * GROUP_SZ, :]
        m1 = jnp.max(vg, axis=0, keepdims=True)
        first = jnp.min(jnp.where(vg == m1, rid, float(GROUP_SZ)), axis=0, keepdims=True)
        m2 = jnp.max(jnp.where(rid == first, -jnp.inf, vg), axis=0, keepdims=True)
        grp.append(m1 + m2)
    eid = lax.broadcasted_iota(i32, (N_EXPERTS, tm), 0).astype(f32)
    keep = [jnp.zeros((1, tm), f32) for _ in range(N_GROUPS)]
    for _ in range(TOPK_GROUPS):
        gm = functools.reduce(jnp.maximum, grp)
        gi = jnp.full((1, tm), float(N_GROUPS), f32)
        for g in range(N_GROUPS - 1, -1, -1):
            gi = jnp.where(grp[g] == gm, float(g), gi)
        hits = [gi == float(g) for g in range(N_GROUPS)]
        grp = [jnp.where(hit, -jnp.inf, sc) for hit, sc in zip(hits, grp)]
        keep = [jnp.where(hit, 1.0, kp) for hit, kp in zip(hits, keep)]
    masked = jnp.concatenate(
        [jnp.where(jnp.broadcast_to(keep[g], (GROUP_SZ, tm)) > 0.5,
                   biased[g * GROUP_SZ:(g + 1) * GROUP_SZ, :], -jnp.inf)
         for g in range(N_GROUPS)], axis=0)

    onehot = jnp.zeros((N_EXPERTS, tm), f32)
    idxs, gts = [], []
    for _ in range(TOP_K):
        mx = jnp.max(masked, axis=0, keepdims=True)
        ix = jnp.min(jnp.where(masked == mx, eid, float(N_EXPERTS)), axis=0, keepdims=True)
        hit = eid == ix
        gts.append(jnp.sum(jnp.where(hit, scores, 0.0), axis=0, keepdims=True))
        idxs.append(ix)
        masked = jnp.where(hit, -jnp.inf, masked)
        onehot = onehot + jnp.where(hit, 1.0, 0.0)
    gsum = functools.reduce(lambda p, q: p + q, gts)
    for kk in range(TOP_K):
        idx_ref[kk:kk + 1, :] = idxs[kk].astype(i32)
        gate_ref[kk:kk + 1, :] = gts[kk] / gsum * ROUTED_SCALE

    oh = onehot.astype(bf16)
    before = carry_ref[...] + jnp.dot(oh, tri_ref[...], preferred_element_type=f32)
    for kk in range(TOP_K):
        rk = jnp.sum(jnp.where(eid == idxs[kk], before, 0.0), axis=0, keepdims=True)
        rank_ref[kk:kk + 1, :] = rk.astype(i32)
    total = carry_ref[...] + jnp.dot(oh, ones_ref[...], preferred_element_type=f32)
    carry_ref[...] = total
    cnt_ref[...] = total


def _mixout(attn, rec, x2, attn_gain, rec_gain, w_out_b, g1, b1, rw_hi, rw_lo, rbias, alpha):
    T = x2.shape[0]
    tm = TM_MIX
    tok = lambda i: (i, 0)
    fixed = lambda i: (0, 0)
    tri = jnp.asarray(np.triu(np.ones((tm, tm), np.float32), k=1), bf16)
    ones = jnp.ones((tm, tm), bf16)
    aw = pl.BlockSpec((tm, ATTN_W), tok)
    row = lambda n: pl.BlockSpec((1, n), fixed)
    kt = pl.BlockSpec((TOP_K, tm), lambda i: (0, i))
    return pl.pallas_call(
        functools.partial(_mixout_body, alpha=alpha),
        grid=(T // tm,),
        in_specs=[aw] * 2 + [pl.BlockSpec((tm, D_MODEL), tok), row(ATTN_W), row(REC_W),
                             pl.BlockSpec((D_MODEL, D_MODEL), fixed), row(D_MODEL), row(D_MODEL),
                             pl.BlockSpec((N_EXPERTS, D_MODEL), fixed),
                             pl.BlockSpec((N_EXPERTS, D_MODEL), fixed),
                             pl.BlockSpec((N_EXPERTS, LANES), fixed),
                             pl.BlockSpec((tm, tm), fixed), pl.BlockSpec((tm, tm), fixed)],
        out_specs=[pl.BlockSpec((tm, D_MODEL), tok),
                   pl.BlockSpec((tm * CHUNKS, LANES), tok),
                   kt, kt, kt, pl.BlockSpec((N_EXPERTS, tm), fixed)],
        out_shape=[jax.ShapeDtypeStruct((T, D_MODEL), f32),
                   jax.ShapeDtypeStruct((T * CHUNKS, LANES), u32),
                   jax.ShapeDtypeStruct((TOP_K, T), i32),
                   jax.ShapeDtypeStruct((TOP_K, T), f32),
                   jax.ShapeDtypeStruct((TOP_K, T), i32),
                   jax.ShapeDtypeStruct((N_EXPERTS, tm), f32)],
        scratch_shapes=[pltpu.VMEM((N_EXPERTS, tm), f32)],
        compiler_params=_cparams(("arbitrary",)),
        name="mixout",
    )(attn, rec, x2, attn_gain, rec_gain, w_out_b, g1, b1,
      rw_hi, rw_lo, rbias, tri, ones)


def _slots_body(idx_ref, rank_ref, start_ref, dest_ref):
    tm = TM_SLOT
    eid = lax.broadcasted_iota(i32, (N_EXPERTS, tm), 0)
    start = jnp.concatenate([start_ref[...]] * (tm // LANES), axis=1)
    for kk in range(TOP_K):
        hit = eid == idx_ref[kk:kk + 1, :]
        base = jnp.sum(jnp.where(hit, start, 0.0), axis=0, keepdims=True)
        dest_ref[kk:kk + 1, :] = base.astype(i32) + rank_ref[kk:kk + 1, :]


def _slots(idx, rank, pad_start):
    T = idx.shape[1]
    tm = TM_SLOT
    kt = pl.BlockSpec((TOP_K, tm), lambda i: (0, i))
    start = jnp.broadcast_to(pad_start.astype(f32).reshape(N_EXPERTS, 1), (N_EXPERTS, LANES))
    return pl.pallas_call(
        _slots_body,
        grid=(T // tm,),
        in_specs=[kt, kt, pl.BlockSpec((N_EXPERTS, LANES), lambda i: (0, 0))],
        out_specs=kt,
        out_shape=jax.ShapeDtypeStruct((TOP_K, T), i32),
        compiler_params=_cparams(("parallel",)),
        name="slots",
    )(idx, rank, start)


def _dispatch_body(dest_ref, x_ref, xs_ref, sem):
    tm = TM_MIX

    def copy(t, k):
        return pltpu.make_async_copy(x_ref.at[t], xs_ref.at[dest_ref[k, t]], sem)

    def issue(t, carry):
        for k in range(TOP_K):
            copy(t, k).start(priority=k % DMA_THREADS)
        return carry

    def drain(t, carry):
        for k in range(TOP_K):
            copy(t, k).wait()
        return carry

    lax.fori_loop(0, tm, issue, 0)
    lax.fori_loop(0, tm, drain, 0)


def _dispatch(dest, x1p, n_slots):
    T = x1p.shape[0]
    tm = TM_MIX
    return pl.pallas_call(
        _dispatch_body,
        grid=(T // tm,),
        in_specs=[pl.BlockSpec((TOP_K, tm), lambda i: (0, i), memory_space=pltpu.SMEM),
                  pl.BlockSpec((tm, CHUNKS, LANES), lambda i: (i, 0, 0))],
        out_specs=pl.BlockSpec(memory_space=pl.ANY),
        out_shape=jax.ShapeDtypeStruct((n_slots, CHUNKS, LANES), u32),
        scratch_shapes=[pltpu.SemaphoreType.DMA(())],
        compiler_params=_cparams(("arbitrary",)),
        name="dispatch",
    )(dest, x1p)


def _experts_body(be_ref, nx_ref, nv_ref, nb_ref, xs_ref, wg_hbm, wu_hbm, wd_hbm, ys_ref,
                  fga, fua, fda, fgb, fub, fdb, sga, sua, sda, sgb, sub, sdb, sems):
    p = pl.program_id(0)
    blocks = (2 * p, 2 * p + 1)
    hbm_w = (wg_hbm, wu_hbm, wd_hbm)
    f32_w = ((fga, fua, fda), (fgb, fub, fdb))
    bf_w = ((sga, sua, sda), (sgb, sub, sdb))
    live_step = blocks[0] < nb_ref[0]

    def fetch(half, e):
        return [pltpu.make_async_copy(src.at[e], dst, sems.at[half])
                for src, dst in zip(hbm_w, f32_w[half])]

    for half, blk in enumerate(blocks):
        @pl.when(jnp.logical_and(live_step, p == 0))
        def _(half=half, blk=blk):
            for c in fetch(half, be_ref[blk]):
                c.start()

        prev = jnp.maximum(blk - 2, 0)

        @pl.when(jnp.logical_and(live_step,
                                 jnp.logical_or(p == 0, be_ref[blk] != be_ref[prev])))
        def _(half=half, blk=blk):
            for c in fetch(half, be_ref[blk]):
                c.wait()
            for src, dst in zip(f32_w[half], bf_w[half]):
                dst[...] = src[...].astype(bf16)

            @pl.when(nx_ref[blk] >= 0)
            def _():
                for c in fetch(half, nx_ref[blk]):
                    c.start()

    @pl.when(live_step)
    def _():
        xs = []
        for half, blk in enumerate(blocks):
            rows = jnp.where(blk < nb_ref[0], nv_ref[blk], 0)
            live = lax.broadcasted_iota(i32, (BM, WORDS), 0) < rows
            words = _slab_chunks(xs_ref, BM, half * BM)
            xs.append(_unpack_rows(jnp.where(live, words, u32(0))))

        def up(half, w_ref):
            lo, hi = xs[half]
            return (jnp.dot(lo, w_ref[:WORDS, :], preferred_element_type=f32)
                    + jnp.dot(hi, w_ref[WORDS:, :], preferred_element_type=f32))

        gs = [up(half, bf_w[half][0]) for half in range(2)]
        us = [up(half, bf_w[half][1]) for half in range(2)]
        hs = [(g * jax.nn.sigmoid(g) * u).astype(bf16) for g, u in zip(gs, us)]
        ys = [jnp.dot(h, bf_w[half][2][...], preferred_element_type=f32)
              for half, h in enumerate(hs)]
        for half in range(2):
            _store_slab_chunks(ys_ref, _pack_rows(ys[half]), BM, half * BM)


def _experts(block_e, next_e, block_rows, nb_used, xs, wg, wu, wd):
    P = xs.shape[0] // CHUNKS
    nb = P // BM
    assert nb % 2 == 0
    last = lambda nbu: jnp.maximum((nbu[0] - 1) // 2, 0)
    rows = lambda p, be, nx, nv, nbu: (jnp.minimum(p, last(nbu)), 0)
    any_spec = pl.BlockSpec(memory_space=pl.ANY)
    up_shape, down_shape = (D_MODEL, EXPERT_H), (EXPERT_H, D_MODEL)
    per_stream = lambda dt: [pltpu.VMEM(up_shape, dt), pltpu.VMEM(up_shape, dt),
                             pltpu.VMEM(down_shape, dt)]
    gs = pltpu.PrefetchScalarGridSpec(
        num_scalar_prefetch=4,
        grid=(nb // 2,),
        in_specs=[pl.BlockSpec((2 * BM * CHUNKS, LANES), rows), any_spec, any_spec, any_spec],
        out_specs=pl.BlockSpec((2 * BM * CHUNKS, LANES), rows),
        scratch_shapes=per_stream(f32) * 2 + per_stream(bf16) * 2
        + [pltpu.SemaphoreType.DMA((2,))],
    )
    return pl.pallas_call(
        _experts_body,
        grid_spec=gs,
        out_shape=jax.ShapeDtypeStruct((P * CHUNKS, LANES), u32),
        compiler_params=_cparams(("arbitrary",)),
        name="experts",
    )(block_e, next_e, block_rows, nb_used, xs, wg, wu, wd)


def _combine_body(dest_ref, dnext_ref, gate_ref, x1_ref, ys_ref, ysflat_ref, sg_ref, su_ref, sd_ref,
                  g2_ref, b2_ref, out_ref, gb00, gb01, gb10, gb11, sems, *, alpha):
    tm = TM_CMB
    j = pl.program_id(0)
    n = pl.num_programs(0)
    gbufs = ((gb00, gb01), (gb10, gb11))

    def issue(d_ref, st, half):
        base = half * tm

        def step(t, carry):
            for k in range(TOP_K):
                src = ys_ref.at[d_ref[(base + t) * TOP_K + k]]
                dst = gbufs[st][half].at[pl.ds((k * tm + t) * CHUNKS, CHUNKS)]
                pltpu.make_async_copy(src, dst, sems.at[st, half]).start(priority=k % DMA_THREADS)
            return carry
        lax.fori_loop(0, tm, step, 0)

    def finish(st, half):
        rows = pl.ds(half * tm, tm)
        x1 = x1_ref[rows, :]
        xb = x1.astype(bf16)
        sg = jnp.dot(xb, sg_ref[...], preferred_element_type=f32)
        su = jnp.dot(xb, su_ref[...], preferred_element_type=f32)
        shared = jnp.dot((sg * jax.nn.sigmoid(sg) * su).astype(bf16), sd_ref[...],
                         preferred_element_type=f32)
        flat = gbufs[st][half]
        pltpu.make_async_copy(ysflat_ref.at[pl.ds(0, TOP_K * tm * CHUNKS)], flat,
                              sems.at[st, half]).wait()

        gates = gate_ref[rows, :]
        lo = [jnp.zeros((tm, LANES), f32) for _ in range(CHUNKS)]
        hi = [jnp.zeros((tm, LANES), f32) for _ in range(CHUNKS)]
        for k in range(TOP_K):
            gk = jnp.broadcast_to(gates[:, k:k + 1], (tm, LANES))
            for c in range(CHUNKS):
                w = flat[pl.ds(k * tm * CHUNKS + c, tm, stride=CHUNKS), :]
                lo[c] = lo[c] + gk * lax.bitcast_convert_type(w << 16, f32)
                hi[c] = hi[c] + gk * lax.bitcast_convert_type(w & u32(0xFFFF0000), f32)
        routed = jnp.concatenate(lo + hi, axis=1)
        out_ref[rows, :] = _layer_norm(alpha * x1 + (routed + shared), g2_ref[...], b2_ref[...])

    @pl.when(j == 0)
    def _():
        issue(dest_ref, 0, 0)
        issue(dest_ref, 0, 1)

    for st in range(2):
        @pl.when(lax.rem(j, 2) == st)
        def _(st=st):
            @pl.when(j + 1 < n)
            def _():
                issue(dnext_ref, 1 - st, 0)
                issue(dnext_ref, 1 - st, 1)

            finish(st, 0)
            finish(st, 1)


def _combine(dest, gate, x1, ys, sg, su, sd, g2, b2, alpha):
    T = x1.shape[0]
    tm = TM_CMB
    n = T // (2 * tm)
    tok = lambda j: (j, 0)
    fixed = lambda j: (0, 0)
    tbl = pl.BlockSpec((2 * tm * TOP_K,), lambda j: (j,), memory_space=pltpu.SMEM)
    tbl_next = pl.BlockSpec((2 * tm * TOP_K,), lambda j: (jnp.minimum(j + 1, n - 1),),
                            memory_space=pltpu.SMEM)
    any_spec = pl.BlockSpec(memory_space=pl.ANY)
    return pl.pallas_call(
        functools.partial(_combine_body, alpha=alpha),
        grid=(n,),
        in_specs=[tbl, tbl_next, pl.BlockSpec((2 * tm, TOP_K), tok),
                  pl.BlockSpec((2 * tm, D_MODEL), tok), any_spec, any_spec,
                  pl.BlockSpec(sg.shape, fixed), pl.BlockSpec(su.shape, fixed),
                  pl.BlockSpec(sd.shape, fixed), pl.BlockSpec((1, D_MODEL), fixed),
                  pl.BlockSpec((1, D_MODEL), fixed)],
        out_specs=pl.BlockSpec((2 * tm, D_MODEL), tok),
        out_shape=jax.ShapeDtypeStruct((T, D_MODEL), f32),
        scratch_shapes=[pltpu.VMEM((TOP_K * tm * CHUNKS, LANES), u32)] * 4
        + [pltpu.SemaphoreType.DMA((2, 2))],
        compiler_params=_cparams(("arbitrary",)),
        name="combine",
    )(dest, dest, gate, x1, ys, ys.reshape(ys.shape[0] * CHUNKS, LANES), sg, su, sd, g2, b2)


def _rope_inv_freq():
    half = ROT_DIM // 2
    inv = ROPE_THETA ** (-jnp.arange(half, dtype=f32) * 2.0 / ROT_DIM)
    j = np.arange(LANES) % HEAD_DIM
    table = jnp.where(j < ROT_DIM, inv[j % half], 0.0)
    return table.reshape(1, LANES).astype(f32)


def _gate_weights(ga_w, ga_b, gx_w, gx_b):
    ng = REC_W // LANES
    per = LANES // HEAD_DIM
    def bd(w):
        w = w.reshape(ng, per, HEAD_DIM, HEAD_DIM)
        z = jnp.zeros((ng, LANES, LANES), w.dtype)
        for p in range(per):
            z = z.at[:, p * HEAD_DIM:(p + 1) * HEAD_DIM, p * HEAD_DIM:(p + 1) * HEAD_DIM].set(w[:, p])
        return z
    wg = jnp.concatenate([bd(ga_w[0]), bd(gx_w[0]), bd(ga_w[1]), bd(gx_w[1])], axis=-1).astype(bf16)
    grp = lambda b: b.reshape(ng, 1, LANES)
    gb = jnp.concatenate([grp(ga_b[0]), grp(gx_b[0]), grp(ga_b[1]), grp(gx_b[1])], axis=-1)
    return wg, gb


def _layer(x, positions, w_in, attn_gain, conv_w, conv_b, ga_w, ga_b, gx_w, gx_b, lam, rec_gain,
           w_out, ln1_g, ln1_b, router_w, router_bias, e_wg, e_wu, e_wd, s_wg, s_wu, s_wd,
           ln2_g, ln2_b, alpha):
    B, S, _ = x.shape
    T = B * S
    x2 = x.reshape(T, D_MODEL)
    pos2 = positions.reshape(T, 1)

    q, k, v, rx, rg = _inproj(x2, pos2, w_in.astype(bf16), _rope_inv_freq())
    attn = _attention(q, k, v, B, S)
    wg, gb = _gate_weights(ga_w, ga_b, gx_w, gx_b)
    rec = _rec(rx, rg, conv_w, conv_b.reshape(1, REC_W), wg, gb, lam, B, S).reshape(T, REC_W)

    rw_t = router_w.T
    rw_hi = rw_t.astype(bf16)
    rw_lo = (rw_t - rw_hi.astype(f32)).astype(bf16)
    rbias = jnp.broadcast_to(router_bias.reshape(N_EXPERTS, 1), (N_EXPERTS, LANES))
    x1, x1p, idx, gate, rank, cnt = _mixout(
        attn, rec, x2,
        attn_gain.reshape(1, ATTN_W), rec_gain.reshape(1, REC_W), w_out.astype(bf16),
        ln1_g.reshape(1, D_MODEL), ln1_b.reshape(1, D_MODEL), rw_hi, rw_lo, rbias, alpha)

    counts = cnt[:, 0].astype(i32)
    padded = (counts + BM - 1) // BM * BM
    pad_end = jnp.cumsum(padded)
    pad_start = pad_end - padded
    slot = _slots(idx, rank, pad_start)
    nb = (T * TOP_K + N_EXPERTS * (BM - 1)) // BM + 1
    nb += nb % 2
    nb_used = pad_end[-1] // BM
    h = (nb_used + 1) // 2
    pos = jnp.arange(nb, dtype=i32)
    blk = jnp.where(pos % 2 == 0, pos // 2, h + pos // 2)
    live = jnp.logical_and(pos < 2 * h, blk < nb_used)
    first_row = blk * BM
    block_e = jnp.minimum(jnp.sum((pad_end[None, :] <= first_row[:, None]).astype(i32), axis=1),
                          N_EXPERTS - 1)
    block_e = jnp.where(live, block_e, N_EXPERTS - 1)
    block_rows = jnp.where(live, jnp.clip((pad_start + counts)[block_e] - first_row, 0, BM), 0)
    pairs = block_e.reshape(nb // 2, 2)
    step_id = jnp.arange(nb // 2, dtype=i32)[:, None]
    change = jnp.concatenate([pairs[1:] != pairs[:-1], jnp.zeros((1, 2), bool)], axis=0)
    change = jnp.logical_and(change, step_id + 1 < h)
    nxt_step = lax.cummin(jnp.where(change, step_id + 1, nb), axis=0, reverse=True)
    next_e = jnp.where(nxt_step < nb // 2,
                       jnp.take_along_axis(pairs, jnp.minimum(nxt_step, nb // 2 - 1), axis=0), -1)
    next_e = next_e.reshape(nb).astype(i32)
    sblk = slot // BM
    dest = jnp.where(sblk < h, 2 * sblk, 2 * (sblk - h) + 1) * BM + slot % BM
    nb_used = (2 * h).astype(i32).reshape(1)

    xs = _dispatch(dest, x1p.reshape(T, CHUNKS, LANES), nb * BM)
    ys = _experts(block_e.astype(i32), next_e, block_rows.astype(i32), nb_used,
                  xs.reshape(nb * BM * CHUNKS, LANES), e_wg, e_wu, e_wd)
    ys = ys.reshape(nb * BM, CHUNKS, LANES)
    out = _combine(dest.T.reshape(-1), gate.T, x1, ys, s_wg.astype(bf16), s_wu.astype(bf16), s_wd.astype(bf16),
                   ln2_g.reshape(1, D_MODEL), ln2_b.reshape(1, D_MODEL), alpha)
    return out.reshape(B, S, D_MODEL)


def kernel(x, positions, w_in, attn_gain, rec_conv_w, rec_conv_b, rec_gate_a_w, rec_gate_a_b,
           rec_gate_x_w, rec_gate_x_b, rec_lambda, rec_gain, w_out, ln1_g, ln1_b, router_w,
           router_bias, exp_w_gate, exp_w_up, exp_w_down, shared_w_gate, shared_w_up,
           shared_w_down, ln2_g, ln2_b):
    depth = w_in.shape[0]
    alpha = (2 * depth) ** 0.25
    for l in range(depth):
        x = _layer(x, positions, w_in[l], attn_gain[l], rec_conv_w[l], rec_conv_b[l],
                   rec_gate_a_w[l], rec_gate_a_b[l], rec_gate_x_w[l], rec_gate_x_b[l],
                   rec_lambda[l], rec_gain[l], w_out[l], ln1_g[l], ln1_b[l], router_w[l],
                   router_bias[l], exp_w_gate[l], exp_w_up[l], exp_w_down[l], shared_w_gate[l],
                   shared_w_up[l], shared_w_down[l], ln2_g[l], ln2_b[l], alpha)
    return x
```

```python
import functools
import math

import jax
import jax.numpy as jnp
import numpy as np
from jax import lax
from jax.experimental import pallas as pl
from jax.experimental.pallas import tpu as pltpu

f32 = jnp.float32
bf16 = jnp.bfloat16
i32 = jnp.int32

D_MODEL = 1024
ATTN_W = 512
REC_W = 512
HEAD_DIM = 64
ROT_DIM = 16
ROPE_THETA = 500000.0
BRANCHES = ((128, 1), (512, 4), (2048, 16))
HALF_BAND = 64
CONV_W = 4
RG_LRU_C = 8.0
N_EXPERTS = 256
TOP_K = 8
N_GROUPS = 8
GROUP_SZ = N_EXPERTS // N_GROUPS
TOPK_GROUPS = 4
EXPERT_H = 256
ROUTED_SCALE = 2.5
LN_EPS = 1e-5
NEG = -1e30

LANES = 128
SUBLANES = 8
WORDS = D_MODEL // 2
CHUNKS = WORDS // LANES
VMEM_LIMIT = 56 * 1024 * 1024
DMA_THREADS = 2

TM_IN = 512
TQ = 128
KW = TQ + 2 * HALF_BAND
SB = TQ * max(d for _, d in BRANCHES)
MERGE_ROWS = 256
TS = 256
TM_MIX = 512
TM_SLOT = 512
TM_CMB = 128
BM = 256


def _cparams(sem):
    return pltpu.CompilerParams(dimension_semantics=sem, vmem_limit_bytes=VMEM_LIMIT)


def _inproj_body(x_ref, pos_ref, w_ref, invf_ref, q_ref, k_ref, v_ref, rx_ref, rg_ref):
    xb = x_ref[...].astype(bf16)
    ang = pos_ref[...].astype(f32) * invf_ref[...]
    cos = jnp.cos(ang)
    sin = jnp.sin(ang)
    j = lax.broadcasted_iota(i32, (1, LANES), 1) % HEAD_DIM
    half = ROT_DIM // 2
    s_lo = jnp.where(j < half, -sin, 0.0)
    s_hi = jnp.where((j >= half) & (j < ROT_DIM), sin, 0.0)
    rep = ATTN_W // LANES
    cos_w = jnp.concatenate([cos] * rep, axis=1)
    s_lo_w = jnp.concatenate([s_lo] * rep, axis=1)
    s_hi_w = jnp.concatenate([s_hi] * rep, axis=1)

    def proj(c0, n):
        return jnp.dot(xb, w_ref[:, c0:c0 + n], preferred_element_type=f32)

    def rope(t):
        return (t * cos_w + pltpu.roll(t, ATTN_W - half, 1) * s_lo_w
                + pltpu.roll(t, half, 1) * s_hi_w)

    q_ref[...] = rope(proj(0, ATTN_W)) * (HEAD_DIM ** -0.5)
    k_ref[...] = rope(proj(ATTN_W, ATTN_W))
    v_ref[...] = proj(2 * ATTN_W, ATTN_W)
    rx_ref[...] = proj(3 * ATTN_W, REC_W)
    rg_ref[...] = proj(3 * ATTN_W + REC_W, REC_W)


def _inproj(x2, pos2, w_in_b, invf):
    T = x2.shape[0]
    tm = TM_IN
    in_w = w_in_b.shape[1]
    tok = lambda i: (i, 0)
    fixed = lambda i: (0, 0)
    return pl.pallas_call(
        _inproj_body,
        grid=(T // tm,),
        in_specs=[pl.BlockSpec((tm, D_MODEL), tok), pl.BlockSpec((tm, 1), tok),
                  pl.BlockSpec((D_MODEL, in_w), fixed), pl.BlockSpec((1, LANES), fixed)],
        out_specs=[pl.BlockSpec((tm, ATTN_W), tok)] * 3 + [pl.BlockSpec((tm, REC_W), tok)] * 2,
        out_shape=[jax.ShapeDtypeStruct((T, ATTN_W), f32)] * 3
        + [jax.ShapeDtypeStruct((T, REC_W), f32)] * 2,
        compiler_params=_cparams(("parallel",)),
        name="inproj",
    )(x2, pos2, w_in_b, invf)


def _attn_body(q_ref, k_ref, v_ref, out_ref, o1, o2, o3, l1, l2, l3, bias_ref, *, S):
    o_sc, l_sc = (o1, o2, o3), (l1, l2, l3)
    lane = lax.broadcasted_iota(i32, (1, LANES), 1)
    head0 = lane < HEAD_DIM
    rel = (lax.broadcasted_iota(i32, (TQ, KW), 0) - lax.broadcasted_iota(i32, (TQ, KW), 1))
    tiles = SB // TQ
    for case in range(3):
        bias_ref[case] = jnp.where(jnp.abs(rel + case * HALF_BAND) <= HALF_BAND, 0.0, NEG)

    def rows(ref, start, n, d):
        return ref[pl.ds(start, n), :] if d == 1 else ref[pl.ds(start, n, stride=d), :]

    def tile(ti, n0):
        ctx = []
        for g, (_, d) in enumerate(BRANCHES):
            L = S // d
            sh = d.bit_length() - 1
            r = jnp.bitwise_and(ti, d - 1)
            m0 = jnp.right_shift(n0, sh) + jnp.right_shift(ti, sh) * TQ
            ks = jnp.clip(m0 - HALF_BAND, 0, L - KW)
            q = rows(q_ref, r + d * m0, TQ, d).astype(bf16)
            k = rows(k_ref, r + d * ks, KW, d).astype(bf16)
            v = rows(v_ref, r + d * ks, KW, d).astype(bf16)
            bias = bias_ref[(m0 - ks) // HALF_BAND]
            local = r + d * (m0 - jnp.right_shift(n0, sh))
            ss = []
            for sel in (head0, jnp.logical_not(head0)):
                qh = jnp.where(sel, q, jnp.zeros_like(q))
                ss.append(lax.dot_general(qh, k, (((1,), (1,)), ((), ())),
                                          preferred_element_type=f32))
            ctx.append((d, v, bias, local, ss))
        soft = []
        for d, v, bias, local, ss in ctx:
            ps = []
            for s in ss:
                s = s + bias
                m = jnp.max(s, axis=-1, keepdims=True)
                p = jnp.exp(s - m)
                den = jnp.sum(p, axis=-1, keepdims=True)
                ps.append((p.astype(bf16), den, m + jnp.log(den)))
            soft.append(ps)
        for g, ((d, v, bias, local, ss), ps) in enumerate(zip(ctx, soft)):
            outs = [(jnp.dot(p, v, preferred_element_type=f32) / den, lse) for p, den, lse in ps]
            o_val = jnp.where(head0, outs[0][0], outs[1][0])
            l_val = jnp.where(head0, outs[0][1], outs[1][1])
            if d == 1:
                o_sc[g][pl.ds(local, TQ), :] = o_val
                l_sc[g][pl.ds(local, TQ), :] = l_val
            else:
                o_sc[g][pl.ds(local, TQ, stride=d), :] = o_val
                l_sc[g][pl.ds(local, TQ, stride=d), :] = l_val
        return n0

    def merge(c, n0):
        sl = pl.ds(pl.multiple_of(c * MERGE_ROWS, MERGE_ROWS), MERGE_ROWS)
        ls = [l_sc[g][sl, :] for g in range(len(BRANCHES))]
        mx = functools.reduce(jnp.maximum, ls)
        es = [jnp.exp(l - mx) for l in ls]
        num = functools.reduce(lambda a, b: a + b, [e * o_sc[g][sl, :] for g, e in enumerate(es)])
        den = functools.reduce(lambda a, b: a + b, es)
        dst = pl.ds(pl.multiple_of(n0 + c * MERGE_ROWS, MERGE_ROWS), MERGE_ROWS)
        out_ref[dst, :] = (num / den).astype(bf16)
        return n0

    def superblock(sb, carry):
        n0 = pl.multiple_of(sb * SB, SB)
        lax.fori_loop(0, tiles, tile, n0)
        lax.fori_loop(0, SB // MERGE_ROWS, merge, n0)
        return carry

    lax.fori_loop(0, S // SB, superblock, 0)


def _attention(q, k, v, B, S):
    assert S % SB == 0 and all(S // d >= KW for _, d in BRANCHES)
    hp = ATTN_W // LANES
    view = lambda t: t.reshape(B, S, ATTN_W)
    spec = pl.BlockSpec((None, S, LANES), lambda b, h: (b, 0, h))
    out = pl.pallas_call(
        functools.partial(_attn_body, S=S),
        grid=(B, hp),
        in_specs=[spec, spec, spec],
        out_specs=spec,
        out_shape=jax.ShapeDtypeStruct((B, S, ATTN_W), bf16),
        scratch_shapes=[pltpu.VMEM((SB, LANES), f32)] * (2 * len(BRANCHES))
        + [pltpu.VMEM((3, TQ, KW), f32)],
        compiler_params=_cparams(("parallel", "parallel")),
        name="attention",
    )(view(q), view(k), view(v))
    return out.reshape(B * S, ATTN_W)


def _rec_body(rx_ref, rg_ref, cw_ref, cb_ref, wg_ref, gb_ref, lam_ref, out_ref, rxp_ref, hf_ref, *, S):
    nch = S // TS
    ntile = TS // SUBLANES
    zeros8 = jnp.zeros((SUBLANES, LANES), f32)
    rxp_ref[pl.ds(0, SUBLANES), :] = zeros8
    rxp_ref[pl.ds(S + SUBLANES, SUBLANES), :] = zeros8

    def pad_copy(c, carry):
        t0 = pl.multiple_of(c * TS, TS)
        rxp_ref[pl.ds(t0 + SUBLANES, TS), :] = rx_ref[pl.ds(t0, TS), :]
        return carry

    lax.fori_loop(0, nch, pad_copy, 0)

    lam = lam_ref[...]
    neg_sp = -RG_LRU_C * (jnp.maximum(-lam, 0.0) + jnp.log1p(jnp.exp(-jnp.abs(lam))))
    cw = cw_ref[...]
    cb = cb_ref[...]
    sub = lax.broadcasted_iota(i32, (ntile, SUBLANES, LANES), 1)
    nrow = TS + 2 * SUBLANES

    def gates(ci, d):
        t0 = pl.multiple_of(ci * TS, TS)
        xw = rxp_ref[pl.ds(t0, nrow), :]
        u = (cw[0:1] * pltpu.roll(xw, 2, 0) + cw[1:2] * pltpu.roll(xw, 1, 0) + cw[2:3] * xw
             + cw[3:4] * pltpu.roll(xw, nrow - 1, 0))[SUBLANES:SUBLANES + TS] + cb
        c0 = d * 2 * LANES
        g = jnp.dot(u.astype(bf16), wg_ref[:, c0:c0 + 2 * LANES], preferred_element_type=f32)
        g = g + gb_ref[:, c0:c0 + 2 * LANES]
        r = jax.nn.sigmoid(g[:, :LANES])
        gi = jax.nn.sigmoid(g[:, LANES:])
        a = jnp.exp(neg_sp[d:d + 1] * r)
        b = jnp.sqrt(1.0 - a * a) * gi * u
        return a.reshape(ntile, SUBLANES, LANES), b.reshape(ntile, SUBLANES, LANES)

    def chunk(i, carry):
        cf, cbk = carry
        a, b = gates(i, 0)
        for s in (1, 2, 4):
            ok = sub >= s
            a_s = pltpu.roll(a, s, 1)
            b_s = pltpu.roll(b, s, 1)
            b = jnp.where(ok, a * b_s + b, b)
            a = jnp.where(ok, a * a_s, a)
        t0 = pl.multiple_of(i * TS, TS)
        for j in range(ntile):
            h = a[j] * cf + b[j]
            hf_ref[pl.ds(t0 + j * SUBLANES, SUBLANES), :] = h
            cf = h[SUBLANES - 1:SUBLANES, :]
        ib = nch - 1 - i
        a, b = gates(ib, 1)
        for s in (1, 2, 4):
            ok = sub < SUBLANES - s
            a_s = pltpu.roll(a, SUBLANES - s, 1)
            b_s = pltpu.roll(b, SUBLANES - s, 1)
            b = jnp.where(ok, a * b_s + b, b)
            a = jnp.where(ok, a * a_s, a)
        t0 = pl.multiple_of(ib * TS, TS)
        for j in range(ntile - 1, -1, -1):
            h = a[j] * cbk + b[j]
            out_ref[pl.ds(t0 + j * SUBLANES, SUBLANES), :] = h
            cbk = h[0:1, :]
        return cf, cbk

    zrow = jnp.zeros((1, LANES), f32)
    lax.fori_loop(0, nch, chunk, (zrow, zrow))

    def finish(c, carry):
        t0 = pl.multiple_of(c * TS, TS)
        sl = pl.ds(t0, TS)
        out_ref[sl, :] = (hf_ref[sl, :] + out_ref[sl, :]) * jax.nn.gelu(rg_ref[sl, :])
        return carry

    lax.fori_loop(0, nch, finish, 0)


def _rec(rx, rg, conv_w, conv_b, wg, gb, lam, B, S):
    ng = REC_W // LANES
    assert S % TS == 0
    seq = pl.BlockSpec((None, S, LANES), lambda b, c: (b, 0, c))
    return pl.pallas_call(
        functools.partial(_rec_body, S=S),
        grid=(B, ng),
        in_specs=[seq, seq,
                  pl.BlockSpec((CONV_W, LANES), lambda b, c: (0, c)),
                  pl.BlockSpec((1, LANES), lambda b, c: (0, c)),
                  pl.BlockSpec((None, LANES, 4 * LANES), lambda b, c: (c, 0, 0)),
                  pl.BlockSpec((None, 1, 4 * LANES), lambda b, c: (c, 0, 0)),
                  pl.BlockSpec((2, LANES), lambda b, c: (0, c))],
        out_specs=seq,
        out_shape=jax.ShapeDtypeStruct((B, S, REC_W), f32),
        scratch_shapes=[pltpu.VMEM((S + 2 * SUBLANES, LANES), f32), pltpu.VMEM((S, LANES), f32)],
        compiler_params=_cparams(("parallel", "parallel")),
        name="rec",
    )(rx.reshape(B, S, REC_W), rg.reshape(B, S, REC_W), conv_w, conv_b, wg, gb, lam)


u32 = jnp.uint32


def _pack_rows(x):
    r = lax.bitcast_convert_type(x.astype(bf16).astype(f32), u32)
    return (r[:, WORDS:] & u32(0xFFFF0000)) | (r[:, :WORDS] >> 16)


def _unpack_rows(w):
    lo = lax.bitcast_convert_type(w << 16, f32).astype(bf16)
    hi = lax.bitcast_convert_type(w & u32(0xFFFF0000), f32).astype(bf16)
    return lo, hi


def _slab_chunks(flat, rows, base=0):
    return jnp.concatenate(
        [flat[pl.ds(base * CHUNKS + c, rows, stride=CHUNKS), :] for c in range(CHUNKS)], axis=1)


def _store_slab_chunks(flat, val, rows, base=0):
    for c in range(CHUNKS):
        flat[pl.ds(base * CHUNKS + c, rows, stride=CHUNKS), :] = val[:, c * LANES:(c + 1) * LANES]


def _rms(t, gain):
    return t * lax.rsqrt(jnp.mean(t * t, axis=-1, keepdims=True) + LN_EPS) * gain


def _layer_norm(z, g, b):
    mu = jnp.mean(z, axis=-1, keepdims=True)
    zc = z - mu
    var = jnp.mean(zc * zc, axis=-1, keepdims=True)
    return zc * lax.rsqrt(var + LN_EPS) * g + b


def _mixout_body(attn_ref, rec_ref, x_ref,
                 ag_ref, rgn_ref, wo_ref, g1_ref, b1_ref, rwh_ref, rwl_ref, rb_ref, tri_ref, ones_ref,
                 x1_ref, x1p_ref, idx_ref, gate_ref, rank_ref, cnt_ref, carry_ref, *, alpha):
    tm = TM_MIX

    @pl.when(pl.program_id(0) == 0)
    def _():
        carry_ref[...] = jnp.zeros_like(carry_ref)

    attn_n = _rms(attn_ref[...].astype(f32), ag_ref[...]).astype(bf16)
    rec_n = _rms(rec_ref[...], rgn_ref[...]).astype(bf16)
    y = (jnp.dot(attn_n, wo_ref[0:ATTN_W, :], preferred_element_type=f32)
         + jnp.dot(rec_n, wo_ref[ATTN_W:, :], preferred_element_type=f32))
    x1 = _layer_norm(alpha * x_ref[...] + y, g1_ref[...], b1_ref[...])
    x1_ref[...] = x1
    _store_slab_chunks(x1p_ref, _pack_rows(x1), tm)

    hi = x1.astype(bf16)
    lo = (x1 - hi.astype(f32)).astype(bf16)
    nt = (((1,), (1,)), ((), ()))
    logits = (lax.dot_general(rwh_ref[...], hi, nt, preferred_element_type=f32)
              + lax.dot_general(rwh_ref[...], lo, nt, preferred_element_type=f32)
              + lax.dot_general(rwl_ref[...], hi, nt, preferred_element_type=f32))
    scores = jax.nn.sigmoid(logits)
    biased = scores + jnp.concatenate([rb_ref[...]] * (tm // LANES), axis=1)

    rid = lax.broadcasted_iota(i32, (GROUP_SZ, tm), 0).astype(f32)
    grp = []
    for g in range(N_GROUPS):
        vg = biased[g * GROUP_SZ:(g + 1) * GROUP_SZ, :]
        m1 = jnp.max(vg, axis=0, keepdims=True)
        first = jnp.min(jnp.where(vg == m1, rid, float(GROUP_SZ)), axis=0, keepdims=True)
        m2 = jnp.max(jnp.where(rid == first, -jnp.inf, vg), axis=0, keepdims=True)
        grp.append(m1 + m2)
    eid = lax.broadcasted_iota(i32, (N_EXPERTS, tm), 0).astype(f32)
    keep = [jnp.zeros((1, tm), f32) for _ in range(N_GROUPS)]
    for _ in range(TOPK_GROUPS):
        gm = functools.reduce(jnp.maximum, grp)
        gi = jnp.full((1, tm), float(N_GROUPS), f32)
        for g in range(N_GROUPS - 1, -1, -1):
            gi = jnp.where(grp[g] == gm, float(g), gi)
        hits = [gi == float(g) for g in range(N_GROUPS)]
        grp = [jnp.where(hit, -jnp.inf, sc) for hit, sc in zip(hits, grp)]
        keep = [jnp.where(hit, 1.0, kp) for hit, kp in zip(hits, keep)]
    masked = jnp.concatenate(
        [jnp.where(jnp.broadcast_to(keep[g], (GROUP_SZ, tm)) > 0.5,
                   biased[g * GROUP_SZ:(g + 1) * GROUP_SZ, :], -jnp.inf)
         for g in range(N_GROUPS)], axis=0)

    onehot = jnp.zeros((N_EXPERTS, tm), f32)
    idxs, gts = [], []
    for _ in range(TOP_K):
        mx = jnp.max(masked, axis=0, keepdims=True)
        ix = jnp.min(jnp.where(masked == mx, eid, float(N_EXPERTS)), axis=0, keepdims=True)
        hit = eid == ix
        gts.append(jnp.sum(jnp.where(hit, scores, 0.0), axis=0, keepdims=True))
        idxs.append(ix)
        masked = jnp.where(hit, -jnp.inf, masked)
        onehot = onehot + jnp.where(hit, 1.0, 0.0)
    gsum = functools.reduce(lambda p, q: p + q, gts)
    for kk in range(TOP_K):
        idx_ref[kk:kk + 1, :] = idxs[kk].astype(i32)
        gate_ref[kk:kk + 1, :] = gts[kk] / gsum * ROUTED_SCALE

    oh = onehot.astype(bf16)
    before = carry_ref[...] + jnp.dot(oh, tri_ref[...], preferred_element_type=f32)
    for kk in range(TOP_K):
        rk = jnp.sum(jnp.where(eid == idxs[kk], before, 0.0), axis=0, keepdims=True)
        rank_ref[kk:kk + 1, :] = rk.astype(i32)
    total = carry_ref[...] + jnp.dot(oh, ones_ref[...], preferred_element_type=f32)
    carry_ref[...] = total
    cnt_ref[...] = total


def _mixout(attn, rec, x2, attn_gain, rec_gain, w_out_b, g1, b1, rw_hi, rw_lo, rbias, alpha):
    T = x2.shape[0]
    tm = TM_MIX
    tok = lambda i: (i, 0)
    fixed = lambda i: (0, 0)
    tri = jnp.asarray(np.triu(np.ones((tm, tm), np.float32), k=1), bf16)
    ones = jnp.ones((tm, tm), bf16)
    aw = pl.BlockSpec((tm, ATTN_W), tok)
    row = lambda n: pl.BlockSpec((1, n), fixed)
    kt = pl.BlockSpec((TOP_K, tm), lambda i: (0, i))
    return pl.pallas_call(
        functools.partial(_mixout_body, alpha=alpha),
        grid=(T // tm,),
        in_specs=[aw] * 2 + [pl.BlockSpec((tm, D_MODEL), tok), row(ATTN_W), row(REC_W),
                             pl.BlockSpec((D_MODEL, D_MODEL), fixed), row(D_MODEL), row(D_MODEL),
                             pl.BlockSpec((N_EXPERTS, D_MODEL), fixed),
                             pl.BlockSpec((N_EXPERTS, D_MODEL), fixed),
                             pl.BlockSpec((N_EXPERTS, LANES), fixed),
                             pl.BlockSpec((tm, tm), fixed), pl.BlockSpec((tm, tm), fixed)],
        out_specs=[pl.BlockSpec((tm, D_MODEL), tok),
                   pl.BlockSpec((tm * CHUNKS, LANES), tok),
                   kt, kt, kt, pl.BlockSpec((N_EXPERTS, tm), fixed)],
        out_shape=[jax.ShapeDtypeStruct((T, D_MODEL), f32),
                   jax.ShapeDtypeStruct((T * CHUNKS, LANES), u32),
                   jax.ShapeDtypeStruct((TOP_K, T), i32),
                   jax.ShapeDtypeStruct((TOP_K, T), f32),
                   jax.ShapeDtypeStruct((TOP_K, T), i32),
                   jax.ShapeDtypeStruct((N_EXPERTS, tm), f32)],
        scratch_shapes=[pltpu.VMEM((N_EXPERTS, tm), f32)],
        compiler_params=_cparams(("arbitrary",)),
        name="mixout",
    )(attn, rec, x2, attn_gain, rec_gain, w_out_b, g1, b1,
      rw_hi, rw_lo, rbias, tri, ones)


def _slots_body(idx_ref, rank_ref, start_ref, dest_ref):
    tm = TM_SLOT
    eid = lax.broadcasted_iota(i32, (N_EXPERTS, tm), 0)
    start = jnp.concatenate([start_ref[...]] * (tm // LANES), axis=1)
    for kk in range(TOP_K):
        hit = eid == idx_ref[kk:kk + 1, :]
        base = jnp.sum(jnp.where(hit, start, 0.0), axis=0, keepdims=True)
        dest_ref[kk:kk + 1, :] = base.astype(i32) + rank_ref[kk:kk + 1, :]


def _slots(idx, rank, pad_start):
    T = idx.shape[1]
    tm = TM_SLOT
    kt = pl.BlockSpec((TOP_K, tm), lambda i: (0, i))
    start = jnp.broadcast_to(pad_start.astype(f32).reshape(N_EXPERTS, 1), (N_EXPERTS, LANES))
    return pl.pallas_call(
        _slots_body,
        grid=(T // tm,),
        in_specs=[kt, kt, pl.BlockSpec((N_EXPERTS, LANES), lambda i: (0, 0))],
        out_specs=kt,
        out_shape=jax.ShapeDtypeStruct((TOP_K, T), i32),
        compiler_params=_cparams(("parallel",)),
        name="slots",
    )(idx, rank, start)


def _dispatch_body(dest_ref, x_ref, xs_ref, sem):
    tm = TM_MIX

    def copy(t, k):
        return pltpu.make_async_copy(x_ref.at[t], xs_ref.at[dest_ref[k, t]], sem)

    def issue(t, carry):
        for k in range(TOP_K):
            copy(t, k).start(priority=k % DMA_THREADS)
        return carry

    def drain(t, carry):
        for k in range(TOP_K):
            copy(t, k).wait()
        return carry

    lax.fori_loop(0, tm, issue, 0)
    lax.fori_loop(0, tm, drain, 0)


def _dispatch(dest, x1p, n_slots):
    T = x1p.shape[0]
    tm = TM_MIX
    return pl.pallas_call(
        _dispatch_body,
        grid=(T // tm,),
        in_specs=[pl.BlockSpec((TOP_K, tm), lambda i: (0, i), memory_space=pltpu.SMEM),
                  pl.BlockSpec((tm, CHUNKS, LANES), lambda i: (i, 0, 0))],
        out_specs=pl.BlockSpec(memory_space=pl.ANY),
        out_shape=jax.ShapeDtypeStruct((n_slots, CHUNKS, LANES), u32),
        scratch_shapes=[pltpu.SemaphoreType.DMA(())],
        compiler_params=_cparams(("arbitrary",)),
        name="dispatch",
    )(dest, x1p)


def _experts_body(be_ref, nx_ref, nv_ref, nb_ref, xs_ref, wg_hbm, wu_hbm, wd_hbm, ys_ref,
                  fga, fua, fda, fgb, fub, fdb, sga, sua, sda, sgb, sub, sdb, sems):
    p = pl.program_id(0)
    blocks = (2 * p, 2 * p + 1)
    hbm_w = (wg_hbm, wu_hbm, wd_hbm)
    f32_w = ((fga, fua, fda), (fgb, fub, fdb))
    bf_w = ((sga, sua, sda), (sgb, sub, sdb))
    live_step = blocks[0] < nb_ref[0]

    def fetch(half, e):
        return [pltpu.make_async_copy(src.at[e], dst, sems.at[half])
                for src, dst in zip(hbm_w, f32_w[half])]

    for half, blk in enumerate(blocks):
        @pl.when(jnp.logical_and(live_step, p == 0))
        def _(half=half, blk=blk):
            for c in fetch(half, be_ref[blk]):
                c.start()

        prev = jnp.maximum(blk - 2, 0)

        @pl.when(jnp.logical_and(live_step,
                                 jnp.logical_or(p == 0, be_ref[blk] != be_ref[prev])))
        def _(half=half, blk=blk):
            for c in fetch(half, be_ref[blk]):
                c.wait()
            for src, dst in zip(f32_w[half], bf_w[half]):
                dst[...] = src[...].astype(bf16)

            @pl.when(nx_ref[blk] >= 0)
            def _():
                for c in fetch(half, nx_ref[blk]):
                    c.start()

    @pl.when(live_step)
    def _():
        xs = []
        for half, blk in enumerate(blocks):
            rows = jnp.where(blk < nb_ref[0], nv_ref[blk], 0)
            live = lax.broadcasted_iota(i32, (BM, WORDS), 0) < rows
            words = _slab_chunks(xs_ref, BM, half * BM)
            xs.append(_unpack_rows(jnp.where(live, words, u32(0))))

        def up(half, w_ref):
            lo, hi = xs[half]
            return (jnp.dot(lo, w_ref[:WORDS, :], preferred_element_type=f32)
                    + jnp.dot(hi, w_ref[WORDS:, :], preferred_element_type=f32))

        gs = [up(half, bf_w[half][0]) for half in range(2)]
        us = [up(half, bf_w[half][1]) for half in range(2)]
        hs = [(g * jax.nn.sigmoid(g) * u).astype(bf16) for g, u in zip(gs, us)]
        ys = [jnp.dot(h, bf_w[half][2][...], preferred_element_type=f32)
              for half, h in enumerate(hs)]
        for half in range(2):
            _store_slab_chunks(ys_ref, _pack_rows(ys[half]), BM, half * BM)


def _experts(block_e, next_e, block_rows, nb_used, xs, wg, wu, wd):
    P = xs.shape[0] // CHUNKS
    nb = P // BM
    assert nb % 2 == 0
    last = lambda nbu: jnp.maximum((nbu[0] - 1) // 2, 0)
    rows = lambda p, be, nx, nv, nbu: (jnp.minimum(p, last(nbu)), 0)
    any_spec = pl.BlockSpec(memory_space=pl.ANY)
    up_shape, down_shape = (D_MODEL, EXPERT_H), (EXPERT_H, D_MODEL)
    per_stream = lambda dt: [pltpu.VMEM(up_shape, dt), pltpu.VMEM(up_shape, dt),
                             pltpu.VMEM(down_shape, dt)]
    gs = pltpu.PrefetchScalarGridSpec(
        num_scalar_prefetch=4,
        grid=(nb // 2,),
        in_specs=[pl.BlockSpec((2 * BM * CHUNKS, LANES), rows), any_spec, any_spec, any_spec],
        out_specs=pl.BlockSpec((2 * BM * CHUNKS, LANES), rows),
        scratch_shapes=per_stream(f32) * 2 + per_stream(bf16) * 2
        + [pltpu.SemaphoreType.DMA((2,))],
    )
    return pl.pallas_call(
        _experts_body,
        grid_spec=gs,
        out_shape=jax.ShapeDtypeStruct((P * CHUNKS, LANES), u32),
        compiler_params=_cparams(("arbitrary",)),
        name="experts",
    )(block_e, next_e, block_rows, nb_used, xs, wg, wu, wd)


def _combine_body(dest_ref, dnext_ref, gate_ref, x1_ref, ys_ref, ysflat_ref, sg_ref, su_ref, sd_ref,
                  g2_ref, b2_ref, out_ref, gb00, gb01, gb10, gb11, sems, *, alpha):
    tm = TM_CMB
    j = pl.program_id(0)
    n = pl.num_programs(0)
    gbufs = ((gb00, gb01), (gb10, gb11))

    def issue(d_ref, st, half):
        base = half * tm

        def step(t, carry):
            for k in range(TOP_K):
                src = ys_ref.at[d_ref[(base + t) * TOP_K + k]]
                dst = gbufs[st][half].at[pl.ds((k * tm + t) * CHUNKS, CHUNKS)]
                pltpu.make_async_copy(src, dst, sems.at[st, half]).start(priority=k % DMA_THREADS)
            return carry
        lax.fori_loop(0, tm, step, 0)

    def finish(st, half):
        rows = pl.ds(half * tm, tm)
        x1 = x1_ref[rows, :]
        xb = x1.astype(bf16)
        sg = jnp.dot(xb, sg_ref[...], preferred_element_type=f32)
        su = jnp.dot(xb, su_ref[...], preferred_element_type=f32)
        shared = jnp.dot((sg * jax.nn.sigmoid(sg) * su).astype(bf16), sd_ref[...],
                         preferred_element_type=f32)
        flat = gbufs[st][half]
        pltpu.make_async_copy(ysflat_ref.at[pl.ds(0, TOP_K * tm * CHUNKS)], flat,
                              sems.at[st, half]).wait()

        gates = gate_ref[rows, :]
        lo = [jnp.zeros((tm, LANES), f32) for _ in range(CHUNKS)]
        hi = [jnp.zeros((tm, LANES), f32) for _ in range(CHUNKS)]
        for k in range(TOP_K):
            gk = jnp.broadcast_to(gates[:, k:k + 1], (tm, LANES))
            for c in range(CHUNKS):
                w = flat[pl.ds(k * tm * CHUNKS + c, tm, stride=CHUNKS), :]
                lo[c] = lo[c] + gk * lax.bitcast_convert_type(w << 16, f32)
                hi[c] = hi[c] + gk * lax.bitcast_convert_type(w & u32(0xFFFF0000), f32)
        routed = jnp.concatenate(lo + hi, axis=1)
        out_ref[rows, :] = _layer_norm(alpha * x1 + (routed + shared), g2_ref[...], b2_ref[...])

    @pl.when(j == 0)
    def _():
        issue(dest_ref, 0, 0)
        issue(dest_ref, 0, 1)

    for st in range(2):
        @pl.when(lax.rem(j, 2) == st)
        def _(st=st):
            @pl.when(j + 1 < n)
            def _():
                issue(dnext_ref, 1 - st, 0)
                issue(dnext_ref, 1 - st, 1)

            finish(st, 0)
            finish(st, 1)


def _combine(dest, gate, x1, ys, sg, su, sd, g2, b2, alpha):
    T = x1.shape[0]
    tm = TM_CMB
    n = T // (2 * tm)
    tok = lambda j: (j, 0)
    fixed = lambda j: (0, 0)
    tbl = pl.BlockSpec((2 * tm * TOP_K,), lambda j: (j,), memory_space=pltpu.SMEM)
    tbl_next = pl.BlockSpec((2 * tm * TOP_K,), lambda j: (jnp.minimum(j + 1, n - 1),),
                            memory_space=pltpu.SMEM)
    any_spec = pl.BlockSpec(memory_space=pl.ANY)
    return pl.pallas_call(
        functools.partial(_combine_body, alpha=alpha),
        grid=(n,),
        in_specs=[tbl, tbl_next, pl.BlockSpec((2 * tm, TOP_K), tok),
                  pl.BlockSpec((2 * tm, D_MODEL), tok), any_spec, any_spec,
                  pl.BlockSpec(sg.shape, fixed), pl.BlockSpec(su.shape, fixed),
                  pl.BlockSpec(sd.shape, fixed), pl.BlockSpec((1, D_MODEL), fixed),
                  pl.BlockSpec((1, D_MODEL), fixed)],
        out_specs=pl.BlockSpec((2 * tm, D_MODEL), tok),
        out_shape=jax.ShapeDtypeStruct((T, D_MODEL), f32),
        scratch_shapes=[pltpu.VMEM((TOP_K * tm * CHUNKS, LANES), u32)] * 4
        + [pltpu.SemaphoreType.DMA((2, 2))],
        compiler_params=_cparams(("arbitrary",)),
        name="combine",
    )(dest, dest, gate, x1, ys, ys.reshape(ys.shape[0] * CHUNKS, LANES), sg, su, sd, g2, b2)


def _rope_inv_freq():
    half = ROT_DIM // 2
    inv = ROPE_THETA ** (-jnp.arange(half, dtype=f32) * 2.0 / ROT_DIM)
    j = np.arange(LANES) % HEAD_DIM
    table = jnp.where(j < ROT_DIM, inv[j % half], 0.0)
    return table.reshape(1, LANES).astype(f32)


def _gate_weights(ga_w, ga_b, gx_w, gx_b):
    ng = REC_W // LANES
    per = LANES // HEAD_DIM
    def bd(w):
        w = w.reshape(ng, per, HEAD_DIM, HEAD_DIM)
        z = jnp.zeros((ng, LANES, LANES), w.dtype)
        for p in range(per):
            z = z.at[:, p * HEAD_DIM:(p + 1) * HEAD_DIM, p * HEAD_DIM:(p + 1) * HEAD_DIM].set(w[:, p])
        return z
    wg = jnp.concatenate([bd(ga_w[0]), bd(gx_w[0]), bd(ga_w[1]), bd(gx_w[1])], axis=-1).astype(bf16)
    grp = lambda b: b.reshape(ng, 1, LANES)
    gb = jnp.concatenate([grp(ga_b[0]), grp(gx_b[0]), grp(ga_b[1]), grp(gx_b[1])], axis=-1)
    return wg, gb


def _layer(x, positions, w_in, attn_gain, conv_w, conv_b, ga_w, ga_b, gx_w, gx_b, lam, rec_gain,
           w_out, ln1_g, ln1_b, router_w, router_bias, e_wg, e_wu, e_wd, s_wg, s_wu, s_wd,
           ln2_g, ln2_b, alpha):
    B, S, _ = x.shape
    T = B * S
    x2 = x.reshape(T, D_MODEL)
    pos2 = positions.reshape(T, 1)

    q, k, v, rx, rg = _inproj(x2, pos2, w_in.astype(bf16), _rope_inv_freq())
    attn = _attention(q, k, v, B, S)
    wg, gb = _gate_weights(ga_w, ga_b, gx_w, gx_b)
    rec = _rec(rx, rg, conv_w, conv_b.reshape(1, REC_W), wg, gb, lam, B, S).reshape(T, REC_W)

    rw_t = router_w.T
    rw_hi = rw_t.astype(bf16)
    rw_lo = (rw_t - rw_hi.astype(f32)).astype(bf16)
    rbias = jnp.broadcast_to(router_bias.reshape(N_EXPERTS, 1), (N_EXPERTS, LANES))
    x1, x1p, idx, gate, rank, cnt = _mixout(
        attn, rec, x2,
        attn_gain.reshape(1, ATTN_W), rec_gain.reshape(1, REC_W), w_out.astype(bf16),
        ln1_g.reshape(1, D_MODEL), ln1_b.reshape(1, D_MODEL), rw_hi, rw_lo, rbias, alpha)

    counts = cnt[:, 0].astype(i32)
    padded = (counts + BM - 1) // BM * BM
    pad_end = jnp.cumsum(padded)
    pad_start = pad_end - padded
    slot = _slots(idx, rank, pad_start)
    nb = (T * TOP_K + N_EXPERTS * (BM - 1)) // BM + 1
    nb += nb % 2
    nb_used = pad_end[-1] // BM
    h = (nb_used + 1) // 2
    pos = jnp.arange(nb, dtype=i32)
    blk = jnp.where(pos % 2 == 0, pos // 2, h + pos // 2)
    live = jnp.logical_and(pos < 2 * h, blk < nb_used)
    first_row = blk * BM
    block_e = jnp.minimum(jnp.sum((pad_end[None, :] <= first_row[:, None]).astype(i32), axis=1),
                          N_EXPERTS - 1)
    block_e = jnp.where(live, block_e, N_EXPERTS - 1)
    block_rows = jnp.where(live, jnp.clip((pad_start + counts)[block_e] - first_row, 0, BM), 0)
    pairs = block_e.reshape(nb // 2, 2)
    step_id = jnp.arange(nb // 2, dtype=i32)[:, None]
    change = jnp.concatenate([pairs[1:] != pairs[:-1], jnp.zeros((1, 2), bool)], axis=0)
    change = jnp.logical_and(change, step_id + 1 < h)
    nxt_step = lax.cummin(jnp.where(change, step_id + 1, nb), axis=0, reverse=True)
    next_e = jnp.where(nxt_step < nb // 2,
                       jnp.take_along_axis(pairs, jnp.minimum(nxt_step, nb // 2 - 1), axis=0), -1)
    next_e = next_e.reshape(nb).astype(i32)
    sblk = slot // BM
    dest = jnp.where(sblk < h, 2 * sblk, 2 * (sblk - h) + 1) * BM + slot % BM
    nb_used = (2 * h).astype(i32).reshape(1)

    xs = _dispatch(dest, x1p.reshape(T, CHUNKS, LANES), nb * BM)
    ys = _experts(block_e.astype(i32), next_e, block_rows.astype(i32), nb_used,
                  xs.reshape(nb * BM * CHUNKS, LANES), e_wg, e_wu, e_wd)
    ys = ys.reshape(nb * BM, CHUNKS, LANES)
    out = _combine(dest.T.reshape(-1), gate.T, x1, ys, s_wg.astype(bf16), s_wu.astype(bf16), s_wd.astype(bf16),
                   ln2_g.reshape(1, D_MODEL), ln2_b.reshape(1, D_MODEL), alpha)
    return out.reshape(B, S, D_MODEL)


def kernel(x, positions, w_in, attn_gain, rec_conv_w, rec_conv_b, rec_gate_a_w, rec_gate_a_b,
           rec_gate_x_w, rec_gate_x_b, rec_lambda, rec_gain, w_out, ln1_g, ln1_b, router_w,
           router_bias, exp_w_gate, exp_w_up, exp_w_down, shared_w_gate, shared_w_up,
           shared_w_down, ln2_g, ln2_b):
    depth = w_in.shape[0]
    alpha = (2 * depth) ** 0.25
    for l in range(depth):
        x = _layer(x, positions, w_in[l], attn_gain[l], rec_conv_w[l], rec_conv_b[l],
                   rec_gate_a_w[l], rec_gate_a_b[l], rec_gate_x_w[l], rec_gate_x_b[l],
                   rec_lambda[l], rec_gain[l], w_out[l], ln1_g[l], ln1_b[l], router_w[l],
                   router_bias[l], exp_w_gate[l], exp_w_up[l], exp_w_down[l], shared_w_gate[l],
                   shared_w_up[l], shared_w_down[l], ln2_g[l], ln2_b[l], alpha)
    return x
```

```python
import functools
import math

import jax
import jax.numpy as jnp
import numpy as np
from jax import lax
from jax.experimental import pallas as pl
from jax.experimental.pallas import tpu as pltpu

f32 = jnp.float32
bf16 = jnp.bfloat16
i32 = jnp.int32

D_MODEL = 1024
ATTN_W = 512
REC_W = 512
HEAD_DIM = 64
ROT_DIM = 16
ROPE_THETA = 500000.0
BRANCHES = ((128, 1), (512, 4), (2048, 16))
HALF_BAND = 64
CONV_W = 4
RG_LRU_C = 8.0
N_EXPERTS = 256
TOP_K = 8
N_GROUPS = 8
GROUP_SZ = N_EXPERTS // N_GROUPS
TOPK_GROUPS = 4
EXPERT_H = 256
ROUTED_SCALE = 2.5
LN_EPS = 1e-5
NEG = -1e30

LANES = 128
SUBLANES = 8
WORDS = D_MODEL // 2
CHUNKS = WORDS // LANES
VMEM_LIMIT = 56 * 1024 * 1024
DMA_THREADS = 2

TM_IN = 512
TQ = 128
KW = TQ + 2 * HALF_BAND
SB = TQ * max(d for _, d in BRANCHES)
MERGE_ROWS = 256
TS = 256
TM_MIX = 512
TM_SLOT = 512
TM_CMB = 128
BM = 256


def _cparams(sem):
    return pltpu.CompilerParams(dimension_semantics=sem, vmem_limit_bytes=VMEM_LIMIT)


def _inproj_body(x_ref, pos_ref, w_ref, invf_ref, q_ref, k_ref, v_ref, rx_ref, rg_ref):
    xb = x_ref[...].astype(bf16)
    ang = pos_ref[...].astype(f32) * invf_ref[...]
    cos = jnp.cos(ang)
    sin = jnp.sin(ang)
    j = lax.broadcasted_iota(i32, (1, LANES), 1) % HEAD_DIM
    half = ROT_DIM // 2
    s_lo = jnp.where(j < half, -sin, 0.0)
    s_hi = jnp.where((j >= half) & (j < ROT_DIM), sin, 0.0)
    rep = ATTN_W // LANES
    cos_w = jnp.concatenate([cos] * rep, axis=1)
    s_lo_w = jnp.concatenate([s_lo] * rep, axis=1)
    s_hi_w = jnp.concatenate([s_hi] * rep, axis=1)

    def proj(c0, n):
        return jnp.dot(xb, w_ref[:, c0:c0 + n], preferred_element_type=f32)

    def rope(t):
        return (t * cos_w + pltpu.roll(t, ATTN_W - half, 1) * s_lo_w
                + pltpu.roll(t, half, 1) * s_hi_w)

    q_ref[...] = rope(proj(0, ATTN_W)) * (HEAD_DIM ** -0.5)
    k_ref[...] = rope(proj(ATTN_W, ATTN_W))
    v_ref[...] = proj(2 * ATTN_W, ATTN_W)
    rx_ref[...] = proj(3 * ATTN_W, REC_W)
    rg_ref[...] = proj(3 * ATTN_W + REC_W, REC_W)


def _inproj(x2, pos2, w_in_b, invf):
    T = x2.shape[0]
    tm = TM_IN
    in_w = w_in_b.shape[1]
    tok = lambda i: (i, 0)
    fixed = lambda i: (0, 0)
    return pl.pallas_call(
        _inproj_body,
        grid=(T // tm,),
        in_specs=[pl.BlockSpec((tm, D_MODEL), tok), pl.BlockSpec((tm, 1), tok),
                  pl.BlockSpec((D_MODEL, in_w), fixed), pl.BlockSpec((1, LANES), fixed)],
        out_specs=[pl.BlockSpec((tm, ATTN_W), tok)] * 3 + [pl.BlockSpec((tm, REC_W), tok)] * 2,
        out_shape=[jax.ShapeDtypeStruct((T, ATTN_W), f32)] * 3
        + [jax.ShapeDtypeStruct((T, REC_W), f32)] * 2,
        compiler_params=_cparams(("parallel",)),
        name="inproj",
    )(x2, pos2, w_in_b, invf)


def _attn_body(q_ref, k_ref, v_ref, out_ref, o1, o2, o3, l1, l2, l3, bias_ref, *, S):
    o_sc, l_sc = (o1, o2, o3), (l1, l2, l3)
    lane = lax.broadcasted_iota(i32, (1, LANES), 1)
    head0 = lane < HEAD_DIM
    rel = (lax.broadcasted_iota(i32, (TQ, KW), 0) - lax.broadcasted_iota(i32, (TQ, KW), 1))
    tiles = SB // TQ
    for case in range(3):
        bias_ref[case] = jnp.where(jnp.abs(rel + case * HALF_BAND) <= HALF_BAND, 0.0, NEG)

    def rows(ref, start, n, d):
        return ref[pl.ds(start, n), :] if d == 1 else ref[pl.ds(start, n, stride=d), :]

    def tile(ti, n0):
        ctx = []
        for g, (_, d) in enumerate(BRANCHES):
            L = S // d
            sh = d.bit_length() - 1
            r = jnp.bitwise_and(ti, d - 1)
            m0 = jnp.right_shift(n0, sh) + jnp.right_shift(ti, sh) * TQ
            ks = jnp.clip(m0 - HALF_BAND, 0, L - KW)
            q = rows(q_ref, r + d * m0, TQ, d).astype(bf16)
            k = rows(k_ref, r + d * ks, KW, d).astype(bf16)
            v = rows(v_ref, r + d * ks, KW, d).astype(bf16)
            bias = bias_ref[(m0 - ks) // HALF_BAND]
            local = r + d * (m0 - jnp.right_shift(n0, sh))
            ss = []
            for sel in (head0, jnp.logical_not(head0)):
                qh = jnp.where(sel, q, jnp.zeros_like(q))
                ss.append(lax.dot_general(qh, k, (((1,), (1,)), ((), ())),
                                          preferred_element_type=f32))
            ctx.append((d, v, bias, local, ss))
        soft = []
        for d, v, bias, local, ss in ctx:
            ps = []
            for s in ss:
                s = s + bias
                m = jnp.max(s, axis=-1, keepdims=True)
                p = jnp.exp(s - m)
                den = jnp.sum(p, axis=-1, keepdims=True)
                ps.append((p.astype(bf16), den, m + jnp.log(den)))
            soft.append(ps)
        for g, ((d, v, bias, local, ss), ps) in enumerate(zip(ctx, soft)):
            outs = [(jnp.dot(p, v, preferred_element_type=f32) / den, lse) for p, den, lse in ps]
            o_val = jnp.where(head0, outs[0][0], outs[1][0])
            l_val = jnp.where(head0, outs[0][1], outs[1][1])
            if d == 1:
                o_sc[g][pl.ds(local, TQ), :] = o_val
                l_sc[g][pl.ds(local, TQ), :] = l_val
            else:
                o_sc[g][pl.ds(local, TQ, stride=d), :] = o_val
                l_sc[g][pl.ds(local, TQ, stride=d), :] = l_val
        return n0

    def merge(c, n0):
        sl = pl.ds(pl.multiple_of(c * MERGE_ROWS, MERGE_ROWS), MERGE_ROWS)
        ls = [l_sc[g][sl, :] for g in range(len(BRANCHES))]
        mx = functools.reduce(jnp.maximum, ls)
        es = [jnp.exp(l - mx) for l in ls]
        num = functools.reduce(lambda a, b: a + b, [e * o_sc[g][sl, :] for g, e in enumerate(es)])
        den = functools.reduce(lambda a, b: a + b, es)
        dst = pl.ds(pl.multiple_of(n0 + c * MERGE_ROWS, MERGE_ROWS), MERGE_ROWS)
        out_ref[dst, :] = (num / den).astype(bf16)
        return n0

    def superblock(sb, carry):
        n0 = pl.multiple_of(sb * SB, SB)
        lax.fori_loop(0, tiles, tile, n0)
        lax.fori_loop(0, SB // MERGE_ROWS, merge, n0)
        return carry

    lax.fori_loop(0, S // SB, superblock, 0)


def _attention(q, k, v, B, S):
    assert S % SB == 0 and all(S // d >= KW for _, d in BRANCHES)
    hp = ATTN_W // LANES
    view = lambda t: t.reshape(B, S, ATTN_W)
    spec = pl.BlockSpec((None, S, LANES), lambda b, h: (b, 0, h))
    out = pl.pallas_call(
        functools.partial(_attn_body, S=S),
        grid=(B, hp),
        in_specs=[spec, spec, spec],
        out_specs=spec,
        out_shape=jax.ShapeDtypeStruct((B, S, ATTN_W), bf16),
        scratch_shapes=[pltpu.VMEM((SB, LANES), f32)] * (2 * len(BRANCHES))
        + [pltpu.VMEM((3, TQ, KW), f32)],
        compiler_params=_cparams(("parallel", "parallel")),
        name="attention",
    )(view(q), view(k), view(v))
    return out.reshape(B * S, ATTN_W)


def _rec_body(rx_ref, rg_ref, cw_ref, cb_ref, wg_ref, gb_ref, lam_ref, out_ref, rxp_ref, hf_ref, *, S):
    nch = S // TS
    ntile = TS // SUBLANES
    zeros8 = jnp.zeros((SUBLANES, LANES), f32)
    rxp_ref[pl.ds(0, SUBLANES), :] = zeros8
    rxp_ref[pl.ds(S + SUBLANES, SUBLANES), :] = zeros8

    def pad_copy(c, carry):
        t0 = pl.multiple_of(c * TS, TS)
        rxp_ref[pl.ds(t0 + SUBLANES, TS), :] = rx_ref[pl.ds(t0, TS), :]
        return carry

    lax.fori_loop(0, nch, pad_copy, 0)

    lam = lam_ref[...]
    neg_sp = -RG_LRU_C * (jnp.maximum(-lam, 0.0) + jnp.log1p(jnp.exp(-jnp.abs(lam))))
    cw = cw_ref[...]
    cb = cb_ref[...]
    sub = lax.broadcasted_iota(i32, (ntile, SUBLANES, LANES), 1)
    nrow = TS + 2 * SUBLANES

    def gates(ci, d):
        t0 = pl.multiple_of(ci * TS, TS)
        xw = rxp_ref[pl.ds(t0, nrow), :]
        u = (cw[0:1] * pltpu.roll(xw, 2, 0) + cw[1:2] * pltpu.roll(xw, 1, 0) + cw[2:3] * xw
             + cw[3:4] * pltpu.roll(xw, nrow - 1, 0))[SUBLANES:SUBLANES + TS] + cb
        c0 = d * 2 * LANES
        g = jnp.dot(u.astype(bf16), wg_ref[:, c0:c0 + 2 * LANES], preferred_element_type=f32)
        g = g + gb_ref[:, c0:c0 + 2 * LANES]
        r = jax.nn.sigmoid(g[:, :LANES])
        gi = jax.nn.sigmoid(g[:, LANES:])
        a = jnp.exp(neg_sp[d:d + 1] * r)
        b = jnp.sqrt(1.0 - a * a) * gi * u
        return a.reshape(ntile, SUBLANES, LANES), b.reshape(ntile, SUBLANES, LANES)

    def chunk(i, carry):
        cf, cbk = carry
        a, b = gates(i, 0)
        for s in (1, 2, 4):
            ok = sub >= s
            a_s = pltpu.roll(a, s, 1)
            b_s = pltpu.roll(b, s, 1)
            b = jnp.where(ok, a * b_s + b, b)
            a = jnp.where(ok, a * a_s, a)
        t0 = pl.multiple_of(i * TS, TS)
        for j in range(ntile):
            h = a[j] * cf + b[j]
            hf_ref[pl.ds(t0 + j * SUBLANES, SUBLANES), :] = h
            cf = h[SUBLANES - 1:SUBLANES, :]
        ib = nch - 1 - i
        a, b = gates(ib, 1)
        for s in (1, 2, 4):
            ok = sub < SUBLANES - s
            a_s = pltpu.roll(a, SUBLANES - s, 1)
            b_s = pltpu.roll(b, SUBLANES - s, 1)
            b = jnp.where(ok, a * b_s + b, b)
            a = jnp.where(ok, a * a_s, a)
        t0 = pl.multiple_of(ib * TS, TS)
        for j in range(ntile - 1, -1, -1):
            h = a[j] * cbk + b[j]
            out_ref[pl.ds(t0 + j * SUBLANES, SUBLANES), :] = h
            cbk = h[0:1, :]
        return cf, cbk

    zrow = jnp.zeros((1, LANES), f32)
    lax.fori_loop(0, nch, chunk, (zrow, zrow))

    def finish(c, carry):
        t0 = pl.multiple_of(c * TS, TS)
        sl = pl.ds(t0, TS)
        out_ref[sl, :] = (hf_ref[sl, :] + out_ref[sl, :]) * jax.nn.gelu(rg_ref[sl, :])
        return carry

    lax.fori_loop(0, nch, finish, 0)


def _rec(rx, rg, conv_w, conv_b, wg, gb, lam, B, S):
    ng = REC_W // LANES
    assert S % TS == 0
    seq = pl.BlockSpec((None, S, LANES), lambda b, c: (b, 0, c))
    return pl.pallas_call(
        functools.partial(_rec_body, S=S),
        grid=(B, ng),
        in_specs=[seq, seq,
                  pl.BlockSpec((CONV_W, LANES), lambda b, c: (0, c)),
                  pl.BlockSpec((1, LANES), lambda b, c: (0, c)),
                  pl.BlockSpec((None, LANES, 4 * LANES), lambda b, c: (c, 0, 0)),
                  pl.BlockSpec((None, 1, 4 * LANES), lambda b, c: (c, 0, 0)),
                  pl.BlockSpec((2, LANES), lambda b, c: (0, c))],
        out_specs=seq,
        out_shape=jax.ShapeDtypeStruct((B, S, REC_W), f32),
        scratch_shapes=[pltpu.VMEM((S + 2 * SUBLANES, LANES), f32), pltpu.VMEM((S, LANES), f32)],
        compiler_params=_cparams(("parallel", "parallel")),
        name="rec",
    )(rx.reshape(B, S, REC_W), rg.reshape(B, S, REC_W), conv_w, conv_b, wg, gb, lam)


u32 = jnp.uint32


def _pack_rows(x):
    r = lax.bitcast_convert_type(x.astype(bf16).astype(f32), u32)
    return (r[:, WORDS:] & u32(0xFFFF0000)) | (r[:, :WORDS] >> 16)


def _unpack_rows(w):
    lo = lax.bitcast_convert_type(w << 16, f32).astype(bf16)
    hi = lax.bitcast_convert_type(w & u32(0xFFFF0000), f32).astype(bf16)
    return lo, hi


def _slab_chunks(flat, rows, base=0):
    return jnp.concatenate(
        [flat[pl.ds(base * CHUNKS + c, rows, stride=CHUNKS), :] for c in range(CHUNKS)], axis=1)


def _store_slab_chunks(flat, val, rows, base=0):
    for c in range(CHUNKS):
        flat[pl.ds(base * CHUNKS + c, rows, stride=CHUNKS), :] = val[:, c * LANES:(c + 1) * LANES]


def _rms(t, gain):
    return t * lax.rsqrt(jnp.mean(t * t, axis=-1, keepdims=True) + LN_EPS) * gain


def _layer_norm(z, g, b):
    mu = jnp.mean(z, axis=-1, keepdims=True)
    zc = z - mu
    var = jnp.mean(zc * zc, axis=-1, keepdims=True)
    return zc * lax.rsqrt(var + LN_EPS) * g + b


def _mixout_body(attn_ref, rec_ref, x_ref,
                 ag_ref, rgn_ref, wo_ref, g1_ref, b1_ref, rwh_ref, rwl_ref, rb_ref, tri_ref, ones_ref,
                 x1_ref, x1p_ref, idx_ref, gate_ref, rank_ref, cnt_ref, carry_ref, *, alpha):
    tm = TM_MIX

    @pl.when(pl.program_id(0) == 0)
    def _():
        carry_ref[...] = jnp.zeros_like(carry_ref)

    attn_n = _rms(attn_ref[...].astype(f32), ag_ref[...]).astype(bf16)
    rec_n = _rms(rec_ref[...], rgn_ref[...]).astype(bf16)
    y = (jnp.dot(attn_n, wo_ref[0:ATTN_W, :], preferred_element_type=f32)
         + jnp.dot(rec_n, wo_ref[ATTN_W:, :], preferred_element_type=f32))
    x1 = _layer_norm(alpha * x_ref[...] + y, g1_ref[...], b1_ref[...])
    x1_ref[...] = x1
    _store_slab_chunks(x1p_ref, _pack_rows(x1), tm)

    hi = x1.astype(bf16)
    lo = (x1 - hi.astype(f32)).astype(bf16)
    nt = (((1,), (1,)), ((), ()))
    logits = (lax.dot_general(rwh_ref[...], hi, nt, preferred_element_type=f32)
              + lax.dot_general(rwh_ref[...], lo, nt, preferred_element_type=f32)
              + lax.dot_general(rwl_ref[...], hi, nt, preferred_element_type=f32))
    scores = jax.nn.sigmoid(logits)
    biased = scores + jnp.concatenate([rb_ref[...]] * (tm // LANES), axis=1)

    rid = lax.broadcasted_iota(i32, (GROUP_SZ, tm), 0).astype(f32)
    grp = []
    for g in range(N_GROUPS):
        vg = biased[g * GROUP_SZ:(g + 1) * GROUP_SZ, :]
        m1 = jnp.max(vg, axis=0, keepdims=True)
        first = jnp.min(jnp.where(vg == m1, rid, float(GROUP_SZ)), axis=0, keepdims=True)
        m2 = jnp.max(jnp.where(rid == first, -jnp.inf, vg), axis=0, keepdims=True)
        grp.append(m1 + m2)
    eid = lax.broadcasted_iota(i32, (N_EXPERTS, tm), 0).astype(f32)
    keep = [jnp.zeros((1, tm), f32) for _ in range(N_GROUPS)]
    for _ in range(TOPK_GROUPS):
        gm = functools.reduce(jnp.maximum, grp)
        gi = jnp.full((1, tm), float(N_GROUPS), f32)
        for g in range(N_GROUPS - 1, -1, -1):
            gi = jnp.where(grp[g] == gm, float(g), gi)
        hits = [gi == float(g) for g in range(N_GROUPS)]
        grp = [jnp.where(hit, -jnp.inf, sc) for hit, sc in zip(hits, grp)]
        keep = [jnp.where(hit, 1.0, kp) for hit, kp in zip(hits, keep)]
    masked = jnp.concatenate(
        [jnp.where(jnp.broadcast_to(keep[g], (GROUP_SZ, tm)) > 0.5,
                   biased[g * GROUP_SZ:(g + 1) * GROUP_SZ, :], -jnp.inf)
         for g in range(N_GROUPS)], axis=0)

    onehot = jnp.zeros((N_EXPERTS, tm), f32)
    idxs, gts = [], []
    for _ in range(TOP_K):
        mx = jnp.max(masked, axis=0, keepdims=True)
        ix = jnp.min(jnp.where(masked == mx, eid, float(N_EXPERTS)), axis=0, keepdims=True)
        hit = eid == ix
        gts.append(jnp.sum(jnp.where(hit, scores, 0.0), axis=0, keepdims=True))
        idxs.append(ix)
        masked = jnp.where(hit, -jnp.inf, masked)
        onehot = onehot + jnp.where(hit, 1.0, 0.0)
    gsum = functools.reduce(lambda p, q: p + q, gts)
    for kk in range(TOP_K):
        idx_ref[kk:kk + 1, :] = idxs[kk].astype(i32)
        gate_ref[kk:kk + 1, :] = gts[kk] / gsum * ROUTED_SCALE

    oh = onehot.astype(bf16)
    before = carry_ref[...] + jnp.dot(oh, tri_ref[...], preferred_element_type=f32)
    for kk in range(TOP_K):
        rk = jnp.sum(jnp.where(eid == idxs[kk], before, 0.0), axis=0, keepdims=True)
        rank_ref[kk:kk + 1, :] = rk.astype(i32)
    total = carry_ref[...] + jnp.dot(oh, ones_ref[...], preferred_element_type=f32)
    carry_ref[...] = total
    cnt_ref[...] = total


def _mixout(attn, rec, x2, attn_gain, rec_gain, w_out_b, g1, b1, rw_hi, rw_lo, rbias, alpha):
    T = x2.shape[0]
    tm = TM_MIX
    tok = lambda i: (i, 0)
    fixed = lambda i: (0, 0)
    tri = jnp.asarray(np.triu(np.ones((tm, tm), np.float32), k=1), bf16)
    ones = jnp.ones((tm, tm), bf16)
    aw = pl.BlockSpec((tm, ATTN_W), tok)
    row = lambda n: pl.BlockSpec((1, n), fixed)
    kt = pl.BlockSpec((TOP_K, tm), lambda i: (0, i))
    return pl.pallas_call(
        functools.partial(_mixout_body, alpha=alpha),
        grid=(T // tm,),
        in_specs=[aw] * 2 + [pl.BlockSpec((tm, D_MODEL), tok), row(ATTN_W), row(REC_W),
                             pl.BlockSpec((D_MODEL, D_MODEL), fixed), row(D_MODEL), row(D_MODEL),
                             pl.BlockSpec((N_EXPERTS, D_MODEL), fixed),
                             pl.BlockSpec((N_EXPERTS, D_MODEL), fixed),
                             pl.BlockSpec((N_EXPERTS, LANES), fixed),
                             pl.BlockSpec((tm, tm), fixed), pl.BlockSpec((tm, tm), fixed)],
        out_specs=[pl.BlockSpec((tm, D_MODEL), tok),
                   pl.BlockSpec((tm * CHUNKS, LANES), tok),
                   kt, kt, kt, pl.BlockSpec((N_EXPERTS, tm), fixed)],
        out_shape=[jax.ShapeDtypeStruct((T, D_MODEL), f32),
                   jax.ShapeDtypeStruct((T * CHUNKS, LANES), u32),
                   jax.ShapeDtypeStruct((TOP_K, T), i32),
                   jax.ShapeDtypeStruct((TOP_K, T), f32),
                   jax.ShapeDtypeStruct((TOP_K, T), i32),
                   jax.ShapeDtypeStruct((N_EXPERTS, tm), f32)],
        scratch_shapes=[pltpu.VMEM((N_EXPERTS, tm), f32)],
        compiler_params=_cparams(("arbitrary",)),
        name="mixout",
    )(attn, rec, x2, attn_gain, rec_gain, w_out_b, g1, b1,
      rw_hi, rw_lo, rbias, tri, ones)


def _slots_body(idx_ref, rank_ref, start_ref, dest_ref):
    tm = TM_SLOT
    eid = lax.broadcasted_iota(i32, (N_EXPERTS, tm), 0)
    start = jnp.concatenate([start_ref[...]] * (tm // LANES), axis=1)
    for kk in range(TOP_K):
        hit = eid == idx_ref[kk:kk + 1, :]
        base = jnp.sum(jnp.where(hit, start, 0.0), axis=0, keepdims=True)
        dest_ref[kk:kk + 1, :] = base.astype(i32) + rank_ref[kk:kk + 1, :]


def _slots(idx, rank, pad_start):
    T = idx.shape[1]
    tm = TM_SLOT
    kt = pl.BlockSpec((TOP_K, tm), lambda i: (0, i))
    start = jnp.broadcast_to(pad_start.astype(f32).reshape(N_EXPERTS, 1), (N_EXPERTS, LANES))
    return pl.pallas_call(
        _slots_body,
        grid=(T // tm,),
        in_specs=[kt, kt, pl.BlockSpec((N_EXPERTS, LANES), lambda i: (0, 0))],
        out_specs=kt,
        out_shape=jax.ShapeDtypeStruct((TOP_K, T), i32),
        compiler_params=_cparams(("parallel",)),
        name="slots",
    )(idx, rank, start)


def _dispatch_body(dest_ref, x_ref, x1_ref, sg_ref, su_ref, sd_ref, xs_ref, pre_ref, sem, *, alpha):
    tm = TM_MIX

    def copy(t, k):
        return pltpu.make_async_copy(x_ref.at[t], xs_ref.at[dest_ref[k, t]], sem)

    def issue(t, carry):
        for k in range(TOP_K):
            copy(t, k).start(priority=k % DMA_THREADS)
        return carry

    def drain(t, carry):
        for k in range(TOP_K):
            copy(t, k).wait()
        return carry

    lax.fori_loop(0, tm, issue, 0)
    x1 = x1_ref[...]
    xb = x1.astype(bf16)
    sg = jnp.dot(xb, sg_ref[...], preferred_element_type=f32)
    su = jnp.dot(xb, su_ref[...], preferred_element_type=f32)
    shared = jnp.dot((sg * jax.nn.sigmoid(sg) * su).astype(bf16), sd_ref[...],
                     preferred_element_type=f32)
    pre_ref[...] = alpha * x1 + shared
    lax.fori_loop(0, tm, drain, 0)


def _dispatch(dest, x1p, x1, sg, su, sd, n_slots, alpha):
    T = x1p.shape[0]
    tm = TM_MIX
    tok = lambda i: (i, 0)
    fixed = lambda i: (0, 0)
    return pl.pallas_call(
        functools.partial(_dispatch_body, alpha=alpha),
        grid=(T // tm,),
        in_specs=[pl.BlockSpec((TOP_K, tm), lambda i: (0, i), memory_space=pltpu.SMEM),
                  pl.BlockSpec((tm, CHUNKS, LANES), lambda i: (i, 0, 0)),
                  pl.BlockSpec((tm, D_MODEL), tok),
                  pl.BlockSpec(sg.shape, fixed), pl.BlockSpec(su.shape, fixed),
                  pl.BlockSpec(sd.shape, fixed)],
        out_specs=[pl.BlockSpec(memory_space=pl.ANY), pl.BlockSpec((tm, D_MODEL), tok)],
        out_shape=[jax.ShapeDtypeStruct((n_slots, CHUNKS, LANES), u32),
                   jax.ShapeDtypeStruct((T, D_MODEL), f32)],
        scratch_shapes=[pltpu.SemaphoreType.DMA(())],
        compiler_params=_cparams(("arbitrary",)),
        name="dispatch",
    )(dest, x1p, x1, sg, su, sd)


def _experts_body(be_ref, nx_ref, nv_ref, nb_ref, xs_ref, wg_hbm, wu_hbm, wd_hbm, ys_ref,
                  fga, fua, fda, fgb, fub, fdb, sga, sua, sda, sgb, sub, sdb, sems):
    p = pl.program_id(0)
    blocks = (2 * p, 2 * p + 1)
    hbm_w = (wg_hbm, wu_hbm, wd_hbm)
    f32_w = ((fga, fua, fda), (fgb, fub, fdb))
    bf_w = ((sga, sua, sda), (sgb, sub, sdb))
    live_step = blocks[0] < nb_ref[0]

    def fetch(half, e):
        return [pltpu.make_async_copy(src.at[e], dst, sems.at[half])
                for src, dst in zip(hbm_w, f32_w[half])]

    for half, blk in enumerate(blocks):
        @pl.when(jnp.logical_and(live_step, p == 0))
        def _(half=half, blk=blk):
            for c in fetch(half, be_ref[blk]):
                c.start()

        prev = jnp.maximum(blk - 2, 0)

        @pl.when(jnp.logical_and(live_step,
                                 jnp.logical_or(p == 0, be_ref[blk] != be_ref[prev])))
        def _(half=half, blk=blk):
            for c in fetch(half, be_ref[blk]):
                c.wait()
            for src, dst in zip(f32_w[half], bf_w[half]):
                dst[...] = src[...].astype(bf16)

            @pl.when(nx_ref[blk] >= 0)
            def _():
                for c in fetch(half, nx_ref[blk]):
                    c.start()

    @pl.when(live_step)
    def _():
        xs = []
        for half, blk in enumerate(blocks):
            rows = jnp.where(blk < nb_ref[0], nv_ref[blk], 0)
            live = lax.broadcasted_iota(i32, (BM, WORDS), 0) < rows
            words = _slab_chunks(xs_ref, BM, half * BM)
            xs.append(_unpack_rows(jnp.where(live, words, u32(0))))

        def up(half, w_ref):
            lo, hi = xs[half]
            return (jnp.dot(lo, w_ref[:WORDS, :], preferred_element_type=f32)
                    + jnp.dot(hi, w_ref[WORDS:, :], preferred_element_type=f32))

        gs = [up(half, bf_w[half][0]) for half in range(2)]
        us = [up(half, bf_w[half][1]) for half in range(2)]
        hs = [(g * jax.nn.sigmoid(g) * u).astype(bf16) for g, u in zip(gs, us)]
        ys = [jnp.dot(h, bf_w[half][2][...], preferred_element_type=f32)
              for half, h in enumerate(hs)]
        for half in range(2):
            _store_slab_chunks(ys_ref, _pack_rows(ys[half]), BM, half * BM)


def _experts(block_e, next_e, block_rows, nb_used, xs, wg, wu, wd):
    P = xs.shape[0] // CHUNKS
    nb = P // BM
    assert nb % 2 == 0
    last = lambda nbu: jnp.maximum((nbu[0] - 1) // 2, 0)
    rows = lambda p, be, nx, nv, nbu: (jnp.minimum(p, last(nbu)), 0)
    any_spec = pl.BlockSpec(memory_space=pl.ANY)
    up_shape, down_shape = (D_MODEL, EXPERT_H), (EXPERT_H, D_MODEL)
    per_stream = lambda dt: [pltpu.VMEM(up_shape, dt), pltpu.VMEM(up_shape, dt),
                             pltpu.VMEM(down_shape, dt)]
    gs = pltpu.PrefetchScalarGridSpec(
        num_scalar_prefetch=4,
        grid=(nb // 2,),
        in_specs=[pl.BlockSpec((2 * BM * CHUNKS, LANES), rows), any_spec, any_spec, any_spec],
        out_specs=pl.BlockSpec((2 * BM * CHUNKS, LANES), rows),
        scratch_shapes=per_stream(f32) * 2 + per_stream(bf16) * 2
        + [pltpu.SemaphoreType.DMA((2,))],
    )
    return pl.pallas_call(
        _experts_body,
        grid_spec=gs,
        out_shape=jax.ShapeDtypeStruct((P * CHUNKS, LANES), u32),
        compiler_params=_cparams(("arbitrary",)),
        name="experts",
    )(block_e, next_e, block_rows, nb_used, xs, wg, wu, wd)


def _combine_body(dest_ref, dnext_ref, gate_ref, pre_ref, ys_ref, ysflat_ref,
                  g2_ref, b2_ref, out_ref, gb00, gb01, gb10, gb11, sems):
    tm = TM_CMB
    j = pl.program_id(0)
    n = pl.num_programs(0)
    gbufs = ((gb00, gb01), (gb10, gb11))

    def issue(d_ref, st, half):
        base = half * tm

        def step(t, carry):
            for k in range(TOP_K):
                src = ys_ref.at[d_ref[(base + t) * TOP_K + k]]
                dst = gbufs[st][half].at[pl.ds((k * tm + t) * CHUNKS, CHUNKS)]
                pltpu.make_async_copy(src, dst, sems.at[st, half]).start(priority=k % DMA_THREADS)
            return carry
        lax.fori_loop(0, tm, step, 0)

    def finish(st, half):
        rows = pl.ds(half * tm, tm)
        flat = gbufs[st][half]
        pltpu.make_async_copy(ysflat_ref.at[pl.ds(0, TOP_K * tm * CHUNKS)], flat,
                              sems.at[st, half]).wait()

        gates = gate_ref[rows, :]
        lo = [jnp.zeros((tm, LANES), f32) for _ in range(CHUNKS)]
        hi = [jnp.zeros((tm, LANES), f32) for _ in range(CHUNKS)]
        for k in range(TOP_K):
            gk = jnp.broadcast_to(gates[:, k:k + 1], (tm, LANES))
            for c in range(CHUNKS):
                w = flat[pl.ds(k * tm * CHUNKS + c, tm, stride=CHUNKS), :]
                lo[c] = lo[c] + gk * lax.bitcast_convert_type(w << 16, f32)
                hi[c] = hi[c] + gk * lax.bitcast_convert_type(w & u32(0xFFFF0000), f32)
        routed = jnp.concatenate(lo + hi, axis=1)
        out_ref[rows, :] = _layer_norm(pre_ref[rows, :] + routed, g2_ref[...], b2_ref[...])

    @pl.when(j == 0)
    def _():
        issue(dest_ref, 0, 0)
        issue(dest_ref, 0, 1)

    for st in range(2):
        @pl.when(lax.rem(j, 2) == st)
        def _(st=st):
            @pl.when(j + 1 < n)
            def _():
                issue(dnext_ref, 1 - st, 0)
                issue(dnext_ref, 1 - st, 1)

            finish(st, 0)
            finish(st, 1)


def _combine(dest, gate, pre, ys, g2, b2):
    T = pre.shape[0]
    tm = TM_CMB
    n = T // (2 * tm)
    tok = lambda j: (j, 0)
    fixed = lambda j: (0, 0)
    tbl = pl.BlockSpec((2 * tm * TOP_K,), lambda j: (j,), memory_space=pltpu.SMEM)
    tbl_next = pl.BlockSpec((2 * tm * TOP_K,), lambda j: (jnp.minimum(j + 1, n - 1),),
                            memory_space=pltpu.SMEM)
    any_spec = pl.BlockSpec(memory_space=pl.ANY)
    return pl.pallas_call(
        _combine_body,
        grid=(n,),
        in_specs=[tbl, tbl_next, pl.BlockSpec((2 * tm, TOP_K), tok),
                  pl.BlockSpec((2 * tm, D_MODEL), tok), any_spec, any_spec,
                  pl.BlockSpec((1, D_MODEL), fixed), pl.BlockSpec((1, D_MODEL), fixed)],
        out_specs=pl.BlockSpec((2 * tm, D_MODEL), tok),
        out_shape=jax.ShapeDtypeStruct((T, D_MODEL), f32),
        scratch_shapes=[pltpu.VMEM((TOP_K * tm * CHUNKS, LANES), u32)] * 4
        + [pltpu.SemaphoreType.DMA((2, 2))],
        compiler_params=_cparams(("arbitrary",)),
        name="combine",
    )(dest, dest, gate, pre, ys, ys.reshape(ys.shape[0] * CHUNKS, LANES), g2, b2)


def _rope_inv_freq():
    half = ROT_DIM // 2
    inv = ROPE_THETA ** (-jnp.arange(half, dtype=f32) * 2.0 / ROT_DIM)
    j = np.arange(LANES) % HEAD_DIM
    table = jnp.where(j < ROT_DIM, inv[j % half], 0.0)
    return table.reshape(1, LANES).astype(f32)


def _gate_weights(ga_w, ga_b, gx_w, gx_b):
    ng = REC_W // LANES
    per = LANES // HEAD_DIM
    def bd(w):
        w = w.reshape(ng, per, HEAD_DIM, HEAD_DIM)
        z = jnp.zeros((ng, LANES, LANES), w.dtype)
        for p in range(per):
            z = z.at[:, p * HEAD_DIM:(p + 1) * HEAD_DIM, p * HEAD_DIM:(p + 1) * HEAD_DIM].set(w[:, p])
        return z
    wg = jnp.concatenate([bd(ga_w[0]), bd(gx_w[0]), bd(ga_w[1]), bd(gx_w[1])], axis=-1).astype(bf16)
    grp = lambda b: b.reshape(ng, 1, LANES)
    gb = jnp.concatenate([grp(ga_b[0]), grp(gx_b[0]), grp(ga_b[1]), grp(gx_b[1])], axis=-1)
    return wg, gb


def _layer(x, positions, w_in, attn_gain, conv_w, conv_b, ga_w, ga_b, gx_w, gx_b, lam, rec_gain,
           w_out, ln1_g, ln1_b, router_w, router_bias, e_wg, e_wu, e_wd, s_wg, s_wu, s_wd,
           ln2_g, ln2_b, alpha):
    B, S, _ = x.shape
    T = B * S
    x2 = x.reshape(T, D_MODEL)
    pos2 = positions.reshape(T, 1)

    q, k, v, rx, rg = _inproj(x2, pos2, w_in.astype(bf16), _rope_inv_freq())
    attn = _attention(q, k, v, B, S)
    wg, gb = _gate_weights(ga_w, ga_b, gx_w, gx_b)
    rec = _rec(rx, rg, conv_w, conv_b.reshape(1, REC_W), wg, gb, lam, B, S).reshape(T, REC_W)

    rw_t = router_w.T
    rw_hi = rw_t.astype(bf16)
    rw_lo = (rw_t - rw_hi.astype(f32)).astype(bf16)
    rbias = jnp.broadcast_to(router_bias.reshape(N_EXPERTS, 1), (N_EXPERTS, LANES))
    x1, x1p, idx, gate, rank, cnt = _mixout(
        attn, rec, x2,
        attn_gain.reshape(1, ATTN_W), rec_gain.reshape(1, REC_W), w_out.astype(bf16),
        ln1_g.reshape(1, D_MODEL), ln1_b.reshape(1, D_MODEL), rw_hi, rw_lo, rbias, alpha)

    counts = cnt[:, 0].astype(i32)
    padded = (counts + BM - 1) // BM * BM
    pad_end = jnp.cumsum(padded)
    pad_start = pad_end - padded
    slot = _slots(idx, rank, pad_start)
    nb = (T * TOP_K + N_EXPERTS * (BM - 1)) // BM + 1
    nb += nb % 2
    nb_used = pad_end[-1] // BM
    h = (nb_used + 1) // 2
    pos = jnp.arange(nb, dtype=i32)
    blk = jnp.where(pos % 2 == 0, pos // 2, h + pos // 2)
    live = jnp.logical_and(pos < 2 * h, blk < nb_used)
    first_row = blk * BM
    block_e = jnp.minimum(jnp.sum((pad_end[None, :] <= first_row[:, None]).astype(i32), axis=1),
                          N_EXPERTS - 1)
    block_e = jnp.where(live, block_e, N_EXPERTS - 1)
    block_rows = jnp.where(live, jnp.clip((pad_start + counts)[block_e] - first_row, 0, BM), 0)
    pairs = block_e.reshape(nb // 2, 2)
    step_id = jnp.arange(nb // 2, dtype=i32)[:, None]
    change = jnp.concatenate([pairs[1:] != pairs[:-1], jnp.zeros((1, 2), bool)], axis=0)
    change = jnp.logical_and(change, step_id + 1 < h)
    nxt_step = lax.cummin(jnp.where(change, step_id + 1, nb), axis=0, reverse=True)
    next_e = jnp.where(nxt_step < nb // 2,
                       jnp.take_along_axis(pairs, jnp.minimum(nxt_step, nb // 2 - 1), axis=0), -1)
    next_e = next_e.reshape(nb).astype(i32)
    sblk = slot // BM
    dest = jnp.where(sblk < h, 2 * sblk, 2 * (sblk - h) + 1) * BM + slot % BM
    nb_used = (2 * h).astype(i32).reshape(1)

    xs, pre = _dispatch(dest, x1p.reshape(T, CHUNKS, LANES), x1, s_wg.astype(bf16),
                        s_wu.astype(bf16), s_wd.astype(bf16), nb * BM, alpha)
    ys = _experts(block_e.astype(i32), next_e, block_rows.astype(i32), nb_used,
                  xs.reshape(nb * BM * CHUNKS, LANES), e_wg, e_wu, e_wd)
    ys = ys.reshape(nb * BM, CHUNKS, LANES)
    out = _combine(dest.T.reshape(-1), gate.T, pre, ys,
                   ln2_g.reshape(1, D_MODEL), ln2_b.reshape(1, D_MODEL))
    return out.reshape(B, S, D_MODEL)


def kernel(x, positions, w_in, attn_gain, rec_conv_w, rec_conv_b, rec_gate_a_w, rec_gate_a_b,
           rec_gate_x_w, rec_gate_x_b, rec_lambda, rec_gain, w_out, ln1_g, ln1_b, router_w,
           router_bias, exp_w_gate, exp_w_up, exp_w_down, shared_w_gate, shared_w_up,
           shared_w_down, ln2_g, ln2_b):
    depth = w_in.shape[0]
    alpha = (2 * depth) ** 0.25
    for l in range(depth):
        x = _layer(x, positions, w_in[l], attn_gain[l], rec_conv_w[l], rec_conv_b[l],
                   rec_gate_a_w[l], rec_gate_a_b[l], rec_gate_x_w[l], rec_gate_x_b[l],
                   rec_lambda[l], rec_gain[l], w_out[l], ln1_g[l], ln1_b[l], router_w[l],
                   router_bias[l], exp_w_gate[l], exp_w_up[l], exp_w_down[l], shared_w_gate[l],
                   shared_w_up[l], shared_w_down[l], ln2_g[l], ln2_b[l], alpha)
    return x
```

```python
import functools
import math

import jax
import jax.numpy as jnp
import numpy as np
from jax import lax
from jax.experimental import pallas as pl
from jax.experimental.pallas import tpu as pltpu

f32 = jnp.float32
bf16 = jnp.bfloat16
i32 = jnp.int32

D_MODEL = 1024
ATTN_W = 512
REC_W = 512
HEAD_DIM = 64
ROT_DIM = 16
ROPE_THETA = 500000.0
BRANCHES = ((128, 1), (512, 4), (2048, 16))
HALF_BAND = 64
CONV_W = 4
RG_LRU_C = 8.0
N_EXPERTS = 256
TOP_K = 8
N_GROUPS = 8
GROUP_SZ = N_EXPERTS // N_GROUPS
TOPK_GROUPS = 4
EXPERT_H = 256
ROUTED_SCALE = 2.5
LN_EPS = 1e-5
NEG = -1e30

LANES = 128
SUBLANES = 8
WORDS = D_MODEL // 2
CHUNKS = WORDS // LANES
VMEM_LIMIT = 56 * 1024 * 1024
DMA_THREADS = 2

TM_IN = 512
TQ = 128
KW = TQ + 2 * HALF_BAND
SB = TQ * max(d for _, d in BRANCHES)
MERGE_ROWS = 256
TS = 256
TM_MIX = 512
TM_SLOT = 512
TM_CMB = 128
BM = 256


def _cparams(sem):
    return pltpu.CompilerParams(dimension_semantics=sem, vmem_limit_bytes=VMEM_LIMIT)


def _inproj_body(x_ref, pos_ref, w_ref, invf_ref, q_ref, k_ref, v_ref, rx_ref, rg_ref):
    xb = x_ref[...].astype(bf16)
    ang = pos_ref[...].astype(f32) * invf_ref[...]
    cos = jnp.cos(ang)
    sin = jnp.sin(ang)
    j = lax.broadcasted_iota(i32, (1, LANES), 1) % HEAD_DIM
    half = ROT_DIM // 2
    s_lo = jnp.where(j < half, -sin, 0.0)
    s_hi = jnp.where((j >= half) & (j < ROT_DIM), sin, 0.0)
    rep = ATTN_W // LANES
    cos_w = jnp.concatenate([cos] * rep, axis=1)
    s_lo_w = jnp.concatenate([s_lo] * rep, axis=1)
    s_hi_w = jnp.concatenate([s_hi] * rep, axis=1)

    def proj(c0, n):
        return jnp.dot(xb, w_ref[:, c0:c0 + n], preferred_element_type=f32)

    def rope(t):
        return (t * cos_w + pltpu.roll(t, ATTN_W - half, 1) * s_lo_w
                + pltpu.roll(t, half, 1) * s_hi_w)

    q_ref[...] = rope(proj(0, ATTN_W)) * (HEAD_DIM ** -0.5)
    k_ref[...] = rope(proj(ATTN_W, ATTN_W))
    v_ref[...] = proj(2 * ATTN_W, ATTN_W)
    rx_ref[...] = proj(3 * ATTN_W, REC_W)
    rg_ref[...] = proj(3 * ATTN_W + REC_W, REC_W)


def _inproj(x2, pos2, w_in_b, invf):
    T = x2.shape[0]
    tm = TM_IN
    in_w = w_in_b.shape[1]
    tok = lambda i: (i, 0)
    fixed = lambda i: (0, 0)
    return pl.pallas_call(
        _inproj_body,
        grid=(T // tm,),
        in_specs=[pl.BlockSpec((tm, D_MODEL), tok), pl.BlockSpec((tm, 1), tok),
                  pl.BlockSpec((D_MODEL, in_w), fixed), pl.BlockSpec((1, LANES), fixed)],
        out_specs=[pl.BlockSpec((tm, ATTN_W), tok)] * 3 + [pl.BlockSpec((tm, REC_W), tok)] * 2,
        out_shape=[jax.ShapeDtypeStruct((T, ATTN_W), f32)] * 3
        + [jax.ShapeDtypeStruct((T, REC_W), f32)] * 2,
        compiler_params=_cparams(("parallel",)),
        name="inproj",
    )(x2, pos2, w_in_b, invf)


def _attn_body(q_ref, k_ref, v_ref, out_ref, o1, o2, o3, l1, l2, l3, bias_ref, *, S):
    o_sc, l_sc = (o1, o2, o3), (l1, l2, l3)
    lane = lax.broadcasted_iota(i32, (1, LANES), 1)
    head0 = lane < HEAD_DIM
    rel = (lax.broadcasted_iota(i32, (TQ, KW), 0) - lax.broadcasted_iota(i32, (TQ, KW), 1))
    tiles = SB // TQ
    for case in range(3):
        bias_ref[case] = jnp.where(jnp.abs(rel + case * HALF_BAND) <= HALF_BAND, 0.0, NEG)

    def rows(ref, start, n, d):
        return ref[pl.ds(start, n), :] if d == 1 else ref[pl.ds(start, n, stride=d), :]

    def tile(ti, n0):
        ctx = []
        for g, (_, d) in enumerate(BRANCHES):
            L = S // d
            sh = d.bit_length() - 1
            r = jnp.bitwise_and(ti, d - 1)
            m0 = jnp.right_shift(n0, sh) + jnp.right_shift(ti, sh) * TQ
            ks = jnp.clip(m0 - HALF_BAND, 0, L - KW)
            q = rows(q_ref, r + d * m0, TQ, d).astype(bf16)
            k = rows(k_ref, r + d * ks, KW, d).astype(bf16)
            v = rows(v_ref, r + d * ks, KW, d).astype(bf16)
            bias = bias_ref[(m0 - ks) // HALF_BAND]
            local = r + d * (m0 - jnp.right_shift(n0, sh))
            ss = []
            for sel in (head0, jnp.logical_not(head0)):
                qh = jnp.where(sel, q, jnp.zeros_like(q))
                ss.append(lax.dot_general(qh, k, (((1,), (1,)), ((), ())),
                                          preferred_element_type=f32))
            ctx.append((d, v, bias, local, ss))
        soft = []
        for d, v, bias, local, ss in ctx:
            ps = []
            for s in ss:
                s = s + bias
                m = jnp.max(s, axis=-1, keepdims=True)
                p = jnp.exp(s - m)
                den = jnp.sum(p, axis=-1, keepdims=True)
                ps.append((p.astype(bf16), den, m + jnp.log(den)))
            soft.append(ps)
        for g, ((d, v, bias, local, ss), ps) in enumerate(zip(ctx, soft)):
            outs = [(jnp.dot(p, v, preferred_element_type=f32) / den, lse) for p, den, lse in ps]
            o_val = jnp.where(head0, outs[0][0], outs[1][0])
            l_val = jnp.where(head0, outs[0][1], outs[1][1])
            if d == 1:
                o_sc[g][pl.ds(local, TQ), :] = o_val
                l_sc[g][pl.ds(local, TQ), :] = l_val
            else:
                o_sc[g][pl.ds(local, TQ, stride=d), :] = o_val
                l_sc[g][pl.ds(local, TQ, stride=d), :] = l_val
        return n0

    def merge(c, n0):
        sl = pl.ds(pl.multiple_of(c * MERGE_ROWS, MERGE_ROWS), MERGE_ROWS)
        ls = [l_sc[g][sl, :] for g in range(len(BRANCHES))]
        mx = functools.reduce(jnp.maximum, ls)
        es = [jnp.exp(l - mx) for l in ls]
        num = functools.reduce(lambda a, b: a + b, [e * o_sc[g][sl, :] for g, e in enumerate(es)])
        den = functools.reduce(lambda a, b: a + b, es)
        dst = pl.ds(pl.multiple_of(n0 + c * MERGE_ROWS, MERGE_ROWS), MERGE_ROWS)
        out_ref[dst, :] = (num / den).astype(bf16)
        return n0

    def superblock(sb, carry):
        n0 = pl.multiple_of(sb * SB, SB)
        lax.fori_loop(0, tiles, tile, n0)
        lax.fori_loop(0, SB // MERGE_ROWS, merge, n0)
        return carry

    lax.fori_loop(0, S // SB, superblock, 0)


def _attention(q, k, v, B, S):
    assert S % SB == 0 and all(S // d >= KW for _, d in BRANCHES)
    hp = ATTN_W // LANES
    view = lambda t: t.reshape(B, S, ATTN_W)
    spec = pl.BlockSpec((None, S, LANES), lambda b, h: (b, 0, h))
    out = pl.pallas_call(
        functools.partial(_attn_body, S=S),
        grid=(B, hp),
        in_specs=[spec, spec, spec],
        out_specs=spec,
        out_shape=jax.ShapeDtypeStruct((B, S, ATTN_W), bf16),
        scratch_shapes=[pltpu.VMEM((SB, LANES), f32)] * (2 * len(BRANCHES))
        + [pltpu.VMEM((3, TQ, KW), f32)],
        compiler_params=_cparams(("parallel", "parallel")),
        name="attention",
    )(view(q), view(k), view(v))
    return out.reshape(B * S, ATTN_W)


def _rec_body(rx_ref, rg_ref, cw_ref, cb_ref, wg_ref, gb_ref, lam_ref, out_ref, rxp_ref, hf_ref, *, S):
    nch = S // TS
    ntile = TS // SUBLANES
    zeros8 = jnp.zeros((SUBLANES, LANES), f32)
    rxp_ref[pl.ds(0, SUBLANES), :] = zeros8
    rxp_ref[pl.ds(S + SUBLANES, SUBLANES), :] = zeros8

    def pad_copy(c, carry):
        t0 = pl.multiple_of(c * TS, TS)
        rxp_ref[pl.ds(t0 + SUBLANES, TS), :] = rx_ref[pl.ds(t0, TS), :]
        return carry

    lax.fori_loop(0, nch, pad_copy, 0)

    lam = lam_ref[...]
    neg_sp = -RG_LRU_C * (jnp.maximum(-lam, 0.0) + jnp.log1p(jnp.exp(-jnp.abs(lam))))
    cw = cw_ref[...]
    cb = cb_ref[...]
    sub = lax.broadcasted_iota(i32, (ntile, SUBLANES, LANES), 1)
    nrow = TS + 2 * SUBLANES

    def gates(ci, d):
        t0 = pl.multiple_of(ci * TS, TS)
        xw = rxp_ref[pl.ds(t0, nrow), :]
        u = (cw[0:1] * pltpu.roll(xw, 2, 0) + cw[1:2] * pltpu.roll(xw, 1, 0) + cw[2:3] * xw
             + cw[3:4] * pltpu.roll(xw, nrow - 1, 0))[SUBLANES:SUBLANES + TS] + cb
        c0 = d * 2 * LANES
        g = jnp.dot(u.astype(bf16), wg_ref[:, c0:c0 + 2 * LANES], preferred_element_type=f32)
        g = g + gb_ref[:, c0:c0 + 2 * LANES]
        r = jax.nn.sigmoid(g[:, :LANES])
        gi = jax.nn.sigmoid(g[:, LANES:])
        a = jnp.exp(neg_sp[d:d + 1] * r)
        b = jnp.sqrt(1.0 - a * a) * gi * u
        return a.reshape(ntile, SUBLANES, LANES), b.reshape(ntile, SUBLANES, LANES)

    def chunk(i, carry):
        cf, cbk = carry
        a, b = gates(i, 0)
        for s in (1, 2, 4):
            ok = sub >= s
            a_s = pltpu.roll(a, s, 1)
            b_s = pltpu.roll(b, s, 1)
            b = jnp.where(ok, a * b_s + b, b)
            a = jnp.where(ok, a * a_s, a)
        t0 = pl.multiple_of(i * TS, TS)
        for j in range(ntile):
            h = a[j] * cf + b[j]
            hf_ref[pl.ds(t0 + j * SUBLANES, SUBLANES), :] = h
            cf = h[SUBLANES - 1:SUBLANES, :]
        ib = nch - 1 - i
        a, b = gates(ib, 1)
        for s in (1, 2, 4):
            ok = sub < SUBLANES - s
            a_s = pltpu.roll(a, SUBLANES - s, 1)
            b_s = pltpu.roll(b, SUBLANES - s, 1)
            b = jnp.where(ok, a * b_s + b, b)
            a = jnp.where(ok, a * a_s, a)
        t0 = pl.multiple_of(ib * TS, TS)
        for j in range(ntile - 1, -1, -1):
            h = a[j] * cbk + b[j]
            out_ref[pl.ds(t0 + j * SUBLANES, SUBLANES), :] = h
            cbk = h[0:1, :]
        return cf, cbk

    zrow = jnp.zeros((1, LANES), f32)
    lax.fori_loop(0, nch, chunk, (zrow, zrow))

    def finish(c, carry):
        t0 = pl.multiple_of(c * TS, TS)
        sl = pl.ds(t0, TS)
        out_ref[sl, :] = (hf_ref[sl, :] + out_ref[sl, :]) * jax.nn.gelu(rg_ref[sl, :])
        return carry

    lax.fori_loop(0, nch, finish, 0)


def _rec(rx, rg, conv_w, conv_b, wg, gb, lam, B, S):
    ng = REC_W // LANES
    assert S % TS == 0
    seq = pl.BlockSpec((None, S, LANES), lambda b, c: (b, 0, c))
    return pl.pallas_call(
        functools.partial(_rec_body, S=S),
        grid=(B, ng),
        in_specs=[seq, seq,
                  pl.BlockSpec((CONV_W, LANES), lambda b, c: (0, c)),
                  pl.BlockSpec((1, LANES), lambda b, c: (0, c)),
                  pl.BlockSpec((None, LANES, 4 * LANES), lambda b, c: (c, 0, 0)),
                  pl.BlockSpec((None, 1, 4 * LANES), lambda b, c: (c, 0, 0)),
                  pl.BlockSpec((2, LANES), lambda b, c: (0, c))],
        out_specs=seq,
        out_shape=jax.ShapeDtypeStruct((B, S, REC_W), f32),
        scratch_shapes=[pltpu.VMEM((S + 2 * SUBLANES, LANES), f32), pltpu.VMEM((S, LANES), f32)],
        compiler_params=_cparams(("parallel", "parallel")),
        name="rec",
    )(rx.reshape(B, S, REC_W), rg.reshape(B, S, REC_W), conv_w, conv_b, wg, gb, lam)


u32 = jnp.uint32


def _pack_rows(x):
    r = lax.bitcast_convert_type(x.astype(bf16).astype(f32), u32)
    return (r[:, WORDS:] & u32(0xFFFF0000)) | (r[:, :WORDS] >> 16)


def _unpack_rows(w):
    lo = lax.bitcast_convert_type(w << 16, f32).astype(bf16)
    hi = lax.bitcast_convert_type(w & u32(0xFFFF0000), f32).astype(bf16)
    return lo, hi


def _slab_chunks(flat, rows, base=0):
    return jnp.concatenate(
        [flat[pl.ds(base * CHUNKS + c, rows, stride=CHUNKS), :] for c in range(CHUNKS)], axis=1)


def _store_slab_chunks(flat, val, rows, base=0):
    for c in range(CHUNKS):
        flat[pl.ds(base * CHUNKS + c, rows, stride=CHUNKS), :] = val[:, c * LANES:(c + 1) * LANES]


def _rms(t, gain):
    return t * lax.rsqrt(jnp.mean(t * t, axis=-1, keepdims=True) + LN_EPS) * gain


def _layer_norm(z, g, b):
    mu = jnp.mean(z, axis=-1, keepdims=True)
    zc = z - mu
    var = jnp.mean(zc * zc, axis=-1, keepdims=True)
    return zc * lax.rsqrt(var + LN_EPS) * g + b


def _mixout_body(attn_ref, rec_ref, x_ref,
                 ag_ref, rgn_ref, wo_ref, g1_ref, b1_ref, rwh_ref, rwl_ref, rb_ref, tri_ref, ones_ref,
                 sg_ref, su_ref, sd_ref, pre_ref, x1p_ref, idx_ref, gate_ref, rank_ref, cnt_ref, carry_ref, *, alpha):
    tm = TM_MIX

    @pl.when(pl.program_id(0) == 0)
    def _():
        carry_ref[...] = jnp.zeros_like(carry_ref)

    attn_n = _rms(attn_ref[...].astype(f32), ag_ref[...]).astype(bf16)
    rec_n = _rms(rec_ref[...], rgn_ref[...]).astype(bf16)
    y = (jnp.dot(attn_n, wo_ref[0:ATTN_W, :], preferred_element_type=f32)
         + jnp.dot(rec_n, wo_ref[ATTN_W:, :], preferred_element_type=f32))
    x1 = _layer_norm(alpha * x_ref[...] + y, g1_ref[...], b1_ref[...])
    _store_slab_chunks(x1p_ref, _pack_rows(x1), tm)

    hi = x1.astype(bf16)
    sg = jnp.dot(hi, sg_ref[...], preferred_element_type=f32)
    su = jnp.dot(hi, su_ref[...], preferred_element_type=f32)
    shared = jnp.dot((sg * jax.nn.sigmoid(sg) * su).astype(bf16), sd_ref[...],
                     preferred_element_type=f32)
    pre_ref[...] = alpha * x1 + shared

    lo = (x1 - hi.astype(f32)).astype(bf16)
    nt = (((1,), (1,)), ((), ()))
    logits = (lax.dot_general(rwh_ref[...], hi, nt, preferred_element_type=f32)
              + lax.dot_general(rwh_ref[...], lo, nt, preferred_element_type=f32)
              + lax.dot_general(rwl_ref[...], hi, nt, preferred_element_type=f32))
    scores = jax.nn.sigmoid(logits)
    biased = scores + jnp.concatenate([rb_ref[...]] * (tm // LANES), axis=1)

    rid = lax.broadcasted_iota(i32, (GROUP_SZ, tm), 0).astype(f32)
    grp = []
    for g in range(N_GROUPS):
        vg = biased[g * GROUP_SZ:(g + 1) * GROUP_SZ, :]
        m1 = jnp.max(vg, axis=0, keepdims=True)
        first = jnp.min(jnp.where(vg == m1, rid, float(GROUP_SZ)), axis=0, keepdims=True)
        m2 = jnp.max(jnp.where(rid == first, -jnp.inf, vg), axis=0, keepdims=True)
        grp.append(m1 + m2)
    eid = lax.broadcasted_iota(i32, (N_EXPERTS, tm), 0).astype(f32)
    keep = [jnp.zeros((1, tm), f32) for _ in range(N_GROUPS)]
    for _ in range(TOPK_GROUPS):
        gm = functools.reduce(jnp.maximum, grp)
        gi = jnp.full((1, tm), float(N_GROUPS), f32)
        for g in range(N_GROUPS - 1, -1, -1):
            gi = jnp.where(grp[g] == gm, float(g), gi)
        hits = [gi == float(g) for g in range(N_GROUPS)]
        grp = [jnp.where(hit, -jnp.inf, sc) for hit, sc in zip(hits, grp)]
        keep = [jnp.where(hit, 1.0, kp) for hit, kp in zip(hits, keep)]
    masked = jnp.concatenate(
        [jnp.where(jnp.broadcast_to(keep[g], (GROUP_SZ, tm)) > 0.5,
                   biased[g * GROUP_SZ:(g + 1) * GROUP_SZ, :], -jnp.inf)
         for g in range(N_GROUPS)], axis=0)

    onehot = jnp.zeros((N_EXPERTS, tm), f32)
    idxs, gts = [], []
    for _ in range(TOP_K):
        mx = jnp.max(masked, axis=0, keepdims=True)
        ix = jnp.min(jnp.where(masked == mx, eid, float(N_EXPERTS)), axis=0, keepdims=True)
        hit = eid == ix
        gts.append(jnp.sum(jnp.where(hit, scores, 0.0), axis=0, keepdims=True))
        idxs.append(ix)
        masked = jnp.where(hit, -jnp.inf, masked)
        onehot = onehot + jnp.where(hit, 1.0, 0.0)
    gsum = functools.reduce(lambda p, q: p + q, gts)
    for kk in range(TOP_K):
        idx_ref[kk:kk + 1, :] = idxs[kk].astype(i32)
        gate_ref[kk:kk + 1, :] = gts[kk] / gsum * ROUTED_SCALE

    oh = onehot.astype(bf16)
    before = carry_ref[...] + jnp.dot(oh, tri_ref[...], preferred_element_type=f32)
    for kk in range(TOP_K):
        rk = jnp.sum(jnp.where(eid == idxs[kk], before, 0.0), axis=0, keepdims=True)
        rank_ref[kk:kk + 1, :] = rk.astype(i32)
    total = carry_ref[...] + jnp.dot(oh, ones_ref[...], preferred_element_type=f32)
    carry_ref[...] = total
    cnt_ref[...] = total


def _mixout(attn, rec, x2, attn_gain, rec_gain, w_out_b, g1, b1, rw_hi, rw_lo, rbias, sg, su, sd,
            alpha):
    T = x2.shape[0]
    tm = TM_MIX
    tok = lambda i: (i, 0)
    fixed = lambda i: (0, 0)
    tri = jnp.asarray(np.triu(np.ones((tm, tm), np.float32), k=1), bf16)
    ones = jnp.ones((tm, tm), bf16)
    aw = pl.BlockSpec((tm, ATTN_W), tok)
    row = lambda n: pl.BlockSpec((1, n), fixed)
    kt = pl.BlockSpec((TOP_K, tm), lambda i: (0, i))
    return pl.pallas_call(
        functools.partial(_mixout_body, alpha=alpha),
        grid=(T // tm,),
        in_specs=[aw] * 2 + [pl.BlockSpec((tm, D_MODEL), tok), row(ATTN_W), row(REC_W),
                             pl.BlockSpec((D_MODEL, D_MODEL), fixed), row(D_MODEL), row(D_MODEL),
                             pl.BlockSpec((N_EXPERTS, D_MODEL), fixed),
                             pl.BlockSpec((N_EXPERTS, D_MODEL), fixed),
                             pl.BlockSpec((N_EXPERTS, LANES), fixed),
                             pl.BlockSpec((tm, tm), fixed), pl.BlockSpec((tm, tm), fixed),
                             pl.BlockSpec(sg.shape, fixed), pl.BlockSpec(su.shape, fixed),
                             pl.BlockSpec(sd.shape, fixed)],
        out_specs=[pl.BlockSpec((tm, D_MODEL), tok),
                   pl.BlockSpec((tm * CHUNKS, LANES), tok),
                   kt, kt, kt, pl.BlockSpec((N_EXPERTS, tm), fixed)],
        out_shape=[jax.ShapeDtypeStruct((T, D_MODEL), f32),
                   jax.ShapeDtypeStruct((T * CHUNKS, LANES), u32),
                   jax.ShapeDtypeStruct((TOP_K, T), i32),
                   jax.ShapeDtypeStruct((TOP_K, T), f32),
                   jax.ShapeDtypeStruct((TOP_K, T), i32),
                   jax.ShapeDtypeStruct((N_EXPERTS, tm), f32)],
        scratch_shapes=[pltpu.VMEM((N_EXPERTS, tm), f32)],
        compiler_params=_cparams(("arbitrary",)),
        name="mixout",
    )(attn, rec, x2, attn_gain, rec_gain, w_out_b, g1, b1,
      rw_hi, rw_lo, rbias, tri, ones, sg, su, sd)


def _slots_body(idx_ref, rank_ref, start_ref, dest_ref):
    tm = TM_SLOT
    eid = lax.broadcasted_iota(i32, (N_EXPERTS, tm), 0)
    start = jnp.concatenate([start_ref[...]] * (tm // LANES), axis=1)
    for kk in range(TOP_K):
        hit = eid == idx_ref[kk:kk + 1, :]
        base = jnp.sum(jnp.where(hit, start, 0.0), axis=0, keepdims=True)
        dest_ref[kk:kk + 1, :] = base.astype(i32) + rank_ref[kk:kk + 1, :]


def _slots(idx, rank, pad_start):
    T = idx.shape[1]
    tm = TM_SLOT
    kt = pl.BlockSpec((TOP_K, tm), lambda i: (0, i))
    start = jnp.broadcast_to(pad_start.astype(f32).reshape(N_EXPERTS, 1), (N_EXPERTS, LANES))
    return pl.pallas_call(
        _slots_body,
        grid=(T // tm,),
        in_specs=[kt, kt, pl.BlockSpec((N_EXPERTS, LANES), lambda i: (0, 0))],
        out_specs=kt,
        out_shape=jax.ShapeDtypeStruct((TOP_K, T), i32),
        compiler_params=_cparams(("parallel",)),
        name="slots",
    )(idx, rank, start)


def _dispatch_body(dest_ref, x_ref, xs_ref, sem):
    tm = TM_MIX

    def copy(t, k):
        return pltpu.make_async_copy(x_ref.at[t], xs_ref.at[dest_ref[k, t]], sem)

    def issue(t, carry):
        for k in range(TOP_K):
            copy(t, k).start(priority=k % DMA_THREADS)
        return carry

    def drain(t, carry):
        for k in range(TOP_K):
            copy(t, k).wait()
        return carry

    lax.fori_loop(0, tm, issue, 0)
    lax.fori_loop(0, tm, drain, 0)


def _dispatch(dest, x1p, n_slots):
    T = x1p.shape[0]
    tm = TM_MIX
    return pl.pallas_call(
        _dispatch_body,
        grid=(T // tm,),
        in_specs=[pl.BlockSpec((TOP_K, tm), lambda i: (0, i), memory_space=pltpu.SMEM),
                  pl.BlockSpec((tm, CHUNKS, LANES), lambda i: (i, 0, 0))],
        out_specs=pl.BlockSpec(memory_space=pl.ANY),
        out_shape=jax.ShapeDtypeStruct((n_slots, CHUNKS, LANES), u32),
        scratch_shapes=[pltpu.SemaphoreType.DMA(())],
        compiler_params=_cparams(("arbitrary",)),
        name="dispatch",
    )(dest, x1p)


def _experts_body(be_ref, nx_ref, nv_ref, nb_ref, xs_ref, wg_hbm, wu_hbm, wd_hbm, ys_ref,
                  fga, fua, fda, fgb, fub, fdb, sga, sua, sda, sgb, sub, sdb, sems):
    p = pl.program_id(0)
    blocks = (2 * p, 2 * p + 1)
    hbm_w = (wg_hbm, wu_hbm, wd_hbm)
    f32_w = ((fga, fua, fda), (fgb, fub, fdb))
    bf_w = ((sga, sua, sda), (sgb, sub, sdb))
    live_step = blocks[0] < nb_ref[0]

    def fetch(half, e):
        return [pltpu.make_async_copy(src.at[e], dst, sems.at[half])
                for src, dst in zip(hbm_w, f32_w[half])]

    for half, blk in enumerate(blocks):
        @pl.when(jnp.logical_and(live_step, p == 0))
        def _(half=half, blk=blk):
            for c in fetch(half, be_ref[blk]):
                c.start()

        prev = jnp.maximum(blk - 2, 0)

        @pl.when(jnp.logical_and(live_step,
                                 jnp.logical_or(p == 0, be_ref[blk] != be_ref[prev])))
        def _(half=half, blk=blk):
            for c in fetch(half, be_ref[blk]):
                c.wait()
            for src, dst in zip(f32_w[half], bf_w[half]):
                dst[...] = src[...].astype(bf16)

            @pl.when(nx_ref[blk] >= 0)
            def _():
                for c in fetch(half, nx_ref[blk]):
                    c.start()

    @pl.when(live_step)
    def _():
        xs = []
        for half, blk in enumerate(blocks):
            rows = jnp.where(blk < nb_ref[0], nv_ref[blk], 0)
            live = lax.broadcasted_iota(i32, (BM, WORDS), 0) < rows
            words = _slab_chunks(xs_ref, BM, half * BM)
            xs.append(_unpack_rows(jnp.where(live, words, u32(0))))

        def up(half, w_ref):
            lo, hi = xs[half]
            return (jnp.dot(lo, w_ref[:WORDS, :], preferred_element_type=f32)
                    + jnp.dot(hi, w_ref[WORDS:, :], preferred_element_type=f32))

        gs = [up(half, bf_w[half][0]) for half in range(2)]
        us = [up(half, bf_w[half][1]) for half in range(2)]
        hs = [(g * jax.nn.sigmoid(g) * u).astype(bf16) for g, u in zip(gs, us)]
        ys = [jnp.dot(h, bf_w[half][2][...], preferred_element_type=f32)
              for half, h in enumerate(hs)]
        for half in range(2):
            _store_slab_chunks(ys_ref, _pack_rows(ys[half]), BM, half * BM)


def _experts(block_e, next_e, block_rows, nb_used, xs, wg, wu, wd):
    P = xs.shape[0] // CHUNKS
    nb = P // BM
    assert nb % 2 == 0
    last = lambda nbu: jnp.maximum((nbu[0] - 1) // 2, 0)
    rows = lambda p, be, nx, nv, nbu: (jnp.minimum(p, last(nbu)), 0)
    any_spec = pl.BlockSpec(memory_space=pl.ANY)
    up_shape, down_shape = (D_MODEL, EXPERT_H), (EXPERT_H, D_MODEL)
    per_stream = lambda dt: [pltpu.VMEM(up_shape, dt), pltpu.VMEM(up_shape, dt),
                             pltpu.VMEM(down_shape, dt)]
    gs = pltpu.PrefetchScalarGridSpec(
        num_scalar_prefetch=4,
        grid=(nb // 2,),
        in_specs=[pl.BlockSpec((2 * BM * CHUNKS, LANES), rows), any_spec, any_spec, any_spec],
        out_specs=pl.BlockSpec((2 * BM * CHUNKS, LANES), rows),
        scratch_shapes=per_stream(f32) * 2 + per_stream(bf16) * 2
        + [pltpu.SemaphoreType.DMA((2,))],
    )
    return pl.pallas_call(
        _experts_body,
        grid_spec=gs,
        out_shape=jax.ShapeDtypeStruct((P * CHUNKS, LANES), u32),
        compiler_params=_cparams(("arbitrary",)),
        name="experts",
    )(block_e, next_e, block_rows, nb_used, xs, wg, wu, wd)


def _combine_body(dest_ref, dnext_ref, gate_ref, pre_ref, ys_ref, ysflat_ref,
                  g2_ref, b2_ref, out_ref, gb00, gb01, gb10, gb11, sems):
    tm = TM_CMB
    j = pl.program_id(0)
    n = pl.num_programs(0)
    gbufs = ((gb00, gb01), (gb10, gb11))

    def issue(d_ref, st, half):
        base = half * tm

        def step(t, carry):
            for k in range(TOP_K):
                src = ys_ref.at[d_ref[(base + t) * TOP_K + k]]
                dst = gbufs[st][half].at[pl.ds((k * tm + t) * CHUNKS, CHUNKS)]
                pltpu.make_async_copy(src, dst, sems.at[st, half]).start(priority=k % DMA_THREADS)
            return carry
        lax.fori_loop(0, tm, step, 0)

    def finish(st, half):
        rows = pl.ds(half * tm, tm)
        flat = gbufs[st][half]
        pltpu.make_async_copy(ysflat_ref.at[pl.ds(0, TOP_K * tm * CHUNKS)], flat,
                              sems.at[st, half]).wait()

        gates = gate_ref[rows, :]
        lo = [jnp.zeros((tm, LANES), f32) for _ in range(CHUNKS)]
        hi = [jnp.zeros((tm, LANES), f32) for _ in range(CHUNKS)]
        for k in range(TOP_K):
            gk = jnp.broadcast_to(gates[:, k:k + 1], (tm, LANES))
            for c in range(CHUNKS):
                w = flat[pl.ds(k * tm * CHUNKS + c, tm, stride=CHUNKS), :]
                lo[c] = lo[c] + gk * lax.bitcast_convert_type(w << 16, f32)
                hi[c] = hi[c] + gk * lax.bitcast_convert_type(w & u32(0xFFFF0000), f32)
        routed = jnp.concatenate(lo + hi, axis=1)
        out_ref[rows, :] = _layer_norm(pre_ref[rows, :] + routed, g2_ref[...], b2_ref[...])

    @pl.when(j == 0)
    def _():
        issue(dest_ref, 0, 0)
        issue(dest_ref, 0, 1)

    for st in range(2):
        @pl.when(lax.rem(j, 2) == st)
        def _(st=st):
            @pl.when(j + 1 < n)
            def _():
                issue(dnext_ref, 1 - st, 0)
                issue(dnext_ref, 1 - st, 1)

            finish(st, 0)
            finish(st, 1)


def _combine(dest, gate, pre, ys, g2, b2):
    T = pre.shape[0]
    tm = TM_CMB
    n = T // (2 * tm)
    tok = lambda j: (j, 0)
    fixed = lambda j: (0, 0)
    tbl = pl.BlockSpec((2 * tm * TOP_K,), lambda j: (j,), memory_space=pltpu.SMEM)
    tbl_next = pl.BlockSpec((2 * tm * TOP_K,), lambda j: (jnp.minimum(j + 1, n - 1),),
                            memory_space=pltpu.SMEM)
    any_spec = pl.BlockSpec(memory_space=pl.ANY)
    return pl.pallas_call(
        _combine_body,
        grid=(n,),
        in_specs=[tbl, tbl_next, pl.BlockSpec((2 * tm, TOP_K), tok),
                  pl.BlockSpec((2 * tm, D_MODEL), tok), any_spec, any_spec,
                  pl.BlockSpec((1, D_MODEL), fixed), pl.BlockSpec((1, D_MODEL), fixed)],
        out_specs=pl.BlockSpec((2 * tm, D_MODEL), tok),
        out_shape=jax.ShapeDtypeStruct((T, D_MODEL), f32),
        scratch_shapes=[pltpu.VMEM((TOP_K * tm * CHUNKS, LANES), u32)] * 4
        + [pltpu.SemaphoreType.DMA((2, 2))],
        compiler_params=_cparams(("arbitrary",)),
        name="combine",
    )(dest, dest, gate, pre, ys, ys.reshape(ys.shape[0] * CHUNKS, LANES), g2, b2)


def _rope_inv_freq():
    half = ROT_DIM // 2
    inv = ROPE_THETA ** (-jnp.arange(half, dtype=f32) * 2.0 / ROT_DIM)
    j = np.arange(LANES) % HEAD_DIM
    table = jnp.where(j < ROT_DIM, inv[j % half], 0.0)
    return table.reshape(1, LANES).astype(f32)


def _gate_weights(ga_w, ga_b, gx_w, gx_b):
    ng = REC_W // LANES
    per = LANES // HEAD_DIM
    def bd(w):
        w = w.reshape(ng, per, HEAD_DIM, HEAD_DIM)
        z = jnp.zeros((ng, LANES, LANES), w.dtype)
        for p in range(per):
            z = z.at[:, p * HEAD_DIM:(p + 1) * HEAD_DIM, p * HEAD_DIM:(p + 1) * HEAD_DIM].set(w[:, p])
        return z
    wg = jnp.concatenate([bd(ga_w[0]), bd(gx_w[0]), bd(ga_w[1]), bd(gx_w[1])], axis=-1).astype(bf16)
    grp = lambda b: b.reshape(ng, 1, LANES)
    gb = jnp.concatenate([grp(ga_b[0]), grp(gx_b[0]), grp(ga_b[1]), grp(gx_b[1])], axis=-1)
    return wg, gb


def _layer(x, positions, w_in, attn_gain, conv_w, conv_b, ga_w, ga_b, gx_w, gx_b, lam, rec_gain,
           w_out, ln1_g, ln1_b, router_w, router_bias, e_wg, e_wu, e_wd, s_wg, s_wu, s_wd,
           ln2_g, ln2_b, alpha):
    B, S, _ = x.shape
    T = B * S
    x2 = x.reshape(T, D_MODEL)
    pos2 = positions.reshape(T, 1)

    q, k, v, rx, rg = _inproj(x2, pos2, w_in.astype(bf16), _rope_inv_freq())
    attn = _attention(q, k, v, B, S)
    wg, gb = _gate_weights(ga_w, ga_b, gx_w, gx_b)
    rec = _rec(rx, rg, conv_w, conv_b.reshape(1, REC_W), wg, gb, lam, B, S).reshape(T, REC_W)

    rw_t = router_w.T
    rw_hi = rw_t.astype(bf16)
    rw_lo = (rw_t - rw_hi.astype(f32)).astype(bf16)
    rbias = jnp.broadcast_to(router_bias.reshape(N_EXPERTS, 1), (N_EXPERTS, LANES))
    pre, x1p, idx, gate, rank, cnt = _mixout(
        attn, rec, x2,
        attn_gain.reshape(1, ATTN_W), rec_gain.reshape(1, REC_W), w_out.astype(bf16),
        ln1_g.reshape(1, D_MODEL), ln1_b.reshape(1, D_MODEL), rw_hi, rw_lo, rbias,
        s_wg.astype(bf16), s_wu.astype(bf16), s_wd.astype(bf16), alpha)

    counts = cnt[:, 0].astype(i32)
    padded = (counts + BM - 1) // BM * BM
    pad_end = jnp.cumsum(padded)
    pad_start = pad_end - padded
    slot = _slots(idx, rank, pad_start)
    nb = (T * TOP_K + N_EXPERTS * (BM - 1)) // BM + 1
    nb += nb % 2
    nb_used = pad_end[-1] // BM
    h = (nb_used + 1) // 2
    pos = jnp.arange(nb, dtype=i32)
    blk = jnp.where(pos % 2 == 0, pos // 2, h + pos // 2)
    live = jnp.logical_and(pos < 2 * h, blk < nb_used)
    first_row = blk * BM
    block_e = jnp.minimum(jnp.sum((pad_end[None, :] <= first_row[:, None]).astype(i32), axis=1),
                          N_EXPERTS - 1)
    block_e = jnp.where(live, block_e, N_EXPERTS - 1)
    block_rows = jnp.where(live, jnp.clip((pad_start + counts)[block_e] - first_row, 0, BM), 0)
    pairs = block_e.reshape(nb // 2, 2)
    step_id = jnp.arange(nb // 2, dtype=i32)[:, None]
    change = jnp.concatenate([pairs[1:] != pairs[:-1], jnp.zeros((1, 2), bool)], axis=0)
    change = jnp.logical_and(change, step_id + 1 < h)
    nxt_step = lax.cummin(jnp.where(change, step_id + 1, nb), axis=0, reverse=True)
    next_e = jnp.where(nxt_step < nb // 2,
                       jnp.take_along_axis(pairs, jnp.minimum(nxt_step, nb // 2 - 1), axis=0), -1)
    next_e = next_e.reshape(nb).astype(i32)
    sblk = slot // BM
    dest = jnp.where(sblk < h, 2 * sblk, 2 * (sblk - h) + 1) * BM + slot % BM
    nb_used = (2 * h).astype(i32).reshape(1)

    xs = _dispatch(dest, x1p.reshape(T, CHUNKS, LANES), nb * BM)
    ys = _experts(block_e.astype(i32), next_e, block_rows.astype(i32), nb_used,
                  xs.reshape(nb * BM * CHUNKS, LANES), e_wg, e_wu, e_wd)
    ys = ys.reshape(nb * BM, CHUNKS, LANES)
    out = _combine(dest.T.reshape(-1), gate.T, pre, ys,
                   ln2_g.reshape(1, D_MODEL), ln2_b.reshape(1, D_MODEL))
    return out.reshape(B, S, D_MODEL)


def kernel(x, positions, w_in, attn_gain, rec_conv_w, rec_conv_b, rec_gate_a_w, rec_gate_a_b,
           rec_gate_x_w, rec_gate_x_b, rec_lambda, rec_gain, w_out, ln1_g, ln1_b, router_w,
           router_bias, exp_w_gate, exp_w_up, exp_w_down, shared_w_gate, shared_w_up,
           shared_w_down, ln2_g, ln2_b):
    depth = w_in.shape[0]
    alpha = (2 * depth) ** 0.25
    for l in range(depth):
        x = _layer(x, positions, w_in[l], attn_gain[l], rec_conv_w[l], rec_conv_b[l],
                   rec_gate_a_w[l], rec_gate_a_b[l], rec_gate_x_w[l], rec_gate_x_b[l],
                   rec_lambda[l], rec_gain[l], w_out[l], ln1_g[l], ln1_b[l], router_w[l],
                   router_bias[l], exp_w_gate[l], exp_w_up[l], exp_w_down[l], shared_w_gate[l],
                   shared_w_up[l], shared_w_down[l], ln2_g[l], ln2_b[l], alpha)
    return x
```

```python
import functools

import jax
import jax.numpy as jnp
import numpy as np
from jax import lax
from jax.experimental import pallas as pl
from jax.experimental.pallas import tpu as pltpu

f32 = jnp.float32
bf16 = jnp.bfloat16
i32 = jnp.int32
u32 = jnp.uint32

D_MODEL = 1024
ATTN_W = 512
REC_W = 512
HEAD_DIM = 64
ROT_DIM = 16
ROPE_THETA = 500000.0
BRANCHES = ((128, 1), (512, 4), (2048, 16))
HALF_BAND = 64
CONV_W = 4
RG_LRU_C = 8.0
N_EXPERTS = 256
TOP_K = 8
N_GROUPS = 8
GROUP_SZ = N_EXPERTS // N_GROUPS
TOPK_GROUPS = 4
EXPERT_H = 256
ROUTED_SCALE = 2.5
LN_EPS = 1e-5
NEG = -1e30

LANES = 128
SUBLANES = 8
WORDS = D_MODEL // 2
CHUNKS = WORDS // LANES
VMEM_LIMIT = 56 * 1024 * 1024
DMA_THREADS = 2

TM_IN = 512
TQ = 128
KW = TQ + 2 * HALF_BAND
SB = TQ * max(d for _, d in BRANCHES)
MERGE_ROWS = 256
TS = 256
TM_MIX = 512
TM_SLOT = 2048
TM_CMB = 128
BM = 256


def _cparams(sem):
    return pltpu.CompilerParams(dimension_semantics=sem, vmem_limit_bytes=VMEM_LIMIT)


def _inproj_body(x_ref, pos_ref, w_ref, invf_ref, q_ref, k_ref, v_ref, rx_ref, rg_ref):
    xb = x_ref[...].astype(bf16)
    ang = pos_ref[...].astype(f32) * invf_ref[...]
    cos = jnp.cos(ang)
    sin = jnp.sin(ang)
    j = lax.broadcasted_iota(i32, (1, LANES), 1) % HEAD_DIM
    half = ROT_DIM // 2
    s_lo = jnp.where(j < half, -sin, 0.0)
    s_hi = jnp.where((j >= half) & (j < ROT_DIM), sin, 0.0)
    rep = ATTN_W // LANES
    cos_w = jnp.concatenate([cos] * rep, axis=1)
    s_lo_w = jnp.concatenate([s_lo] * rep, axis=1)
    s_hi_w = jnp.concatenate([s_hi] * rep, axis=1)

    def proj(c0, n):
        return jnp.dot(xb, w_ref[:, c0:c0 + n], preferred_element_type=f32)

    def rope(t):
        return (t * cos_w + pltpu.roll(t, ATTN_W - half, 1) * s_lo_w
                + pltpu.roll(t, half, 1) * s_hi_w)

    q_ref[...] = rope(proj(0, ATTN_W)) * (HEAD_DIM ** -0.5)
    k_ref[...] = rope(proj(ATTN_W, ATTN_W))
    v_ref[...] = proj(2 * ATTN_W, ATTN_W)
    rx_ref[...] = proj(3 * ATTN_W, REC_W)
    rg_ref[...] = proj(3 * ATTN_W + REC_W, REC_W)


def _inproj(x2, pos2, w_in_b, invf):
    T = x2.shape[0]
    tm = TM_IN
    in_w = w_in_b.shape[1]
    tok = lambda i: (i, 0)
    fixed = lambda i: (0, 0)
    return pl.pallas_call(
        _inproj_body,
        grid=(T // tm,),
        in_specs=[pl.BlockSpec((tm, D_MODEL), tok), pl.BlockSpec((tm, 1), tok),
                  pl.BlockSpec((D_MODEL, in_w), fixed), pl.BlockSpec((1, LANES), fixed)],
        out_specs=[pl.BlockSpec((tm, ATTN_W), tok)] * 3 + [pl.BlockSpec((tm, REC_W), tok)] * 2,
        out_shape=[jax.ShapeDtypeStruct((T, ATTN_W), f32)] * 3
        + [jax.ShapeDtypeStruct((T, REC_W), f32)] * 2,
        compiler_params=_cparams(("parallel",)),
        name="inproj",
    )(x2, pos2, w_in_b, invf)


def _attn_body(q_ref, k_ref, v_ref, out_ref, o1, o2, o3, l1, l2, l3, bias_ref, *, S):
    o_sc, l_sc = (o1, o2, o3), (l1, l2, l3)
    lane = lax.broadcasted_iota(i32, (1, LANES), 1)
    head0 = lane < HEAD_DIM
    rel = (lax.broadcasted_iota(i32, (TQ, KW), 0) - lax.broadcasted_iota(i32, (TQ, KW), 1))
    tiles = SB // TQ
    for case in range(3):
        bias_ref[case] = jnp.where(jnp.abs(rel + case * HALF_BAND) <= HALF_BAND, 0.0, NEG)

    def rows(ref, start, n, d):
        return ref[pl.ds(start, n), :] if d == 1 else ref[pl.ds(start, n, stride=d), :]

    def tile(ti, n0):
        ctx = []
        for g, (_, d) in enumerate(BRANCHES):
            L = S // d
            sh = d.bit_length() - 1
            r = jnp.bitwise_and(ti, d - 1)
            m0 = jnp.right_shift(n0, sh) + jnp.right_shift(ti, sh) * TQ
            ks = jnp.clip(m0 - HALF_BAND, 0, L - KW)
            q = rows(q_ref, r + d * m0, TQ, d).astype(bf16)
            k = rows(k_ref, r + d * ks, KW, d).astype(bf16)
            v = rows(v_ref, r + d * ks, KW, d).astype(bf16)
            bias = bias_ref[(m0 - ks) // HALF_BAND]
            local = r + d * (m0 - jnp.right_shift(n0, sh))
            ss = []
            for sel in (head0, jnp.logical_not(head0)):
                qh = jnp.where(sel, q, jnp.zeros_like(q))
                ss.append(lax.dot_general(qh, k, (((1,), (1,)), ((), ())),
                                          preferred_element_type=f32))
            ctx.append((d, v, bias, local, ss))
        soft = []
        for d, v, bias, local, ss in ctx:
            ps = []
            for s in ss:
                s = s + bias
                m = jnp.max(s, axis=-1, keepdims=True)
                p = jnp.exp(s - m)
                den = jnp.sum(p, axis=-1, keepdims=True)
                ps.append((p.astype(bf16), den, m + jnp.log(den)))
            soft.append(ps)
        for g, ((d, v, bias, local, ss), ps) in enumerate(zip(ctx, soft)):
            outs = [(jnp.dot(p, v, preferred_element_type=f32) / den, lse) for p, den, lse in ps]
            o_val = jnp.where(head0, outs[0][0], outs[1][0])
            l_val = jnp.where(head0, outs[0][1], outs[1][1])
            if d == 1:
                o_sc[g][pl.ds(local, TQ), :] = o_val
                l_sc[g][pl.ds(local, TQ), :] = l_val
            else:
                o_sc[g][pl.ds(local, TQ, stride=d), :] = o_val
                l_sc[g][pl.ds(local, TQ, stride=d), :] = l_val
        return n0

    def merge(c, n0):
        sl = pl.ds(pl.multiple_of(c * MERGE_ROWS, MERGE_ROWS), MERGE_ROWS)
        ls = [l_sc[g][sl, :] for g in range(len(BRANCHES))]
        mx = functools.reduce(jnp.maximum, ls)
        es = [jnp.exp(l - mx) for l in ls]
        num = functools.reduce(lambda a, b: a + b, [e * o_sc[g][sl, :] for g, e in enumerate(es)])
        den = functools.reduce(lambda a, b: a + b, es)
        dst = pl.ds(pl.multiple_of(n0 + c * MERGE_ROWS, MERGE_ROWS), MERGE_ROWS)
        out_ref[dst, :] = (num / den).astype(bf16)
        return n0

    def superblock(sb, carry):
        n0 = pl.multiple_of(sb * SB, SB)
        lax.fori_loop(0, tiles, tile, n0)
        lax.fori_loop(0, SB // MERGE_ROWS, merge, n0)
        return carry

    lax.fori_loop(0, S // SB, superblock, 0)


def _attention(q, k, v, B, S):
    assert S % SB == 0 and all(S // d >= KW for _, d in BRANCHES)
    hp = ATTN_W // LANES
    view = lambda t: t.reshape(B, S, ATTN_W)
    spec = pl.BlockSpec((None, S, LANES), lambda b, h: (b, 0, h))
    out = pl.pallas_call(
        functools.partial(_attn_body, S=S),
        grid=(B, hp),
        in_specs=[spec, spec, spec],
        out_specs=spec,
        out_shape=jax.ShapeDtypeStruct((B, S, ATTN_W), bf16),
        scratch_shapes=[pltpu.VMEM((SB, LANES), f32)] * (2 * len(BRANCHES))
        + [pltpu.VMEM((3, TQ, KW), f32)],
        compiler_params=_cparams(("parallel", "parallel")),
        name="attention",
    )(view(q), view(k), view(v))
    return out.reshape(B * S, ATTN_W)


def _rec_body(rx_ref, rg_ref, cw_ref, cb_ref, wg_ref, gb_ref, lam_ref, out_ref, rxp_ref, hf_ref, *, S):
    nch = S // TS
    ntile = TS // SUBLANES
    zeros8 = jnp.zeros((SUBLANES, LANES), f32)
    rxp_ref[pl.ds(0, SUBLANES), :] = zeros8
    rxp_ref[pl.ds(S + SUBLANES, SUBLANES), :] = zeros8

    def pad_copy(c, carry):
        t0 = pl.multiple_of(c * TS, TS)
        rxp_ref[pl.ds(t0 + SUBLANES, TS), :] = rx_ref[pl.ds(t0, TS), :]
        return carry

    lax.fori_loop(0, nch, pad_copy, 0)

    lam = lam_ref[...]
    neg_sp = -RG_LRU_C * (jnp.maximum(-lam, 0.0) + jnp.log1p(jnp.exp(-jnp.abs(lam))))
    cw = cw_ref[...]
    cb = cb_ref[...]
    sub = lax.broadcasted_iota(i32, (ntile, SUBLANES, LANES), 1)
    nrow = TS + 2 * SUBLANES

    def gates(ci, d):
        t0 = pl.multiple_of(ci * TS, TS)
        xw = rxp_ref[pl.ds(t0, nrow), :]
        u = (cw[0:1] * pltpu.roll(xw, 2, 0) + cw[1:2] * pltpu.roll(xw, 1, 0) + cw[2:3] * xw
             + cw[3:4] * pltpu.roll(xw, nrow - 1, 0))[SUBLANES:SUBLANES + TS] + cb
        c0 = d * 2 * LANES
        g = jnp.dot(u.astype(bf16), wg_ref[:, c0:c0 + 2 * LANES], preferred_element_type=f32)
        g = g + gb_ref[:, c0:c0 + 2 * LANES]
        r = jax.nn.sigmoid(g[:, :LANES])
        gi = jax.nn.sigmoid(g[:, LANES:])
        a = jnp.exp(neg_sp[d:d + 1] * r)
        b = jnp.sqrt(1.0 - a * a) * gi * u
        return a.reshape(ntile, SUBLANES, LANES), b.reshape(ntile, SUBLANES, LANES)

    def chunk(i, carry):
        cf, cbk = carry
        a, b = gates(i, 0)
        for s in (1, 2, 4):
            ok = sub >= s
            a_s = pltpu.roll(a, s, 1)
            b_s = pltpu.roll(b, s, 1)
            b = jnp.where(ok, a * b_s + b, b)
            a = jnp.where(ok, a * a_s, a)
        t0 = pl.multiple_of(i * TS, TS)
        for j in range(ntile):
            h = a[j] * cf + b[j]
            hf_ref[pl.ds(t0 + j * SUBLANES, SUBLANES), :] = h
            cf = h[SUBLANES - 1:SUBLANES, :]
        ib = nch - 1 - i
        a, b = gates(ib, 1)
        for s in (1, 2, 4):
            ok = sub < SUBLANES - s
            a_s = pltpu.roll(a, SUBLANES - s, 1)
            b_s = pltpu.roll(b, SUBLANES - s, 1)
            b = jnp.where(ok, a * b_s + b, b)
            a = jnp.where(ok, a * a_s, a)
        t0 = pl.multiple_of(ib * TS, TS)
        for j in range(ntile - 1, -1, -1):
            h = a[j] * cbk + b[j]
            out_ref[pl.ds(t0 + j * SUBLANES, SUBLANES), :] = h
            cbk = h[0:1, :]
        return cf, cbk

    zrow = jnp.zeros((1, LANES), f32)
    lax.fori_loop(0, nch, chunk, (zrow, zrow))

    def finish(c, carry):
        t0 = pl.multiple_of(c * TS, TS)
        sl = pl.ds(t0, TS)
        out_ref[sl, :] = (hf_ref[sl, :] + out_ref[sl, :]) * jax.nn.gelu(rg_ref[sl, :])
        return carry

    lax.fori_loop(0, nch, finish, 0)


def _rec(rx, rg, conv_w, conv_b, wg, gb, lam, B, S):
    ng = REC_W // LANES
    assert S % TS == 0
    seq = pl.BlockSpec((None, S, LANES), lambda b, c: (b, 0, c))
    return pl.pallas_call(
        functools.partial(_rec_body, S=S),
        grid=(B, ng),
        in_specs=[seq, seq,
                  pl.BlockSpec((CONV_W, LANES), lambda b, c: (0, c)),
                  pl.BlockSpec((1, LANES), lambda b, c: (0, c)),
                  pl.BlockSpec((None, LANES, 4 * LANES), lambda b, c: (c, 0, 0)),
                  pl.BlockSpec((None, 1, 4 * LANES), lambda b, c: (c, 0, 0)),
                  pl.BlockSpec((2, LANES), lambda b, c: (0, c))],
        out_specs=seq,
        out_shape=jax.ShapeDtypeStruct((B, S, REC_W), f32),
        scratch_shapes=[pltpu.VMEM((S + 2 * SUBLANES, LANES), f32), pltpu.VMEM((S, LANES), f32)],
        compiler_params=_cparams(("parallel", "parallel")),
        name="rec",
    )(rx.reshape(B, S, REC_W), rg.reshape(B, S, REC_W), conv_w, conv_b, wg, gb, lam)


def _pack_rows(x):
    r = lax.bitcast_convert_type(x.astype(bf16).astype(f32), u32)
    return (r[:, WORDS:] & u32(0xFFFF0000)) | (r[:, :WORDS] >> 16)


def _unpack_rows(w):
    lo = lax.bitcast_convert_type(w << 16, f32).astype(bf16)
    hi = lax.bitcast_convert_type(w & u32(0xFFFF0000), f32).astype(bf16)
    return lo, hi


def _slab_chunks(flat, rows, base=0):
    return jnp.concatenate(
        [flat[pl.ds(base * CHUNKS + c, rows, stride=CHUNKS), :] for c in range(CHUNKS)], axis=1)


def _store_slab_chunks(flat, val, rows, base=0):
    for c in range(CHUNKS):
        flat[pl.ds(base * CHUNKS + c, rows, stride=CHUNKS), :] = val[:, c * LANES:(c + 1) * LANES]


def _rms(t, gain):
    return t * lax.rsqrt(jnp.mean(t * t, axis=-1, keepdims=True) + LN_EPS) * gain


def _layer_norm(z, g, b):
    mu = jnp.mean(z, axis=-1, keepdims=True)
    zc = z - mu
    var = jnp.mean(zc * zc, axis=-1, keepdims=True)
    return zc * lax.rsqrt(var + LN_EPS) * g + b


def _mixout_body(attn_ref, rec_ref, x_ref,
                 ag_ref, rgn_ref, wo_ref, g1_ref, b1_ref, rwh_ref, rwl_ref, rb_ref, tri_ref, ones_ref,
                 sg_ref, su_ref, sd_ref, pre_ref, x1p_ref, idx_ref, gate_ref, rank_ref, cnt_ref, carry_ref, *, alpha):
    tm = TM_MIX

    @pl.when(pl.program_id(0) == 0)
    def _():
        carry_ref[...] = jnp.zeros_like(carry_ref)

    attn_n = _rms(attn_ref[...].astype(f32), ag_ref[...]).astype(bf16)
    rec_n = _rms(rec_ref[...], rgn_ref[...]).astype(bf16)
    y = (jnp.dot(attn_n, wo_ref[0:ATTN_W, :], preferred_element_type=f32)
         + jnp.dot(rec_n, wo_ref[ATTN_W:, :], preferred_element_type=f32))
    x1 = _layer_norm(alpha * x_ref[...] + y, g1_ref[...], b1_ref[...])
    _store_slab_chunks(x1p_ref, _pack_rows(x1), tm)

    hi = x1.astype(bf16)
    sg = jnp.dot(hi, sg_ref[...], preferred_element_type=f32)
    su = jnp.dot(hi, su_ref[...], preferred_element_type=f32)
    shared = jnp.dot((sg * jax.nn.sigmoid(sg) * su).astype(bf16), sd_ref[...],
                     preferred_element_type=f32)
    pre_ref[...] = alpha * x1 + shared

    lo = (x1 - hi.astype(f32)).astype(bf16)
    nt = (((1,), (1,)), ((), ()))
    logits = (lax.dot_general(rwh_ref[...], hi, nt, preferred_element_type=f32)
              + lax.dot_general(rwh_ref[...], lo, nt, preferred_element_type=f32)
              + lax.dot_general(rwl_ref[...], hi, nt, preferred_element_type=f32))
    scores = jax.nn.sigmoid(logits)
    biased = scores + jnp.concatenate([rb_ref[...]] * (tm // LANES), axis=1)

    rid = lax.broadcasted_iota(i32, (GROUP_SZ, tm), 0).astype(f32)
    grp = []
    for g in range(N_GROUPS):
        vg = biased[g * GROUP_SZ:(g + 1) * GROUP_SZ, :]
        m1 = jnp.max(vg, axis=0, keepdims=True)
        first = jnp.min(jnp.where(vg == m1, rid, float(GROUP_SZ)), axis=0, keepdims=True)
        m2 = jnp.max(jnp.where(rid == first, -jnp.inf, vg), axis=0, keepdims=True)
        grp.append(m1 + m2)
    eid = lax.broadcasted_iota(i32, (N_EXPERTS, tm), 0).astype(f32)
    keep = [jnp.zeros((1, tm), f32) for _ in range(N_GROUPS)]
    for _ in range(TOPK_GROUPS):
        gm = functools.reduce(jnp.maximum, grp)
        gi = jnp.full((1, tm), float(N_GROUPS), f32)
        for g in range(N_GROUPS - 1, -1, -1):
            gi = jnp.where(grp[g] == gm, float(g), gi)
        hits = [gi == float(g) for g in range(N_GROUPS)]
        grp = [jnp.where(hit, -jnp.inf, sc) for hit, sc in zip(hits, grp)]
        keep = [jnp.where(hit, 1.0, kp) for hit, kp in zip(hits, keep)]
    masked = jnp.concatenate(
        [jnp.where(jnp.broadcast_to(keep[g], (GROUP_SZ, tm)) > 0.5,
                   biased[g * GROUP_SZ:(g + 1) * GROUP_SZ, :], -jnp.inf)
         for g in range(N_GROUPS)], axis=0)

    onehot = jnp.zeros((N_EXPERTS, tm), f32)
    idxs, gts = [], []
    for _ in range(TOP_K):
        mx = jnp.max(masked, axis=0, keepdims=True)
        ix = jnp.min(jnp.where(masked == mx, eid, float(N_EXPERTS)), axis=0, keepdims=True)
        hit = eid == ix
        gts.append(jnp.sum(jnp.where(hit, scores, 0.0), axis=0, keepdims=True))
        idxs.append(ix)
        masked = jnp.where(hit, -jnp.inf, masked)
        onehot = onehot + jnp.where(hit, 1.0, 0.0)
    gsum = functools.reduce(lambda p, q: p + q, gts)
    for kk in range(TOP_K):
        idx_ref[kk:kk + 1, :] = idxs[kk].astype(i32)
        gate_ref[kk:kk + 1, :] = gts[kk] / gsum * ROUTED_SCALE

    oh = onehot.astype(bf16)
    before = carry_ref[...] + jnp.dot(oh, tri_ref[...], preferred_element_type=f32)
    for kk in range(TOP_K):
        rk = jnp.sum(jnp.where(eid == idxs[kk], before, 0.0), axis=0, keepdims=True)
        rank_ref[kk:kk + 1, :] = rk.astype(i32)
    total = carry_ref[...] + jnp.dot(oh, ones_ref[...], preferred_element_type=f32)
    carry_ref[...] = total
    cnt_ref[...] = total


def _mixout(attn, rec, x2, attn_gain, rec_gain, w_out_b, g1, b1, rw_hi, rw_lo, rbias, sg, su, sd,
            alpha):
    T = x2.shape[0]
    tm = TM_MIX
    tok = lambda i: (i, 0)
    fixed = lambda i: (0, 0)
    tri = jnp.asarray(np.triu(np.ones((tm, tm), np.float32), k=1), bf16)
    ones = jnp.ones((tm, tm), bf16)
    aw = pl.BlockSpec((tm, ATTN_W), tok)
    row = lambda n: pl.BlockSpec((1, n), fixed)
    kt = pl.BlockSpec((TOP_K, tm), lambda i: (0, i))
    return pl.pallas_call(
        functools.partial(_mixout_body, alpha=alpha),
        grid=(T // tm,),
        in_specs=[aw] * 2 + [pl.BlockSpec((tm, D_MODEL), tok), row(ATTN_W), row(REC_W),
                             pl.BlockSpec((D_MODEL, D_MODEL), fixed), row(D_MODEL), row(D_MODEL),
                             pl.BlockSpec((N_EXPERTS, D_MODEL), fixed),
                             pl.BlockSpec((N_EXPERTS, D_MODEL), fixed),
                             pl.BlockSpec((N_EXPERTS, LANES), fixed),
                             pl.BlockSpec((tm, tm), fixed), pl.BlockSpec((tm, tm), fixed),
                             pl.BlockSpec(sg.shape, fixed), pl.BlockSpec(su.shape, fixed),
                             pl.BlockSpec(sd.shape, fixed)],
        out_specs=[pl.BlockSpec((tm, D_MODEL), tok),
                   pl.BlockSpec((tm * CHUNKS, LANES), tok),
                   kt, kt, kt, pl.BlockSpec((N_EXPERTS, tm), fixed)],
        out_shape=[jax.ShapeDtypeStruct((T, D_MODEL), f32),
                   jax.ShapeDtypeStruct((T * CHUNKS, LANES), u32),
                   jax.ShapeDtypeStruct((TOP_K, T), i32),
                   jax.ShapeDtypeStruct((TOP_K, T), f32),
                   jax.ShapeDtypeStruct((TOP_K, T), i32),
                   jax.ShapeDtypeStruct((N_EXPERTS, tm), f32)],
        scratch_shapes=[pltpu.VMEM((N_EXPERTS, tm), f32)],
        compiler_params=_cparams(("arbitrary",)),
        name="mixout",
    )(attn, rec, x2, attn_gain, rec_gain, w_out_b, g1, b1,
      rw_hi, rw_lo, rbias, tri, ones, sg, su, sd)


def _slots_body(idx_ref, rank_ref, start_ref, dest_ref):
    tm = TM_SLOT
    eid = lax.broadcasted_iota(i32, (N_EXPERTS, tm), 0)
    start = jnp.concatenate([start_ref[...]] * (tm // LANES), axis=1)
    for kk in range(TOP_K):
        hit = eid == idx_ref[kk:kk + 1, :]
        base = jnp.sum(jnp.where(hit, start, 0.0), axis=0, keepdims=True)
        dest_ref[kk:kk + 1, :] = base.astype(i32) + rank_ref[kk:kk + 1, :]


def _slots(idx, rank, pad_start):
    T = idx.shape[1]
    tm = TM_SLOT
    kt = pl.BlockSpec((TOP_K, tm), lambda i: (0, i))
    start = jnp.broadcast_to(pad_start.astype(f32).reshape(N_EXPERTS, 1), (N_EXPERTS, LANES))
    return pl.pallas_call(
        _slots_body,
        grid=(T // tm,),
        in_specs=[kt, kt, pl.BlockSpec((N_EXPERTS, LANES), lambda i: (0, 0))],
        out_specs=kt,
        out_shape=jax.ShapeDtypeStruct((TOP_K, T), i32),
        compiler_params=_cparams(("parallel",)),
        name="slots",
    )(idx, rank, start)


def _dispatch_body(dest_ref, x_ref, xs_ref, sem):
    tm = TM_MIX

    def copy(t, k):
        return pltpu.make_async_copy(x_ref.at[t], xs_ref.at[dest_ref[k, t]], sem)

    def issue(t, carry):
        for k in range(TOP_K):
            copy(t, k).start(priority=k % DMA_THREADS)
        return carry

    def drain(t, carry):
        for k in range(TOP_K):
            copy(t, k).wait()
        return carry

    lax.fori_loop(0, tm, issue, 0)
    lax.fori_loop(0, tm, drain, 0)


def _dispatch(dest, x1p, n_slots):
    T = x1p.shape[0]
    tm = TM_MIX
    return pl.pallas_call(
        _dispatch_body,
        grid=(T // tm,),
        in_specs=[pl.BlockSpec((TOP_K, tm), lambda i: (0, i), memory_space=pltpu.SMEM),
                  pl.BlockSpec((tm, CHUNKS, LANES), lambda i: (i, 0, 0))],
        out_specs=pl.BlockSpec(memory_space=pl.ANY),
        out_shape=jax.ShapeDtypeStruct((n_slots, CHUNKS, LANES), u32),
        scratch_shapes=[pltpu.SemaphoreType.DMA(())],
        compiler_params=_cparams(("arbitrary",)),
        name="dispatch",
    )(dest, x1p)


def _experts_body(be_ref, nx_ref, nv_ref, nb_ref, xs_ref, wg_hbm, wu_hbm, wd_hbm, ys_ref,
                  fga, fua, fda, fgb, fub, fdb, sga, sua, sda, sgb, sub, sdb, sems):
    p = pl.program_id(0)
    blocks = (2 * p, 2 * p + 1)
    hbm_w = (wg_hbm, wu_hbm, wd_hbm)
    f32_w = ((fga, fua, fda), (fgb, fub, fdb))
    bf_w = ((sga, sua, sda), (sgb, sub, sdb))
    live_step = blocks[0] < nb_ref[0]

    def fetch(half, e):
        return [pltpu.make_async_copy(src.at[e], dst, sems.at[half])
                for src, dst in zip(hbm_w, f32_w[half])]

    for half, blk in enumerate(blocks):
        @pl.when(jnp.logical_and(live_step, p == 0))
        def _(half=half, blk=blk):
            for c in fetch(half, be_ref[blk]):
                c.start()

        prev = jnp.maximum(blk - 2, 0)

        @pl.when(jnp.logical_and(live_step,
                                 jnp.logical_or(p == 0, be_ref[blk] != be_ref[prev])))
        def _(half=half, blk=blk):
            for c in fetch(half, be_ref[blk]):
                c.wait()
            for src, dst in zip(f32_w[half], bf_w[half]):
                dst[...] = src[...].astype(bf16)

            @pl.when(nx_ref[blk] >= 0)
            def _():
                for c in fetch(half, nx_ref[blk]):
                    c.start()

    @pl.when(live_step)
    def _():
        xs = []
        for half, blk in enumerate(blocks):
            rows = jnp.where(blk < nb_ref[0], nv_ref[blk], 0)
            live = lax.broadcasted_iota(i32, (BM, WORDS), 0) < rows
            words = _slab_chunks(xs_ref, BM, half * BM)
            xs.append(_unpack_rows(jnp.where(live, words, u32(0))))

        def up(half, w_ref):
            lo, hi = xs[half]
            return (jnp.dot(lo, w_ref[:WORDS, :], preferred_element_type=f32)
                    + jnp.dot(hi, w_ref[WORDS:, :], preferred_element_type=f32))

        gs = [up(half, bf_w[half][0]) for half in range(2)]
        us = [up(half, bf_w[half][1]) for half in range(2)]
        hs = [(g * jax.nn.sigmoid(g) * u).astype(bf16) for g, u in zip(gs, us)]
        ys = [jnp.dot(h, bf_w[half][2][...], preferred_element_type=f32)
              for half, h in enumerate(hs)]
        for half in range(2):
            _store_slab_chunks(ys_ref, _pack_rows(ys[half]), BM, half * BM)


def _experts(block_e, next_e, block_rows, nb_used, xs, wg, wu, wd):
    P = xs.shape[0] // CHUNKS
    nb = P // BM
    assert nb % 2 == 0
    last = lambda nbu: jnp.maximum((nbu[0] - 1) // 2, 0)
    rows = lambda p, be, nx, nv, nbu: (jnp.minimum(p, last(nbu)), 0)
    any_spec = pl.BlockSpec(memory_space=pl.ANY)
    up_shape, down_shape = (D_MODEL, EXPERT_H), (EXPERT_H, D_MODEL)
    per_stream = lambda dt: [pltpu.VMEM(up_shape, dt), pltpu.VMEM(up_shape, dt),
                             pltpu.VMEM(down_shape, dt)]
    gs = pltpu.PrefetchScalarGridSpec(
        num_scalar_prefetch=4,
        grid=(nb // 2,),
        in_specs=[pl.BlockSpec((2 * BM * CHUNKS, LANES), rows), any_spec, any_spec, any_spec],
        out_specs=pl.BlockSpec((2 * BM * CHUNKS, LANES), rows),
        scratch_shapes=per_stream(f32) * 2 + per_stream(bf16) * 2
        + [pltpu.SemaphoreType.DMA((2,))],
    )
    return pl.pallas_call(
        _experts_body,
        grid_spec=gs,
        out_shape=jax.ShapeDtypeStruct((P * CHUNKS, LANES), u32),
        compiler_params=_cparams(("arbitrary",)),
        name="experts",
    )(block_e, next_e, block_rows, nb_used, xs, wg, wu, wd)


def _combine_body(dest_ref, dnext_ref, gate_ref, pre_ref, ys_ref, ysflat_ref,
                  g2_ref, b2_ref, out_ref, gb00, gb01, gb10, gb11, sems):
    tm = TM_CMB
    j = pl.program_id(0)
    n = pl.num_programs(0)
    gbufs = ((gb00, gb01), (gb10, gb11))

    def issue(d_ref, st, half):
        base = half * tm

        def step(t, carry):
            for k in range(TOP_K):
                src = ys_ref.at[d_ref[(base + t) * TOP_K + k]]
                dst = gbufs[st][half].at[pl.ds((k * tm + t) * CHUNKS, CHUNKS)]
                pltpu.make_async_copy(src, dst, sems.at[st, half]).start(priority=k % DMA_THREADS)
            return carry
        lax.fori_loop(0, tm, step, 0)

    def finish(st, half):
        rows = pl.ds(half * tm, tm)
        flat = gbufs[st][half]
        pltpu.make_async_copy(ysflat_ref.at[pl.ds(0, TOP_K * tm * CHUNKS)], flat,
                              sems.at[st, half]).wait()

        gates = gate_ref[rows, :]
        lo = [jnp.zeros((tm, LANES), f32) for _ in range(CHUNKS)]
        hi = [jnp.zeros((tm, LANES), f32) for _ in range(CHUNKS)]
        for k in range(TOP_K):
            gk = jnp.broadcast_to(gates[:, k:k + 1], (tm, LANES))
            for c in range(CHUNKS):
                w = flat[pl.ds(k * tm * CHUNKS + c, tm, stride=CHUNKS), :]
                lo[c] = lo[c] + gk * lax.bitcast_convert_type(w << 16, f32)
                hi[c] = hi[c] + gk * lax.bitcast_convert_type(w & u32(0xFFFF0000), f32)
        routed = jnp.concatenate(lo + hi, axis=1)
        out_ref[rows, :] = _layer_norm(pre_ref[rows, :] + routed, g2_ref[...], b2_ref[...])

    @pl.when(j == 0)
    def _():
        issue(dest_ref, 0, 0)
        issue(dest_ref, 0, 1)

    for st in range(2):
        @pl.when(lax.rem(j, 2) == st)
        def _(st=st):
            @pl.when(j + 1 < n)
            def _():
                issue(dnext_ref, 1 - st, 0)
                issue(dnext_ref, 1 - st, 1)

            finish(st, 0)
            finish(st, 1)


def _combine(dest, gate, pre, ys, g2, b2):
    T = pre.shape[0]
    tm = TM_CMB
    n = T // (2 * tm)
    tok = lambda j: (j, 0)
    fixed = lambda j: (0, 0)
    tbl = pl.BlockSpec((2 * tm * TOP_K,), lambda j: (j,), memory_space=pltpu.SMEM)
    tbl_next = pl.BlockSpec((2 * tm * TOP_K,), lambda j: (jnp.minimum(j + 1, n - 1),),
                            memory_space=pltpu.SMEM)
    any_spec = pl.BlockSpec(memory_space=pl.ANY)
    return pl.pallas_call(
        _combine_body,
        grid=(n,),
        in_specs=[tbl, tbl_next, pl.BlockSpec((2 * tm, TOP_K), tok),
                  pl.BlockSpec((2 * tm, D_MODEL), tok), any_spec, any_spec,
                  pl.BlockSpec((1, D_MODEL), fixed), pl.BlockSpec((1, D_MODEL), fixed)],
        out_specs=pl.BlockSpec((2 * tm, D_MODEL), tok),
        out_shape=jax.ShapeDtypeStruct((T, D_MODEL), f32),
        scratch_shapes=[pltpu.VMEM((TOP_K * tm * CHUNKS, LANES), u32)] * 4
        + [pltpu.SemaphoreType.DMA((2, 2))],
        compiler_params=_cparams(("arbitrary",)),
        name="combine",
    )(dest, dest, gate, pre, ys, ys.reshape(ys.shape[0] * CHUNKS, LANES), g2, b2)


def _rope_inv_freq():
    half = ROT_DIM // 2
    inv = ROPE_THETA ** (-jnp.arange(half, dtype=f32) * 2.0 / ROT_DIM)
    j = np.arange(LANES) % HEAD_DIM
    table = jnp.where(j < ROT_DIM, inv[j % half], 0.0)
    return table.reshape(1, LANES).astype(f32)


def _gate_weights(ga_w, ga_b, gx_w, gx_b):
    ng = REC_W // LANES
    per = LANES // HEAD_DIM
    def bd(w):
        w = w.reshape(ng, per, HEAD_DIM, HEAD_DIM)
        z = jnp.zeros((ng, LANES, LANES), w.dtype)
        for p in range(per):
            z = z.at[:, p * HEAD_DIM:(p + 1) * HEAD_DIM, p * HEAD_DIM:(p + 1) * HEAD_DIM].set(w[:, p])
        return z
    wg = jnp.concatenate([bd(ga_w[0]), bd(gx_w[0]), bd(ga_w[1]), bd(gx_w[1])], axis=-1).astype(bf16)
    grp = lambda b: b.reshape(ng, 1, LANES)
    gb = jnp.concatenate([grp(ga_b[0]), grp(gx_b[0]), grp(ga_b[1]), grp(gx_b[1])], axis=-1)
    return wg, gb


def _layer(x, positions, w_in, attn_gain, conv_w, conv_b, ga_w, ga_b, gx_w, gx_b, lam, rec_gain,
           w_out, ln1_g, ln1_b, router_w, router_bias, e_wg, e_wu, e_wd, s_wg, s_wu, s_wd,
           ln2_g, ln2_b, alpha):
    B, S, _ = x.shape
    T = B * S
    x2 = x.reshape(T, D_MODEL)
    pos2 = positions.reshape(T, 1)

    q, k, v, rx, rg = _inproj(x2, pos2, w_in.astype(bf16), _rope_inv_freq())
    attn = _attention(q, k, v, B, S)
    wg, gb = _gate_weights(ga_w, ga_b, gx_w, gx_b)
    rec = _rec(rx, rg, conv_w, conv_b.reshape(1, REC_W), wg, gb, lam, B, S).reshape(T, REC_W)

    rw_t = router_w.T
    rw_hi = rw_t.astype(bf16)
    rw_lo = (rw_t - rw_hi.astype(f32)).astype(bf16)
    rbias = jnp.broadcast_to(router_bias.reshape(N_EXPERTS, 1), (N_EXPERTS, LANES))
    pre, x1p, idx, gate, rank, cnt = _mixout(
        attn, rec, x2,
        attn_gain.reshape(1, ATTN_W), rec_gain.reshape(1, REC_W), w_out.astype(bf16),
        ln1_g.reshape(1, D_MODEL), ln1_b.reshape(1, D_MODEL), rw_hi, rw_lo, rbias,
        s_wg.astype(bf16), s_wu.astype(bf16), s_wd.astype(bf16), alpha)

    counts = cnt[:, 0].astype(i32)
    padded = (counts + BM - 1) // BM * BM
    pad_end = jnp.cumsum(padded)
    pad_start = pad_end - padded
    slot = _slots(idx, rank, pad_start)
    nb = (T * TOP_K + N_EXPERTS * (BM - 1)) // BM + 1
    nb += nb % 2
    nb_used = pad_end[-1] // BM
    h = (nb_used + 1) // 2
    pos = jnp.arange(nb, dtype=i32)
    blk = jnp.where(pos % 2 == 0, pos // 2, h + pos // 2)
    live = jnp.logical_and(pos < 2 * h, blk < nb_used)
    first_row = blk * BM
    block_e = jnp.minimum(jnp.sum((pad_end[None, :] <= first_row[:, None]).astype(i32), axis=1),
                          N_EXPERTS - 1)
    block_e = jnp.where(live, block_e, N_EXPERTS - 1)
    block_rows = jnp.where(live, jnp.clip((pad_start + counts)[block_e] - first_row, 0, BM), 0)
    pairs = block_e.reshape(nb // 2, 2)
    step_id = jnp.arange(nb // 2, dtype=i32)[:, None]
    change = jnp.concatenate([pairs[1:] != pairs[:-1], jnp.zeros((1, 2), bool)], axis=0)
    change = jnp.logical_and(change, step_id + 1 < h)
    nxt_step = lax.cummin(jnp.where(change, step_id + 1, nb), axis=0, reverse=True)
    next_e = jnp.where(nxt_step < nb // 2,
                       jnp.take_along_axis(pairs, jnp.minimum(nxt_step, nb // 2 - 1), axis=0), -1)
    next_e = next_e.reshape(nb).astype(i32)
    sblk = slot // BM
    dest = jnp.where(sblk < h, 2 * sblk, 2 * (sblk - h) + 1) * BM + slot % BM
    nb_used = (2 * h).astype(i32).reshape(1)

    xs = _dispatch(dest, x1p.reshape(T, CHUNKS, LANES), nb * BM)
    ys = _experts(block_e.astype(i32), next_e, block_rows.astype(i32), nb_used,
                  xs.reshape(nb * BM * CHUNKS, LANES), e_wg, e_wu, e_wd)
    ys = ys.reshape(nb * BM, CHUNKS, LANES)
    out = _combine(dest.T.reshape(-1), gate.T, pre, ys,
                   ln2_g.reshape(1, D_MODEL), ln2_b.reshape(1, D_MODEL))
    return out.reshape(B, S, D_MODEL)


def kernel(x, positions, w_in, attn_gain, rec_conv_w, rec_conv_b, rec_gate_a_w, rec_gate_a_b,
           rec_gate_x_w, rec_gate_x_b, rec_lambda, rec_gain, w_out, ln1_g, ln1_b, router_w,
           router_bias, exp_w_gate, exp_w_up, exp_w_down, shared_w_gate, shared_w_up,
           shared_w_down, ln2_g, ln2_b):
    depth = w_in.shape[0]
    alpha = (2 * depth) ** 0.25
    for l in range(depth):
        x = _layer(x, positions, w_in[l], attn_gain[l], rec_conv_w[l], rec_conv_b[l],
                   rec_gate_a_w[l], rec_gate_a_b[l], rec_gate_x_w[l], rec_gate_x_b[l],
                   rec_lambda[l], rec_gain[l], w_out[l], ln1_g[l], ln1_b[l], router_w[l],
                   router_bias[l], exp_w_gate[l], exp_w_up[l], exp_w_down[l], shared_w_gate[l],
                   shared_w_up[l], shared_w_down[l], ln2_g[l], ln2_b[l], alpha)
    return x
```

```python
import functools

import jax
import jax.numpy as jnp
import numpy as np
from jax import lax
from jax.experimental import pallas as pl
from jax.experimental.pallas import tpu as pltpu

f32 = jnp.float32
bf16 = jnp.bfloat16
i32 = jnp.int32
u32 = jnp.uint32

D_MODEL = 1024
ATTN_W = 512
REC_W = 512
HEAD_DIM = 64
ROT_DIM = 16
ROPE_THETA = 500000.0
BRANCHES = ((128, 1), (512, 4), (2048, 16))
HALF_BAND = 64
CONV_W = 4
RG_LRU_C = 8.0
N_EXPERTS = 256
TOP_K = 8
N_GROUPS = 8
GROUP_SZ = N_EXPERTS // N_GROUPS
TOPK_GROUPS = 4
EXPERT_H = 256
ROUTED_SCALE = 2.5
LN_EPS = 1e-5
NEG = -1e30

LANES = 128
SUBLANES = 8
WORDS = D_MODEL // 2
CHUNKS = WORDS // LANES
VMEM_LIMIT = 56 * 1024 * 1024
DMA_THREADS = 2

TM_IN = 512
TQ = 128
KW = TQ + 2 * HALF_BAND
SB = TQ * max(d for _, d in BRANCHES)
MERGE_ROWS = 256
TS = 256
TM_MIX = 512
TM_SLOT = 2048
TM_CMB = 256
BM = 256


def _cparams(sem):
    return pltpu.CompilerParams(dimension_semantics=sem, vmem_limit_bytes=VMEM_LIMIT)


def _inproj_body(x_ref, pos_ref, w_ref, invf_ref, q_ref, k_ref, v_ref, rx_ref, rg_ref):
    xb = x_ref[...].astype(bf16)
    ang = pos_ref[...].astype(f32) * invf_ref[...]
    cos = jnp.cos(ang)
    sin = jnp.sin(ang)
    j = lax.broadcasted_iota(i32, (1, LANES), 1) % HEAD_DIM
    half = ROT_DIM // 2
    s_lo = jnp.where(j < half, -sin, 0.0)
    s_hi = jnp.where((j >= half) & (j < ROT_DIM), sin, 0.0)
    rep = ATTN_W // LANES
    cos_w = jnp.concatenate([cos] * rep, axis=1)
    s_lo_w = jnp.concatenate([s_lo] * rep, axis=1)
    s_hi_w = jnp.concatenate([s_hi] * rep, axis=1)

    def proj(c0, n):
        return jnp.dot(xb, w_ref[:, c0:c0 + n], preferred_element_type=f32)

    def rope(t):
        return (t * cos_w + pltpu.roll(t, ATTN_W - half, 1) * s_lo_w
                + pltpu.roll(t, half, 1) * s_hi_w)

    q_ref[...] = rope(proj(0, ATTN_W)) * (HEAD_DIM ** -0.5)
    k_ref[...] = rope(proj(ATTN_W, ATTN_W))
    v_ref[...] = proj(2 * ATTN_W, ATTN_W)
    rx_ref[...] = proj(3 * ATTN_W, REC_W)
    rg_ref[...] = proj(3 * ATTN_W + REC_W, REC_W)


def _inproj(x2, pos2, w_in_b, invf):
    T = x2.shape[0]
    tm = TM_IN
    in_w = w_in_b.shape[1]
    tok = lambda i: (i, 0)
    fixed = lambda i: (0, 0)
    return pl.pallas_call(
        _inproj_body,
        grid=(T // tm,),
        in_specs=[pl.BlockSpec((tm, D_MODEL), tok), pl.BlockSpec((tm, 1), tok),
                  pl.BlockSpec((D_MODEL, in_w), fixed), pl.BlockSpec((1, LANES), fixed)],
        out_specs=[pl.BlockSpec((tm, ATTN_W), tok)] * 3 + [pl.BlockSpec((tm, REC_W), tok)] * 2,
        out_shape=[jax.ShapeDtypeStruct((T, ATTN_W), f32)] * 3
        + [jax.ShapeDtypeStruct((T, REC_W), f32)] * 2,
        compiler_params=_cparams(("parallel",)),
        name="inproj",
    )(x2, pos2, w_in_b, invf)


def _attn_body(q_ref, k_ref, v_ref, out_ref, o1, o2, o3, l1, l2, l3, bias_ref, *, S):
    o_sc, l_sc = (o1, o2, o3), (l1, l2, l3)
    lane = lax.broadcasted_iota(i32, (1, LANES), 1)
    head0 = lane < HEAD_DIM
    rel = (lax.broadcasted_iota(i32, (TQ, KW), 0) - lax.broadcasted_iota(i32, (TQ, KW), 1))
    tiles = SB // TQ
    for case in range(3):
        bias_ref[case] = jnp.where(jnp.abs(rel + case * HALF_BAND) <= HALF_BAND, 0.0, NEG)

    def rows(ref, start, n, d):
        return ref[pl.ds(start, n), :] if d == 1 else ref[pl.ds(start, n, stride=d), :]

    def tile(ti, n0):
        ctx = []
        for g, (_, d) in enumerate(BRANCHES):
            L = S // d
            sh = d.bit_length() - 1
            r = jnp.bitwise_and(ti, d - 1)
            m0 = jnp.right_shift(n0, sh) + jnp.right_shift(ti, sh) * TQ
            ks = jnp.clip(m0 - HALF_BAND, 0, L - KW)
            q = rows(q_ref, r + d * m0, TQ, d).astype(bf16)
            k = rows(k_ref, r + d * ks, KW, d).astype(bf16)
            v = rows(v_ref, r + d * ks, KW, d).astype(bf16)
            bias = bias_ref[(m0 - ks) // HALF_BAND]
            local = r + d * (m0 - jnp.right_shift(n0, sh))
            ss = []
            for sel in (head0, jnp.logical_not(head0)):
                qh = jnp.where(sel, q, jnp.zeros_like(q))
                ss.append(lax.dot_general(qh, k, (((1,), (1,)), ((), ())),
                                          preferred_element_type=f32))
            ctx.append((d, v, bias, local, ss))
        soft = []
        for d, v, bias, local, ss in ctx:
            ps = []
            for s in ss:
                s = s + bias
                m = jnp.max(s, axis=-1, keepdims=True)
                p = jnp.exp(s - m)
                den = jnp.sum(p, axis=-1, keepdims=True)
                ps.append((p.astype(bf16), den, m + jnp.log(den)))
            soft.append(ps)
        for g, ((d, v, bias, local, ss), ps) in enumerate(zip(ctx, soft)):
            outs = [(jnp.dot(p, v, preferred_element_type=f32) / den, lse) for p, den, lse in ps]
            o_val = jnp.where(head0, outs[0][0], outs[1][0])
            l_val = jnp.where(head0, outs[0][1], outs[1][1])
            if d == 1:
                o_sc[g][pl.ds(local, TQ), :] = o_val
                l_sc[g][pl.ds(local, TQ), :] = l_val
            else:
                o_sc[g][pl.ds(local, TQ, stride=d), :] = o_val
                l_sc[g][pl.ds(local, TQ, stride=d), :] = l_val
        return n0

    def merge(c, n0):
        sl = pl.ds(pl.multiple_of(c * MERGE_ROWS, MERGE_ROWS), MERGE_ROWS)
        ls = [l_sc[g][sl, :] for g in range(len(BRANCHES))]
        mx = functools.reduce(jnp.maximum, ls)
        es = [jnp.exp(l - mx) for l in ls]
        num = functools.reduce(lambda a, b: a + b, [e * o_sc[g][sl, :] for g, e in enumerate(es)])
        den = functools.reduce(lambda a, b: a + b, es)
        dst = pl.ds(pl.multiple_of(n0 + c * MERGE_ROWS, MERGE_ROWS), MERGE_ROWS)
        out_ref[dst, :] = (num / den).astype(bf16)
        return n0

    def superblock(sb, carry):
        n0 = pl.multiple_of(sb * SB, SB)
        lax.fori_loop(0, tiles, tile, n0)
        lax.fori_loop(0, SB // MERGE_ROWS, merge, n0)
        return carry

    lax.fori_loop(0, S // SB, superblock, 0)


def _attention(q, k, v, B, S):
    assert S % SB == 0 and all(S // d >= KW for _, d in BRANCHES)
    hp = ATTN_W // LANES
    view = lambda t: t.reshape(B, S, ATTN_W)
    spec = pl.BlockSpec((None, S, LANES), lambda b, h: (b, 0, h))
    out = pl.pallas_call(
        functools.partial(_attn_body, S=S),
        grid=(B, hp),
        in_specs=[spec, spec, spec],
        out_specs=spec,
        out_shape=jax.ShapeDtypeStruct((B, S, ATTN_W), bf16),
        scratch_shapes=[pltpu.VMEM((SB, LANES), f32)] * (2 * len(BRANCHES))
        + [pltpu.VMEM((3, TQ, KW), f32)],
        compiler_params=_cparams(("parallel", "parallel")),
        name="attention",
    )(view(q), view(k), view(v))
    return out.reshape(B * S, ATTN_W)


def _rec_body(rx_ref, rg_ref, cw_ref, cb_ref, wg_ref, gb_ref, lam_ref, out_ref, rxp_ref, hf_ref, *, S):
    nch = S // TS
    ntile = TS // SUBLANES
    zeros8 = jnp.zeros((SUBLANES, LANES), f32)
    rxp_ref[pl.ds(0, SUBLANES), :] = zeros8
    rxp_ref[pl.ds(S + SUBLANES, SUBLANES), :] = zeros8

    def pad_copy(c, carry):
        t0 = pl.multiple_of(c * TS, TS)
        rxp_ref[pl.ds(t0 + SUBLANES, TS), :] = rx_ref[pl.ds(t0, TS), :]
        return carry

    lax.fori_loop(0, nch, pad_copy, 0)

    lam = lam_ref[...]
    neg_sp = -RG_LRU_C * (jnp.maximum(-lam, 0.0) + jnp.log1p(jnp.exp(-jnp.abs(lam))))
    cw = cw_ref[...]
    cb = cb_ref[...]
    sub = lax.broadcasted_iota(i32, (ntile, SUBLANES, LANES), 1)
    nrow = TS + 2 * SUBLANES

    def gates(ci, d):
        t0 = pl.multiple_of(ci * TS, TS)
        xw = rxp_ref[pl.ds(t0, nrow), :]
        u = (cw[0:1] * pltpu.roll(xw, 2, 0) + cw[1:2] * pltpu.roll(xw, 1, 0) + cw[2:3] * xw
             + cw[3:4] * pltpu.roll(xw, nrow - 1, 0))[SUBLANES:SUBLANES + TS] + cb
        c0 = d * 2 * LANES
        g = jnp.dot(u.astype(bf16), wg_ref[:, c0:c0 + 2 * LANES], preferred_element_type=f32)
        g = g + gb_ref[:, c0:c0 + 2 * LANES]
        r = jax.nn.sigmoid(g[:, :LANES])
        gi = jax.nn.sigmoid(g[:, LANES:])
        a = jnp.exp(neg_sp[d:d + 1] * r)
        b = jnp.sqrt(1.0 - a * a) * gi * u
        return a.reshape(ntile, SUBLANES, LANES), b.reshape(ntile, SUBLANES, LANES)

    def chunk(i, carry):
        cf, cbk = carry
        a, b = gates(i, 0)
        for s in (1, 2, 4):
            ok = sub >= s
            a_s = pltpu.roll(a, s, 1)
            b_s = pltpu.roll(b, s, 1)
            b = jnp.where(ok, a * b_s + b, b)
            a = jnp.where(ok, a * a_s, a)
        t0 = pl.multiple_of(i * TS, TS)
        for j in range(ntile):
            h = a[j] * cf + b[j]
            hf_ref[pl.ds(t0 + j * SUBLANES, SUBLANES), :] = h
            cf = h[SUBLANES - 1:SUBLANES, :]
        ib = nch - 1 - i
        a, b = gates(ib, 1)
        for s in (1, 2, 4):
            ok = sub < SUBLANES - s
            a_s = pltpu.roll(a, SUBLANES - s, 1)
            b_s = pltpu.roll(b, SUBLANES - s, 1)
            b = jnp.where(ok, a * b_s + b, b)
            a = jnp.where(ok, a * a_s, a)
        t0 = pl.multiple_of(ib * TS, TS)
        for j in range(ntile - 1, -1, -1):
            h = a[j] * cbk + b[j]
            out_ref[pl.ds(t0 + j * SUBLANES, SUBLANES), :] = h
            cbk = h[0:1, :]
        return cf, cbk

    zrow = jnp.zeros((1, LANES), f32)
    lax.fori_loop(0, nch, chunk, (zrow, zrow))

    def finish(c, carry):
        t0 = pl.multiple_of(c * TS, TS)
        sl = pl.ds(t0, TS)
        out_ref[sl, :] = (hf_ref[sl, :] + out_ref[sl, :]) * jax.nn.gelu(rg_ref[sl, :])
        return carry

    lax.fori_loop(0, nch, finish, 0)


def _rec(rx, rg, conv_w, conv_b, wg, gb, lam, B, S):
    ng = REC_W // LANES
    assert S % TS == 0
    seq = pl.BlockSpec((None, S, LANES), lambda b, c: (b, 0, c))
    return pl.pallas_call(
        functools.partial(_rec_body, S=S),
        grid=(B, ng),
        in_specs=[seq, seq,
                  pl.BlockSpec((CONV_W, LANES), lambda b, c: (0, c)),
                  pl.BlockSpec((1, LANES), lambda b, c: (0, c)),
                  pl.BlockSpec((None, LANES, 4 * LANES), lambda b, c: (c, 0, 0)),
                  pl.BlockSpec((None, 1, 4 * LANES), lambda b, c: (c, 0, 0)),
                  pl.BlockSpec((2, LANES), lambda b, c: (0, c))],
        out_specs=seq,
        out_shape=jax.ShapeDtypeStruct((B, S, REC_W), f32),
        scratch_shapes=[pltpu.VMEM((S + 2 * SUBLANES, LANES), f32), pltpu.VMEM((S, LANES), f32)],
        compiler_params=_cparams(("parallel", "parallel")),
        name="rec",
    )(rx.reshape(B, S, REC_W), rg.reshape(B, S, REC_W), conv_w, conv_b, wg, gb, lam)


def _pack_rows(x):
    r = lax.bitcast_convert_type(x.astype(bf16).astype(f32), u32)
    return (r[:, WORDS:] & u32(0xFFFF0000)) | (r[:, :WORDS] >> 16)


def _unpack_rows(w):
    lo = lax.bitcast_convert_type(w << 16, f32).astype(bf16)
    hi = lax.bitcast_convert_type(w & u32(0xFFFF0000), f32).astype(bf16)
    return lo, hi


def _slab_chunks(flat, rows, base=0):
    return jnp.concatenate(
        [flat[pl.ds(base * CHUNKS + c, rows, stride=CHUNKS), :] for c in range(CHUNKS)], axis=1)


def _store_slab_chunks(flat, val, rows, base=0):
    for c in range(CHUNKS):
        flat[pl.ds(base * CHUNKS + c, rows, stride=CHUNKS), :] = val[:, c * LANES:(c + 1) * LANES]


def _rms(t, gain):
    return t * lax.rsqrt(jnp.mean(t * t, axis=-1, keepdims=True) + LN_EPS) * gain


def _layer_norm(z, g, b):
    mu = jnp.mean(z, axis=-1, keepdims=True)
    zc = z - mu
    var = jnp.mean(zc * zc, axis=-1, keepdims=True)
    return zc * lax.rsqrt(var + LN_EPS) * g + b


def _mixout_body(attn_ref, rec_ref, x_ref,
                 ag_ref, rgn_ref, wo_ref, g1_ref, b1_ref, rwh_ref, rwl_ref, rb_ref, tri_ref, ones_ref,
                 sg_ref, su_ref, sd_ref, pre_ref, x1p_ref, idx_ref, gate_ref, rank_ref, cnt_ref, carry_ref, *, alpha):
    tm = TM_MIX

    @pl.when(pl.program_id(0) == 0)
    def _():
        carry_ref[...] = jnp.zeros_like(carry_ref)

    attn_n = _rms(attn_ref[...].astype(f32), ag_ref[...]).astype(bf16)
    rec_n = _rms(rec_ref[...], rgn_ref[...]).astype(bf16)
    y = (jnp.dot(attn_n, wo_ref[0:ATTN_W, :], preferred_element_type=f32)
         + jnp.dot(rec_n, wo_ref[ATTN_W:, :], preferred_element_type=f32))
    x1 = _layer_norm(alpha * x_ref[...] + y, g1_ref[...], b1_ref[...])
    _store_slab_chunks(x1p_ref, _pack_rows(x1), tm)

    hi = x1.astype(bf16)
    sg = jnp.dot(hi, sg_ref[...], preferred_element_type=f32)
    su = jnp.dot(hi, su_ref[...], preferred_element_type=f32)
    shared = jnp.dot((sg * jax.nn.sigmoid(sg) * su).astype(bf16), sd_ref[...],
                     preferred_element_type=f32)
    pre_ref[...] = alpha * x1 + shared

    lo = (x1 - hi.astype(f32)).astype(bf16)
    nt = (((1,), (1,)), ((), ()))
    logits = (lax.dot_general(rwh_ref[...], hi, nt, preferred_element_type=f32)
              + lax.dot_general(rwh_ref[...], lo, nt, preferred_element_type=f32)
              + lax.dot_general(rwl_ref[...], hi, nt, preferred_element_type=f32))
    scores = jax.nn.sigmoid(logits)
    biased = scores + jnp.concatenate([rb_ref[...]] * (tm // LANES), axis=1)

    rid = lax.broadcasted_iota(i32, (GROUP_SZ, tm), 0).astype(f32)
    grp = []
    for g in range(N_GROUPS):
        vg = biased[g * GROUP_SZ:(g + 1) * GROUP_SZ, :]
        m1 = jnp.max(vg, axis=0, keepdims=True)
        first = jnp.min(jnp.where(vg == m1, rid, float(GROUP_SZ)), axis=0, keepdims=True)
        m2 = jnp.max(jnp.where(rid == first, -jnp.inf, vg), axis=0, keepdims=True)
        grp.append(m1 + m2)
    eid = lax.broadcasted_iota(i32, (N_EXPERTS, tm), 0).astype(f32)
    keep = [jnp.zeros((1, tm), f32) for _ in range(N_GROUPS)]
    for _ in range(TOPK_GROUPS):
        gm = functools.reduce(jnp.maximum, grp)
        gi = jnp.full((1, tm), float(N_GROUPS), f32)
        for g in range(N_GROUPS - 1, -1, -1):
            gi = jnp.where(grp[g] == gm, float(g), gi)
        hits = [gi == float(g) for g in range(N_GROUPS)]
        grp = [jnp.where(hit, -jnp.inf, sc) for hit, sc in zip(hits, grp)]
        keep = [jnp.where(hit, 1.0, kp) for hit, kp in zip(hits, keep)]
    masked = jnp.concatenate(
        [jnp.where(jnp.broadcast_to(keep[g], (GROUP_SZ, tm)) > 0.5,
                   biased[g * GROUP_SZ:(g + 1) * GROUP_SZ, :], -jnp.inf)
         for g in range(N_GROUPS)], axis=0)

    onehot = jnp.zeros((N_EXPERTS, tm), f32)
    idxs, gts = [], []
    for _ in range(TOP_K):
        mx = jnp.max(masked, axis=0, keepdims=True)
        ix = jnp.min(jnp.where(masked == mx, eid, float(N_EXPERTS)), axis=0, keepdims=True)
        hit = eid == ix
        gts.append(jnp.sum(jnp.where(hit, scores, 0.0), axis=0, keepdims=True))
        idxs.append(ix)
        masked = jnp.where(hit, -jnp.inf, masked)
        onehot = onehot + jnp.where(hit, 1.0, 0.0)
    gsum = functools.reduce(lambda p, q: p + q, gts)
    for kk in range(TOP_K):
        idx_ref[kk:kk + 1, :] = idxs[kk].astype(i32)
        gate_ref[kk:kk + 1, :] = gts[kk] / gsum * ROUTED_SCALE

    oh = onehot.astype(bf16)
    before = carry_ref[...] + jnp.dot(oh, tri_ref[...], preferred_element_type=f32)
    for kk in range(TOP_K):
        rk = jnp.sum(jnp.where(eid == idxs[kk], before, 0.0), axis=0, keepdims=True)
        rank_ref[kk:kk + 1, :] = rk.astype(i32)
    total = carry_ref[...] + jnp.dot(oh, ones_ref[...], preferred_element_type=f32)
    carry_ref[...] = total
    cnt_ref[...] = total


def _mixout(attn, rec, x2, attn_gain, rec_gain, w_out_b, g1, b1, rw_hi, rw_lo, rbias, sg, su, sd,
            alpha):
    T = x2.shape[0]
    tm = TM_MIX
    tok = lambda i: (i, 0)
    fixed = lambda i: (0, 0)
    tri = jnp.asarray(np.triu(np.ones((tm, tm), np.float32), k=1), bf16)
    ones = jnp.ones((tm, tm), bf16)
    aw = pl.BlockSpec((tm, ATTN_W), tok)
    row = lambda n: pl.BlockSpec((1, n), fixed)
    kt = pl.BlockSpec((TOP_K, tm), lambda i: (0, i))
    return pl.pallas_call(
        functools.partial(_mixout_body, alpha=alpha),
        grid=(T // tm,),
        in_specs=[aw] * 2 + [pl.BlockSpec((tm, D_MODEL), tok), row(ATTN_W), row(REC_W),
                             pl.BlockSpec((D_MODEL, D_MODEL), fixed), row(D_MODEL), row(D_MODEL),
                             pl.BlockSpec((N_EXPERTS, D_MODEL), fixed),
                             pl.BlockSpec((N_EXPERTS, D_MODEL), fixed),
                             pl.BlockSpec((N_EXPERTS, LANES), fixed),
                             pl.BlockSpec((tm, tm), fixed), pl.BlockSpec((tm, tm), fixed),
                             pl.BlockSpec(sg.shape, fixed), pl.BlockSpec(su.shape, fixed),
                             pl.BlockSpec(sd.shape, fixed)],
        out_specs=[pl.BlockSpec((tm, D_MODEL), tok),
                   pl.BlockSpec((tm * CHUNKS, LANES), tok),
                   kt, kt, kt, pl.BlockSpec((N_EXPERTS, tm), fixed)],
        out_shape=[jax.ShapeDtypeStruct((T, D_MODEL), f32),
                   jax.ShapeDtypeStruct((T * CHUNKS, LANES), u32),
                   jax.ShapeDtypeStruct((TOP_K, T), i32),
                   jax.ShapeDtypeStruct((TOP_K, T), f32),
                   jax.ShapeDtypeStruct((TOP_K, T), i32),
                   jax.ShapeDtypeStruct((N_EXPERTS, tm), f32)],
        scratch_shapes=[pltpu.VMEM((N_EXPERTS, tm), f32)],
        compiler_params=_cparams(("arbitrary",)),
        name="mixout",
    )(attn, rec, x2, attn_gain, rec_gain, w_out_b, g1, b1,
      rw_hi, rw_lo, rbias, tri, ones, sg, su, sd)


def _slots_body(idx_ref, rank_ref, start_ref, dest_ref):
    tm = TM_SLOT
    eid = lax.broadcasted_iota(i32, (N_EXPERTS, tm), 0)
    start = jnp.concatenate([start_ref[...]] * (tm // LANES), axis=1)
    for kk in range(TOP_K):
        hit = eid == idx_ref[kk:kk + 1, :]
        base = jnp.sum(jnp.where(hit, start, 0.0), axis=0, keepdims=True)
        dest_ref[kk:kk + 1, :] = base.astype(i32) + rank_ref[kk:kk + 1, :]


def _slots(idx, rank, pad_start):
    T = idx.shape[1]
    tm = TM_SLOT
    kt = pl.BlockSpec((TOP_K, tm), lambda i: (0, i))
    start = jnp.broadcast_to(pad_start.astype(f32).reshape(N_EXPERTS, 1), (N_EXPERTS, LANES))
    return pl.pallas_call(
        _slots_body,
        grid=(T // tm,),
        in_specs=[kt, kt, pl.BlockSpec((N_EXPERTS, LANES), lambda i: (0, 0))],
        out_specs=kt,
        out_shape=jax.ShapeDtypeStruct((TOP_K, T), i32),
        compiler_params=_cparams(("parallel",)),
        name="slots",
    )(idx, rank, start)


def _dispatch_body(dest_ref, x_ref, xs_ref, sem):
    tm = TM_MIX

    def copy(t, k):
        return pltpu.make_async_copy(x_ref.at[t], xs_ref.at[dest_ref[k, t]], sem)

    def issue(t, carry):
        for k in range(TOP_K):
            copy(t, k).start(priority=k % DMA_THREADS)
        return carry

    def drain(t, carry):
        for k in range(TOP_K):
            copy(t, k).wait()
        return carry

    lax.fori_loop(0, tm, issue, 0)
    lax.fori_loop(0, tm, drain, 0)


def _dispatch(dest, x1p, n_slots):
    T = x1p.shape[0]
    tm = TM_MIX
    return pl.pallas_call(
        _dispatch_body,
        grid=(T // tm,),
        in_specs=[pl.BlockSpec((TOP_K, tm), lambda i: (0, i), memory_space=pltpu.SMEM),
                  pl.BlockSpec((tm, CHUNKS, LANES), lambda i: (i, 0, 0))],
        out_specs=pl.BlockSpec(memory_space=pl.ANY),
        out_shape=jax.ShapeDtypeStruct((n_slots, CHUNKS, LANES), u32),
        scratch_shapes=[pltpu.SemaphoreType.DMA(())],
        compiler_params=_cparams(("arbitrary",)),
        name="dispatch",
    )(dest, x1p)


def _experts_body(be_ref, nx_ref, nv_ref, nb_ref, xs_ref, wg_hbm, wu_hbm, wd_hbm, ys_ref,
                  fga, fua, fda, fgb, fub, fdb, sga, sua, sda, sgb, sub, sdb, sems):
    p = pl.program_id(0)
    blocks = (2 * p, 2 * p + 1)
    hbm_w = (wg_hbm, wu_hbm, wd_hbm)
    f32_w = ((fga, fua, fda), (fgb, fub, fdb))
    bf_w = ((sga, sua, sda), (sgb, sub, sdb))
    live_step = blocks[0] < nb_ref[0]

    def fetch(half, e):
        return [pltpu.make_async_copy(src.at[e], dst, sems.at[half])
                for src, dst in zip(hbm_w, f32_w[half])]

    for half, blk in enumerate(blocks):
        @pl.when(jnp.logical_and(live_step, p == 0))
        def _(half=half, blk=blk):
            for c in fetch(half, be_ref[blk]):
                c.start()

        prev = jnp.maximum(blk - 2, 0)

        @pl.when(jnp.logical_and(live_step,
                                 jnp.logical_or(p == 0, be_ref[blk] != be_ref[prev])))
        def _(half=half, blk=blk):
            for c in fetch(half, be_ref[blk]):
                c.wait()
            for src, dst in zip(f32_w[half], bf_w[half]):
                dst[...] = src[...].astype(bf16)

            @pl.when(nx_ref[blk] >= 0)
            def _():
                for c in fetch(half, nx_ref[blk]):
                    c.start()

    @pl.when(live_step)
    def _():
        xs = []
        for half, blk in enumerate(blocks):
            rows = jnp.where(blk < nb_ref[0], nv_ref[blk], 0)
            live = lax.broadcasted_iota(i32, (BM, WORDS), 0) < rows
            words = _slab_chunks(xs_ref, BM, half * BM)
            xs.append(_unpack_rows(jnp.where(live, words, u32(0))))

        def up(half, w_ref):
            lo, hi = xs[half]
            return (jnp.dot(lo, w_ref[:WORDS, :], preferred_element_type=f32)
                    + jnp.dot(hi, w_ref[WORDS:, :], preferred_element_type=f32))

        gs = [up(half, bf_w[half][0]) for half in range(2)]
        us = [up(half, bf_w[half][1]) for half in range(2)]
        hs = [(g * jax.nn.sigmoid(g) * u).astype(bf16) for g, u in zip(gs, us)]
        ys = [jnp.dot(h, bf_w[half][2][...], preferred_element_type=f32)
              for half, h in enumerate(hs)]
        for half in range(2):
            _store_slab_chunks(ys_ref, _pack_rows(ys[half]), BM, half * BM)


def _experts(block_e, next_e, block_rows, nb_used, xs, wg, wu, wd):
    P = xs.shape[0] // CHUNKS
    nb = P // BM
    assert nb % 2 == 0
    last = lambda nbu: jnp.maximum((nbu[0] - 1) // 2, 0)
    rows = lambda p, be, nx, nv, nbu: (jnp.minimum(p, last(nbu)), 0)
    any_spec = pl.BlockSpec(memory_space=pl.ANY)
    up_shape, down_shape = (D_MODEL, EXPERT_H), (EXPERT_H, D_MODEL)
    per_stream = lambda dt: [pltpu.VMEM(up_shape, dt), pltpu.VMEM(up_shape, dt),
                             pltpu.VMEM(down_shape, dt)]
    gs = pltpu.PrefetchScalarGridSpec(
        num_scalar_prefetch=4,
        grid=(nb // 2,),
        in_specs=[pl.BlockSpec((2 * BM * CHUNKS, LANES), rows), any_spec, any_spec, any_spec],
        out_specs=pl.BlockSpec((2 * BM * CHUNKS, LANES), rows),
        scratch_shapes=per_stream(f32) * 2 + per_stream(bf16) * 2
        + [pltpu.SemaphoreType.DMA((2,))],
    )
    return pl.pallas_call(
        _experts_body,
        grid_spec=gs,
        out_shape=jax.ShapeDtypeStruct((P * CHUNKS, LANES), u32),
        compiler_params=_cparams(("arbitrary",)),
        name="experts",
    )(block_e, next_e, block_rows, nb_used, xs, wg, wu, wd)


def _combine_body(dest_ref, dnext_ref, gate_ref, pre_ref, ys_ref, ysflat_ref,
                  g2_ref, b2_ref, out_ref, gb00, gb01, gb10, gb11, sems):
    tm = TM_CMB
    j = pl.program_id(0)
    n = pl.num_programs(0)
    gbufs = ((gb00, gb01), (gb10, gb11))

    def issue(d_ref, st, half):
        base = half * tm

        def step(t, carry):
            for k in range(TOP_K):
                src = ys_ref.at[d_ref[(base + t) * TOP_K + k]]
                dst = gbufs[st][half].at[pl.ds((k * tm + t) * CHUNKS, CHUNKS)]
                pltpu.make_async_copy(src, dst, sems.at[st, half]).start(priority=k % DMA_THREADS)
            return carry
        lax.fori_loop(0, tm, step, 0)

    def finish(st, half):
        rows = pl.ds(half * tm, tm)
        flat = gbufs[st][half]
        pltpu.make_async_copy(ysflat_ref.at[pl.ds(0, TOP_K * tm * CHUNKS)], flat,
                              sems.at[st, half]).wait()

        gates = gate_ref[rows, :]
        lo = [jnp.zeros((tm, LANES), f32) for _ in range(CHUNKS)]
        hi = [jnp.zeros((tm, LANES), f32) for _ in range(CHUNKS)]
        for k in range(TOP_K):
            gk = jnp.broadcast_to(gates[:, k:k + 1], (tm, LANES))
            for c in range(CHUNKS):
                w = flat[pl.ds(k * tm * CHUNKS + c, tm, stride=CHUNKS), :]
                lo[c] = lo[c] + gk * lax.bitcast_convert_type(w << 16, f32)
                hi[c] = hi[c] + gk * lax.bitcast_convert_type(w & u32(0xFFFF0000), f32)
        routed = jnp.concatenate(lo + hi, axis=1)
        out_ref[rows, :] = _layer_norm(pre_ref[rows, :] + routed, g2_ref[...], b2_ref[...])

    @pl.when(j == 0)
    def _():
        issue(dest_ref, 0, 0)
        issue(dest_ref, 0, 1)

    for st in range(2):
        @pl.when(lax.rem(j, 2) == st)
        def _(st=st):
            @pl.when(j + 1 < n)
            def _():
                issue(dnext_ref, 1 - st, 0)
                issue(dnext_ref, 1 - st, 1)

            finish(st, 0)
            finish(st, 1)


def _combine(dest, gate, pre, ys, g2, b2):
    T = pre.shape[0]
    tm = TM_CMB
    n = T // (2 * tm)
    tok = lambda j: (j, 0)
    fixed = lambda j: (0, 0)
    tbl = pl.BlockSpec((2 * tm * TOP_K,), lambda j: (j,), memory_space=pltpu.SMEM)
    tbl_next = pl.BlockSpec((2 * tm * TOP_K,), lambda j: (jnp.minimum(j + 1, n - 1),),
                            memory_space=pltpu.SMEM)
    any_spec = pl.BlockSpec(memory_space=pl.ANY)
    return pl.pallas_call(
        _combine_body,
        grid=(n,),
        in_specs=[tbl, tbl_next, pl.BlockSpec((2 * tm, TOP_K), tok),
                  pl.BlockSpec((2 * tm, D_MODEL), tok), any_spec, any_spec,
                  pl.BlockSpec((1, D_MODEL), fixed), pl.BlockSpec((1, D_MODEL), fixed)],
        out_specs=pl.BlockSpec((2 * tm, D_MODEL), tok),
        out_shape=jax.ShapeDtypeStruct((T, D_MODEL), f32),
        scratch_shapes=[pltpu.VMEM((TOP_K * tm * CHUNKS, LANES), u32)] * 4
        + [pltpu.SemaphoreType.DMA((2, 2))],
        compiler_params=_cparams(("arbitrary",)),
        name="combine",
    )(dest, dest, gate, pre, ys, ys.reshape(ys.shape[0] * CHUNKS, LANES), g2, b2)


def _rope_inv_freq():
    half = ROT_DIM // 2
    inv = ROPE_THETA ** (-jnp.arange(half, dtype=f32) * 2.0 / ROT_DIM)
    j = np.arange(LANES) % HEAD_DIM
    table = jnp.where(j < ROT_DIM, inv[j % half], 0.0)
    return table.reshape(1, LANES).astype(f32)


def _gate_weights(ga_w, ga_b, gx_w, gx_b):
    ng = REC_W // LANES
    per = LANES // HEAD_DIM
    def bd(w):
        w = w.reshape(ng, per, HEAD_DIM, HEAD_DIM)
        z = jnp.zeros((ng, LANES, LANES), w.dtype)
        for p in range(per):
            z = z.at[:, p * HEAD_DIM:(p + 1) * HEAD_DIM, p * HEAD_DIM:(p + 1) * HEAD_DIM].set(w[:, p])
        return z
    wg = jnp.concatenate([bd(ga_w[0]), bd(gx_w[0]), bd(ga_w[1]), bd(gx_w[1])], axis=-1).astype(bf16)
    grp = lambda b: b.reshape(ng, 1, LANES)
    gb = jnp.concatenate([grp(ga_b[0]), grp(gx_b[0]), grp(ga_b[1]), grp(gx_b[1])], axis=-1)
    return wg, gb


def _layer(x, positions, w_in, attn_gain, conv_w, conv_b, ga_w, ga_b, gx_w, gx_b, lam, rec_gain,
           w_out, ln1_g, ln1_b, router_w, router_bias, e_wg, e_wu, e_wd, s_wg, s_wu, s_wd,
           ln2_g, ln2_b, alpha):
    B, S, _ = x.shape
    T = B * S
    x2 = x.reshape(T, D_MODEL)
    pos2 = positions.reshape(T, 1)

    q, k, v, rx, rg = _inproj(x2, pos2, w_in.astype(bf16), _rope_inv_freq())
    attn = _attention(q, k, v, B, S)
    wg, gb = _gate_weights(ga_w, ga_b, gx_w, gx_b)
    rec = _rec(rx, rg, conv_w, conv_b.reshape(1, REC_W), wg, gb, lam, B, S).reshape(T, REC_W)

    rw_t = router_w.T
    rw_hi = rw_t.astype(bf16)
    rw_lo = (rw_t - rw_hi.astype(f32)).astype(bf16)
    rbias = jnp.broadcast_to(router_bias.reshape(N_EXPERTS, 1), (N_EXPERTS, LANES))
    pre, x1p, idx, gate, rank, cnt = _mixout(
        attn, rec, x2,
        attn_gain.reshape(1, ATTN_W), rec_gain.reshape(1, REC_W), w_out.astype(bf16),
        ln1_g.reshape(1, D_MODEL), ln1_b.reshape(1, D_MODEL), rw_hi, rw_lo, rbias,
        s_wg.astype(bf16), s_wu.astype(bf16), s_wd.astype(bf16), alpha)

    counts = cnt[:, 0].astype(i32)
    padded = (counts + BM - 1) // BM * BM
    pad_end = jnp.cumsum(padded)
    pad_start = pad_end - padded
    slot = _slots(idx, rank, pad_start)
    nb = (T * TOP_K + N_EXPERTS * (BM - 1)) // BM + 1
    nb += nb % 2
    nb_used = pad_end[-1] // BM
    h = (nb_used + 1) // 2
    pos = jnp.arange(nb, dtype=i32)
    blk = jnp.where(pos % 2 == 0, pos // 2, h + pos // 2)
    live = jnp.logical_and(pos < 2 * h, blk < nb_used)
    first_row = blk * BM
    block_e = jnp.minimum(jnp.sum((pad_end[None, :] <= first_row[:, None]).astype(i32), axis=1),
                          N_EXPERTS - 1)
    block_e = jnp.where(live, block_e, N_EXPERTS - 1)
    block_rows = jnp.where(live, jnp.clip((pad_start + counts)[block_e] - first_row, 0, BM), 0)
    pairs = block_e.reshape(nb // 2, 2)
    step_id = jnp.arange(nb // 2, dtype=i32)[:, None]
    change = jnp.concatenate([pairs[1:] != pairs[:-1], jnp.zeros((1, 2), bool)], axis=0)
    change = jnp.logical_and(change, step_id + 1 < h)
    nxt_step = lax.cummin(jnp.where(change, step_id + 1, nb), axis=0, reverse=True)
    next_e = jnp.where(nxt_step < nb // 2,
                       jnp.take_along_axis(pairs, jnp.minimum(nxt_step, nb // 2 - 1), axis=0), -1)
    next_e = next_e.reshape(nb).astype(i32)
    sblk = slot // BM
    dest = jnp.where(sblk < h, 2 * sblk, 2 * (sblk - h) + 1) * BM + slot % BM
    nb_used = (2 * h).astype(i32).reshape(1)

    xs = _dispatch(dest, x1p.reshape(T, CHUNKS, LANES), nb * BM)
    ys = _experts(block_e.astype(i32), next_e, block_rows.astype(i32), nb_used,
                  xs.reshape(nb * BM * CHUNKS, LANES), e_wg, e_wu, e_wd)
    ys = ys.reshape(nb * BM, CHUNKS, LANES)
    out = _combine(dest.T.reshape(-1), gate.T, pre, ys,
                   ln2_g.reshape(1, D_MODEL), ln2_b.reshape(1, D_MODEL))
    return out.reshape(B, S, D_MODEL)


def kernel(x, positions, w_in, attn_gain, rec_conv_w, rec_conv_b, rec_gate_a_w, rec_gate_a_b,
           rec_gate_x_w, rec_gate_x_b, rec_lambda, rec_gain, w_out, ln1_g, ln1_b, router_w,
           router_bias, exp_w_gate, exp_w_up, exp_w_down, shared_w_gate, shared_w_up,
           shared_w_down, ln2_g, ln2_b):
    depth = w_in.shape[0]
    alpha = (2 * depth) ** 0.25
    for l in range(depth):
        x = _layer(x, positions, w_in[l], attn_gain[l], rec_conv_w[l], rec_conv_b[l],
                   rec_gate_a_w[l], rec_gate_a_b[l], rec_gate_x_w[l], rec_gate_x_b[l],
                   rec_lambda[l], rec_gain[l], w_out[l], ln1_g[l], ln1_b[l], router_w[l],
                   router_bias[l], exp_w_gate[l], exp_w_up[l], exp_w_down[l], shared_w_gate[l],
                   shared_w_up[l], shared_w_down[l], ln2_g[l], ln2_b[l], alpha)
    return x
```

```python
import functools

import jax
import jax.numpy as jnp
import numpy as np
from jax import lax
from jax.experimental import pallas as pl
from jax.experimental.pallas import tpu as pltpu

f32 = jnp.float32
bf16 = jnp.bfloat16
i32 = jnp.int32
u32 = jnp.uint32

D_MODEL = 1024
ATTN_W = 512
REC_W = 512
HEAD_DIM = 64
ROT_DIM = 16
ROPE_THETA = 500000.0
BRANCHES = ((128, 1), (512, 4), (2048, 16))
HALF_BAND = 64
CONV_W = 4
RG_LRU_C = 8.0
N_EXPERTS = 256
TOP_K = 8
N_GROUPS = 8
GROUP_SZ = N_EXPERTS // N_GROUPS
TOPK_GROUPS = 4
EXPERT_H = 256
ROUTED_SCALE = 2.5
LN_EPS = 1e-5
NEG = -1e30

LANES = 128
SUBLANES = 8
WORDS = D_MODEL // 2
CHUNKS = WORDS // LANES
VMEM_LIMIT = 56 * 1024 * 1024
DMA_THREADS = 2

TM_IN = 512
TQ = 128
KW = TQ + 2 * HALF_BAND
SB = TQ * max(d for _, d in BRANCHES)
MERGE_ROWS = 256
TS = 256
TM_MIX = 512
TM_SLOT = 2048
TM_CMB = 256
BM = 256


def _cparams(sem):
    return pltpu.CompilerParams(dimension_semantics=sem, vmem_limit_bytes=VMEM_LIMIT)


def _inproj_body(x_ref, pos_ref, w_ref, invf_ref, q_ref, k_ref, v_ref, rx_ref, rg_ref):
    xb = x_ref[...].astype(bf16)
    ang = pos_ref[...].astype(f32) * invf_ref[...]
    cos = jnp.cos(ang)
    sin = jnp.sin(ang)
    j = lax.broadcasted_iota(i32, (1, LANES), 1) % HEAD_DIM
    half = ROT_DIM // 2
    s_lo = jnp.where(j < half, -sin, 0.0)
    s_hi = jnp.where((j >= half) & (j < ROT_DIM), sin, 0.0)
    rep = ATTN_W // LANES
    cos_w = jnp.concatenate([cos] * rep, axis=1)
    s_lo_w = jnp.concatenate([s_lo] * rep, axis=1)
    s_hi_w = jnp.concatenate([s_hi] * rep, axis=1)

    def proj(c0, n):
        return jnp.dot(xb, w_ref[:, c0:c0 + n], preferred_element_type=f32)

    def rope(t):
        return (t * cos_w + pltpu.roll(t, ATTN_W - half, 1) * s_lo_w
                + pltpu.roll(t, half, 1) * s_hi_w)

    q_ref[...] = rope(proj(0, ATTN_W)) * (HEAD_DIM ** -0.5)
    k_ref[...] = rope(proj(ATTN_W, ATTN_W))
    v_ref[...] = proj(2 * ATTN_W, ATTN_W)
    rx_ref[...] = proj(3 * ATTN_W, REC_W)
    rg_ref[...] = proj(3 * ATTN_W + REC_W, REC_W)


def _inproj(x2, pos2, w_in_b, invf):
    T = x2.shape[0]
    tm = TM_IN
    in_w = w_in_b.shape[1]
    tok = lambda i: (i, 0)
    fixed = lambda i: (0, 0)
    return pl.pallas_call(
        _inproj_body,
        grid=(T // tm,),
        in_specs=[pl.BlockSpec((tm, D_MODEL), tok), pl.BlockSpec((tm, 1), tok),
                  pl.BlockSpec((D_MODEL, in_w), fixed), pl.BlockSpec((1, LANES), fixed)],
        out_specs=[pl.BlockSpec((tm, ATTN_W), tok)] * 3 + [pl.BlockSpec((tm, REC_W), tok)] * 2,
        out_shape=[jax.ShapeDtypeStruct((T, ATTN_W), f32)] * 3
        + [jax.ShapeDtypeStruct((T, REC_W), f32)] * 2,
        compiler_params=_cparams(("parallel",)),
        name="inproj",
    )(x2, pos2, w_in_b, invf)


def _attn_body(q_ref, k_ref, v_ref, out_ref, o1, o2, o3, l1, l2, l3, bias_ref, *, S):
    o_sc, l_sc = (o1, o2, o3), (l1, l2, l3)
    lane = lax.broadcasted_iota(i32, (1, LANES), 1)
    head0 = lane < HEAD_DIM
    rel = (lax.broadcasted_iota(i32, (TQ, KW), 0) - lax.broadcasted_iota(i32, (TQ, KW), 1))
    tiles = SB // TQ
    for case in range(3):
        bias_ref[case] = jnp.where(jnp.abs(rel + case * HALF_BAND) <= HALF_BAND, 0.0, NEG)

    def rows(ref, start, n, d):
        return ref[pl.ds(start, n), :] if d == 1 else ref[pl.ds(start, n, stride=d), :]

    def tile(ti, n0):
        ctx = []
        for g, (_, d) in enumerate(BRANCHES):
            L = S // d
            sh = d.bit_length() - 1
            r = jnp.bitwise_and(ti, d - 1)
            m0 = jnp.right_shift(n0, sh) + jnp.right_shift(ti, sh) * TQ
            ks = jnp.clip(m0 - HALF_BAND, 0, L - KW)
            q = rows(q_ref, r + d * m0, TQ, d).astype(bf16)
            k = rows(k_ref, r + d * ks, KW, d).astype(bf16)
            v = rows(v_ref, r + d * ks, KW, d).astype(bf16)
            bias = bias_ref[(m0 - ks) // HALF_BAND]
            local = r + d * (m0 - jnp.right_shift(n0, sh))
            ss = []
            for sel in (head0, jnp.logical_not(head0)):
                qh = jnp.where(sel, q, jnp.zeros_like(q))
                ss.append(lax.dot_general(qh, k, (((1,), (1,)), ((), ())),
                                          preferred_element_type=f32))
            ctx.append((d, v, bias, local, ss))
        soft = []
        for d, v, bias, local, ss in ctx:
            ps = []
            for s in ss:
                s = s + bias
                m = jnp.max(s, axis=-1, keepdims=True)
                p = jnp.exp(s - m)
                den = jnp.sum(p, axis=-1, keepdims=True)
                ps.append((p.astype(bf16), den, m + jnp.log(den)))
            soft.append(ps)
        for g, ((d, v, bias, local, ss), ps) in enumerate(zip(ctx, soft)):
            outs = [(jnp.dot(p, v, preferred_element_type=f32) / den, lse) for p, den, lse in ps]
            o_val = jnp.where(head0, outs[0][0], outs[1][0])
            l_val = jnp.where(head0, outs[0][1], outs[1][1])
            if d == 1:
                o_sc[g][pl.ds(local, TQ), :] = o_val
                l_sc[g][pl.ds(local, TQ), :] = l_val
            else:
                o_sc[g][pl.ds(local, TQ, stride=d), :] = o_val
                l_sc[g][pl.ds(local, TQ, stride=d), :] = l_val
        return n0

    def merge(c, n0):
        sl = pl.ds(pl.multiple_of(c * MERGE_ROWS, MERGE_ROWS), MERGE_ROWS)
        ls = [l_sc[g][sl, :] for g in range(len(BRANCHES))]
        mx = functools.reduce(jnp.maximum, ls)
        es = [jnp.exp(l - mx) for l in ls]
        num = functools.reduce(lambda a, b: a + b, [e * o_sc[g][sl, :] for g, e in enumerate(es)])
        den = functools.reduce(lambda a, b: a + b, es)
        dst = pl.ds(pl.multiple_of(n0 + c * MERGE_ROWS, MERGE_ROWS), MERGE_ROWS)
        out_ref[dst, :] = (num / den).astype(bf16)
        return n0

    def superblock(sb, carry):
        n0 = pl.multiple_of(sb * SB, SB)
        lax.fori_loop(0, tiles, tile, n0)
        lax.fori_loop(0, SB // MERGE_ROWS, merge, n0)
        return carry

    lax.fori_loop(0, S // SB, superblock, 0)


def _attention(q, k, v, B, S):
    assert S % SB == 0 and all(S // d >= KW for _, d in BRANCHES)
    hp = ATTN_W // LANES
    view = lambda t: t.reshape(B, S, ATTN_W)
    spec = pl.BlockSpec((None, S, LANES), lambda b, h: (b, 0, h))
    out = pl.pallas_call(
        functools.partial(_attn_body, S=S),
        grid=(B, hp),
        in_specs=[spec, spec, spec],
        out_specs=spec,
        out_shape=jax.ShapeDtypeStruct((B, S, ATTN_W), bf16),
        scratch_shapes=[pltpu.VMEM((SB, LANES), f32)] * (2 * len(BRANCHES))
        + [pltpu.VMEM((3, TQ, KW), f32)],
        compiler_params=_cparams(("parallel", "parallel")),
        name="attention",
    )(view(q), view(k), view(v))
    return out.reshape(B * S, ATTN_W)


def _rec_body(rx_ref, rg_ref, cw_ref, cb_ref, wg_ref, gb_ref, lam_ref, out_ref, rxp_ref, hf_ref, *, S):
    nch = S // TS
    ntile = TS // SUBLANES
    zeros8 = jnp.zeros((SUBLANES, LANES), f32)
    rxp_ref[pl.ds(0, SUBLANES), :] = zeros8
    rxp_ref[pl.ds(S + SUBLANES, SUBLANES), :] = zeros8

    def pad_copy(c, carry):
        t0 = pl.multiple_of(c * TS, TS)
        rxp_ref[pl.ds(t0 + SUBLANES, TS), :] = rx_ref[pl.ds(t0, TS), :]
        return carry

    lax.fori_loop(0, nch, pad_copy, 0)

    lam = lam_ref[...]
    neg_sp = -RG_LRU_C * (jnp.maximum(-lam, 0.0) + jnp.log1p(jnp.exp(-jnp.abs(lam))))
    cw = cw_ref[...]
    cb = cb_ref[...]
    sub = lax.broadcasted_iota(i32, (ntile, SUBLANES, LANES), 1)
    nrow = TS + 2 * SUBLANES

    def gates(ci, d):
        t0 = pl.multiple_of(ci * TS, TS)
        xw = rxp_ref[pl.ds(t0, nrow), :]
        u = (cw[0:1] * pltpu.roll(xw, 2, 0) + cw[1:2] * pltpu.roll(xw, 1, 0) + cw[2:3] * xw
             + cw[3:4] * pltpu.roll(xw, nrow - 1, 0))[SUBLANES:SUBLANES + TS] + cb
        c0 = d * 2 * LANES
        g = jnp.dot(u.astype(bf16), wg_ref[:, c0:c0 + 2 * LANES], preferred_element_type=f32)
        g = g + gb_ref[:, c0:c0 + 2 * LANES]
        r = jax.nn.sigmoid(g[:, :LANES])
        gi = jax.nn.sigmoid(g[:, LANES:])
        a = jnp.exp(neg_sp[d:d + 1] * r)
        b = jnp.sqrt(1.0 - a * a) * gi * u
        return a.reshape(ntile, SUBLANES, LANES), b.reshape(ntile, SUBLANES, LANES)

    def chunk(i, carry):
        cf, cbk = carry
        a, b = gates(i, 0)
        for s in (1, 2, 4):
            ok = sub >= s
            a_s = pltpu.roll(a, s, 1)
            b_s = pltpu.roll(b, s, 1)
            b = jnp.where(ok, a * b_s + b, b)
            a = jnp.where(ok, a * a_s, a)
        t0 = pl.multiple_of(i * TS, TS)
        for j in range(ntile):
            h = a[j] * cf + b[j]
            hf_ref[pl.ds(t0 + j * SUBLANES, SUBLANES), :] = h
            cf = h[SUBLANES - 1:SUBLANES, :]
        ib = nch - 1 - i
        a, b = gates(ib, 1)
        for s in (1, 2, 4):
            ok = sub < SUBLANES - s
            a_s = pltpu.roll(a, SUBLANES - s, 1)
            b_s = pltpu.roll(b, SUBLANES - s, 1)
            b = jnp.where(ok, a * b_s + b, b)
            a = jnp.where(ok, a * a_s, a)
        t0 = pl.multiple_of(ib * TS, TS)
        for j in range(ntile - 1, -1, -1):
            h = a[j] * cbk + b[j]
            out_ref[pl.ds(t0 + j * SUBLANES, SUBLANES), :] = h
            cbk = h[0:1, :]
        return cf, cbk

    zrow = jnp.zeros((1, LANES), f32)
    lax.fori_loop(0, nch, chunk, (zrow, zrow))

    def finish(c, carry):
        t0 = pl.multiple_of(c * TS, TS)
        sl = pl.ds(t0, TS)
        out_ref[sl, :] = (hf_ref[sl, :] + out_ref[sl, :]) * jax.nn.gelu(rg_ref[sl, :])
        return carry

    lax.fori_loop(0, nch, finish, 0)


def _rec(rx, rg, conv_w, conv_b, wg, gb, lam, B, S):
    ng = REC_W // LANES
    assert S % TS == 0
    seq = pl.BlockSpec((None, S, LANES), lambda b, c: (b, 0, c))
    return pl.pallas_call(
        functools.partial(_rec_body, S=S),
        grid=(B, ng),
        in_specs=[seq, seq,
                  pl.BlockSpec((CONV_W, LANES), lambda b, c: (0, c)),
                  pl.BlockSpec((1, LANES), lambda b, c: (0, c)),
                  pl.BlockSpec((None, LANES, 4 * LANES), lambda b, c: (c, 0, 0)),
                  pl.BlockSpec((None, 1, 4 * LANES), lambda b, c: (c, 0, 0)),
                  pl.BlockSpec((2, LANES), lambda b, c: (0, c))],
        out_specs=seq,
        out_shape=jax.ShapeDtypeStruct((B, S, REC_W), f32),
        scratch_shapes=[pltpu.VMEM((S + 2 * SUBLANES, LANES), f32), pltpu.VMEM((S, LANES), f32)],
        compiler_params=_cparams(("parallel", "parallel")),
        name="rec",
    )(rx.reshape(B, S, REC_W), rg.reshape(B, S, REC_W), conv_w, conv_b, wg, gb, lam)


def _pack_rows(x):
    r = lax.bitcast_convert_type(x.astype(bf16).astype(f32), u32)
    return (r[:, WORDS:] & u32(0xFFFF0000)) | (r[:, :WORDS] >> 16)


def _unpack_rows(w):
    lo = lax.bitcast_convert_type(w << 16, f32).astype(bf16)
    hi = lax.bitcast_convert_type(w & u32(0xFFFF0000), f32).astype(bf16)
    return lo, hi


def _slab_chunks(flat, rows, base=0):
    return jnp.concatenate(
        [flat[pl.ds(base * CHUNKS + c, rows, stride=CHUNKS), :] for c in range(CHUNKS)], axis=1)


def _store_slab_chunks(flat, val, rows, base=0):
    for c in range(CHUNKS):
        flat[pl.ds(base * CHUNKS + c, rows, stride=CHUNKS), :] = val[:, c * LANES:(c + 1) * LANES]


def _rms(t, gain):
    return t * lax.rsqrt(jnp.mean(t * t, axis=-1, keepdims=True) + LN_EPS) * gain


def _layer_norm(z, g, b):
    mu = jnp.mean(z, axis=-1, keepdims=True)
    zc = z - mu
    var = jnp.mean(zc * zc, axis=-1, keepdims=True)
    return zc * lax.rsqrt(var + LN_EPS) * g + b


def _mixout_body(attn_ref, rec_ref, x_ref,
                 ag_ref, rgn_ref, wo_ref, g1_ref, b1_ref, rwh_ref, rwl_ref, rb_ref, tri_ref, ones_ref,
                 sg_ref, su_ref, sd_ref, pre_ref, x1p_ref, idx_ref, gate_ref, rank_ref, cnt_ref, carry_ref, *, alpha):
    tm = TM_MIX

    @pl.when(pl.program_id(0) == 0)
    def _():
        carry_ref[...] = jnp.zeros_like(carry_ref)

    attn_n = _rms(attn_ref[...].astype(f32), ag_ref[...]).astype(bf16)
    rec_n = _rms(rec_ref[...], rgn_ref[...]).astype(bf16)
    y = (jnp.dot(attn_n, wo_ref[0:ATTN_W, :], preferred_element_type=f32)
         + jnp.dot(rec_n, wo_ref[ATTN_W:, :], preferred_element_type=f32))
    x1 = _layer_norm(alpha * x_ref[...] + y, g1_ref[...], b1_ref[...])
    _store_slab_chunks(x1p_ref, _pack_rows(x1), tm)

    hi = x1.astype(bf16)
    sg = jnp.dot(hi, sg_ref[...], preferred_element_type=f32)
    su = jnp.dot(hi, su_ref[...], preferred_element_type=f32)
    shared = jnp.dot((sg * jax.nn.sigmoid(sg) * su).astype(bf16), sd_ref[...],
                     preferred_element_type=f32)
    pre_ref[...] = alpha * x1 + shared

    lo = (x1 - hi.astype(f32)).astype(bf16)
    nt = (((1,), (1,)), ((), ()))
    logits = (lax.dot_general(rwh_ref[...], hi, nt, preferred_element_type=f32)
              + lax.dot_general(rwh_ref[...], lo, nt, preferred_element_type=f32)
              + lax.dot_general(rwl_ref[...], hi, nt, preferred_element_type=f32))
    scores = jax.nn.sigmoid(logits)
    biased = scores + jnp.concatenate([rb_ref[...]] * (tm // LANES), axis=1)

    rid = lax.broadcasted_iota(i32, (GROUP_SZ, tm), 0).astype(f32)
    grp = []
    for g in range(N_GROUPS):
        vg = biased[g * GROUP_SZ:(g + 1) * GROUP_SZ, :]
        m1 = jnp.max(vg, axis=0, keepdims=True)
        first = jnp.min(jnp.where(vg == m1, rid, float(GROUP_SZ)), axis=0, keepdims=True)
        m2 = jnp.max(jnp.where(rid == first, -jnp.inf, vg), axis=0, keepdims=True)
        grp.append(m1 + m2)
    eid = lax.broadcasted_iota(i32, (N_EXPERTS, tm), 0).astype(f32)
    keep = [jnp.zeros((1, tm), f32) for _ in range(N_GROUPS)]
    for _ in range(TOPK_GROUPS):
        gm = functools.reduce(jnp.maximum, grp)
        gi = jnp.full((1, tm), float(N_GROUPS), f32)
        for g in range(N_GROUPS - 1, -1, -1):
            gi = jnp.where(grp[g] == gm, float(g), gi)
        hits = [gi == float(g) for g in range(N_GROUPS)]
        grp = [jnp.where(hit, -jnp.inf, sc) for hit, sc in zip(hits, grp)]
        keep = [jnp.where(hit, 1.0, kp) for hit, kp in zip(hits, keep)]
    masked = jnp.concatenate(
        [jnp.where(jnp.broadcast_to(keep[g], (GROUP_SZ, tm)) > 0.5,
                   biased[g * GROUP_SZ:(g + 1) * GROUP_SZ, :], -jnp.inf)
         for g in range(N_GROUPS)], axis=0)

    onehot = jnp.zeros((N_EXPERTS, tm), f32)
    idxs, gts = [], []
    for _ in range(TOP_K):
        mx = jnp.max(masked, axis=0, keepdims=True)
        ix = jnp.min(jnp.where(masked == mx, eid, float(N_EXPERTS)), axis=0, keepdims=True)
        hit = eid == ix
        gts.append(jnp.sum(jnp.where(hit, scores, 0.0), axis=0, keepdims=True))
        idxs.append(ix)
        masked = jnp.where(hit, -jnp.inf, masked)
        onehot = onehot + jnp.where(hit, 1.0, 0.0)
    gsum = functools.reduce(lambda p, q: p + q, gts)
    for kk in range(TOP_K):
        idx_ref[kk:kk + 1, :] = idxs[kk].astype(i32)
        gate_ref[kk:kk + 1, :] = gts[kk] / gsum * ROUTED_SCALE

    oh = onehot.astype(bf16)
    before = carry_ref[...] + jnp.dot(oh, tri_ref[...], preferred_element_type=f32)
    for kk in range(TOP_K):
        rk = jnp.sum(jnp.where(eid == idxs[kk], before, 0.0), axis=0, keepdims=True)
        rank_ref[kk:kk + 1, :] = rk.astype(i32)
    total = carry_ref[...] + jnp.dot(oh, ones_ref[...], preferred_element_type=f32)
    carry_ref[...] = total
    cnt_ref[...] = total


def _mixout(attn, rec, x2, attn_gain, rec_gain, w_out_b, g1, b1, rw_hi, rw_lo, rbias, sg, su, sd,
            alpha):
    T = x2.shape[0]
    tm = TM_MIX
    tok = lambda i: (i, 0)
    fixed = lambda i: (0, 0)
    tri = jnp.asarray(np.triu(np.ones((tm, tm), np.float32), k=1), bf16)
    ones = jnp.ones((tm, tm), bf16)
    aw = pl.BlockSpec((tm, ATTN_W), tok)
    row = lambda n: pl.BlockSpec((1, n), fixed)
    kt = pl.BlockSpec((TOP_K, tm), lambda i: (0, i))
    return pl.pallas_call(
        functools.partial(_mixout_body, alpha=alpha),
        grid=(T // tm,),
        in_specs=[aw] * 2 + [pl.BlockSpec((tm, D_MODEL), tok), row(ATTN_W), row(REC_W),
                             pl.BlockSpec((D_MODEL, D_MODEL), fixed), row(D_MODEL), row(D_MODEL),
                             pl.BlockSpec((N_EXPERTS, D_MODEL), fixed),
                             pl.BlockSpec((N_EXPERTS, D_MODEL), fixed),
                             pl.BlockSpec((N_EXPERTS, LANES), fixed),
                             pl.BlockSpec((tm, tm), fixed), pl.BlockSpec((tm, tm), fixed),
                             pl.BlockSpec(sg.shape, fixed), pl.BlockSpec(su.shape, fixed),
                             pl.BlockSpec(sd.shape, fixed)],
        out_specs=[pl.BlockSpec((tm, D_MODEL), tok),
                   pl.BlockSpec((tm * CHUNKS, LANES), tok),
                   kt, kt, kt, pl.BlockSpec((N_EXPERTS, tm), fixed)],
        out_shape=[jax.ShapeDtypeStruct((T, D_MODEL), f32),
                   jax.ShapeDtypeStruct((T * CHUNKS, LANES), u32),
                   jax.ShapeDtypeStruct((TOP_K, T), i32),
                   jax.ShapeDtypeStruct((TOP_K, T), f32),
                   jax.ShapeDtypeStruct((TOP_K, T), i32),
                   jax.ShapeDtypeStruct((N_EXPERTS, tm), f32)],
        scratch_shapes=[pltpu.VMEM((N_EXPERTS, tm), f32)],
        compiler_params=_cparams(("arbitrary",)),
        name="mixout",
    )(attn, rec, x2, attn_gain, rec_gain, w_out_b, g1, b1,
      rw_hi, rw_lo, rbias, tri, ones, sg, su, sd)


def _slots_body(idx_ref, rank_ref, start_ref, dest_ref):
    tm = TM_SLOT
    eid = lax.broadcasted_iota(i32, (N_EXPERTS, tm), 0)
    start = jnp.concatenate([start_ref[...]] * (tm // LANES), axis=1)
    for kk in range(TOP_K):
        hit = eid == idx_ref[kk:kk + 1, :]
        base = jnp.sum(jnp.where(hit, start, 0.0), axis=0, keepdims=True)
        dest_ref[kk:kk + 1, :] = base.astype(i32) + rank_ref[kk:kk + 1, :]


def _slots(idx, rank, pad_start):
    T = idx.shape[1]
    tm = TM_SLOT
    kt = pl.BlockSpec((TOP_K, tm), lambda i: (0, i))
    start = jnp.broadcast_to(pad_start.astype(f32).reshape(N_EXPERTS, 1), (N_EXPERTS, LANES))
    return pl.pallas_call(
        _slots_body,
        grid=(T // tm,),
        in_specs=[kt, kt, pl.BlockSpec((N_EXPERTS, LANES), lambda i: (0, 0))],
        out_specs=kt,
        out_shape=jax.ShapeDtypeStruct((TOP_K, T), i32),
        compiler_params=_cparams(("parallel",)),
        name="slots",
    )(idx, rank, start)


def _dispatch_body(dest_ref, x_ref, xs_ref, sem):
    tm = TM_MIX

    def copy(t, k):
        return pltpu.make_async_copy(x_ref.at[t], xs_ref.at[dest_ref[k, t]], sem)

    def issue(t, carry):
        for k in range(TOP_K):
            copy(t, k).start(priority=k % DMA_THREADS)
        return carry

    def drain(t, carry):
        for k in range(TOP_K):
            copy(t, k).wait()
        return carry

    lax.fori_loop(0, tm, issue, 0)
    lax.fori_loop(0, tm, drain, 0)


def _dispatch(dest, x1p, n_slots):
    T = x1p.shape[0]
    tm = TM_MIX
    return pl.pallas_call(
        _dispatch_body,
        grid=(T // tm,),
        in_specs=[pl.BlockSpec((TOP_K, tm), lambda i: (0, i), memory_space=pltpu.SMEM),
                  pl.BlockSpec((tm, CHUNKS, LANES), lambda i: (i, 0, 0))],
        out_specs=pl.BlockSpec(memory_space=pl.ANY),
        out_shape=jax.ShapeDtypeStruct((n_slots, CHUNKS, LANES), u32),
        scratch_shapes=[pltpu.SemaphoreType.DMA(())],
        compiler_params=_cparams(("arbitrary",)),
        name="dispatch",
    )(dest, x1p)


def _experts_body(be_ref, nx_ref, nv_ref, nb_ref, xs_ref, wg_hbm, wu_hbm, wd_hbm, ys_ref,
                  fga, fua, fda, fgb, fub, fdb, sga, sua, sda, sgb, sub, sdb, sems):
    p = pl.program_id(0)
    blocks = (2 * p, 2 * p + 1)
    hbm_w = (wg_hbm, wu_hbm, wd_hbm)
    f32_w = ((fga, fua, fda), (fgb, fub, fdb))
    bf_w = ((sga, sua, sda), (sgb, sub, sdb))
    live_step = blocks[0] < nb_ref[0]

    def fetch(half, e):
        return [pltpu.make_async_copy(src.at[e], dst, sems.at[half])
                for src, dst in zip(hbm_w, f32_w[half])]

    for half, blk in enumerate(blocks):
        @pl.when(jnp.logical_and(live_step, p == 0))
        def _(half=half, blk=blk):
            for c in fetch(half, be_ref[blk]):
                c.start()

        prev = jnp.maximum(blk - 2, 0)

        @pl.when(jnp.logical_and(live_step,
                                 jnp.logical_or(p == 0, be_ref[blk] != be_ref[prev])))
        def _(half=half, blk=blk):
            for c in fetch(half, be_ref[blk]):
                c.wait()
            for src, dst in zip(f32_w[half], bf_w[half]):
                dst[...] = src[...].astype(bf16)

            @pl.when(nx_ref[blk] >= 0)
            def _():
                for c in fetch(half, nx_ref[blk]):
                    c.start(priority=1)

    @pl.when(live_step)
    def _():
        xs = []
        for half, blk in enumerate(blocks):
            rows = jnp.where(blk < nb_ref[0], nv_ref[blk], 0)
            live = lax.broadcasted_iota(i32, (BM, WORDS), 0) < rows
            words = _slab_chunks(xs_ref, BM, half * BM)
            xs.append(_unpack_rows(jnp.where(live, words, u32(0))))

        def up(half, w_ref):
            lo, hi = xs[half]
            return (jnp.dot(lo, w_ref[:WORDS, :], preferred_element_type=f32)
                    + jnp.dot(hi, w_ref[WORDS:, :], preferred_element_type=f32))

        gs = [up(half, bf_w[half][0]) for half in range(2)]
        us = [up(half, bf_w[half][1]) for half in range(2)]
        hs = [(g * jax.nn.sigmoid(g) * u).astype(bf16) for g, u in zip(gs, us)]
        ys = [jnp.dot(h, bf_w[half][2][...], preferred_element_type=f32)
              for half, h in enumerate(hs)]
        for half in range(2):
            _store_slab_chunks(ys_ref, _pack_rows(ys[half]), BM, half * BM)


def _experts(block_e, next_e, block_rows, nb_used, xs, wg, wu, wd):
    P = xs.shape[0] // CHUNKS
    nb = P // BM
    assert nb % 2 == 0
    last = lambda nbu: jnp.maximum((nbu[0] - 1) // 2, 0)
    rows = lambda p, be, nx, nv, nbu: (jnp.minimum(p, last(nbu)), 0)
    any_spec = pl.BlockSpec(memory_space=pl.ANY)
    up_shape, down_shape = (D_MODEL, EXPERT_H), (EXPERT_H, D_MODEL)
    per_stream = lambda dt: [pltpu.VMEM(up_shape, dt), pltpu.VMEM(up_shape, dt),
                             pltpu.VMEM(down_shape, dt)]
    gs = pltpu.PrefetchScalarGridSpec(
        num_scalar_prefetch=4,
        grid=(nb // 2,),
        in_specs=[pl.BlockSpec((2 * BM * CHUNKS, LANES), rows), any_spec, any_spec, any_spec],
        out_specs=pl.BlockSpec((2 * BM * CHUNKS, LANES), rows),
        scratch_shapes=per_stream(f32) * 2 + per_stream(bf16) * 2
        + [pltpu.SemaphoreType.DMA((2,))],
    )
    return pl.pallas_call(
        _experts_body,
        grid_spec=gs,
        out_shape=jax.ShapeDtypeStruct((P * CHUNKS, LANES), u32),
        compiler_params=_cparams(("arbitrary",)),
        name="experts",
    )(block_e, next_e, block_rows, nb_used, xs, wg, wu, wd)


def _combine_body(dest_ref, dnext_ref, gate_ref, pre_ref, ys_ref, ysflat_ref,
                  g2_ref, b2_ref, out_ref, gb00, gb01, gb10, gb11, sems):
    tm = TM_CMB
    j = pl.program_id(0)
    n = pl.num_programs(0)
    gbufs = ((gb00, gb01), (gb10, gb11))

    def issue(d_ref, st, half):
        base = half * tm

        def step(t, carry):
            for k in range(TOP_K):
                src = ys_ref.at[d_ref[(base + t) * TOP_K + k]]
                dst = gbufs[st][half].at[pl.ds((k * tm + t) * CHUNKS, CHUNKS)]
                pltpu.make_async_copy(src, dst, sems.at[st, half]).start(priority=k % DMA_THREADS)
            return carry
        lax.fori_loop(0, tm, step, 0)

    def finish(st, half):
        rows = pl.ds(half * tm, tm)
        flat = gbufs[st][half]
        pltpu.make_async_copy(ysflat_ref.at[pl.ds(0, TOP_K * tm * CHUNKS)], flat,
                              sems.at[st, half]).wait()

        gates = gate_ref[rows, :]
        lo = [jnp.zeros((tm, LANES), f32) for _ in range(CHUNKS)]
        hi = [jnp.zeros((tm, LANES), f32) for _ in range(CHUNKS)]
        for k in range(TOP_K):
            gk = jnp.broadcast_to(gates[:, k:k + 1], (tm, LANES))
            for c in range(CHUNKS):
                w = flat[pl.ds(k * tm * CHUNKS + c, tm, stride=CHUNKS), :]
                lo[c] = lo[c] + gk * lax.bitcast_convert_type(w << 16, f32)
                hi[c] = hi[c] + gk * lax.bitcast_convert_type(w & u32(0xFFFF0000), f32)
        routed = jnp.concatenate(lo + hi, axis=1)
        out_ref[rows, :] = _layer_norm(pre_ref[rows, :] + routed, g2_ref[...], b2_ref[...])

    @pl.when(j == 0)
    def _():
        issue(dest_ref, 0, 0)
        issue(dest_ref, 0, 1)

    for st in range(2):
        @pl.when(lax.rem(j, 2) == st)
        def _(st=st):
            @pl.when(j + 1 < n)
            def _():
                issue(dnext_ref, 1 - st, 0)
                issue(dnext_ref, 1 - st, 1)

            finish(st, 0)
            finish(st, 1)


def _combine(dest, gate, pre, ys, g2, b2):
    T = pre.shape[0]
    tm = TM_CMB
    n = T // (2 * tm)
    tok = lambda j: (j, 0)
    fixed = lambda j: (0, 0)
    tbl = pl.BlockSpec((2 * tm * TOP_K,), lambda j: (j,), memory_space=pltpu.SMEM)
    tbl_next = pl.BlockSpec((2 * tm * TOP_K,), lambda j: (jnp.minimum(j + 1, n - 1),),
                            memory_space=pltpu.SMEM)
    any_spec = pl.BlockSpec(memory_space=pl.ANY)
    return pl.pallas_call(
        _combine_body,
        grid=(n,),
        in_specs=[tbl, tbl_next, pl.BlockSpec((2 * tm, TOP_K), tok),
                  pl.BlockSpec((2 * tm, D_MODEL), tok), any_spec, any_spec,
                  pl.BlockSpec((1, D_MODEL), fixed), pl.BlockSpec((1, D_MODEL), fixed)],
        out_specs=pl.BlockSpec((2 * tm, D_MODEL), tok),
        out_shape=jax.ShapeDtypeStruct((T, D_MODEL), f32),
        scratch_shapes=[pltpu.VMEM((TOP_K * tm * CHUNKS, LANES), u32)] * 4
        + [pltpu.SemaphoreType.DMA((2, 2))],
        compiler_params=_cparams(("arbitrary",)),
        name="combine",
    )(dest, dest, gate, pre, ys, ys.reshape(ys.shape[0] * CHUNKS, LANES), g2, b2)


def _rope_inv_freq():
    half = ROT_DIM // 2
    inv = ROPE_THETA ** (-jnp.arange(half, dtype=f32) * 2.0 / ROT_DIM)
    j = np.arange(LANES) % HEAD_DIM
    table = jnp.where(j < ROT_DIM, inv[j % half], 0.0)
    return table.reshape(1, LANES).astype(f32)


def _gate_weights(ga_w, ga_b, gx_w, gx_b):
    ng = REC_W // LANES
    per = LANES // HEAD_DIM
    def bd(w):
        w = w.reshape(ng, per, HEAD_DIM, HEAD_DIM)
        z = jnp.zeros((ng, LANES, LANES), w.dtype)
        for p in range(per):
            z = z.at[:, p * HEAD_DIM:(p + 1) * HEAD_DIM, p * HEAD_DIM:(p + 1) * HEAD_DIM].set(w[:, p])
        return z
    wg = jnp.concatenate([bd(ga_w[0]), bd(gx_w[0]), bd(ga_w[1]), bd(gx_w[1])], axis=-1).astype(bf16)
    grp = lambda b: b.reshape(ng, 1, LANES)
    gb = jnp.concatenate([grp(ga_b[0]), grp(gx_b[0]), grp(ga_b[1]), grp(gx_b[1])], axis=-1)
    return wg, gb


def _layer(x, positions, w_in, attn_gain, conv_w, conv_b, ga_w, ga_b, gx_w, gx_b, lam, rec_gain,
           w_out, ln1_g, ln1_b, router_w, router_bias, e_wg, e_wu, e_wd, s_wg, s_wu, s_wd,
           ln2_g, ln2_b, alpha):
    B, S, _ = x.shape
    T = B * S
    x2 = x.reshape(T, D_MODEL)
    pos2 = positions.reshape(T, 1)

    q, k, v, rx, rg = _inproj(x2, pos2, w_in.astype(bf16), _rope_inv_freq())
    attn = _attention(q, k, v, B, S)
    wg, gb = _gate_weights(ga_w, ga_b, gx_w, gx_b)
    rec = _rec(rx, rg, conv_w, conv_b.reshape(1, REC_W), wg, gb, lam, B, S).reshape(T, REC_W)

    rw_t = router_w.T
    rw_hi = rw_t.astype(bf16)
    rw_lo = (rw_t - rw_hi.astype(f32)).astype(bf16)
    rbias = jnp.broadcast_to(router_bias.reshape(N_EXPERTS, 1), (N_EXPERTS, LANES))
    pre, x1p, idx, gate, rank, cnt = _mixout(
        attn, rec, x2,
        attn_gain.reshape(1, ATTN_W), rec_gain.reshape(1, REC_W), w_out.astype(bf16),
        ln1_g.reshape(1, D_MODEL), ln1_b.reshape(1, D_MODEL), rw_hi, rw_lo, rbias,
        s_wg.astype(bf16), s_wu.astype(bf16), s_wd.astype(bf16), alpha)

    counts = cnt[:, 0].astype(i32)
    padded = (counts + BM - 1) // BM * BM
    pad_end = jnp.cumsum(padded)
    pad_start = pad_end - padded
    slot = _slots(idx, rank, pad_start)
    nb = (T * TOP_K + N_EXPERTS * (BM - 1)) // BM + 1
    nb += nb % 2
    nb_used = pad_end[-1] // BM
    h = (nb_used + 1) // 2
    pos = jnp.arange(nb, dtype=i32)
    blk = jnp.where(pos % 2 == 0, pos // 2, h + pos // 2)
    live = jnp.logical_and(pos < 2 * h, blk < nb_used)
    first_row = blk * BM
    block_e = jnp.minimum(jnp.sum((pad_end[None, :] <= first_row[:, None]).astype(i32), axis=1),
                          N_EXPERTS - 1)
    block_e = jnp.where(live, block_e, N_EXPERTS - 1)
    block_rows = jnp.where(live, jnp.clip((pad_start + counts)[block_e] - first_row, 0, BM), 0)
    pairs = block_e.reshape(nb // 2, 2)
    step_id = jnp.arange(nb // 2, dtype=i32)[:, None]
    change = jnp.concatenate([pairs[1:] != pairs[:-1], jnp.zeros((1, 2), bool)], axis=0)
    change = jnp.logical_and(change, step_id + 1 < h)
    nxt_step = lax.cummin(jnp.where(change, step_id + 1, nb), axis=0, reverse=True)
    next_e = jnp.where(nxt_step < nb // 2,
                       jnp.take_along_axis(pairs, jnp.minimum(nxt_step, nb // 2 - 1), axis=0), -1)
    next_e = next_e.reshape(nb).astype(i32)
    sblk = slot // BM
    dest = jnp.where(sblk < h, 2 * sblk, 2 * (sblk - h) + 1) * BM + slot % BM
    nb_used = (2 * h).astype(i32).reshape(1)

    xs = _dispatch(dest, x1p.reshape(T, CHUNKS, LANES), nb * BM)
    ys = _experts(block_e.astype(i32), next_e, block_rows.astype(i32), nb_used,
                  xs.reshape(nb * BM * CHUNKS, LANES), e_wg, e_wu, e_wd)
    ys = ys.reshape(nb * BM, CHUNKS, LANES)
    out = _combine(dest.T.reshape(-1), gate.T, pre, ys,
                   ln2_g.reshape(1, D_MODEL), ln2_b.reshape(1, D_MODEL))
    return out.reshape(B, S, D_MODEL)


def kernel(x, positions, w_in, attn_gain, rec_conv_w, rec_conv_b, rec_gate_a_w, rec_gate_a_b,
           rec_gate_x_w, rec_gate_x_b, rec_lambda, rec_gain, w_out, ln1_g, ln1_b, router_w,
           router_bias, exp_w_gate, exp_w_up, exp_w_down, shared_w_gate, shared_w_up,
           shared_w_down, ln2_g, ln2_b):
    depth = w_in.shape[0]
    alpha = (2 * depth) ** 0.25
    for l in range(depth):
        x = _layer(x, positions, w_in[l], attn_gain[l], rec_conv_w[l], rec_conv_b[l],
                   rec_gate_a_w[l], rec_gate_a_b[l], rec_gate_x_w[l], rec_gate_x_b[l],
                   rec_lambda[l], rec_gain[l], w_out[l], ln1_g[l], ln1_b[l], router_w[l],
                   router_bias[l], exp_w_gate[l], exp_w_up[l], exp_w_down[l], shared_w_gate[l],
                   shared_w_up[l], shared_w_down[l], ln2_g[l], ln2_b[l], alpha)
    return x
```
